```python
import math
import jax, jax.numpy as jnp
from jax import lax
import numpy as np

D_MODEL = 1024
BATCH = 16
SEQ = 2048
DEPTH = 2

N_MIXERS = 2
N_LAYERS_A = (DEPTH + 1) // 2
N_LAYERS_B = DEPTH // 2

A_HEADS = 4
A_QK_DIM = D_MODEL // (2 * A_HEADS)
A_V_DIM = D_MODEL // A_HEADS
A_CHUNK = 128
A_HQ = A_HEADS * A_QK_DIM
A_HV = A_HEADS * A_V_DIM
A_IN_COLS = 2 * A_HQ + 2 * A_HV + 2 * A_HEADS

B_HEADS = 8
B_HEAD_DIM = D_MODEL // (2 * B_HEADS)
B_V_DIM = 2 * B_HEAD_DIM
B_ROT_DIM = B_HEAD_DIM // 4
ROPE_THETA = 500000.0
Q_BLOCK = 128
B_IN_COLS = 3 * D_MODEL

N_EXPERTS = 16
N_GROUPS = 4
EXPERTS_PER_GROUP = N_EXPERTS // N_GROUPS
TOP_K = 2
D_EXPERT = 512
MOE_BLOCK = 128

EPS = 1e-6

kernel_name = "hybrid_mlstm_diffattn_grouped_moe"


def _rms(x, g):
    xf = x.astype(jnp.float32)
    y = xf * lax.rsqrt(jnp.mean(xf * xf, axis=-1, keepdims=True) + EPS)
    return y * g.astype(jnp.float32)


def _chunks(t):
    b_, h_, s_ = t.shape[:3]
    t = t.reshape(b_, h_, s_ // A_CHUNK, A_CHUNK, *t.shape[3:])
    return jnp.moveaxis(t, 2, 0)


def _mlstm_chunkwise(q, k, v, i_pre, log_f):
    b_, h_, s_, dk = q.shape
    dv = v.shape[-1]
    causal = jnp.tril(jnp.ones((A_CHUNK, A_CHUNK), dtype=bool))

    def step(carry, inp):
        C, n, m = carry
        qc, kc, vc, ic, fc = inp
        b = jnp.cumsum(fc, axis=-1)
        log_d = b[..., :, None] - b[..., None, :] + ic[..., None, :]
        log_d = jnp.where(causal, log_d, -jnp.inf)
        log_inter = b + m[..., None]
        m_t = jnp.maximum(jnp.max(log_d, axis=-1), log_inter)
        d = jnp.exp(log_d - m_t[..., None])
        inter = jnp.exp(log_inter - m_t)
        s = jnp.einsum('bhtd,bhsd->bhts', qc, kc) * d
        num = jnp.einsum('bhts,bhsv->bhtv', s, vc) + inter[..., None] * jnp.einsum('bhtd,bhdv->bhtv', qc, C)
        den = jnp.sum(s, axis=-1) + inter * jnp.einsum('bhtd,bhd->bht', qc, n)
        h = num / jnp.maximum(jnp.abs(den), jnp.exp(-m_t))[..., None]
        b_last = b[..., -1]
        log_w = b_last[..., None] - b + ic
        m_new = jnp.maximum(b_last + m, jnp.max(log_w, axis=-1))
        w = jnp.exp(log_w - m_new[..., None])
        decay = jnp.exp(b_last + m - m_new)
        kw = kc * w[..., None]
        C_new = decay[..., None, None] * C + jnp.einsum('bhsd,bhsv->bhdv', kw, vc)
        n_new = decay[..., None] * n + jnp.sum(kw, axis=2)
        return (C_new, n_new, m_new), h

    init = (jnp.zeros((b_, h_, dk, dv), jnp.float32), jnp.zeros((b_, h_, dk), jnp.float32),
            jnp.zeros((b_, h_), jnp.float32))
    _, hs = lax.scan(step, init, (_chunks(q), _chunks(k), _chunks(v), _chunks(i_pre), _chunks(log_f)))
    return jnp.moveaxis(hs, 0, 2).reshape(b_, h_, s_, dv)


def _mlstm_mixer(h, w_in, b_if, h_norm, w_out):
    b_, s_, _ = h.shape
    proj = h @ w_in.astype(h.dtype)
    q = proj[..., :A_HQ].reshape(b_, s_, A_HEADS, A_QK_DIM)
    k = proj[..., A_HQ:2 * A_HQ].reshape(b_, s_, A_HEADS, A_QK_DIM)
    v = proj[..., 2 * A_HQ:2 * A_HQ + A_HV].reshape(b_, s_, A_HEADS, A_V_DIM)
    o = proj[..., 2 * A_HQ + A_HV:2 * A_HQ + 2 * A_HV]
    gates = proj[..., 2 * A_HQ + 2 * A_HV:].astype(jnp.float32) + b_if.astype(jnp.float32)
    i_pre = jnp.transpose(gates[..., :A_HEADS], (0, 2, 1))
    log_f = jax.nn.log_sigmoid(jnp.transpose(gates[..., A_HEADS:], (0, 2, 1)))
    q = jnp.transpose(q, (0, 2, 1, 3)).astype(jnp.float32)
    k = jnp.transpose(k, (0, 2, 1, 3)).astype(jnp.float32) * (A_QK_DIM ** -0.5)
    v = jnp.transpose(v, (0, 2, 1, 3)).astype(jnp.float32)
    hout = _mlstm_chunkwise(q, k, v, i_pre, log_f)
    hout = _rms(jnp.transpose(hout, (0, 2, 1, 3)), h_norm.reshape(A_HEADS, A_V_DIM))
    hout = hout.reshape(b_, s_, A_HV) * jax.nn.sigmoid(o.astype(jnp.float32))
    return hout.astype(h.dtype) @ w_out.astype(h.dtype)


def _partial_rope(t, positions):
    half = B_ROT_DIM // 2
    inv_freq = ROPE_THETA ** (-jnp.arange(0, B_ROT_DIM, 2, dtype=jnp.float32) / B_ROT_DIM)
    ang = positions.astype(jnp.float32)[..., None] * inv_freq
    cos = jnp.cos(ang)[:, :, None, None, :]
    sin = jnp.sin(ang)[:, :, None, None, :]
    x1 = t[..., :half]
    x2 = t[..., half:B_ROT_DIM]
    return jnp.concatenate([x1 * cos - x2 * sin, x2 * cos + x1 * sin, t[..., B_ROT_DIM:]], axis=-1)


def _diff_attention(h, positions, w_in, q_norm, k_norm, lam_q1, lam_k1, lam_q2, lam_k2, o_norm, w_out, layer_idx):
    b_, s_, _ = h.shape
    proj = h @ w_in.astype(h.dtype)
    q = proj[..., :D_MODEL].reshape(b_, s_, B_HEADS, 2, B_HEAD_DIM)
    k = proj[..., D_MODEL:2 * D_MODEL].reshape(b_, s_, B_HEADS, 2, B_HEAD_DIM)
    v = proj[..., 2 * D_MODEL:].reshape(b_, s_, B_HEADS, B_V_DIM).astype(jnp.float32)
    q = _partial_rope(_rms(q, q_norm), positions)
    k = _partial_rope(_rms(k, k_norm), positions)
    q = jnp.transpose(q, (0, 2, 3, 1, 4)) * (B_HEAD_DIM ** -0.5)
    k = jnp.transpose(k, (0, 2, 3, 1, 4))
    v = jnp.transpose(v, (0, 2, 1, 3))
    lam_init = 0.8 - 0.6 * math.exp(-0.3 * layer_idx)
    lam = (jnp.exp(jnp.sum(lam_q1.astype(jnp.float32) * lam_k1.astype(jnp.float32)))
           - jnp.exp(jnp.sum(lam_q2.astype(jnp.float32) * lam_k2.astype(jnp.float32))) + lam_init)
    outs = []
    for j in range(s_ // Q_BLOCK):
        s0 = j * Q_BLOCK
        e = s0 + Q_BLOCK
        sc = jnp.einsum('bhcqd,bhckd->bhcqk', q[:, :, :, s0:e], k[:, :, :, :e])
        mask = (s0 + jnp.arange(Q_BLOCK))[:, None] >= jnp.arange(e)[None, :]
        p = jax.nn.softmax(jnp.where(mask, sc, -jnp.inf), axis=-1)
        a = p[:, :, 0] - lam * p[:, :, 1]
        outs.append(jnp.einsum('bhqk,bhkv->bhqv', a, v[:, :, :e]))
    o = jnp.concatenate(outs, axis=2)
    o = _rms(o, o_norm) * (1.0 - lam_init)
    o = jnp.transpose(o, (0, 2, 1, 3)).reshape(b_, s_, D_MODEL)
    return o.astype(h.dtype) @ w_out.astype(h.dtype)


def _grouped_moe(h, w_router, router_bias, w_gu, w_down):
    b_, s_, d_ = h.shape
    n_tok = b_ * s_
    xt = h.reshape(n_tok, d_)
    scores = jax.nn.sigmoid((xt @ w_router.astype(h.dtype)).astype(jnp.float32))
    biased = (scores + router_bias.astype(jnp.float32)).reshape(n_tok, N_GROUPS, EXPERTS_PER_GROUP)
    grp_score = jnp.sum(lax.top_k(biased, 2)[0], axis=-1)
    g_sel = jnp.argmax(grp_score, axis=-1)
    in_grp = jnp.take_along_axis(biased, g_sel[:, None, None], axis=1)[:, 0]
    _, local_idx = lax.top_k(in_grp, TOP_K)
    expert_idx = g_sel[:, None].astype(jnp.int32) * EXPERTS_PER_GROUP + local_idx.astype(jnp.int32)
    gate_w = jnp.take_along_axis(scores, expert_idx, axis=1)
    gate_w = gate_w / jnp.sum(gate_w, axis=-1, keepdims=True)
    n_assign = n_tok * TOP_K
    n_blocks = (n_assign + N_EXPERTS * (MOE_BLOCK - 1) + MOE_BLOCK - 1) // MOE_BLOCK
    cap = n_blocks * MOE_BLOCK
    flat_e = expert_idx.reshape(n_assign)
    flat_tok = jnp.repeat(jnp.arange(n_tok, dtype=jnp.int32), TOP_K)
    order = jnp.argsort(flat_e, stable=True)
    sorted_e = flat_e[order]
    sorted_tok = flat_tok[order]
    sorted_w = gate_w.reshape(n_assign)[order]
    counts = jnp.bincount(flat_e, length=N_EXPERTS).astype(jnp.int32)
    starts = jnp.cumsum(counts) - counts
    padded = ((counts + MOE_BLOCK - 1) // MOE_BLOCK) * MOE_BLOCK
    pad_ends = jnp.cumsum(padded)
    pad_starts = pad_ends - padded
    dest = pad_starts[sorted_e] + (jnp.arange(n_assign, dtype=jnp.int32) - starts[sorted_e])
    buf_tok = jnp.full((cap,), n_tok, jnp.int32).at[dest].set(sorted_tok)
    x_pad = jnp.concatenate([xt, jnp.zeros((1, d_), xt.dtype)], axis=0)
    xb = x_pad[buf_tok].reshape(n_blocks, MOE_BLOCK, d_)
    block_e = jnp.minimum(jnp.searchsorted(pad_ends, jnp.arange(n_blocks, dtype=jnp.int32) * MOE_BLOCK, side='right'),
                          N_EXPERTS - 1)

    def expert_block(args):
        xblk, e = args
        gu = xblk @ w_gu[e].astype(xblk.dtype)
        act = jax.nn.silu(gu[:, :D_EXPERT]) * gu[:, D_EXPERT:]
        return act @ w_down[e].astype(xblk.dtype)

    yb = lax.map(expert_block, (xb, block_e)).reshape(cap, d_)
    contrib = yb[dest].astype(jnp.float32) * sorted_w[:, None]
    y = jax.ops.segment_sum(contrib, sorted_tok, num_segments=n_tok)
    return y.reshape(b_, s_, d_).astype(h.dtype)


def setup_inputs(seed: int = 0) -> dict:
    key = jax.random.key(seed)
    ks = jax.random.split(key, 24)
    f32 = jnp.float32
    nrm = lambda k, shape, s: jax.random.normal(k, shape, f32) * s
    offs = jax.random.randint(ks[2], (BATCH, 1), 0, 4096, dtype=jnp.int32)
    positions = offs + jnp.arange(SEQ, dtype=jnp.int32)[None, :]
    f_bias = jnp.linspace(3.0, 6.0, A_HEADS, dtype=f32)[None, :] + nrm(ks[8], (N_LAYERS_A, A_HEADS), 0.1)
    i_bias = nrm(ks[9], (N_LAYERS_A, A_HEADS), 0.1)
    return {
        "x": nrm(ks[0], (BATCH, SEQ, D_MODEL), 1.0),
        "c": nrm(ks[1], (BATCH, D_MODEL), 1.0),
        "positions": positions,
        "norm1": 1.0 + nrm(ks[3], (DEPTH, D_MODEL), 0.02),
        "norm2": 1.0 + nrm(ks[4], (DEPTH, D_MODEL), 0.02),
        "w_ada": nrm(ks[5], (DEPTH, D_MODEL, 6 * D_MODEL), 0.5 * D_MODEL ** -0.5),
        "b_ada": nrm(ks[6], (DEPTH, 6 * D_MODEL), 0.02),
        "a_w_in": nrm(ks[7], (N_LAYERS_A, D_MODEL, A_IN_COLS), D_MODEL ** -0.5),
        "a_b_if": jnp.concatenate([i_bias, f_bias], axis=-1),
        "a_h_norm": 1.0 + nrm(ks[10], (N_LAYERS_A, A_HV), 0.02),
        "a_w_out": nrm(ks[11], (N_LAYERS_A, A_HV, D_MODEL), A_HV ** -0.5),
        "b_w_in": nrm(ks[12], (N_LAYERS_B, D_MODEL, B_IN_COLS), D_MODEL ** -0.5),
        "b_q_norm": 1.0 + nrm(ks[13], (N_LAYERS_B, B_HEAD_DIM), 0.02),
        "b_k_norm": 1.0 + nrm(ks[14], (N_LAYERS_B, B_HEAD_DIM), 0.02),
        "b_lam_q1": nrm(ks[15], (N_LAYERS_B, B_HEAD_DIM), 0.1),
        "b_lam_k1": nrm(ks[16], (N_LAYERS_B, B_HEAD_DIM), 0.1),
        "b_lam_q2": nrm(ks[17], (N_LAYERS_B, B_HEAD_DIM), 0.1),
        "b_lam_k2": nrm(ks[18], (N_LAYERS_B, B_HEAD_DIM), 0.1),
        "b_o_norm": 1.0 + nrm(ks[19], (N_LAYERS_B, B_V_DIM), 0.02),
        "b_w_out": nrm(ks[20], (N_LAYERS_B, D_MODEL, D_MODEL), D_MODEL ** -0.5),
        "w_router": nrm(ks[21], (D_MODEL, N_EXPERTS), D_MODEL ** -0.5),
        "router_bias": nrm(ks[22], (N_EXPERTS,), 0.01),
        "moe_w_gu": nrm(ks[23], (DEPTH, N_EXPERTS, D_MODEL, 2 * D_EXPERT), D_MODEL ** -0.5),
        "moe_w_down": nrm(jax.random.fold_in(key, 99), (DEPTH, N_EXPERTS, D_EXPERT, D_MODEL), D_EXPERT ** -0.5),
    }


def reference(x, c, positions, norm1, norm2, w_ada, b_ada, a_w_in, a_b_if, a_h_norm, a_w_out,
              b_w_in, b_q_norm, b_k_norm, b_lam_q1, b_lam_k1, b_lam_q2, b_lam_k2, b_o_norm, b_w_out,
              w_router, router_bias, moe_w_gu, moe_w_down):
    c_act = jax.nn.silu(c.astype(jnp.float32))
    for l in range(DEPTH):
        mod = (c_act @ w_ada[l].astype(jnp.float32) + b_ada[l].astype(jnp.float32))[:, None, :]
        sh1, sc1, g1, sh2, sc2, g2 = jnp.split(mod, 6, axis=-1)
        h = (_rms(x, norm1[l]) * (1.0 + sc1) + sh1).astype(x.dtype)
        j = l // N_MIXERS
        if l % N_MIXERS == 0:
            y = _mlstm_mixer(h, a_w_in[j], a_b_if[j], a_h_norm[j], a_w_out[j])
        else:
            y = _diff_attention(h, positions, b_w_in[j], b_q_norm[j], b_k_norm[j], b_lam_q1[j], b_lam_k1[j],
                                b_lam_q2[j], b_lam_k2[j], b_o_norm[j], b_w_out[j], l)
        x = (x.astype(jnp.float32) + g1 * y.astype(jnp.float32)).astype(x.dtype)
        h = (_rms(x, norm2[l]) * (1.0 + sc2) + sh2).astype(x.dtype)
        y = _grouped_moe(h, w_router, router_bias, moe_w_gu[l], moe_w_down[l])
        x = (x.astype(jnp.float32) + g2 * y.astype(jnp.float32)).astype(x.dtype)
    return x
```

```python
import functools
import math

import jax
import jax.numpy as jnp
from jax import lax
from jax.experimental import pallas as pl
from jax.experimental.pallas import tpu as pltpu

D_MODEL = 1024
A_HEADS = 4
A_QK_DIM = 128
A_V_DIM = 256
A_CHUNK = 128
A_HQ = A_HEADS * A_QK_DIM
A_HV = A_HEADS * A_V_DIM
A_MAIN_COLS = 2 * A_HQ + 2 * A_HV

B_HEADS = 8
B_HEAD_DIM = 64
B_V_DIM = 128
B_ROT_DIM = 16
ROPE_THETA = 500000.0

N_EXPERTS = 16
N_GROUPS = 4
EXPERTS_PER_GROUP = 4
TOP_K = 2
D_EXPERT = 512
EPS = 1e-6

LANES = 128
SUBLANES = 8
ROW_TILES = D_MODEL // LANES
assert ROW_TILES == SUBLANES
VMEM_LIMIT = 48 * 1024 * 1024

F32 = jnp.float32
BF16 = jnp.bfloat16
HIGHEST = lax.Precision.HIGHEST


def _params(sem):
    return pltpu.CompilerParams(dimension_semantics=sem, vmem_limit_bytes=VMEM_LIMIT)


def _tile(pref, n):
    t = min(pref, n)
    assert n % t == 0, (pref, n)
    return t


def _ada_kernel(c_ref, w_ref, b_ref, o_ref):
    c = c_ref[...]
    c_act = c * jax.nn.sigmoid(c)
    o_ref[0] = jnp.dot(c_act, w_ref[0], precision=HIGHEST, preferred_element_type=F32) + b_ref[0]


def _ada_mod(c, w_ada, b_ada):
    depth, d, six_d = w_ada.shape
    bsz = c.shape[0]
    tn = _tile(1536, six_d)
    return pl.pallas_call(
        _ada_kernel,
        grid=(depth, six_d // tn),
        in_specs=[
            pl.BlockSpec((bsz, d), lambda l, j: (0, 0)),
            pl.BlockSpec((1, d, tn), lambda l, j: (l, 0, j)),
            pl.BlockSpec((1, 1, tn), lambda l, j: (l, 0, j)),
        ],
        out_specs=pl.BlockSpec((1, bsz, tn), lambda l, j: (l, 0, j)),
        out_shape=jax.ShapeDtypeStruct((depth, bsz, six_d), F32),
        compiler_params=_params(("parallel", "parallel")),
        name="ada_mod",
    )(c, w_ada, b_ada.reshape(depth, 1, six_d))


def _modulated_rms(x, g, sc1p, sh):
    y = x * lax.rsqrt(jnp.mean(x * x, axis=-1, keepdims=True) + EPS)
    return (y * g) * sc1p + sh


def _norm_mm_kernel(x_ref, g_ref, sc_ref, sh_ref, w_ref, *rest, col_chunk, has_gates):
    if has_gates:
        wg_ref, o_ref, og_ref = rest
    else:
        (o_ref,) = rest
    hb = _modulated_rms(x_ref[...], g_ref[...], sc_ref[0], sh_ref[0]).astype(BF16)
    for c0 in range(0, o_ref.shape[1], col_chunk):
        o_ref[:, c0:c0 + col_chunk] = jnp.dot(
            hb, w_ref[:, c0:c0 + col_chunk], preferred_element_type=F32).astype(o_ref.dtype)
    if has_gates:
        og_ref[...] = jnp.dot(hb, wg_ref[...], preferred_element_type=F32)


def _norm_matmul(x, gain, sc1p, sh, w, wg, seq):
    n, d = x.shape
    cols = w.shape[1]
    tm = _tile(512, seq)
    per_b = seq // tm
    has_gates = wg is not None
    in_specs = [
        pl.BlockSpec((tm, d), lambda i: (i, 0)),
        pl.BlockSpec((1, d), lambda i: (0, 0)),
        pl.BlockSpec((1, 1, d), lambda i: (i // per_b, 0, 0)),
        pl.BlockSpec((1, 1, d), lambda i: (i // per_b, 0, 0)),
        pl.BlockSpec((d, cols), lambda i: (0, 0)),
    ]
    out_specs = [pl.BlockSpec((tm, cols), lambda i: (i, 0))]
    out_shape = [jax.ShapeDtypeStruct((n, cols), BF16)]
    args = [x, gain, sc1p, sh, w]
    if has_gates:
        in_specs.append(pl.BlockSpec((d, LANES), lambda i: (0, 0)))
        out_specs.append(pl.BlockSpec((tm, LANES), lambda i: (i, 0)))
        out_shape.append(jax.ShapeDtypeStruct((n, LANES), F32))
        args.append(wg)
    return pl.pallas_call(
        functools.partial(_norm_mm_kernel, col_chunk=512, has_gates=has_gates),
        grid=(n // tm,),
        in_specs=in_specs,
        out_specs=out_specs,
        out_shape=out_shape,
        compiler_params=_params(("parallel",)),
        name="norm_inproj",
    )(*args)


def _log_sigmoid(x):
    return jnp.minimum(x, 0.0) - jnp.log1p(jnp.exp(-jnp.abs(x)))


def _lane_cumsum(x):
    lane = lax.broadcasted_iota(jnp.int32, x.shape, 1)
    sh = 1
    while sh < x.shape[1]:
        x = x + jnp.where(lane >= sh, pltpu.roll(x, sh, axis=1), 0.0)
        sh *= 2
    return x


def _mlstm_kernel(p_ref, g_ref, bias_ref, hn_ref, o_ref, c_ref, n_ref, m_ref, r_ref):
    L = A_CHUNK

    @pl.when(pl.program_id(1) == 0)
    def _():
        c_ref[...] = jnp.zeros_like(c_ref)
        n_ref[...] = jnp.zeros_like(n_ref)
        m_ref[...] = jnp.zeros_like(m_ref)
        r_ref[...] = jnp.zeros_like(r_ref)

    gates = g_ref[...] + bias_ref[...]
    g8 = gates.T[0:SUBLANES]
    bc8 = _lane_cumsum(_log_sigmoid(g8))
    b_rows, i_rows, lw_rows, b_lasts = [], [], [], []
    for h in range(A_HEADS):
        b_row = bc8[A_HEADS + h:A_HEADS + h + 1]
        i_row = g8[h:h + 1]
        b_last = b_row[:, L - 1:L]
        lw_row = b_last - b_row + i_row
        r_ref[h:h + 1, :] = b_row
        r_ref[A_HEADS + h:A_HEADS + h + 1, :] = lw_row
        b_rows.append(b_row)
        i_rows.append(i_row)
        lw_rows.append(lw_row)
        b_lasts.append(b_last)
    cols = r_ref[...].T

    row_id = lax.broadcasted_iota(jnp.int32, (L, L), 0)
    col_id = lax.broadcasted_iota(jnp.int32, (L, L), 1)
    causal = row_id >= col_id

    for h in range(A_HEADS):
        q = p_ref[:, h * A_QK_DIM:(h + 1) * A_QK_DIM]
        ksf = p_ref[:, A_HQ + h * A_QK_DIM:A_HQ + (h + 1) * A_QK_DIM].astype(F32) * (A_QK_DIM ** -0.5)
        v = p_ref[:, 2 * A_HQ + h * A_V_DIM:2 * A_HQ + (h + 1) * A_V_DIM]
        og = p_ref[:, 2 * A_HQ + A_HV + h * A_V_DIM:2 * A_HQ + A_HV + (h + 1) * A_V_DIM].astype(F32)
        b_col = cols[:, h:h + 1]
        lw_col = cols[:, A_HEADS + h:A_HEADS + h + 1]
        m11 = m_ref[h][:, 0:1]
        c_old = c_ref[h]
        n_old = n_ref[h]

        log_d = jnp.where(causal, b_col - b_rows[h] + i_rows[h], -jnp.inf)
        log_inter = b_col + m11
        m_t = jnp.maximum(jnp.max(log_d, axis=1, keepdims=True), log_inter)
        dmat = jnp.exp(log_d - m_t)
        inter = jnp.exp(log_inter - m_t)
        s = lax.dot_general(q, ksf.astype(BF16), (((1,), (1,)), ((), ())), preferred_element_type=F32) * dmat
        num = (jnp.dot(s.astype(BF16), v, preferred_element_type=F32)
               + inter * jnp.dot(q, c_old.astype(BF16), preferred_element_type=F32))
        den = (jnp.sum(s, axis=1, keepdims=True)
               + inter * jnp.sum(q.astype(F32) * n_old, axis=1, keepdims=True))
        hh = num / jnp.maximum(jnp.abs(den), jnp.exp(-m_t))

        m_new = jnp.maximum(b_lasts[h] + m11, jnp.max(lw_rows[h], axis=1, keepdims=True))
        w_col = jnp.exp(lw_col - m_new)
        decay = jnp.exp(b_lasts[h] + m11 - m_new)
        kw = ksf * w_col
        c_ref[h] = decay * c_old + lax.dot_general(
            kw.astype(BF16), v, (((0,), (0,)), ((), ())), preferred_element_type=F32)
        n_ref[h] = decay * n_old + jnp.sum(kw, axis=0, keepdims=True)
        m_ref[h] = jnp.broadcast_to(m_new, (1, LANES))

        hn = hh * lax.rsqrt(jnp.mean(hh * hh, axis=1, keepdims=True) + EPS)
        hn = hn * hn_ref[:, h * A_V_DIM:(h + 1) * A_V_DIM]
        o_ref[:, h * A_V_DIM:(h + 1) * A_V_DIM] = (hn * jax.nn.sigmoid(og)).astype(o_ref.dtype)


def _mlstm(proj, gates, bias_row, h_norm_row, bsz, seq):
    n = proj.shape[0]
    nc = seq // A_CHUNK
    return pl.pallas_call(
        _mlstm_kernel,
        grid=(bsz, nc),
        in_specs=[
            pl.BlockSpec((A_CHUNK, A_MAIN_COLS), lambda b, c: (b * nc + c, 0)),
            pl.BlockSpec((A_CHUNK, LANES), lambda b, c: (b * nc + c, 0)),
            pl.BlockSpec((1, LANES), lambda b, c: (0, 0)),
            pl.BlockSpec((1, A_HV), lambda b, c: (0, 0)),
        ],
        out_specs=pl.BlockSpec((A_CHUNK, A_HV), lambda b, c: (b * nc + c, 0)),
        out_shape=jax.ShapeDtypeStruct((n, A_HV), BF16),
        scratch_shapes=[
            pltpu.VMEM((A_HEADS, A_QK_DIM, A_V_DIM), F32),
            pltpu.VMEM((A_HEADS, 1, A_QK_DIM), F32),
            pltpu.VMEM((A_HEADS, 1, LANES), F32),
            pltpu.VMEM((A_CHUNK, A_CHUNK), F32),
        ],
        compiler_params=_params(("parallel", "arbitrary")),
        name="mlstm",
    )(proj, gates, bias_row, h_norm_row)


def _qk_prep_kernel(p_ref, pos_ref, seg_ref, segt_ref, gain_ref, freq_ref, o_ref):
    tm = p_ref.shape[0]
    x = p_ref[...].astype(F32)
    xx = x * x
    xx_hi = xx.astype(BF16)
    xx_lo = (xx - xx_hi.astype(F32)).astype(BF16)
    seg = seg_ref[...]
    ss = jnp.dot(xx_hi, seg, preferred_element_type=F32) + jnp.dot(xx_lo, seg, preferred_element_type=F32)
    r = lax.rsqrt(ss * (1.0 / B_HEAD_DIM) + EPS)
    r_hi = r.astype(BF16)
    r_lo = (r - r_hi.astype(F32)).astype(BF16)
    segt = segt_ref[...]
    r_full = jnp.dot(r_hi, segt, preferred_element_type=F32) + jnp.dot(r_lo, segt, preferred_element_type=F32)

    pos = jnp.broadcast_to(pos_ref[0].astype(F32), (LANES, tm)).T
    ang = pos * freq_ref[0:1, :]
    rot_lo = freq_ref[1:2, :]
    rot_hi = freq_ref[2:3, :]
    cosv = jnp.cos(ang)
    sinv = jnp.sin(ang)
    c_mul = jnp.where(rot_lo + rot_hi > 0.0, cosv, 1.0)
    s_lo = -sinv * rot_lo
    s_hi = sinv * rot_hi
    half = B_ROT_DIM // 2
    for j in range(2 * D_MODEL // LANES):
        sl = slice(j * LANES, (j + 1) * LANES)
        y = x[:, sl] * r_full[:, sl] * gain_ref[:, sl]
        out = y * c_mul + pltpu.roll(y, LANES - half, axis=1) * s_lo + pltpu.roll(y, half, axis=1) * s_hi
        if j < D_MODEL // LANES:
            out = out * (B_HEAD_DIM ** -0.5)
        o_ref[:, sl] = out.astype(o_ref.dtype)


def _qk_prep(proj, positions_row, q_norm, k_norm, seq):
    n = proj.shape[0]
    two_d = 2 * D_MODEL
    tm = _tile(256, seq)
    n_seg = two_d // B_HEAD_DIM
    seg = (jnp.arange(two_d)[:, None] // B_HEAD_DIM == jnp.arange(LANES)[None, :]).astype(BF16)
    gain = jnp.concatenate([jnp.tile(q_norm, D_MODEL // B_HEAD_DIM), jnp.tile(k_norm, D_MODEL // B_HEAD_DIM)])
    lane = jnp.arange(LANES)
    in_seg = lane % B_HEAD_DIM
    inv_freq = ROPE_THETA ** (-jnp.arange(0, B_ROT_DIM, 2, dtype=F32) / B_ROT_DIM)
    half = B_ROT_DIM // 2
    freq = jnp.where(in_seg < B_ROT_DIM, inv_freq[in_seg % half], 0.0)
    freq_tab = jnp.zeros((SUBLANES, LANES), F32)
    freq_tab = freq_tab.at[0].set(freq).at[1].set((in_seg < half).astype(F32))
    freq_tab = freq_tab.at[2].set(((in_seg >= half) & (in_seg < B_ROT_DIM)).astype(F32))
    del n_seg
    return pl.pallas_call(
        _qk_prep_kernel,
        grid=(n // tm,),
        in_specs=[
            pl.BlockSpec((tm, two_d), lambda i: (i, 0)),
            pl.BlockSpec((1, 1, tm), lambda i: (i, 0, 0)),
            pl.BlockSpec((two_d, LANES), lambda i: (0, 0)),
            pl.BlockSpec((LANES, two_d), lambda i: (0, 0)),
            pl.BlockSpec((1, two_d), lambda i: (0, 0)),
            pl.BlockSpec((SUBLANES, LANES), lambda i: (0, 0)),
        ],
        out_specs=pl.BlockSpec((tm, two_d), lambda i: (i, 0)),
        out_shape=jax.ShapeDtypeStruct((n, two_d), BF16),
        compiler_params=_params(("parallel",)),
        name="qk_prep",
    )(proj, positions_row.reshape(n // tm, 1, tm), seg, seg.T, gain.reshape(1, two_d), freq_tab)


def _diff_attn_kernel(q_ref, k_ref, v_ref, lam_ref, on_ref, o_ref, *, tq, lam_init):
    seq = q_ref.shape[0]
    nq = seq // tq
    lamv = lam_ref[...]
    lam = (jnp.exp(jnp.sum(lamv[0:1] * lamv[1:2], axis=1, keepdims=True))
           - jnp.exp(jnp.sum(lamv[2:3] * lamv[3:4], axis=1, keepdims=True)) + lam_init)
    lane = lax.broadcasted_iota(jnp.int32, (tq, LANES), 1)
    first_map = lane < B_HEAD_DIM
    row_id = lax.broadcasted_iota(jnp.int32, (tq, tq), 0)
    col_id = lax.broadcasted_iota(jnp.int32, (tq, tq), 1)
    causal = row_id >= col_id
    nt = (((1,), (1,)), ((), ()))

    def tile_update(carry, qa, qb, k, v, mask):
        out = []
        for (m_old, l_old, acc), qm in zip(carry, (qa, qb)):
            s = lax.dot_general(qm, k, nt, preferred_element_type=F32)
            if mask is not None:
                s = jnp.where(mask, s, -jnp.inf)
            m_new = jnp.maximum(m_old, jnp.max(s, axis=1, keepdims=True))
            alpha = jnp.exp(m_old - m_new)
            p = jnp.exp(s - m_new)
            l_new = alpha * l_old + jnp.sum(p, axis=1, keepdims=True)
            acc_new = alpha * acc + jnp.dot(p.astype(BF16), v, preferred_element_type=F32)
            out.append((m_new, l_new, acc_new))
        return tuple(out)

    for qi in range(nq):
        q = q_ref[qi * tq:(qi + 1) * tq, :]
        qa = jnp.where(first_map, q, jnp.zeros_like(q))
        qb = jnp.where(first_map, jnp.zeros_like(q), q)
        init = tuple((jnp.full((tq, 1), -jnp.inf, F32), jnp.zeros((tq, 1), F32), jnp.zeros((tq, B_V_DIM), F32))
                     for _ in range(2))

        def body(ki, carry, qa=qa, qb=qb):
            start = pl.multiple_of(ki * tq, tq)
            return tile_update(carry, qa, qb, k_ref[pl.ds(start, tq), :], v_ref[pl.ds(start, tq), :], None)

        carry = lax.fori_loop(0, qi, body, init) if qi > 0 else init
        carry = tile_update(carry, qa, qb, k_ref[qi * tq:(qi + 1) * tq, :], v_ref[qi * tq:(qi + 1) * tq, :], causal)
        (_, l1, a1), (_, l2, a2) = carry
        o = a1 / l1 - lam * (a2 / l2)
        o = o * lax.rsqrt(jnp.mean(o * o, axis=1, keepdims=True) + EPS) * on_ref[...] * (1.0 - lam_init)
        o_ref[qi * tq:(qi + 1) * tq, :] = o.astype(o_ref.dtype)


def _diff_attn(qk, proj, lam_tab, o_norm_row, bsz, seq, lam_init):
    n = qk.shape[0]
    tq = _tile(256, seq)
    return pl.pallas_call(
        functools.partial(_diff_attn_kernel, tq=tq, lam_init=lam_init),
        grid=(bsz, B_HEADS),
        in_specs=[
            pl.BlockSpec((seq, LANES), lambda b, h: (b, h)),
            pl.BlockSpec((seq, LANES), lambda b, h: (b, B_HEADS + h)),
            pl.BlockSpec((seq, B_V_DIM), lambda b, h: (b, 2 * B_HEADS + h)),
            pl.BlockSpec((SUBLANES, LANES), lambda b, h: (0, 0)),
            pl.BlockSpec((1, B_V_DIM), lambda b, h: (0, 0)),
        ],
        out_specs=pl.BlockSpec((seq, B_V_DIM), lambda b, h: (b, h)),
        out_shape=jax.ShapeDtypeStruct((n, D_MODEL), BF16),
        compiler_params=_params(("parallel", "parallel")),
        name="diff_attn",
    )(qk, qk, proj, lam_tab, o_norm_row)


def _store_token_tiles(dst_ref, val):
    rows = val.shape[0]
    for j in range(ROW_TILES):
        dst_ref[pl.ds(j, rows, stride=ROW_TILES), :] = val[:, j * LANES:(j + 1) * LANES]


def _load_token_tiles(src_ref, rows):
    return jnp.concatenate([src_ref[pl.ds(j, rows, stride=ROW_TILES), :] for j in range(ROW_TILES)], axis=1)


def _outproj_kernel(a_ref, w_ref, x_ref, g1_ref, gain_ref, sc_ref, sh_ref, wr_ref, xo_ref, h2_ref, lg_ref):
    y = jnp.dot(a_ref[...], w_ref[...], preferred_element_type=F32)
    xm = x_ref[...] + g1_ref[0] * y
    xo_ref[...] = xm
    h2 = _modulated_rms(xm, gain_ref[...], sc_ref[0], sh_ref[0])
    _store_token_tiles(h2_ref, h2)
    lg_ref[...] = lax.dot_general(wr_ref[...], h2, (((1,), (1,)), ((), ())),
                                  precision=HIGHEST, preferred_element_type=F32)


def _outproj_norm_router(a, w, x, g1, gain, sc1p, sh, w_router_t, seq):
    n, d = x.shape
    tm = _tile(512, seq)
    per_b = seq // tm
    bmap = lambda i: (i // per_b, 0, 0)
    return pl.pallas_call(
        _outproj_kernel,
        grid=(n // tm,),
        in_specs=[
            pl.BlockSpec((tm, a.shape[1]), lambda i: (i, 0)),
            pl.BlockSpec(w.shape, lambda i: (0, 0)),
            pl.BlockSpec((tm, d), lambda i: (i, 0)),
            pl.BlockSpec((1, 1, d), bmap),
            pl.BlockSpec((1, d), lambda i: (0, 0)),
            pl.BlockSpec((1, 1, d), bmap),
            pl.BlockSpec((1, 1, d), bmap),
            pl.BlockSpec((N_EXPERTS, d), lambda i: (0, 0)),
        ],
        out_specs=[
            pl.BlockSpec((tm, d), lambda i: (i, 0)),
            pl.BlockSpec((tm * ROW_TILES, LANES), lambda i: (i, 0)),
            pl.BlockSpec((N_EXPERTS, tm), lambda i: (0, i)),
        ],
        out_shape=[
            jax.ShapeDtypeStruct((n, d), F32),
            jax.ShapeDtypeStruct((n * ROW_TILES, LANES), F32),
            jax.ShapeDtypeStruct((N_EXPERTS, n), F32),
        ],
        compiler_params=_params(("parallel",)),
        name="outproj_norm_router",
    )(a, w, x, g1, gain, sc1p, sh, w_router_t)


def _route_kernel(lg_ref, bias_ref, oi_ref, ow_ref, cnt_ref, carry_ref, tri_ref):
    tr = lg_ref.shape[1]
    step = pl.program_id(0)

    @pl.when(step == 0)
    def _():
        carry_ref[...] = jnp.zeros_like(carry_ref)
        r = lax.broadcasted_iota(jnp.int32, (tr, tr), 0)
        c = lax.broadcasted_iota(jnp.int32, (tr, tr), 1)
        tri_ref[...] = jnp.where(r < c, 1.0, 0.0).astype(BF16)

    scores = jax.nn.sigmoid(lg_ref[...])
    biased = scores + bias_ref[...]
    rows = [biased[e:e + 1, :] for e in range(N_EXPERTS)]
    srows = [scores[e:e + 1, :] for e in range(N_EXPERTS)]

    def top2_sum(a, b, c, d):
        m1, n1 = jnp.maximum(a, b), jnp.minimum(a, b)
        m2, n2 = jnp.maximum(c, d), jnp.minimum(c, d)
        return jnp.maximum(m1, m2) + jnp.maximum(jnp.minimum(m1, m2), jnp.maximum(n1, n2))

    gscore = [top2_sum(*rows[g * EXPERTS_PER_GROUP:(g + 1) * EXPERTS_PER_GROUP]) for g in range(N_GROUPS)]
    best = gscore[0]
    gsel = jnp.zeros_like(best, dtype=jnp.int32)
    for g in range(1, N_GROUPS):
        upd = gscore[g] > best
        gsel = jnp.where(upd, g, gsel)
        best = jnp.where(upd, gscore[g], best)

    def pick(table, j):
        out = table[j]
        for g in range(1, N_GROUPS):
            out = jnp.where(gsel == g, table[g * EXPERTS_PER_GROUP + j], out)
        return out

    in_b = [pick(rows, j) for j in range(EXPERTS_PER_GROUP)]
    in_s = [pick(srows, j) for j in range(EXPERTS_PER_GROUP)]
    v1, i1, s1 = in_b[0], jnp.zeros_like(gsel), in_s[0]
    for j in range(1, EXPERTS_PER_GROUP):
        upd = in_b[j] > v1
        v1 = jnp.where(upd, in_b[j], v1)
        i1 = jnp.where(upd, j, i1)
        s1 = jnp.where(upd, in_s[j], s1)
    v2 = jnp.full_like(v1, -jnp.inf)
    i2 = jnp.zeros_like(gsel)
    s2 = jnp.zeros_like(s1)
    for j in range(EXPERTS_PER_GROUP):
        upd = (i1 != j) & (in_b[j] > v2)
        v2 = jnp.where(upd, in_b[j], v2)
        i2 = jnp.where(upd, j, i2)
        s2 = jnp.where(upd, in_s[j], s2)
    e0 = gsel * EXPERTS_PER_GROUP + i1
    e1 = gsel * EXPERTS_PER_GROUP + i2
    tot = s1 + s2
    ow_ref[0:1, :] = s1 / tot
    ow_ref[1:2, :] = s2 / tot

    eid = lax.broadcasted_iota(jnp.int32, (N_EXPERTS, tr), 0)
    oh0 = jnp.where(eid == e0, 1.0, 0.0)
    oh1 = jnp.where(eid == e1, 1.0, 0.0)
    cnt = oh0 + oh1
    before = jnp.dot(cnt.astype(BF16), tri_ref[...], preferred_element_type=F32) + carry_ref[...]
    oi_ref[0:1, :] = e0
    oi_ref[1:2, :] = e1
    oi_ref[2:3, :] = jnp.sum(oh0 * before, axis=0, keepdims=True).astype(jnp.int32)
    oi_ref[3:4, :] = jnp.sum(oh1 * before, axis=0, keepdims=True).astype(jnp.int32)
    new_carry = carry_ref[...] + jnp.sum(cnt, axis=1, keepdims=True)
    carry_ref[...] = new_carry
    cnt_ref[...] = new_carry.astype(jnp.int32)


def _route(logits_t, router_bias):
    n = logits_t.shape[1]
    tr = _tile(512, n)
    return pl.pallas_call(
        _route_kernel,
        grid=(n // tr,),
        in_specs=[
            pl.BlockSpec((N_EXPERTS, tr), lambda i: (0, i)),
            pl.BlockSpec((N_EXPERTS, 1), lambda i: (0, 0)),
        ],
        out_specs=[
            pl.BlockSpec((4, tr), lambda i: (0, i)),
            pl.BlockSpec((2, tr), lambda i: (0, i)),
            pl.BlockSpec((N_EXPERTS, 1), lambda i: (0, 0)),
        ],
        out_shape=[
            jax.ShapeDtypeStruct((4, n), jnp.int32),
            jax.ShapeDtypeStruct((2, n), F32),
            jax.ShapeDtypeStruct((N_EXPERTS, 1), jnp.int32),
        ],
        scratch_shapes=[pltpu.VMEM((N_EXPERTS, 1), F32), pltpu.VMEM((tr, tr), BF16)],
        compiler_params=_params(("arbitrary",)),
        name="route",
    )(logits_t, router_bias.reshape(N_EXPERTS, 1).astype(F32))


def _row_copy(src_hbm, src_row, dst_hbm, dst_row, sem):
    return pltpu.make_async_copy(
        src_hbm.at[pl.ds(pl.multiple_of(src_row * ROW_TILES, ROW_TILES), ROW_TILES)],
        dst_hbm.at[pl.ds(pl.multiple_of(dst_row * ROW_TILES, ROW_TILES), ROW_TILES)],
        sem)


def _dispatch_kernel(dest_ref, h_hbm, xb_in_hbm, xb_hbm, sem):
    del xb_in_hbm
    td = dest_ref.shape[1]
    base = pl.program_id(0) * td

    def issue(t, c):
        for k in range(TOP_K):
            _row_copy(h_hbm, base + t, xb_hbm, dest_ref[k, t], sem).start()
        return c

    lax.fori_loop(0, td, issue, 0)

    def drain(t, c):
        for k in range(TOP_K):
            _row_copy(h_hbm, base + t, xb_hbm, dest_ref[k, t], sem).wait()
        return c

    lax.fori_loop(0, td, drain, 0)


def _dispatch(dest, h_tiles, cap):
    n = dest.shape[1]
    td = _tile(1024, n)
    xb0 = jnp.zeros((cap * ROW_TILES, LANES), F32)
    return pl.pallas_call(
        _dispatch_kernel,
        grid=(n // td,),
        in_specs=[
            pl.BlockSpec((TOP_K, td), lambda i: (0, i), memory_space=pltpu.SMEM),
            pl.BlockSpec(memory_space=pl.ANY),
            pl.BlockSpec(memory_space=pl.ANY),
        ],
        out_specs=pl.BlockSpec(memory_space=pl.ANY),
        out_shape=jax.ShapeDtypeStruct(xb0.shape, F32),
        scratch_shapes=[pltpu.SemaphoreType.DMA],
        input_output_aliases={2: 0},
        compiler_params=_params(("arbitrary",)),
        name="moe_dispatch",
    )(dest, h_tiles, xb0)


def _expert_kernel(be_ref, nb_ref, x_ref, wgu_ref, wd_ref, y_ref):
    tb = x_ref.shape[0] // ROW_TILES

    @pl.when(pl.program_id(0) < nb_ref[0])
    def _():
        x = _load_token_tiles(x_ref, tb).astype(BF16)
        gu = jnp.dot(x, wgu_ref[0], preferred_element_type=F32)
        gate = gu[:, :D_EXPERT]
        act = gate * jax.nn.sigmoid(gate) * gu[:, D_EXPERT:]
        y = jnp.dot(act.astype(BF16), wd_ref[0], preferred_element_type=F32)
        _store_token_tiles(y_ref, y)

    @pl.when(pl.program_id(0) >= nb_ref[0])
    def _():
        y_ref[...] = jnp.zeros_like(y_ref)


def _experts(block_e, n_used, xb, w_gu, w_down, tb):
    rows = xb.shape[0]
    n_blocks = rows // (tb * ROW_TILES)
    d, two_f = w_gu.shape[1:]
    grid_spec = pltpu.PrefetchScalarGridSpec(
        num_scalar_prefetch=2,
        grid=(n_blocks,),
        in_specs=[
            pl.BlockSpec((tb * ROW_TILES, LANES), lambda i, be, nb: (i, 0)),
            pl.BlockSpec((1, d, two_f), lambda i, be, nb: (be[i], 0, 0)),
            pl.BlockSpec((1, two_f // 2, d), lambda i, be, nb: (be[i], 0, 0)),
        ],
        out_specs=pl.BlockSpec((tb * ROW_TILES, LANES), lambda i, be, nb: (i, 0)),
    )
    return pl.pallas_call(
        _expert_kernel,
        grid_spec=grid_spec,
        out_shape=jax.ShapeDtypeStruct(xb.shape, F32),
        compiler_params=_params(("arbitrary",)),
        name="moe_experts",
    )(block_e, n_used, xb, w_gu, w_down)


def _combine_kernel(dest_ref, yb_hbm, x_ref, w_ref, g2_ref, o_ref, buf0, buf1, sem):
    tc = x_ref.shape[0]
    bufs = (buf0, buf1)

    def copy(t, k):
        return pltpu.make_async_copy(
            yb_hbm.at[pl.ds(pl.multiple_of(dest_ref[k, t] * ROW_TILES, ROW_TILES), ROW_TILES)],
            bufs[k].at[pl.ds(pl.multiple_of(t * ROW_TILES, ROW_TILES), ROW_TILES)],
            sem)

    def issue(t, c):
        for k in range(TOP_K):
            copy(t, k).start()
        return c

    lax.fori_loop(0, tc, issue, 0)

    def drain(t, c):
        for k in range(TOP_K):
            copy(t, k).wait()
        return c

    lax.fori_loop(0, tc, drain, 0)

    wt = jnp.broadcast_to(w_ref[0:1, :], (LANES, tc)).T
    wt1 = jnp.broadcast_to(w_ref[1:2, :], (LANES, tc)).T
    g2 = g2_ref[0]
    for j in range(ROW_TILES):
        sl = slice(j * LANES, (j + 1) * LANES)
        y = (buf0[pl.ds(j, tc, stride=ROW_TILES), :] * wt + buf1[pl.ds(j, tc, stride=ROW_TILES), :] * wt1)
        o_ref[:, sl] = x_ref[:, sl] + g2[:, sl] * y


def _combine(dest, yb, x_mid, gate_w, g2, seq):
    n, d = x_mid.shape
    tc = _tile(256, seq)
    per_b = seq // tc
    return pl.pallas_call(
        _combine_kernel,
        grid=(n // tc,),
        in_specs=[
            pl.BlockSpec((TOP_K, tc), lambda i: (0, i), memory_space=pltpu.SMEM),
            pl.BlockSpec(memory_space=pl.ANY),
            pl.BlockSpec((tc, d), lambda i: (i, 0)),
            pl.BlockSpec((TOP_K, tc), lambda i: (0, i)),
            pl.BlockSpec((1, 1, d), lambda i: (i // per_b, 0, 0)),
        ],
        out_specs=pl.BlockSpec((tc, d), lambda i: (i, 0)),
        out_shape=jax.ShapeDtypeStruct((n, d), F32),
        scratch_shapes=[
            pltpu.VMEM((tc * ROW_TILES, LANES), F32),
            pltpu.VMEM((tc * ROW_TILES, LANES), F32),
            pltpu.SemaphoreType.DMA,
        ],
        compiler_params=_params(("arbitrary",)),
        name="moe_combine",
    )(dest, yb, x_mid, gate_w, g2)


MOE_ROWS = 256


def _moe(x_mid, h_tiles, logits_t, router_bias, w_gu, w_down, g2, seq):
    n = x_mid.shape[0]
    n_assign = n * TOP_K
    n_blocks = (n_assign + N_EXPERTS * (MOE_ROWS - 1) + MOE_ROWS - 1) // MOE_ROWS
    cap = n_blocks * MOE_ROWS
    route_i, gate_w, counts = _route(logits_t, router_bias)
    counts = counts[:, 0]
    padded = ((counts + MOE_ROWS - 1) // MOE_ROWS) * MOE_ROWS
    pad_ends = jnp.cumsum(padded)
    pad_starts = pad_ends - padded
    dest = pad_starts[route_i[0:2]] + route_i[2:4]
    blk_start = jnp.arange(n_blocks, dtype=jnp.int32) * MOE_ROWS
    block_e = jnp.minimum(jnp.sum(blk_start[:, None] >= pad_ends[None, :], axis=1), N_EXPERTS - 1).astype(jnp.int32)
    n_used = (pad_ends[-1:] // MOE_ROWS).astype(jnp.int32)
    block_e = jnp.where(jnp.arange(n_blocks) < n_used[0], block_e, block_e[jnp.maximum(n_used[0] - 1, 0)])
    xb = _dispatch(dest, h_tiles, cap)
    yb = _experts(block_e, n_used, xb, w_gu, w_down, MOE_ROWS)
    return _combine(dest, yb, x_mid, gate_w, g2, seq)


def kernel(x, c, positions, norm1, norm2, w_ada, b_ada, a_w_in, a_b_if, a_h_norm, a_w_out, b_w_in, b_q_norm, b_k_norm, b_lam_q1, b_lam_k1, b_lam_q2, b_lam_k2, b_o_norm, b_w_out, w_router, router_bias, moe_w_gu, moe_w_down):
    bsz, seq, d = x.shape
    depth = w_ada.shape[0]
    n = bsz * seq
    xf = x.reshape(n, d)
    mod = _ada_mod(c, w_ada, b_ada)
    w_router_t = w_router.T
    pos_row = positions.reshape(n)

    for l in range(depth):
        sh1, sc1, g1, sh2, sc2, g2 = [mod[l, :, i * d:(i + 1) * d].reshape(bsz, 1, d) for i in range(6)]
        j = l // 2
        if l % 2 == 0:
            w_in = a_w_in[j]
            w_main = w_in[:, :A_MAIN_COLS].astype(BF16)
            w_gate = jnp.pad(w_in[:, A_MAIN_COLS:], ((0, 0), (0, LANES - 2 * A_HEADS))).astype(BF16)
            proj, gates = _norm_matmul(xf, norm1[l].reshape(1, d), 1.0 + sc1, sh1, w_main, w_gate, seq)
            bias_row = jnp.pad(a_b_if[j], (0, LANES - 2 * A_HEADS)).reshape(1, LANES)
            mixed = _mlstm(proj, gates, bias_row, a_h_norm[j].reshape(1, A_HV), bsz, seq)
            w_out = a_w_out[j].astype(BF16)
        else:
            (proj,) = _norm_matmul(xf, norm1[l].reshape(1, d), 1.0 + sc1, sh1, b_w_in[j].astype(BF16), None, seq)
            qk = _qk_prep(proj, pos_row, b_q_norm[j], b_k_norm[j], seq)
            lam_tab = jnp.zeros((SUBLANES, LANES), F32)
            for r, v in enumerate((b_lam_q1[j], b_lam_k1[j], b_lam_q2[j], b_lam_k2[j])):
                lam_tab = lam_tab.at[r, :B_HEAD_DIM].set(v)
            lam_init = 0.8 - 0.6 * math.exp(-0.3 * l)
            mixed = _diff_attn(qk, proj, lam_tab, b_o_norm[j].reshape(1, B_V_DIM), bsz, seq, lam_init)
            w_out = b_w_out[j].astype(BF16)
        x_mid, h_tiles, logits_t = _outproj_norm_router(
            mixed, w_out, xf, g1, norm2[l].reshape(1, d), 1.0 + sc2, sh2, w_router_t, seq)
        xf = _moe(x_mid, h_tiles, logits_t, router_bias,
                  moe_w_gu[l].astype(BF16), moe_w_down[l].astype(BF16), g2, seq)
    return xf.reshape(bsz, seq, d)
```

```python
import functools
import math

import jax
import jax.numpy as jnp
from jax import lax
from jax.experimental import pallas as pl
from jax.experimental.pallas import tpu as pltpu

D_MODEL = 1024
A_HEADS = 4
A_QK_DIM = 128
A_V_DIM = 256
A_CHUNK = 128
A_HQ = A_HEADS * A_QK_DIM
A_HV = A_HEADS * A_V_DIM
A_MAIN_COLS = 2 * A_HQ + 2 * A_HV

B_HEADS = 8
B_HEAD_DIM = 64
B_V_DIM = 128
B_ROT_DIM = 16
ROPE_THETA = 500000.0

N_EXPERTS = 16
N_GROUPS = 4
EXPERTS_PER_GROUP = 4
TOP_K = 2
D_EXPERT = 512
EPS = 1e-6

LANES = 128
SUBLANES = 8
ROW_TILES = D_MODEL // LANES
assert ROW_TILES == SUBLANES
VMEM_LIMIT = 48 * 1024 * 1024

F32 = jnp.float32
BF16 = jnp.bfloat16
HIGHEST = lax.Precision.HIGHEST


def _params(sem):
    return pltpu.CompilerParams(dimension_semantics=sem, vmem_limit_bytes=VMEM_LIMIT)


def _tile(pref, n):
    t = min(pref, n)
    assert n % t == 0, (pref, n)
    return t


def _ada_kernel(c_ref, w_ref, b_ref, o_ref):
    c = c_ref[...]
    c_act = c * jax.nn.sigmoid(c)
    o_ref[0] = jnp.dot(c_act, w_ref[0], precision=HIGHEST, preferred_element_type=F32) + b_ref[0]


def _ada_mod(c, w_ada, b_ada):
    depth, d, six_d = w_ada.shape
    bsz = c.shape[0]
    tn = _tile(1536, six_d)
    return pl.pallas_call(
        _ada_kernel,
        grid=(depth, six_d // tn),
        in_specs=[
            pl.BlockSpec((bsz, d), lambda l, j: (0, 0)),
            pl.BlockSpec((1, d, tn), lambda l, j: (l, 0, j)),
            pl.BlockSpec((1, 1, tn), lambda l, j: (l, 0, j)),
        ],
        out_specs=pl.BlockSpec((1, bsz, tn), lambda l, j: (l, 0, j)),
        out_shape=jax.ShapeDtypeStruct((depth, bsz, six_d), F32),
        compiler_params=_params(("parallel", "parallel")),
        name="ada_mod",
    )(c, w_ada, b_ada.reshape(depth, 1, six_d))


def _modulated_rms(x, g, sc1p, sh):
    y = x * lax.rsqrt(jnp.mean(x * x, axis=-1, keepdims=True) + EPS)
    return (y * g) * sc1p + sh


def _norm_mm_kernel(x_ref, g_ref, sc_ref, sh_ref, w_ref, *rest, col_chunk, has_gates):
    if has_gates:
        wg_ref, o_ref, og_ref = rest
    else:
        (o_ref,) = rest
    hb = _modulated_rms(x_ref[...], g_ref[...], sc_ref[0], sh_ref[0]).astype(BF16)
    for c0 in range(0, o_ref.shape[1], col_chunk):
        o_ref[:, c0:c0 + col_chunk] = jnp.dot(
            hb, w_ref[:, c0:c0 + col_chunk], preferred_element_type=F32).astype(o_ref.dtype)
    if has_gates:
        og_ref[...] = jnp.dot(hb, wg_ref[...], preferred_element_type=F32)


def _norm_matmul(x, gain, sc1p, sh, w, wg, seq):
    n, d = x.shape
    cols = w.shape[1]
    tm = _tile(512, seq)
    per_b = seq // tm
    has_gates = wg is not None
    in_specs = [
        pl.BlockSpec((tm, d), lambda i: (i, 0)),
        pl.BlockSpec((1, d), lambda i: (0, 0)),
        pl.BlockSpec((1, 1, d), lambda i: (i // per_b, 0, 0)),
        pl.BlockSpec((1, 1, d), lambda i: (i // per_b, 0, 0)),
        pl.BlockSpec((d, cols), lambda i: (0, 0)),
    ]
    out_specs = [pl.BlockSpec((tm, cols), lambda i: (i, 0))]
    out_shape = [jax.ShapeDtypeStruct((n, cols), BF16)]
    args = [x, gain, sc1p, sh, w]
    if has_gates:
        in_specs.append(pl.BlockSpec((d, LANES), lambda i: (0, 0)))
        out_specs.append(pl.BlockSpec((tm, LANES), lambda i: (i, 0)))
        out_shape.append(jax.ShapeDtypeStruct((n, LANES), F32))
        args.append(wg)
    return pl.pallas_call(
        functools.partial(_norm_mm_kernel, col_chunk=512, has_gates=has_gates),
        grid=(n // tm,),
        in_specs=in_specs,
        out_specs=out_specs,
        out_shape=out_shape,
        compiler_params=_params(("parallel",)),
        name="norm_inproj",
    )(*args)


def _log_sigmoid(x):
    return jnp.minimum(x, 0.0) - jnp.log1p(jnp.exp(-jnp.abs(x)))


def _lane_cumsum(x):
    lane = lax.broadcasted_iota(jnp.int32, x.shape, 1)
    sh = 1
    while sh < x.shape[1]:
        x = x + jnp.where(lane >= sh, pltpu.roll(x, sh, axis=1), 0.0)
        sh *= 2
    return x


def _mlstm_kernel(p_ref, g_ref, bias_ref, hn_ref, o_ref, c_ref, n_ref, m_ref, r_ref):
    L = A_CHUNK

    @pl.when(pl.program_id(1) == 0)
    def _():
        c_ref[...] = jnp.zeros_like(c_ref)
        n_ref[...] = jnp.zeros_like(n_ref)
        m_ref[...] = jnp.zeros_like(m_ref)
        r_ref[...] = jnp.zeros_like(r_ref)

    gates = g_ref[...] + bias_ref[...]
    g8 = gates.T[0:SUBLANES]
    bc8 = _lane_cumsum(_log_sigmoid(g8))
    b_rows, i_rows, lw_rows, b_lasts = [], [], [], []
    for h in range(A_HEADS):
        b_row = bc8[A_HEADS + h:A_HEADS + h + 1]
        i_row = g8[h:h + 1]
        b_last = b_row[:, L - 1:L]
        lw_row = b_last - b_row + i_row
        r_ref[h:h + 1, :] = b_row
        r_ref[A_HEADS + h:A_HEADS + h + 1, :] = lw_row
        b_rows.append(b_row)
        i_rows.append(i_row)
        lw_rows.append(lw_row)
        b_lasts.append(b_last)
    cols = r_ref[...].T

    row_id = lax.broadcasted_iota(jnp.int32, (L, L), 0)
    col_id = lax.broadcasted_iota(jnp.int32, (L, L), 1)
    causal = row_id >= col_id

    for h in range(A_HEADS):
        q = p_ref[:, h * A_QK_DIM:(h + 1) * A_QK_DIM]
        ksf = p_ref[:, A_HQ + h * A_QK_DIM:A_HQ + (h + 1) * A_QK_DIM].astype(F32) * (A_QK_DIM ** -0.5)
        v = p_ref[:, 2 * A_HQ + h * A_V_DIM:2 * A_HQ + (h + 1) * A_V_DIM]
        og = p_ref[:, 2 * A_HQ + A_HV + h * A_V_DIM:2 * A_HQ + A_HV + (h + 1) * A_V_DIM].astype(F32)
        b_col = cols[:, h:h + 1]
        lw_col = cols[:, A_HEADS + h:A_HEADS + h + 1]
        m11 = m_ref[h][:, 0:1]
        c_old = c_ref[h]
        n_old = n_ref[h]

        log_d = jnp.where(causal, b_col - b_rows[h] + i_rows[h], -jnp.inf)
        log_inter = b_col + m11
        m_t = jnp.maximum(jnp.max(log_d, axis=1, keepdims=True), log_inter)
        dmat = jnp.exp(log_d - m_t)
        inter = jnp.exp(log_inter - m_t)
        s = lax.dot_general(q, ksf.astype(BF16), (((1,), (1,)), ((), ())), preferred_element_type=F32) * dmat
        num = (jnp.dot(s.astype(BF16), v, preferred_element_type=F32)
               + inter * jnp.dot(q, c_old.astype(BF16), preferred_element_type=F32))
        den = (jnp.sum(s, axis=1, keepdims=True)
               + inter * jnp.sum(q.astype(F32) * n_old, axis=1, keepdims=True))
        hh = num / jnp.maximum(jnp.abs(den), jnp.exp(-m_t))

        m_new = jnp.maximum(b_lasts[h] + m11, jnp.max(lw_rows[h], axis=1, keepdims=True))
        w_col = jnp.exp(lw_col - m_new)
        decay = jnp.exp(b_lasts[h] + m11 - m_new)
        kw = ksf * w_col
        c_ref[h] = decay * c_old + lax.dot_general(
            kw.astype(BF16), v, (((0,), (0,)), ((), ())), preferred_element_type=F32)
        n_ref[h] = decay * n_old + jnp.sum(kw, axis=0, keepdims=True)
        m_ref[h] = jnp.broadcast_to(m_new, (1, LANES))

        hn = hh * lax.rsqrt(jnp.mean(hh * hh, axis=1, keepdims=True) + EPS)
        hn = hn * hn_ref[:, h * A_V_DIM:(h + 1) * A_V_DIM]
        o_ref[:, h * A_V_DIM:(h + 1) * A_V_DIM] = (hn * jax.nn.sigmoid(og)).astype(o_ref.dtype)


def _mlstm(proj, gates, bias_row, h_norm_row, bsz, seq):
    n = proj.shape[0]
    nc = seq // A_CHUNK
    return pl.pallas_call(
        _mlstm_kernel,
        grid=(bsz, nc),
        in_specs=[
            pl.BlockSpec((A_CHUNK, A_MAIN_COLS), lambda b, c: (b * nc + c, 0)),
            pl.BlockSpec((A_CHUNK, LANES), lambda b, c: (b * nc + c, 0)),
            pl.BlockSpec((1, LANES), lambda b, c: (0, 0)),
            pl.BlockSpec((1, A_HV), lambda b, c: (0, 0)),
        ],
        out_specs=pl.BlockSpec((A_CHUNK, A_HV), lambda b, c: (b * nc + c, 0)),
        out_shape=jax.ShapeDtypeStruct((n, A_HV), BF16),
        scratch_shapes=[
            pltpu.VMEM((A_HEADS, A_QK_DIM, A_V_DIM), F32),
            pltpu.VMEM((A_HEADS, 1, A_QK_DIM), F32),
            pltpu.VMEM((A_HEADS, 1, LANES), F32),
            pltpu.VMEM((A_CHUNK, A_CHUNK), F32),
        ],
        compiler_params=_params(("parallel", "arbitrary")),
        name="mlstm",
    )(proj, gates, bias_row, h_norm_row)


def _qk_prep_kernel(p_ref, pos_ref, seg_ref, segt_ref, gain_ref, freq_ref, o_ref):
    tm = p_ref.shape[0]
    x = p_ref[...].astype(F32)
    xx = x * x
    xx_hi = xx.astype(BF16)
    xx_lo = (xx - xx_hi.astype(F32)).astype(BF16)
    seg = seg_ref[...]
    ss = jnp.dot(xx_hi, seg, preferred_element_type=F32) + jnp.dot(xx_lo, seg, preferred_element_type=F32)
    r = lax.rsqrt(ss * (1.0 / B_HEAD_DIM) + EPS)
    r_hi = r.astype(BF16)
    r_lo = (r - r_hi.astype(F32)).astype(BF16)
    segt = segt_ref[...]
    r_full = jnp.dot(r_hi, segt, preferred_element_type=F32) + jnp.dot(r_lo, segt, preferred_element_type=F32)

    pos = jnp.broadcast_to(pos_ref[0].astype(F32), (LANES, tm)).T
    ang = pos * freq_ref[0:1, :]
    rot_lo = freq_ref[1:2, :]
    rot_hi = freq_ref[2:3, :]
    cosv = jnp.cos(ang)
    sinv = jnp.sin(ang)
    c_mul = jnp.where(rot_lo + rot_hi > 0.0, cosv, 1.0)
    s_lo = -sinv * rot_lo
    s_hi = sinv * rot_hi
    half = B_ROT_DIM // 2
    for j in range(2 * D_MODEL // LANES):
        sl = slice(j * LANES, (j + 1) * LANES)
        y = x[:, sl] * r_full[:, sl] * gain_ref[:, sl]
        out = y * c_mul + pltpu.roll(y, LANES - half, axis=1) * s_lo + pltpu.roll(y, half, axis=1) * s_hi
        if j < D_MODEL // LANES:
            out = out * (B_HEAD_DIM ** -0.5)
        o_ref[:, sl] = out.astype(o_ref.dtype)


def _qk_prep(proj, positions_row, q_norm, k_norm, seq):
    n = proj.shape[0]
    two_d = 2 * D_MODEL
    tm = _tile(256, seq)
    n_seg = two_d // B_HEAD_DIM
    seg = (jnp.arange(two_d)[:, None] // B_HEAD_DIM == jnp.arange(LANES)[None, :]).astype(BF16)
    gain = jnp.concatenate([jnp.tile(q_norm, D_MODEL // B_HEAD_DIM), jnp.tile(k_norm, D_MODEL // B_HEAD_DIM)])
    lane = jnp.arange(LANES)
    in_seg = lane % B_HEAD_DIM
    inv_freq = ROPE_THETA ** (-jnp.arange(0, B_ROT_DIM, 2, dtype=F32) / B_ROT_DIM)
    half = B_ROT_DIM // 2
    freq = jnp.where(in_seg < B_ROT_DIM, inv_freq[in_seg % half], 0.0)
    freq_tab = jnp.zeros((SUBLANES, LANES), F32)
    freq_tab = freq_tab.at[0].set(freq).at[1].set((in_seg < half).astype(F32))
    freq_tab = freq_tab.at[2].set(((in_seg >= half) & (in_seg < B_ROT_DIM)).astype(F32))
    del n_seg
    return pl.pallas_call(
        _qk_prep_kernel,
        grid=(n // tm,),
        in_specs=[
            pl.BlockSpec((tm, two_d), lambda i: (i, 0)),
            pl.BlockSpec((1, 1, tm), lambda i: (i, 0, 0)),
            pl.BlockSpec((two_d, LANES), lambda i: (0, 0)),
            pl.BlockSpec((LANES, two_d), lambda i: (0, 0)),
            pl.BlockSpec((1, two_d), lambda i: (0, 0)),
            pl.BlockSpec((SUBLANES, LANES), lambda i: (0, 0)),
        ],
        out_specs=pl.BlockSpec((tm, two_d), lambda i: (i, 0)),
        out_shape=jax.ShapeDtypeStruct((n, two_d), BF16),
        compiler_params=_params(("parallel",)),
        name="qk_prep",
    )(proj, positions_row.reshape(n // tm, 1, tm), seg, seg.T, gain.reshape(1, two_d), freq_tab)


def _diff_attn_kernel(q_ref, k_ref, v_ref, lam_ref, on_ref, o_ref, *, tq, lam_init):
    seq = q_ref.shape[0]
    nq = seq // tq
    lamv = lam_ref[...]
    lam = (jnp.exp(jnp.sum(lamv[0:1] * lamv[1:2], axis=1, keepdims=True))
           - jnp.exp(jnp.sum(lamv[2:3] * lamv[3:4], axis=1, keepdims=True)) + lam_init)
    lane = lax.broadcasted_iota(jnp.int32, (tq, LANES), 1)
    first_map = lane < B_HEAD_DIM
    row_id = lax.broadcasted_iota(jnp.int32, (tq, tq), 0)
    col_id = lax.broadcasted_iota(jnp.int32, (tq, tq), 1)
    causal = row_id >= col_id
    nt = (((1,), (1,)), ((), ()))

    def tile_update(carry, qa, qb, k, v, mask):
        out = []
        for (m_old, l_old, acc), qm in zip(carry, (qa, qb)):
            s = lax.dot_general(qm, k, nt, preferred_element_type=F32)
            if mask is not None:
                s = jnp.where(mask, s, -jnp.inf)
            m_new = jnp.maximum(m_old, jnp.max(s, axis=1, keepdims=True))
            alpha = jnp.exp(m_old - m_new)
            p = jnp.exp(s - m_new)
            l_new = alpha * l_old + jnp.sum(p, axis=1, keepdims=True)
            acc_new = alpha * acc + jnp.dot(p.astype(BF16), v, preferred_element_type=F32)
            out.append((m_new, l_new, acc_new))
        return tuple(out)

    for qi in range(nq):
        q = q_ref[qi * tq:(qi + 1) * tq, :]
        qa = jnp.where(first_map, q, jnp.zeros_like(q))
        qb = jnp.where(first_map, jnp.zeros_like(q), q)
        init = tuple((jnp.full((tq, 1), -jnp.inf, F32), jnp.zeros((tq, 1), F32), jnp.zeros((tq, B_V_DIM), F32))
                     for _ in range(2))

        def body(ki, carry, qa=qa, qb=qb):
            start = pl.multiple_of(ki * tq, tq)
            return tile_update(carry, qa, qb, k_ref[pl.ds(start, tq), :], v_ref[pl.ds(start, tq), :], None)

        carry = lax.fori_loop(0, qi, body, init) if qi > 0 else init
        carry = tile_update(carry, qa, qb, k_ref[qi * tq:(qi + 1) * tq, :], v_ref[qi * tq:(qi + 1) * tq, :], causal)
        (_, l1, a1), (_, l2, a2) = carry
        o = a1 / l1 - lam * (a2 / l2)
        o = o * lax.rsqrt(jnp.mean(o * o, axis=1, keepdims=True) + EPS) * on_ref[...] * (1.0 - lam_init)
        o_ref[qi * tq:(qi + 1) * tq, :] = o.astype(o_ref.dtype)


def _diff_attn(qk, proj, lam_tab, o_norm_row, bsz, seq, lam_init):
    n = qk.shape[0]
    tq = _tile(256, seq)
    return pl.pallas_call(
        functools.partial(_diff_attn_kernel, tq=tq, lam_init=lam_init),
        grid=(bsz, B_HEADS),
        in_specs=[
            pl.BlockSpec((seq, LANES), lambda b, h: (b, h)),
            pl.BlockSpec((seq, LANES), lambda b, h: (b, B_HEADS + h)),
            pl.BlockSpec((seq, B_V_DIM), lambda b, h: (b, 2 * B_HEADS + h)),
            pl.BlockSpec((SUBLANES, LANES), lambda b, h: (0, 0)),
            pl.BlockSpec((1, B_V_DIM), lambda b, h: (0, 0)),
        ],
        out_specs=pl.BlockSpec((seq, B_V_DIM), lambda b, h: (b, h)),
        out_shape=jax.ShapeDtypeStruct((n, D_MODEL), BF16),
        compiler_params=_params(("parallel", "parallel")),
        name="diff_attn",
    )(qk, qk, proj, lam_tab, o_norm_row)


def _store_token_tiles(dst_ref, val):
    rows = val.shape[0]
    for j in range(ROW_TILES):
        dst_ref[pl.ds(j, rows, stride=ROW_TILES), :] = val[:, j * LANES:(j + 1) * LANES]


def _load_token_tiles(src_ref, rows):
    return jnp.concatenate([src_ref[pl.ds(j, rows, stride=ROW_TILES), :] for j in range(ROW_TILES)], axis=1)


def _outproj_kernel(a_ref, w_ref, x_ref, g1_ref, gain_ref, sc_ref, sh_ref, wr_ref, xo_ref, h2_ref, lg_ref):
    y = jnp.dot(a_ref[...], w_ref[...], preferred_element_type=F32)
    xm = x_ref[...] + g1_ref[0] * y
    xo_ref[...] = xm
    h2 = _modulated_rms(xm, gain_ref[...], sc_ref[0], sh_ref[0])
    _store_token_tiles(h2_ref, h2)
    lg_ref[...] = lax.dot_general(wr_ref[...], h2, (((1,), (1,)), ((), ())),
                                  precision=HIGHEST, preferred_element_type=F32)


def _outproj_norm_router(a, w, x, g1, gain, sc1p, sh, w_router_t, seq):
    n, d = x.shape
    tm = _tile(512, seq)
    per_b = seq // tm
    bmap = lambda i: (i // per_b, 0, 0)
    return pl.pallas_call(
        _outproj_kernel,
        grid=(n // tm,),
        in_specs=[
            pl.BlockSpec((tm, a.shape[1]), lambda i: (i, 0)),
            pl.BlockSpec(w.shape, lambda i: (0, 0)),
            pl.BlockSpec((tm, d), lambda i: (i, 0)),
            pl.BlockSpec((1, 1, d), bmap),
            pl.BlockSpec((1, d), lambda i: (0, 0)),
            pl.BlockSpec((1, 1, d), bmap),
            pl.BlockSpec((1, 1, d), bmap),
            pl.BlockSpec((N_EXPERTS, d), lambda i: (0, 0)),
        ],
        out_specs=[
            pl.BlockSpec((tm, d), lambda i: (i, 0)),
            pl.BlockSpec((tm * ROW_TILES, LANES), lambda i: (i, 0)),
            pl.BlockSpec((N_EXPERTS, tm), lambda i: (0, i)),
        ],
        out_shape=[
            jax.ShapeDtypeStruct((n, d), F32),
            jax.ShapeDtypeStruct((n * ROW_TILES, LANES), F32),
            jax.ShapeDtypeStruct((N_EXPERTS, n), F32),
        ],
        compiler_params=_params(("parallel",)),
        name="outproj_norm_router",
    )(a, w, x, g1, gain, sc1p, sh, w_router_t)


def _route_kernel(lg_ref, bias_ref, oi_ref, ow_ref, cnt_ref, carry_ref, tri_ref):
    tr = lg_ref.shape[1]
    step = pl.program_id(0)

    @pl.when(step == 0)
    def _():
        carry_ref[...] = jnp.zeros_like(carry_ref)
        r = lax.broadcasted_iota(jnp.int32, (tr, tr), 0)
        c = lax.broadcasted_iota(jnp.int32, (tr, tr), 1)
        tri_ref[...] = jnp.where(r < c, 1.0, 0.0).astype(BF16)

    scores = jax.nn.sigmoid(lg_ref[...])
    biased = scores + bias_ref[...]
    rows = [biased[e:e + 1, :] for e in range(N_EXPERTS)]
    srows = [scores[e:e + 1, :] for e in range(N_EXPERTS)]

    def top2_sum(a, b, c, d):
        m1, n1 = jnp.maximum(a, b), jnp.minimum(a, b)
        m2, n2 = jnp.maximum(c, d), jnp.minimum(c, d)
        return jnp.maximum(m1, m2) + jnp.maximum(jnp.minimum(m1, m2), jnp.maximum(n1, n2))

    gscore = [top2_sum(*rows[g * EXPERTS_PER_GROUP:(g + 1) * EXPERTS_PER_GROUP]) for g in range(N_GROUPS)]
    best = gscore[0]
    gsel = jnp.zeros_like(best, dtype=jnp.int32)
    for g in range(1, N_GROUPS):
        upd = gscore[g] > best
        gsel = jnp.where(upd, g, gsel)
        best = jnp.where(upd, gscore[g], best)

    def pick(table, j):
        out = table[j]
        for g in range(1, N_GROUPS):
            out = jnp.where(gsel == g, table[g * EXPERTS_PER_GROUP + j], out)
        return out

    in_b = [pick(rows, j) for j in range(EXPERTS_PER_GROUP)]
    in_s = [pick(srows, j) for j in range(EXPERTS_PER_GROUP)]
    v1, i1, s1 = in_b[0], jnp.zeros_like(gsel), in_s[0]
    for j in range(1, EXPERTS_PER_GROUP):
        upd = in_b[j] > v1
        v1 = jnp.where(upd, in_b[j], v1)
        i1 = jnp.where(upd, j, i1)
        s1 = jnp.where(upd, in_s[j], s1)
    v2 = jnp.full_like(v1, -jnp.inf)
    i2 = jnp.zeros_like(gsel)
    s2 = jnp.zeros_like(s1)
    for j in range(EXPERTS_PER_GROUP):
        upd = (i1 != j) & (in_b[j] > v2)
        v2 = jnp.where(upd, in_b[j], v2)
        i2 = jnp.where(upd, j, i2)
        s2 = jnp.where(upd, in_s[j], s2)
    e0 = gsel * EXPERTS_PER_GROUP + i1
    e1 = gsel * EXPERTS_PER_GROUP + i2
    tot = s1 + s2
    ow_ref[0:1, :] = s1 / tot
    ow_ref[1:2, :] = s2 / tot

    eid = lax.broadcasted_iota(jnp.int32, (N_EXPERTS, tr), 0)
    oh0 = jnp.where(eid == e0, 1.0, 0.0)
    oh1 = jnp.where(eid == e1, 1.0, 0.0)
    cnt = oh0 + oh1
    before = jnp.dot(cnt.astype(BF16), tri_ref[...], preferred_element_type=F32) + carry_ref[...]
    oi_ref[0:1, :] = e0
    oi_ref[1:2, :] = e1
    oi_ref[2:3, :] = jnp.sum(oh0 * before, axis=0, keepdims=True).astype(jnp.int32)
    oi_ref[3:4, :] = jnp.sum(oh1 * before, axis=0, keepdims=True).astype(jnp.int32)
    new_carry = carry_ref[...] + jnp.sum(cnt, axis=1, keepdims=True)
    carry_ref[...] = new_carry
    cnt_ref[...] = new_carry.astype(jnp.int32)


def _route(logits_t, router_bias):
    n = logits_t.shape[1]
    tr = _tile(512, n)
    return pl.pallas_call(
        _route_kernel,
        grid=(n // tr,),
        in_specs=[
            pl.BlockSpec((N_EXPERTS, tr), lambda i: (0, i)),
            pl.BlockSpec((N_EXPERTS, 1), lambda i: (0, 0)),
        ],
        out_specs=[
            pl.BlockSpec((4, tr), lambda i: (0, i)),
            pl.BlockSpec((2, tr), lambda i: (0, i)),
            pl.BlockSpec((N_EXPERTS, 1), lambda i: (0, 0)),
        ],
        out_shape=[
            jax.ShapeDtypeStruct((4, n), jnp.int32),
            jax.ShapeDtypeStruct((2, n), F32),
            jax.ShapeDtypeStruct((N_EXPERTS, 1), jnp.int32),
        ],
        scratch_shapes=[pltpu.VMEM((N_EXPERTS, 1), F32), pltpu.VMEM((tr, tr), BF16)],
        compiler_params=_params(("arbitrary",)),
        name="route",
    )(logits_t, router_bias.reshape(N_EXPERTS, 1).astype(F32))


def _row_copy(src_ref, src_row, dst_ref, dst_row, sem):
    return pltpu.make_async_copy(
        src_ref.at[pl.ds(pl.multiple_of(src_row * ROW_TILES, ROW_TILES), ROW_TILES)],
        dst_ref.at[pl.ds(pl.multiple_of(dst_row * ROW_TILES, ROW_TILES), ROW_TILES)],
        sem)


def _dispatch_kernel(dest_ref, h_ref, xb_in_hbm, xb_hbm, sem):
    del xb_in_hbm
    td = dest_ref.shape[1]

    def issue(t, c):
        for k in range(TOP_K):
            _row_copy(h_ref, t, xb_hbm, dest_ref[k, t], sem).start()
        return c

    lax.fori_loop(0, td, issue, 0)

    def drain(t, c):
        for k in range(TOP_K):
            _row_copy(h_ref, t, xb_hbm, dest_ref[k, t], sem).wait()
        return c

    lax.fori_loop(0, td, drain, 0)


def _dispatch(dest, h_tiles, cap):
    n = dest.shape[1]
    td = _tile(512, n)
    xb0 = jnp.zeros((cap * ROW_TILES, LANES), F32)
    return pl.pallas_call(
        _dispatch_kernel,
        grid=(n // td,),
        in_specs=[
            pl.BlockSpec((TOP_K, td), lambda i: (0, i), memory_space=pltpu.SMEM),
            pl.BlockSpec((td * ROW_TILES, LANES), lambda i: (i, 0)),
            pl.BlockSpec(memory_space=pl.ANY),
        ],
        out_specs=pl.BlockSpec(memory_space=pl.ANY),
        out_shape=jax.ShapeDtypeStruct(xb0.shape, F32),
        scratch_shapes=[pltpu.SemaphoreType.DMA],
        input_output_aliases={2: 0},
        compiler_params=_params(("arbitrary",)),
        name="moe_dispatch",
    )(dest, h_tiles, xb0)


def _expert_kernel(be_ref, nb_ref, x_ref, wgu_ref, wd_ref, y_ref):
    tb = x_ref.shape[0] // ROW_TILES

    @pl.when(pl.program_id(0) < nb_ref[0])
    def _():
        x = _load_token_tiles(x_ref, tb).astype(BF16)
        gu = jnp.dot(x, wgu_ref[0], preferred_element_type=F32)
        gate = gu[:, :D_EXPERT]
        act = gate * jax.nn.sigmoid(gate) * gu[:, D_EXPERT:]
        y = jnp.dot(act.astype(BF16), wd_ref[0], preferred_element_type=F32)
        _store_token_tiles(y_ref, y)

    @pl.when(pl.program_id(0) >= nb_ref[0])
    def _():
        y_ref[...] = jnp.zeros_like(y_ref)


def _experts(block_e, n_used, xb, w_gu, w_down, tb):
    rows = xb.shape[0]
    n_blocks = rows // (tb * ROW_TILES)
    d, two_f = w_gu.shape[1:]
    grid_spec = pltpu.PrefetchScalarGridSpec(
        num_scalar_prefetch=2,
        grid=(n_blocks,),
        in_specs=[
            pl.BlockSpec((tb * ROW_TILES, LANES), lambda i, be, nb: (i, 0)),
            pl.BlockSpec((1, d, two_f), lambda i, be, nb: (be[i], 0, 0)),
            pl.BlockSpec((1, two_f // 2, d), lambda i, be, nb: (be[i], 0, 0)),
        ],
        out_specs=pl.BlockSpec((tb * ROW_TILES, LANES), lambda i, be, nb: (i, 0)),
    )
    return pl.pallas_call(
        _expert_kernel,
        grid_spec=grid_spec,
        out_shape=jax.ShapeDtypeStruct(xb.shape, F32),
        compiler_params=_params(("arbitrary",)),
        name="moe_experts",
    )(block_e, n_used, xb, w_gu, w_down)


def _combine_kernel(dest_ref, yb_hbm, x_ref, w_ref, g2_ref, o_ref, buf0, buf1, sem):
    tc = x_ref.shape[0]
    bufs = (buf0, buf1)

    def copy(t, k):
        return pltpu.make_async_copy(
            yb_hbm.at[pl.ds(pl.multiple_of(dest_ref[k, t] * ROW_TILES, ROW_TILES), ROW_TILES)],
            bufs[k].at[pl.ds(pl.multiple_of(t * ROW_TILES, ROW_TILES), ROW_TILES)],
            sem)

    def issue(t, c):
        for k in range(TOP_K):
            copy(t, k).start()
        return c

    lax.fori_loop(0, tc, issue, 0)

    def drain(t, c):
        for k in range(TOP_K):
            copy(t, k).wait()
        return c

    lax.fori_loop(0, tc, drain, 0)

    wt = jnp.broadcast_to(w_ref[0:1, :], (LANES, tc)).T
    wt1 = jnp.broadcast_to(w_ref[1:2, :], (LANES, tc)).T
    g2 = g2_ref[0]
    for j in range(ROW_TILES):
        sl = slice(j * LANES, (j + 1) * LANES)
        y = (buf0[pl.ds(j, tc, stride=ROW_TILES), :] * wt + buf1[pl.ds(j, tc, stride=ROW_TILES), :] * wt1)
        o_ref[:, sl] = x_ref[:, sl] + g2[:, sl] * y


def _combine(dest, yb, x_mid, gate_w, g2, seq):
    n, d = x_mid.shape
    tc = _tile(256, seq)
    per_b = seq // tc
    return pl.pallas_call(
        _combine_kernel,
        grid=(n // tc,),
        in_specs=[
            pl.BlockSpec((TOP_K, tc), lambda i: (0, i), memory_space=pltpu.SMEM),
            pl.BlockSpec(memory_space=pl.ANY),
            pl.BlockSpec((tc, d), lambda i: (i, 0)),
            pl.BlockSpec((TOP_K, tc), lambda i: (0, i)),
            pl.BlockSpec((1, 1, d), lambda i: (i // per_b, 0, 0)),
        ],
        out_specs=pl.BlockSpec((tc, d), lambda i: (i, 0)),
        out_shape=jax.ShapeDtypeStruct((n, d), F32),
        scratch_shapes=[
            pltpu.VMEM((tc * ROW_TILES, LANES), F32),
            pltpu.VMEM((tc * ROW_TILES, LANES), F32),
            pltpu.SemaphoreType.DMA,
        ],
        compiler_params=_params(("arbitrary",)),
        name="moe_combine",
    )(dest, yb, x_mid, gate_w, g2)


MOE_ROWS = 256


def _moe(x_mid, h_tiles, logits_t, router_bias, w_gu, w_down, g2, seq):
    n = x_mid.shape[0]
    n_assign = n * TOP_K
    n_blocks = (n_assign + N_EXPERTS * (MOE_ROWS - 1) + MOE_ROWS - 1) // MOE_ROWS
    cap = n_blocks * MOE_ROWS
    route_i, gate_w, counts = _route(logits_t, router_bias)
    counts = counts[:, 0]
    padded = ((counts + MOE_ROWS - 1) // MOE_ROWS) * MOE_ROWS
    pad_ends = jnp.cumsum(padded)
    pad_starts = pad_ends - padded
    dest = route_i[2:4]
    for e in range(N_EXPERTS):
        dest = dest + jnp.where(route_i[0:2] == e, pad_starts[e], 0)
    blk_start = jnp.arange(n_blocks, dtype=jnp.int32) * MOE_ROWS
    block_e = jnp.minimum(jnp.sum(blk_start[:, None] >= pad_ends[None, :], axis=1), N_EXPERTS - 1).astype(jnp.int32)
    n_used = (pad_ends[-1:] // MOE_ROWS).astype(jnp.int32)
    block_e = jnp.where(jnp.arange(n_blocks) < n_used[0], block_e, block_e[jnp.maximum(n_used[0] - 1, 0)])
    xb = _dispatch(dest, h_tiles, cap)
    yb = _experts(block_e, n_used, xb, w_gu, w_down, MOE_ROWS)
    return _combine(dest, yb, x_mid, gate_w, g2, seq)


def kernel(x, c, positions, norm1, norm2, w_ada, b_ada, a_w_in, a_b_if, a_h_norm, a_w_out, b_w_in, b_q_norm, b_k_norm, b_lam_q1, b_lam_k1, b_lam_q2, b_lam_k2, b_o_norm, b_w_out, w_router, router_bias, moe_w_gu, moe_w_down):
    bsz, seq, d = x.shape
    depth = w_ada.shape[0]
    n = bsz * seq
    xf = x.reshape(n, d)
    mod = _ada_mod(c, w_ada, b_ada)
    w_router_t = w_router.T
    pos_row = positions.reshape(n)

    for l in range(depth):
        sh1, sc1, g1, sh2, sc2, g2 = [mod[l, :, i * d:(i + 1) * d].reshape(bsz, 1, d) for i in range(6)]
        j = l // 2
        if l % 2 == 0:
            w_in = a_w_in[j]
            w_main = w_in[:, :A_MAIN_COLS].astype(BF16)
            w_gate = jnp.pad(w_in[:, A_MAIN_COLS:], ((0, 0), (0, LANES - 2 * A_HEADS))).astype(BF16)
            proj, gates = _norm_matmul(xf, norm1[l].reshape(1, d), 1.0 + sc1, sh1, w_main, w_gate, seq)
            bias_row = jnp.pad(a_b_if[j], (0, LANES - 2 * A_HEADS)).reshape(1, LANES)
            mixed = _mlstm(proj, gates, bias_row, a_h_norm[j].reshape(1, A_HV), bsz, seq)
            w_out = a_w_out[j].astype(BF16)
        else:
            (proj,) = _norm_matmul(xf, norm1[l].reshape(1, d), 1.0 + sc1, sh1, b_w_in[j].astype(BF16), None, seq)
            qk = _qk_prep(proj, pos_row, b_q_norm[j], b_k_norm[j], seq)
            lam_tab = jnp.zeros((SUBLANES, LANES), F32)
            for r, v in enumerate((b_lam_q1[j], b_lam_k1[j], b_lam_q2[j], b_lam_k2[j])):
                lam_tab = lam_tab.at[r, :B_HEAD_DIM].set(v)
            lam_init = 0.8 - 0.6 * math.exp(-0.3 * l)
            mixed = _diff_attn(qk, proj, lam_tab, b_o_norm[j].reshape(1, B_V_DIM), bsz, seq, lam_init)
            w_out = b_w_out[j].astype(BF16)
        x_mid, h_tiles, logits_t = _outproj_norm_router(
            mixed, w_out, xf, g1, norm2[l].reshape(1, d), 1.0 + sc2, sh2, w_router_t, seq)
        xf = _moe(x_mid, h_tiles, logits_t, router_bias,
                  moe_w_gu[l].astype(BF16), moe_w_down[l].astype(BF16), g2, seq)
    return xf.reshape(bsz, seq, d)
```

```python
import functools
import math

import jax
import jax.numpy as jnp
from jax import lax
from jax.experimental import pallas as pl
from jax.experimental.pallas import tpu as pltpu

D_MODEL = 1024
A_HEADS = 4
A_QK_DIM = 128
A_V_DIM = 256
A_CHUNK = 128
A_HQ = A_HEADS * A_QK_DIM
A_HV = A_HEADS * A_V_DIM
A_MAIN_COLS = 2 * A_HQ + 2 * A_HV

B_HEADS = 8
B_HEAD_DIM = 64
B_V_DIM = 128
B_ROT_DIM = 16
ROPE_THETA = 500000.0

N_EXPERTS = 16
N_GROUPS = 4
EXPERTS_PER_GROUP = 4
TOP_K = 2
D_EXPERT = 512
EPS = 1e-6

LANES = 128
SUBLANES = 8
ROW_TILES = D_MODEL // LANES
assert ROW_TILES == SUBLANES
VMEM_LIMIT = 48 * 1024 * 1024

F32 = jnp.float32
BF16 = jnp.bfloat16
HIGHEST = lax.Precision.HIGHEST


def _params(sem):
    return pltpu.CompilerParams(dimension_semantics=sem, vmem_limit_bytes=VMEM_LIMIT)


def _tile(pref, n):
    t = min(pref, n)
    assert n % t == 0, (pref, n)
    return t


def _ada_kernel(c_ref, w_ref, b_ref, o_ref):
    c = c_ref[...]
    c_act = c * jax.nn.sigmoid(c)
    o_ref[0] = jnp.dot(c_act, w_ref[0], precision=HIGHEST, preferred_element_type=F32) + b_ref[0]


def _ada_mod(c, w_ada, b_ada):
    depth, d, six_d = w_ada.shape
    bsz = c.shape[0]
    tn = _tile(1536, six_d)
    return pl.pallas_call(
        _ada_kernel,
        grid=(depth, six_d // tn),
        in_specs=[
            pl.BlockSpec((bsz, d), lambda l, j: (0, 0)),
            pl.BlockSpec((1, d, tn), lambda l, j: (l, 0, j)),
            pl.BlockSpec((1, 1, tn), lambda l, j: (l, 0, j)),
        ],
        out_specs=pl.BlockSpec((1, bsz, tn), lambda l, j: (l, 0, j)),
        out_shape=jax.ShapeDtypeStruct((depth, bsz, six_d), F32),
        compiler_params=_params(("parallel", "parallel")),
        name="ada_mod",
    )(c, w_ada, b_ada.reshape(depth, 1, six_d))


def _modulated_rms(x, g, sc1p, sh):
    y = x * lax.rsqrt(jnp.mean(x * x, axis=-1, keepdims=True) + EPS)
    return (y * g) * sc1p + sh


def _norm_mm_kernel(x_ref, g_ref, sc_ref, sh_ref, w_ref, *rest, col_chunk, has_gates):
    if has_gates:
        wg_ref, o_ref, og_ref = rest
    else:
        (o_ref,) = rest
    hb = _modulated_rms(x_ref[...], g_ref[...], sc_ref[0], sh_ref[0]).astype(BF16)
    for c0 in range(0, o_ref.shape[1], col_chunk):
        o_ref[:, c0:c0 + col_chunk] = jnp.dot(
            hb, w_ref[:, c0:c0 + col_chunk], preferred_element_type=F32).astype(o_ref.dtype)
    if has_gates:
        og_ref[...] = jnp.dot(hb, wg_ref[...], preferred_element_type=F32)


def _norm_matmul(x, gain, sc1p, sh, w, wg, seq):
    n, d = x.shape
    cols = w.shape[1]
    tm = _tile(512, seq)
    per_b = seq // tm
    has_gates = wg is not None
    in_specs = [
        pl.BlockSpec((tm, d), lambda i: (i, 0)),
        pl.BlockSpec((1, d), lambda i: (0, 0)),
        pl.BlockSpec((1, 1, d), lambda i: (i // per_b, 0, 0)),
        pl.BlockSpec((1, 1, d), lambda i: (i // per_b, 0, 0)),
        pl.BlockSpec((d, cols), lambda i: (0, 0)),
    ]
    out_specs = [pl.BlockSpec((tm, cols), lambda i: (i, 0))]
    out_shape = [jax.ShapeDtypeStruct((n, cols), BF16)]
    args = [x, gain, sc1p, sh, w]
    if has_gates:
        in_specs.append(pl.BlockSpec((d, LANES), lambda i: (0, 0)))
        out_specs.append(pl.BlockSpec((tm, LANES), lambda i: (i, 0)))
        out_shape.append(jax.ShapeDtypeStruct((n, LANES), F32))
        args.append(wg)
    return pl.pallas_call(
        functools.partial(_norm_mm_kernel, col_chunk=512, has_gates=has_gates),
        grid=(n // tm,),
        in_specs=in_specs,
        out_specs=out_specs,
        out_shape=out_shape,
        compiler_params=_params(("parallel",)),
        name="norm_inproj",
    )(*args)


def _log_sigmoid(x):
    return jnp.minimum(x, 0.0) - jnp.log1p(jnp.exp(-jnp.abs(x)))


def _lane_cumsum(x):
    lane = lax.broadcasted_iota(jnp.int32, x.shape, 1)
    sh = 1
    while sh < x.shape[1]:
        x = x + jnp.where(lane >= sh, pltpu.roll(x, sh, axis=1), 0.0)
        sh *= 2
    return x


def _mlstm_kernel(p_ref, g_ref, bias_ref, hn_ref, o_ref, c_ref, n_ref, m_ref, r_ref):
    L = A_CHUNK

    @pl.when(pl.program_id(1) == 0)
    def _():
        c_ref[...] = jnp.zeros_like(c_ref)
        n_ref[...] = jnp.zeros_like(n_ref)
        m_ref[...] = jnp.zeros_like(m_ref)
        r_ref[...] = jnp.zeros_like(r_ref)

    gates = g_ref[...] + bias_ref[...]
    g8 = gates.T[0:SUBLANES]
    bc8 = _lane_cumsum(_log_sigmoid(g8))
    b_rows, i_rows, lw_rows, b_lasts = [], [], [], []
    for h in range(A_HEADS):
        b_row = bc8[A_HEADS + h:A_HEADS + h + 1]
        i_row = g8[h:h + 1]
        b_last = b_row[:, L - 1:L]
        lw_row = b_last - b_row + i_row
        r_ref[h:h + 1, :] = b_row
        r_ref[A_HEADS + h:A_HEADS + h + 1, :] = lw_row
        b_rows.append(b_row)
        i_rows.append(i_row)
        lw_rows.append(lw_row)
        b_lasts.append(b_last)
    cols = r_ref[...].T

    row_id = lax.broadcasted_iota(jnp.int32, (L, L), 0)
    col_id = lax.broadcasted_iota(jnp.int32, (L, L), 1)
    causal = row_id >= col_id

    for h in range(A_HEADS):
        q = p_ref[:, h * A_QK_DIM:(h + 1) * A_QK_DIM]
        ksf = p_ref[:, A_HQ + h * A_QK_DIM:A_HQ + (h + 1) * A_QK_DIM].astype(F32) * (A_QK_DIM ** -0.5)
        v = p_ref[:, 2 * A_HQ + h * A_V_DIM:2 * A_HQ + (h + 1) * A_V_DIM]
        og = p_ref[:, 2 * A_HQ + A_HV + h * A_V_DIM:2 * A_HQ + A_HV + (h + 1) * A_V_DIM].astype(F32)
        b_col = cols[:, h:h + 1]
        lw_col = cols[:, A_HEADS + h:A_HEADS + h + 1]
        m11 = m_ref[h][:, 0:1]
        c_old = c_ref[h]
        n_old = n_ref[h]

        log_d = jnp.where(causal, b_col - b_rows[h] + i_rows[h], -jnp.inf)
        log_inter = b_col + m11
        m_t = jnp.maximum(jnp.max(log_d, axis=1, keepdims=True), log_inter)
        dmat = jnp.exp(log_d - m_t)
        inter = jnp.exp(log_inter - m_t)
        s = lax.dot_general(q, ksf.astype(BF16), (((1,), (1,)), ((), ())), preferred_element_type=F32) * dmat
        num = (jnp.dot(s.astype(BF16), v, preferred_element_type=F32)
               + inter * jnp.dot(q, c_old.astype(BF16), preferred_element_type=F32))
        den = (jnp.sum(s, axis=1, keepdims=True)
               + inter * jnp.sum(q.astype(F32) * n_old, axis=1, keepdims=True))
        hh = num / jnp.maximum(jnp.abs(den), jnp.exp(-m_t))

        m_new = jnp.maximum(b_lasts[h] + m11, jnp.max(lw_rows[h], axis=1, keepdims=True))
        w_col = jnp.exp(lw_col - m_new)
        decay = jnp.exp(b_lasts[h] + m11 - m_new)
        kw = ksf * w_col
        c_ref[h] = decay * c_old + lax.dot_general(
            kw.astype(BF16), v, (((0,), (0,)), ((), ())), preferred_element_type=F32)
        n_ref[h] = decay * n_old + jnp.sum(kw, axis=0, keepdims=True)
        m_ref[h] = jnp.broadcast_to(m_new, (1, LANES))

        hn = hh * lax.rsqrt(jnp.mean(hh * hh, axis=1, keepdims=True) + EPS)
        hn = hn * hn_ref[:, h * A_V_DIM:(h + 1) * A_V_DIM]
        o_ref[:, h * A_V_DIM:(h + 1) * A_V_DIM] = (hn * jax.nn.sigmoid(og)).astype(o_ref.dtype)


def _mlstm(proj, gates, bias_row, h_norm_row, bsz, seq):
    n = proj.shape[0]
    nc = seq // A_CHUNK
    return pl.pallas_call(
        _mlstm_kernel,
        grid=(bsz, nc),
        in_specs=[
            pl.BlockSpec((A_CHUNK, A_MAIN_COLS), lambda b, c: (b * nc + c, 0)),
            pl.BlockSpec((A_CHUNK, LANES), lambda b, c: (b * nc + c, 0)),
            pl.BlockSpec((1, LANES), lambda b, c: (0, 0)),
            pl.BlockSpec((1, A_HV), lambda b, c: (0, 0)),
        ],
        out_specs=pl.BlockSpec((A_CHUNK, A_HV), lambda b, c: (b * nc + c, 0)),
        out_shape=jax.ShapeDtypeStruct((n, A_HV), BF16),
        scratch_shapes=[
            pltpu.VMEM((A_HEADS, A_QK_DIM, A_V_DIM), F32),
            pltpu.VMEM((A_HEADS, 1, A_QK_DIM), F32),
            pltpu.VMEM((A_HEADS, 1, LANES), F32),
            pltpu.VMEM((A_CHUNK, A_CHUNK), F32),
        ],
        compiler_params=_params(("parallel", "arbitrary")),
        name="mlstm",
    )(proj, gates, bias_row, h_norm_row)


def _qk_prep_kernel(p_ref, pos_ref, seg_ref, segt_ref, gain_ref, freq_ref, o_ref):
    tm = p_ref.shape[0]
    x = p_ref[...].astype(F32)
    xx = x * x
    xx_hi = xx.astype(BF16)
    xx_lo = (xx - xx_hi.astype(F32)).astype(BF16)
    seg = seg_ref[...]
    ss = jnp.dot(xx_hi, seg, preferred_element_type=F32) + jnp.dot(xx_lo, seg, preferred_element_type=F32)
    r = lax.rsqrt(ss * (1.0 / B_HEAD_DIM) + EPS)
    r_hi = r.astype(BF16)
    r_lo = (r - r_hi.astype(F32)).astype(BF16)
    segt = segt_ref[...]
    r_full = jnp.dot(r_hi, segt, preferred_element_type=F32) + jnp.dot(r_lo, segt, preferred_element_type=F32)

    pos = jnp.broadcast_to(pos_ref[0].astype(F32), (LANES, tm)).T
    ang = pos * freq_ref[0:1, :]
    rot_lo = freq_ref[1:2, :]
    rot_hi = freq_ref[2:3, :]
    cosv = jnp.cos(ang)
    sinv = jnp.sin(ang)
    c_mul = jnp.where(rot_lo + rot_hi > 0.0, cosv, 1.0)
    s_lo = -sinv * rot_lo
    s_hi = sinv * rot_hi
    half = B_ROT_DIM // 2
    for j in range(2 * D_MODEL // LANES):
        sl = slice(j * LANES, (j + 1) * LANES)
        y = x[:, sl] * r_full[:, sl] * gain_ref[:, sl]
        out = y * c_mul + pltpu.roll(y, LANES - half, axis=1) * s_lo + pltpu.roll(y, half, axis=1) * s_hi
        if j < D_MODEL // LANES:
            out = out * (B_HEAD_DIM ** -0.5 * math.log2(math.e))
        o_ref[:, sl] = out.astype(o_ref.dtype)


def _qk_prep(proj, positions_row, q_norm, k_norm, seq):
    n = proj.shape[0]
    two_d = 2 * D_MODEL
    tm = _tile(256, seq)
    n_seg = two_d // B_HEAD_DIM
    seg = (jnp.arange(two_d)[:, None] // B_HEAD_DIM == jnp.arange(LANES)[None, :]).astype(BF16)
    gain = jnp.concatenate([jnp.tile(q_norm, D_MODEL // B_HEAD_DIM), jnp.tile(k_norm, D_MODEL // B_HEAD_DIM)])
    lane = jnp.arange(LANES)
    in_seg = lane % B_HEAD_DIM
    inv_freq = ROPE_THETA ** (-jnp.arange(0, B_ROT_DIM, 2, dtype=F32) / B_ROT_DIM)
    half = B_ROT_DIM // 2
    freq = jnp.where(in_seg < B_ROT_DIM, inv_freq[in_seg % half], 0.0)
    freq_tab = jnp.zeros((SUBLANES, LANES), F32)
    freq_tab = freq_tab.at[0].set(freq).at[1].set((in_seg < half).astype(F32))
    freq_tab = freq_tab.at[2].set(((in_seg >= half) & (in_seg < B_ROT_DIM)).astype(F32))
    del n_seg
    return pl.pallas_call(
        _qk_prep_kernel,
        grid=(n // tm,),
        in_specs=[
            pl.BlockSpec((tm, two_d), lambda i: (i, 0)),
            pl.BlockSpec((1, 1, tm), lambda i: (i, 0, 0)),
            pl.BlockSpec((two_d, LANES), lambda i: (0, 0)),
            pl.BlockSpec((LANES, two_d), lambda i: (0, 0)),
            pl.BlockSpec((1, two_d), lambda i: (0, 0)),
            pl.BlockSpec((SUBLANES, LANES), lambda i: (0, 0)),
        ],
        out_specs=pl.BlockSpec((tm, two_d), lambda i: (i, 0)),
        out_shape=jax.ShapeDtypeStruct((n, two_d), BF16),
        compiler_params=_params(("parallel",)),
        name="qk_prep",
    )(proj, positions_row.reshape(n // tm, 1, tm), seg, seg.T, gain.reshape(1, two_d), freq_tab)


ATTN_TILE = 512


def _diff_attn_kernel(q_ref, k_ref, v_ref, lam_ref, on_ref, o_ref, vt_ref, qm_ref, m_ref, l_ref, acc_ref,
                      *, tile, lam_init):
    seq = q_ref.shape[0]
    n_tiles = seq // tile
    lamv = lam_ref[...]
    lam = (jnp.exp(jnp.sum(lamv[0:1] * lamv[1:2], axis=1, keepdims=True))
           - jnp.exp(jnp.sum(lamv[2:3] * lamv[3:4], axis=1, keepdims=True)) + lam_init)
    tchunk = min(256, seq)
    for c in range(seq // tchunk):
        vt_ref[:, c * tchunk:(c + 1) * tchunk] = v_ref[c * tchunk:(c + 1) * tchunk, :].astype(F32).T.astype(BF16)
    first_map = lax.broadcasted_iota(jnp.int32, (tile, LANES), 1) < B_HEAD_DIM
    key_id = lax.broadcasted_iota(jnp.int32, (tile, tile), 0)
    query_id = lax.broadcasted_iota(jnp.int32, (tile, tile), 1)
    causal = query_id >= key_id
    nt = (((1,), (1,)), ((), ()))

    def tile_step(k, vt, mask):
        scores = [lax.dot_general(k, qm_ref[c], nt, preferred_element_type=F32) for c in range(2)]
        for c, s in enumerate(scores):
            if mask is not None:
                s = jnp.where(mask, s, -jnp.inf)
            m_old = m_ref[c]
            m_new = jnp.maximum(m_old, jnp.max(s, axis=0, keepdims=True))
            alpha = jnp.exp2(m_old - m_new)
            p = jnp.exp2(s - m_new)
            l_ref[c] = alpha * l_ref[c] + jnp.sum(p, axis=0, keepdims=True)
            acc_ref[c] = alpha * acc_ref[c] + jnp.dot(vt, p.astype(BF16), preferred_element_type=F32)
            m_ref[c] = m_new

    for qi in range(n_tiles):
        q = q_ref[qi * tile:(qi + 1) * tile, :]
        qm_ref[0] = jnp.where(first_map, q, jnp.zeros_like(q))
        qm_ref[1] = jnp.where(first_map, jnp.zeros_like(q), q)
        m_ref[...] = jnp.full(m_ref.shape, -jnp.inf, F32)
        l_ref[...] = jnp.zeros(l_ref.shape, F32)
        acc_ref[...] = jnp.zeros(acc_ref.shape, F32)
        for ki in range(qi + 1):
            tile_step(k_ref[ki * tile:(ki + 1) * tile, :], vt_ref[:, ki * tile:(ki + 1) * tile],
                      causal if ki == qi else None)
        o = acc_ref[0] / l_ref[0] - lam * (acc_ref[1] / l_ref[1])
        o = o * lax.rsqrt(jnp.mean(o * o, axis=0, keepdims=True) + EPS) * on_ref[...] * (1.0 - lam_init)
        o_ref[qi * tile:(qi + 1) * tile, :] = o.T.astype(o_ref.dtype)


def _diff_attn(qk, proj, lam_tab, o_norm, bsz, seq, lam_init):
    n = qk.shape[0]
    tile = _tile(ATTN_TILE, seq)
    o_norm_cols = jnp.broadcast_to(o_norm[:, None], (B_V_DIM, tile))
    return pl.pallas_call(
        functools.partial(_diff_attn_kernel, tile=tile, lam_init=lam_init),
        grid=(bsz, B_HEADS),
        in_specs=[
            pl.BlockSpec((seq, LANES), lambda b, h: (b, h)),
            pl.BlockSpec((seq, LANES), lambda b, h: (b, B_HEADS + h)),
            pl.BlockSpec((seq, B_V_DIM), lambda b, h: (b, 2 * B_HEADS + h)),
            pl.BlockSpec((SUBLANES, LANES), lambda b, h: (0, 0)),
            pl.BlockSpec((B_V_DIM, tile), lambda b, h: (0, 0)),
        ],
        out_specs=pl.BlockSpec((seq, B_V_DIM), lambda b, h: (b, h)),
        out_shape=jax.ShapeDtypeStruct((n, D_MODEL), BF16),
        scratch_shapes=[
            pltpu.VMEM((B_V_DIM, seq), BF16),
            pltpu.VMEM((2, tile, LANES), BF16),
            pltpu.VMEM((2, 1, tile), F32),
            pltpu.VMEM((2, 1, tile), F32),
            pltpu.VMEM((2, B_V_DIM, tile), F32),
        ],
        compiler_params=_params(("parallel", "parallel")),
        name="diff_attn",
    )(qk, qk, proj, lam_tab, o_norm_cols)


HALF_D = D_MODEL // 2
HI16 = 0xFFFF0000


def _pack_bf16_pairs(v):
    bits = lax.bitcast_convert_type(v.astype(BF16).astype(F32), jnp.uint32)
    return (bits[:, HALF_D:] & jnp.uint32(HI16)) | (bits[:, :HALF_D] >> 16)


def _unpack_bf16_pairs(words):
    first = lax.bitcast_convert_type(words << 16, F32)
    second = lax.bitcast_convert_type(words & jnp.uint32(HI16), F32)
    return first.astype(BF16), second.astype(BF16)


def _outproj_kernel(a_ref, w_ref, x_ref, g1_ref, gain_ref, sc_ref, sh_ref, wr_ref, xo_ref, hp_ref, lg_ref):
    y = jnp.dot(a_ref[...], w_ref[...], preferred_element_type=F32)
    xm = x_ref[...] + g1_ref[0] * y
    xo_ref[...] = xm
    h2 = _modulated_rms(xm, gain_ref[...], sc_ref[0], sh_ref[0])
    hp_ref[...] = _pack_bf16_pairs(h2)
    lg_ref[...] = lax.dot_general(wr_ref[...], h2, (((1,), (1,)), ((), ())),
                                  precision=HIGHEST, preferred_element_type=F32)


def _outproj_norm_router(a, w, x, g1, gain, sc1p, sh, w_router_t, seq):
    n, d = x.shape
    tm = _tile(512, seq)
    per_b = seq // tm
    bmap = lambda i: (i // per_b, 0, 0)
    return pl.pallas_call(
        _outproj_kernel,
        grid=(n // tm,),
        in_specs=[
            pl.BlockSpec((tm, a.shape[1]), lambda i: (i, 0)),
            pl.BlockSpec(w.shape, lambda i: (0, 0)),
            pl.BlockSpec((tm, d), lambda i: (i, 0)),
            pl.BlockSpec((1, 1, d), bmap),
            pl.BlockSpec((1, d), lambda i: (0, 0)),
            pl.BlockSpec((1, 1, d), bmap),
            pl.BlockSpec((1, 1, d), bmap),
            pl.BlockSpec((N_EXPERTS, d), lambda i: (0, 0)),
        ],
        out_specs=[
            pl.BlockSpec((tm, d), lambda i: (i, 0)),
            pl.BlockSpec((tm, HALF_D), lambda i: (i, 0)),
            pl.BlockSpec((N_EXPERTS, tm), lambda i: (0, i)),
        ],
        out_shape=[
            jax.ShapeDtypeStruct((n, d), F32),
            jax.ShapeDtypeStruct((n, HALF_D), jnp.uint32),
            jax.ShapeDtypeStruct((N_EXPERTS, n), F32),
        ],
        compiler_params=_params(("parallel",)),
        name="outproj_norm_router",
    )(a, w, x, g1, gain, sc1p, sh, w_router_t)


PAIR_LO = (0, 0, 0, 1, 1, 2)
PAIR_HI = (1, 2, 3, 2, 3, 3)
PAIRS_PER_GROUP = len(PAIR_LO)
N_CLASSES = N_GROUPS * PAIRS_PER_GROUP
CLASS_ROWS = 32


def _route_kernel(lg_ref, bias_ref, oi_ref, ow_ref, cnt_ref, carry_ref, tri_ref):
    tr = lg_ref.shape[1]
    step = pl.program_id(0)

    @pl.when(step == 0)
    def _():
        carry_ref[...] = jnp.zeros_like(carry_ref)
        r = lax.broadcasted_iota(jnp.int32, (tr, tr), 0)
        c = lax.broadcasted_iota(jnp.int32, (tr, tr), 1)
        tri_ref[...] = jnp.where(r < c, 1.0, 0.0).astype(BF16)

    scores = jax.nn.sigmoid(lg_ref[...])
    biased = scores + bias_ref[...]
    rows = [biased[e:e + 1, :] for e in range(N_EXPERTS)]
    srows = [scores[e:e + 1, :] for e in range(N_EXPERTS)]

    def top2_sum(a, b, c, d):
        m1, n1 = jnp.maximum(a, b), jnp.minimum(a, b)
        m2, n2 = jnp.maximum(c, d), jnp.minimum(c, d)
        return jnp.maximum(m1, m2) + jnp.maximum(jnp.minimum(m1, m2), jnp.maximum(n1, n2))

    gscore = [top2_sum(*rows[g * EXPERTS_PER_GROUP:(g + 1) * EXPERTS_PER_GROUP]) for g in range(N_GROUPS)]
    best = gscore[0]
    gsel = jnp.zeros_like(best, dtype=jnp.int32)
    for g in range(1, N_GROUPS):
        upd = gscore[g] > best
        gsel = jnp.where(upd, g, gsel)
        best = jnp.where(upd, gscore[g], best)

    def pick(table, j):
        out = table[j]
        for g in range(1, N_GROUPS):
            out = jnp.where(gsel == g, table[g * EXPERTS_PER_GROUP + j], out)
        return out

    in_b = [pick(rows, j) for j in range(EXPERTS_PER_GROUP)]
    in_s = [pick(srows, j) for j in range(EXPERTS_PER_GROUP)]
    v1, i1, s1 = in_b[0], jnp.zeros_like(gsel), in_s[0]
    for j in range(1, EXPERTS_PER_GROUP):
        upd = in_b[j] > v1
        v1 = jnp.where(upd, in_b[j], v1)
        i1 = jnp.where(upd, j, i1)
        s1 = jnp.where(upd, in_s[j], s1)
    v2 = jnp.full_like(v1, -jnp.inf)
    i2 = jnp.zeros_like(gsel)
    s2 = jnp.zeros_like(s1)
    for j in range(EXPERTS_PER_GROUP):
        upd = (i1 != j) & (in_b[j] > v2)
        v2 = jnp.where(upd, in_b[j], v2)
        i2 = jnp.where(upd, j, i2)
        s2 = jnp.where(upd, in_s[j], s2)
    first_is_lo = i1 < i2
    lo = jnp.where(first_is_lo, i1, i2)
    hi = jnp.where(first_is_lo, i2, i1)
    pair = jnp.where(lo == 0, hi - 1, jnp.where(lo == 1, hi + 1, PAIRS_PER_GROUP - 1))
    cls = gsel * PAIRS_PER_GROUP + pair
    tot = s1 + s2
    ow_ref[0:1, :] = jnp.where(first_is_lo, s1, s2) / tot
    ow_ref[1:2, :] = jnp.where(first_is_lo, s2, s1) / tot

    cid = lax.broadcasted_iota(jnp.int32, (CLASS_ROWS, tr), 0)
    onehot = jnp.where(cid == cls, 1.0, 0.0)
    before = jnp.dot(onehot.astype(BF16), tri_ref[...], preferred_element_type=F32) + carry_ref[...]
    oi_ref[0:1, :] = cls
    oi_ref[1:2, :] = jnp.sum(onehot * before, axis=0, keepdims=True).astype(jnp.int32)
    new_carry = carry_ref[...] + jnp.sum(onehot, axis=1, keepdims=True)
    carry_ref[...] = new_carry
    cnt_ref[...] = new_carry.astype(jnp.int32)


def _route(logits_t, router_bias):
    n = logits_t.shape[1]
    tr = _tile(512, n)
    return pl.pallas_call(
        _route_kernel,
        grid=(n // tr,),
        in_specs=[
            pl.BlockSpec((N_EXPERTS, tr), lambda i: (0, i)),
            pl.BlockSpec((N_EXPERTS, 1), lambda i: (0, 0)),
        ],
        out_specs=[
            pl.BlockSpec((2, tr), lambda i: (0, i)),
            pl.BlockSpec((2, tr), lambda i: (0, i)),
            pl.BlockSpec((CLASS_ROWS, 1), lambda i: (0, 0)),
        ],
        out_shape=[
            jax.ShapeDtypeStruct((2, n), jnp.int32),
            jax.ShapeDtypeStruct((2, n), F32),
            jax.ShapeDtypeStruct((CLASS_ROWS, 1), jnp.int32),
        ],
        scratch_shapes=[pltpu.VMEM((CLASS_ROWS, 1), F32), pltpu.VMEM((tr, tr), BF16)],
        compiler_params=_params(("arbitrary",)),
        name="route",
    )(logits_t, router_bias.reshape(N_EXPERTS, 1).astype(F32))


def _row_copy(src_ref, src_row, dst_ref, dst_row, sem):
    return pltpu.make_async_copy(
        src_ref.at[pl.ds(pl.multiple_of(src_row * ROW_TILES, ROW_TILES), ROW_TILES)],
        dst_ref.at[pl.ds(pl.multiple_of(dst_row * ROW_TILES, ROW_TILES), ROW_TILES)],
        sem)


PACKED_TILES = HALF_D // LANES
W_LO_SUBLANE = PACKED_TILES
W_HI_SUBLANE = PACKED_TILES + 1


def _lane_bcast_cols(row, n):
    return jnp.broadcast_to(row, (LANES, n)).T


def _dispatch_kernel(dest_ref, hp_ref, w_ref, xb_in_hbm, xb_hbm, rows_ref, sem):
    del xb_in_hbm
    td = dest_ref.shape[1]

    @pl.when(pl.program_id(0) == 0)
    def _():
        rows_ref[...] = jnp.zeros_like(rows_ref)

    for j in range(PACKED_TILES):
        rows_ref[pl.ds(j, td, stride=ROW_TILES), :] = hp_ref[:, j * LANES:(j + 1) * LANES]
    for r, sub in enumerate((W_LO_SUBLANE, W_HI_SUBLANE)):
        rows_ref[pl.ds(sub, td, stride=ROW_TILES), :] = lax.bitcast_convert_type(
            _lane_bcast_cols(w_ref[r:r + 1, :], td), jnp.uint32)

    def issue(t, c):
        _row_copy(rows_ref, t, xb_hbm, dest_ref[0, t], sem).start()
        return c

    lax.fori_loop(0, td, issue, 0)

    def drain(t, c):
        _row_copy(rows_ref, t, xb_hbm, dest_ref[0, t], sem).wait()
        return c

    lax.fori_loop(0, td, drain, 0)


def _dispatch(dest, h_packed, gate_w, cap):
    n = dest.shape[1]
    td = _tile(512, n)
    xb0 = jnp.zeros((cap * ROW_TILES, LANES), jnp.uint32)
    return pl.pallas_call(
        _dispatch_kernel,
        grid=(n // td,),
        in_specs=[
            pl.BlockSpec((1, td), lambda i: (0, i), memory_space=pltpu.SMEM),
            pl.BlockSpec((td, HALF_D), lambda i: (i, 0)),
            pl.BlockSpec((2, td), lambda i: (0, i)),
            pl.BlockSpec(memory_space=pl.ANY),
        ],
        out_specs=pl.BlockSpec(memory_space=pl.ANY),
        out_shape=jax.ShapeDtypeStruct(xb0.shape, jnp.uint32),
        scratch_shapes=[pltpu.VMEM((td * ROW_TILES, LANES), jnp.uint32), pltpu.SemaphoreType.DMA],
        input_output_aliases={3: 0},
        compiler_params=_params(("arbitrary",)),
        name="moe_dispatch",
    )(dest, h_packed, gate_w, xb0)


def _expert_kernel(ea_ref, eb_ref, nb_ref, x_ref, wgu_a_ref, wd_a_ref, wgu_b_ref, wd_b_ref, y_ref):
    del ea_ref, eb_ref
    tb = x_ref.shape[0] // ROW_TILES

    @pl.when(pl.program_id(0) < nb_ref[0])
    def _():
        halves = [_unpack_bf16_pairs(x_ref[pl.ds(j, tb, stride=ROW_TILES), :]) for j in range(PACKED_TILES)]
        x = jnp.concatenate([h[0] for h in halves] + [h[1] for h in halves], axis=1)
        w_lo = lax.bitcast_convert_type(x_ref[pl.ds(W_LO_SUBLANE, tb, stride=ROW_TILES), :], F32)
        w_hi = lax.bitcast_convert_type(x_ref[pl.ds(W_HI_SUBLANE, tb, stride=ROW_TILES), :], F32)

        def mlp(wgu_ref, wd_ref):
            gu = jnp.dot(x, wgu_ref[0], preferred_element_type=F32)
            gate = gu[:, :D_EXPERT]
            act = gate * jax.nn.sigmoid(gate) * gu[:, D_EXPERT:]
            return jnp.dot(act.astype(BF16), wd_ref[0], preferred_element_type=F32)

        ya = mlp(wgu_a_ref, wd_a_ref)
        yb = mlp(wgu_b_ref, wd_b_ref)
        for j in range(ROW_TILES):
            sl = slice(j * LANES, (j + 1) * LANES)
            y_ref[pl.ds(j, tb, stride=ROW_TILES), :] = w_lo * ya[:, sl] + w_hi * yb[:, sl]

    @pl.when(pl.program_id(0) >= nb_ref[0])
    def _():
        y_ref[...] = jnp.zeros_like(y_ref)


def _experts(block_ea, block_eb, n_used, xb, w_gu, w_down, tb):
    rows = xb.shape[0]
    n_blocks = rows // (tb * ROW_TILES)
    d, two_f = w_gu.shape[1:]
    grid_spec = pltpu.PrefetchScalarGridSpec(
        num_scalar_prefetch=3,
        grid=(n_blocks,),
        in_specs=[
            pl.BlockSpec((tb * ROW_TILES, LANES), lambda i, ea, eb, nb: (i, 0)),
            pl.BlockSpec((1, d, two_f), lambda i, ea, eb, nb: (ea[i], 0, 0)),
            pl.BlockSpec((1, two_f // 2, d), lambda i, ea, eb, nb: (ea[i], 0, 0)),
            pl.BlockSpec((1, d, two_f), lambda i, ea, eb, nb: (eb[i], 0, 0)),
            pl.BlockSpec((1, two_f // 2, d), lambda i, ea, eb, nb: (eb[i], 0, 0)),
        ],
        out_specs=pl.BlockSpec((tb * ROW_TILES, LANES), lambda i, ea, eb, nb: (i, 0)),
    )
    return pl.pallas_call(
        _expert_kernel,
        grid_spec=grid_spec,
        out_shape=jax.ShapeDtypeStruct(xb.shape, F32),
        compiler_params=_params(("arbitrary",)),
        name="moe_experts",
    )(block_ea, block_eb, n_used, xb, w_gu, w_down, w_gu, w_down)


def _combine_kernel(dest_ref, yb_hbm, x_ref, g2_ref, o_ref, buf, sem):
    tc = x_ref.shape[0]

    def issue(t, c):
        _row_copy(yb_hbm, dest_ref[0, t], buf, t, sem).start()
        return c

    lax.fori_loop(0, tc, issue, 0)

    def drain(t, c):
        _row_copy(yb_hbm, dest_ref[0, t], buf, t, sem).wait()
        return c

    lax.fori_loop(0, tc, drain, 0)

    g2 = g2_ref[0]
    for j in range(ROW_TILES):
        sl = slice(j * LANES, (j + 1) * LANES)
        o_ref[:, sl] = x_ref[:, sl] + g2[:, sl] * buf[pl.ds(j, tc, stride=ROW_TILES), :]


def _combine(dest, yb, x_mid, g2, seq):
    n, d = x_mid.shape
    tc = _tile(512, seq)
    per_b = seq // tc
    return pl.pallas_call(
        _combine_kernel,
        grid=(n // tc,),
        in_specs=[
            pl.BlockSpec((1, tc), lambda i: (0, i), memory_space=pltpu.SMEM),
            pl.BlockSpec(memory_space=pl.ANY),
            pl.BlockSpec((tc, d), lambda i: (i, 0)),
            pl.BlockSpec((1, 1, d), lambda i: (i // per_b, 0, 0)),
        ],
        out_specs=pl.BlockSpec((tc, d), lambda i: (i, 0)),
        out_shape=jax.ShapeDtypeStruct((n, d), F32),
        scratch_shapes=[pltpu.VMEM((tc * ROW_TILES, LANES), F32), pltpu.SemaphoreType.DMA],
        compiler_params=_params(("arbitrary",)),
        name="moe_combine",
    )(dest, yb, x_mid, g2)


MOE_ROWS = 256


def _moe(x_mid, h_packed, logits_t, router_bias, w_gu, w_down, g2, seq):
    n = x_mid.shape[0]
    n_blocks = (n + N_CLASSES * (MOE_ROWS - 1) + MOE_ROWS - 1) // MOE_ROWS
    cap = n_blocks * MOE_ROWS
    route_i, gate_w, counts = _route(logits_t, router_bias)
    counts = counts[:N_CLASSES, 0]
    padded = ((counts + MOE_ROWS - 1) // MOE_ROWS) * MOE_ROWS
    pad_ends = jnp.cumsum(padded)
    pad_starts = pad_ends - padded
    dest = route_i[1:2]
    for c in range(N_CLASSES):
        dest = dest + jnp.where(route_i[0:1] == c, pad_starts[c], 0)
    blk_start = jnp.arange(n_blocks, dtype=jnp.int32) * MOE_ROWS
    block_c = jnp.minimum(jnp.sum(blk_start[:, None] >= pad_ends[None, :], axis=1), N_CLASSES - 1)
    n_used = (pad_ends[-1:] // MOE_ROWS).astype(jnp.int32)
    block_c = jnp.where(jnp.arange(n_blocks) < n_used[0], block_c, block_c[jnp.maximum(n_used[0] - 1, 0)])
    group_base = (block_c // PAIRS_PER_GROUP) * EXPERTS_PER_GROUP
    pair = block_c % PAIRS_PER_GROUP
    block_ea = (group_base + jnp.asarray(PAIR_LO, jnp.int32)[pair]).astype(jnp.int32)
    block_eb = (group_base + jnp.asarray(PAIR_HI, jnp.int32)[pair]).astype(jnp.int32)
    xb = _dispatch(dest, h_packed, gate_w, cap)
    yb = _experts(block_ea, block_eb, n_used, xb, w_gu, w_down, MOE_ROWS)
    return _combine(dest, yb, x_mid, g2, seq)


def kernel(x, c, positions, norm1, norm2, w_ada, b_ada, a_w_in, a_b_if, a_h_norm, a_w_out, b_w_in, b_q_norm, b_k_norm, b_lam_q1, b_lam_k1, b_lam_q2, b_lam_k2, b_o_norm, b_w_out, w_router, router_bias, moe_w_gu, moe_w_down):
    bsz, seq, d = x.shape
    depth = w_ada.shape[0]
    n = bsz * seq
    xf = x.reshape(n, d)
    mod = _ada_mod(c, w_ada, b_ada)
    w_router_t = w_router.T
    pos_row = positions.reshape(n)

    for l in range(depth):
        sh1, sc1, g1, sh2, sc2, g2 = [mod[l, :, i * d:(i + 1) * d].reshape(bsz, 1, d) for i in range(6)]
        j = l // 2
        if l % 2 == 0:
            w_in = a_w_in[j]
            w_main = w_in[:, :A_MAIN_COLS].astype(BF16)
            w_gate = jnp.pad(w_in[:, A_MAIN_COLS:], ((0, 0), (0, LANES - 2 * A_HEADS))).astype(BF16)
            proj, gates = _norm_matmul(xf, norm1[l].reshape(1, d), 1.0 + sc1, sh1, w_main, w_gate, seq)
            bias_row = jnp.pad(a_b_if[j], (0, LANES - 2 * A_HEADS)).reshape(1, LANES)
            mixed = _mlstm(proj, gates, bias_row, a_h_norm[j].reshape(1, A_HV), bsz, seq)
            w_out = a_w_out[j].astype(BF16)
        else:
            (proj,) = _norm_matmul(xf, norm1[l].reshape(1, d), 1.0 + sc1, sh1, b_w_in[j].astype(BF16), None, seq)
            qk = _qk_prep(proj, pos_row, b_q_norm[j], b_k_norm[j], seq)
            lam_tab = jnp.zeros((SUBLANES, LANES), F32)
            for r, v in enumerate((b_lam_q1[j], b_lam_k1[j], b_lam_q2[j], b_lam_k2[j])):
                lam_tab = lam_tab.at[r, :B_HEAD_DIM].set(v)
            lam_init = 0.8 - 0.6 * math.exp(-0.3 * l)
            mixed = _diff_attn(qk, proj, lam_tab, b_o_norm[j], bsz, seq, lam_init)
            w_out = b_w_out[j].astype(BF16)
        x_mid, h_packed, logits_t = _outproj_norm_router(
            mixed, w_out, xf, g1, norm2[l].reshape(1, d), 1.0 + sc2, sh2, w_router_t, seq)
        xf = _moe(x_mid, h_packed, logits_t, router_bias,
                  moe_w_gu[l].astype(BF16), moe_w_down[l].astype(BF16), g2, seq)
    return xf.reshape(bsz, seq, d)
```

```python
import functools
import math

import jax
import jax.numpy as jnp
from jax import lax
from jax.experimental import pallas as pl
from jax.experimental.pallas import tpu as pltpu

D_MODEL = 1024
A_HEADS = 4
A_QK_DIM = 128
A_V_DIM = 256
A_CHUNK = 128
A_HQ = A_HEADS * A_QK_DIM
A_HV = A_HEADS * A_V_DIM
A_MAIN_COLS = 2 * A_HQ + 2 * A_HV

B_HEADS = 8
B_HEAD_DIM = 64
B_V_DIM = 128
B_ROT_DIM = 16
ROPE_THETA = 500000.0

N_EXPERTS = 16
N_GROUPS = 4
EXPERTS_PER_GROUP = 4
TOP_K = 2
D_EXPERT = 512
EPS = 1e-6

LANES = 128
SUBLANES = 8
ROW_TILES = D_MODEL // LANES
assert ROW_TILES == SUBLANES
VMEM_LIMIT = 48 * 1024 * 1024

F32 = jnp.float32
BF16 = jnp.bfloat16
HIGHEST = lax.Precision.HIGHEST


def _params(sem):
    return pltpu.CompilerParams(dimension_semantics=sem, vmem_limit_bytes=VMEM_LIMIT)


def _tile(pref, n):
    t = min(pref, n)
    assert n % t == 0, (pref, n)
    return t


def _ada_kernel(c_ref, w_ref, b_ref, o_ref):
    c = c_ref[...]
    c_act = c * jax.nn.sigmoid(c)
    o_ref[0] = jnp.dot(c_act, w_ref[0], precision=HIGHEST, preferred_element_type=F32) + b_ref[0]


def _ada_mod(c, w_ada, b_ada):
    depth, d, six_d = w_ada.shape
    bsz = c.shape[0]
    tn = _tile(1536, six_d)
    return pl.pallas_call(
        _ada_kernel,
        grid=(depth, six_d // tn),
        in_specs=[
            pl.BlockSpec((bsz, d), lambda l, j: (0, 0)),
            pl.BlockSpec((1, d, tn), lambda l, j: (l, 0, j)),
            pl.BlockSpec((1, 1, tn), lambda l, j: (l, 0, j)),
        ],
        out_specs=pl.BlockSpec((1, bsz, tn), lambda l, j: (l, 0, j)),
        out_shape=jax.ShapeDtypeStruct((depth, bsz, six_d), F32),
        compiler_params=_params(("parallel", "parallel")),
        name="ada_mod",
    )(c, w_ada, b_ada.reshape(depth, 1, six_d))


def _modulated_rms(x, g, sc1p, sh):
    y = x * lax.rsqrt(jnp.mean(x * x, axis=-1, keepdims=True) + EPS)
    return (y * g) * sc1p + sh


def _norm_mm_kernel(x_ref, g_ref, sc_ref, sh_ref, w_ref, *rest, col_chunk, has_gates):
    if has_gates:
        wg_ref, o_ref, og_ref = rest
    else:
        (o_ref,) = rest
    hb = _modulated_rms(x_ref[...], g_ref[...], sc_ref[0], sh_ref[0]).astype(BF16)
    for c0 in range(0, o_ref.shape[1], col_chunk):
        o_ref[:, c0:c0 + col_chunk] = jnp.dot(
            hb, w_ref[:, c0:c0 + col_chunk], preferred_element_type=F32).astype(o_ref.dtype)
    if has_gates:
        og_ref[...] = jnp.dot(hb, wg_ref[...], preferred_element_type=F32)


def _norm_matmul(x, gain, sc1p, sh, w, wg, seq):
    n, d = x.shape
    cols = w.shape[1]
    tm = _tile(512, seq)
    per_b = seq // tm
    has_gates = wg is not None
    in_specs = [
        pl.BlockSpec((tm, d), lambda i: (i, 0)),
        pl.BlockSpec((1, d), lambda i: (0, 0)),
        pl.BlockSpec((1, 1, d), lambda i: (i // per_b, 0, 0)),
        pl.BlockSpec((1, 1, d), lambda i: (i // per_b, 0, 0)),
        pl.BlockSpec((d, cols), lambda i: (0, 0)),
    ]
    out_specs = [pl.BlockSpec((tm, cols), lambda i: (i, 0))]
    out_shape = [jax.ShapeDtypeStruct((n, cols), BF16)]
    args = [x, gain, sc1p, sh, w]
    if has_gates:
        in_specs.append(pl.BlockSpec((d, LANES), lambda i: (0, 0)))
        out_specs.append(pl.BlockSpec((tm, LANES), lambda i: (i, 0)))
        out_shape.append(jax.ShapeDtypeStruct((n, LANES), F32))
        args.append(wg)
    return pl.pallas_call(
        functools.partial(_norm_mm_kernel, col_chunk=512, has_gates=has_gates),
        grid=(n // tm,),
        in_specs=in_specs,
        out_specs=out_specs,
        out_shape=out_shape,
        compiler_params=_params(("parallel",)),
        name="norm_inproj",
    )(*args)


def _log_sigmoid(x):
    return jnp.minimum(x, 0.0) - jnp.log1p(jnp.exp(-jnp.abs(x)))


def _lane_cumsum(x):
    lane = lax.broadcasted_iota(jnp.int32, x.shape, 1)
    sh = 1
    while sh < x.shape[1]:
        x = x + jnp.where(lane >= sh, pltpu.roll(x, sh, axis=1), 0.0)
        sh *= 2
    return x


def _mlstm_kernel(p_ref, g_ref, bias_ref, hn_ref, o_ref, c_ref, n_ref, m_ref, r_ref):
    L = A_CHUNK

    @pl.when(pl.program_id(1) == 0)
    def _():
        c_ref[...] = jnp.zeros_like(c_ref)
        n_ref[...] = jnp.zeros_like(n_ref)
        m_ref[...] = jnp.zeros_like(m_ref)
        r_ref[...] = jnp.zeros_like(r_ref)

    gates = g_ref[...] + bias_ref[...]
    g8 = gates.T[0:SUBLANES]
    bc8 = _lane_cumsum(_log_sigmoid(g8))
    b_rows, i_rows, lw_rows, b_lasts = [], [], [], []
    for h in range(A_HEADS):
        b_row = bc8[A_HEADS + h:A_HEADS + h + 1]
        i_row = g8[h:h + 1]
        b_last = b_row[:, L - 1:L]
        lw_row = b_last - b_row + i_row
        r_ref[h:h + 1, :] = b_row
        r_ref[A_HEADS + h:A_HEADS + h + 1, :] = lw_row
        b_rows.append(b_row)
        i_rows.append(i_row)
        lw_rows.append(lw_row)
        b_lasts.append(b_last)
    cols = r_ref[...].T

    row_id = lax.broadcasted_iota(jnp.int32, (L, L), 0)
    col_id = lax.broadcasted_iota(jnp.int32, (L, L), 1)
    causal = row_id >= col_id

    for h in range(A_HEADS):
        q = p_ref[:, h * A_QK_DIM:(h + 1) * A_QK_DIM]
        ksf = p_ref[:, A_HQ + h * A_QK_DIM:A_HQ + (h + 1) * A_QK_DIM].astype(F32) * (A_QK_DIM ** -0.5)
        v = p_ref[:, 2 * A_HQ + h * A_V_DIM:2 * A_HQ + (h + 1) * A_V_DIM]
        og = p_ref[:, 2 * A_HQ + A_HV + h * A_V_DIM:2 * A_HQ + A_HV + (h + 1) * A_V_DIM].astype(F32)
        b_col = cols[:, h:h + 1]
        lw_col = cols[:, A_HEADS + h:A_HEADS + h + 1]
        m11 = m_ref[h][:, 0:1]
        c_old = c_ref[h]
        n_old = n_ref[h]

        log_d = jnp.where(causal, b_col - b_rows[h] + i_rows[h], -jnp.inf)
        log_inter = b_col + m11
        m_t = jnp.maximum(jnp.max(log_d, axis=1, keepdims=True), log_inter)
        dmat = jnp.exp(log_d - m_t)
        inter = jnp.exp(log_inter - m_t)
        s = lax.dot_general(q, ksf.astype(BF16), (((1,), (1,)), ((), ())), preferred_element_type=F32) * dmat
        num = (jnp.dot(s.astype(BF16), v, preferred_element_type=F32)
               + inter * jnp.dot(q, c_old.astype(BF16), preferred_element_type=F32))
        den = (jnp.sum(s, axis=1, keepdims=True)
               + inter * jnp.sum(q.astype(F32) * n_old, axis=1, keepdims=True))
        hh = num / jnp.maximum(jnp.abs(den), jnp.exp(-m_t))

        m_new = jnp.maximum(b_lasts[h] + m11, jnp.max(lw_rows[h], axis=1, keepdims=True))
        w_col = jnp.exp(lw_col - m_new)
        decay = jnp.exp(b_lasts[h] + m11 - m_new)
        kw = ksf * w_col
        c_ref[h] = decay * c_old + lax.dot_general(
            kw.astype(BF16), v, (((0,), (0,)), ((), ())), preferred_element_type=F32)
        n_ref[h] = decay * n_old + jnp.sum(kw, axis=0, keepdims=True)
        m_ref[h] = jnp.broadcast_to(m_new, (1, LANES))

        hn = hh * lax.rsqrt(jnp.mean(hh * hh, axis=1, keepdims=True) + EPS)
        hn = hn * hn_ref[:, h * A_V_DIM:(h + 1) * A_V_DIM]
        o_ref[:, h * A_V_DIM:(h + 1) * A_V_DIM] = (hn * jax.nn.sigmoid(og)).astype(o_ref.dtype)


def _mlstm(proj, gates, bias_row, h_norm_row, bsz, seq):
    n = proj.shape[0]
    nc = seq // A_CHUNK
    return pl.pallas_call(
        _mlstm_kernel,
        grid=(bsz, nc),
        in_specs=[
            pl.BlockSpec((A_CHUNK, A_MAIN_COLS), lambda b, c: (b * nc + c, 0)),
            pl.BlockSpec((A_CHUNK, LANES), lambda b, c: (b * nc + c, 0)),
            pl.BlockSpec((1, LANES), lambda b, c: (0, 0)),
            pl.BlockSpec((1, A_HV), lambda b, c: (0, 0)),
        ],
        out_specs=pl.BlockSpec((A_CHUNK, A_HV), lambda b, c: (b * nc + c, 0)),
        out_shape=jax.ShapeDtypeStruct((n, A_HV), BF16),
        scratch_shapes=[
            pltpu.VMEM((A_HEADS, A_QK_DIM, A_V_DIM), F32),
            pltpu.VMEM((A_HEADS, 1, A_QK_DIM), F32),
            pltpu.VMEM((A_HEADS, 1, LANES), F32),
            pltpu.VMEM((A_CHUNK, A_CHUNK), F32),
        ],
        compiler_params=_params(("parallel", "arbitrary")),
        name="mlstm",
    )(proj, gates, bias_row, h_norm_row)


def _qk_prep_kernel(p_ref, pos_ref, seg_ref, segt_ref, gain_ref, freq_ref, o_ref):
    tm = p_ref.shape[0]
    x = p_ref[...].astype(F32)
    xx = x * x
    xx_hi = xx.astype(BF16)
    xx_lo = (xx - xx_hi.astype(F32)).astype(BF16)
    seg = seg_ref[...]
    ss = jnp.dot(xx_hi, seg, preferred_element_type=F32) + jnp.dot(xx_lo, seg, preferred_element_type=F32)
    r = lax.rsqrt(ss * (1.0 / B_HEAD_DIM) + EPS)
    r_hi = r.astype(BF16)
    r_lo = (r - r_hi.astype(F32)).astype(BF16)
    segt = segt_ref[...]
    r_full = jnp.dot(r_hi, segt, preferred_element_type=F32) + jnp.dot(r_lo, segt, preferred_element_type=F32)

    pos = jnp.broadcast_to(pos_ref[0].astype(F32), (LANES, tm)).T
    ang = pos * freq_ref[0:1, :]
    rot_lo = freq_ref[1:2, :]
    rot_hi = freq_ref[2:3, :]
    cosv = jnp.cos(ang)
    sinv = jnp.sin(ang)
    c_mul = jnp.where(rot_lo + rot_hi > 0.0, cosv, 1.0)
    s_lo = -sinv * rot_lo
    s_hi = sinv * rot_hi
    half = B_ROT_DIM // 2
    for j in range(2 * D_MODEL // LANES):
        sl = slice(j * LANES, (j + 1) * LANES)
        y = x[:, sl] * r_full[:, sl] * gain_ref[:, sl]
        out = y * c_mul + pltpu.roll(y, LANES - half, axis=1) * s_lo + pltpu.roll(y, half, axis=1) * s_hi
        if j < D_MODEL // LANES:
            out = out * (B_HEAD_DIM ** -0.5 * math.log2(math.e))
        o_ref[:, sl] = out.astype(o_ref.dtype)


def _qk_prep(proj, positions_row, q_norm, k_norm, seq):
    n = proj.shape[0]
    two_d = 2 * D_MODEL
    tm = _tile(256, seq)
    n_seg = two_d // B_HEAD_DIM
    seg = (jnp.arange(two_d)[:, None] // B_HEAD_DIM == jnp.arange(LANES)[None, :]).astype(BF16)
    gain = jnp.concatenate([jnp.tile(q_norm, D_MODEL // B_HEAD_DIM), jnp.tile(k_norm, D_MODEL // B_HEAD_DIM)])
    lane = jnp.arange(LANES)
    in_seg = lane % B_HEAD_DIM
    inv_freq = ROPE_THETA ** (-jnp.arange(0, B_ROT_DIM, 2, dtype=F32) / B_ROT_DIM)
    half = B_ROT_DIM // 2
    freq = jnp.where(in_seg < B_ROT_DIM, inv_freq[in_seg % half], 0.0)
    freq_tab = jnp.zeros((SUBLANES, LANES), F32)
    freq_tab = freq_tab.at[0].set(freq).at[1].set((in_seg < half).astype(F32))
    freq_tab = freq_tab.at[2].set(((in_seg >= half) & (in_seg < B_ROT_DIM)).astype(F32))
    del n_seg
    return pl.pallas_call(
        _qk_prep_kernel,
        grid=(n // tm,),
        in_specs=[
            pl.BlockSpec((tm, two_d), lambda i: (i, 0)),
            pl.BlockSpec((1, 1, tm), lambda i: (i, 0, 0)),
            pl.BlockSpec((two_d, LANES), lambda i: (0, 0)),
            pl.BlockSpec((LANES, two_d), lambda i: (0, 0)),
            pl.BlockSpec((1, two_d), lambda i: (0, 0)),
            pl.BlockSpec((SUBLANES, LANES), lambda i: (0, 0)),
        ],
        out_specs=pl.BlockSpec((tm, two_d), lambda i: (i, 0)),
        out_shape=jax.ShapeDtypeStruct((n, two_d), BF16),
        compiler_params=_params(("parallel",)),
        name="qk_prep",
    )(proj, positions_row.reshape(n // tm, 1, tm), seg, seg.T, gain.reshape(1, two_d), freq_tab)


ATTN_TILE = 512


def _diff_attn_kernel(q_ref, k_ref, v_ref, lam_ref, on_ref, o_ref, vt_ref, qm_ref, m_ref, l_ref, acc_ref,
                      *, tile, lam_init):
    seq = q_ref.shape[0]
    n_tiles = seq // tile
    lamv = lam_ref[...]
    lam = (jnp.exp(jnp.sum(lamv[0:1] * lamv[1:2], axis=1, keepdims=True))
           - jnp.exp(jnp.sum(lamv[2:3] * lamv[3:4], axis=1, keepdims=True)) + lam_init)
    tchunk = min(256, seq)
    for c in range(seq // tchunk):
        vt_ref[:, c * tchunk:(c + 1) * tchunk] = v_ref[c * tchunk:(c + 1) * tchunk, :].astype(F32).T.astype(BF16)
    first_map = lax.broadcasted_iota(jnp.int32, (tile, LANES), 1) < B_HEAD_DIM
    key_id = lax.broadcasted_iota(jnp.int32, (tile, tile), 0)
    query_id = lax.broadcasted_iota(jnp.int32, (tile, tile), 1)
    causal = query_id >= key_id
    nt = (((1,), (1,)), ((), ()))

    def tile_step(k, vt, mask):
        scores = [lax.dot_general(k, qm_ref[c], nt, preferred_element_type=F32) for c in range(2)]
        for c, s in enumerate(scores):
            if mask is not None:
                s = jnp.where(mask, s, -jnp.inf)
            m_old = m_ref[c]
            m_new = jnp.maximum(m_old, jnp.max(s, axis=0, keepdims=True))
            alpha = jnp.exp2(m_old - m_new)
            p = jnp.exp2(s - m_new)
            l_ref[c] = alpha * l_ref[c] + jnp.sum(p, axis=0, keepdims=True)
            acc_ref[c] = alpha * acc_ref[c] + jnp.dot(vt, p.astype(BF16), preferred_element_type=F32)
            m_ref[c] = m_new

    for qi in range(n_tiles):
        q = q_ref[qi * tile:(qi + 1) * tile, :]
        qm_ref[0] = jnp.where(first_map, q, jnp.zeros_like(q))
        qm_ref[1] = jnp.where(first_map, jnp.zeros_like(q), q)
        m_ref[...] = jnp.full(m_ref.shape, -jnp.inf, F32)
        l_ref[...] = jnp.zeros(l_ref.shape, F32)
        acc_ref[...] = jnp.zeros(acc_ref.shape, F32)
        for ki in range(qi + 1):
            tile_step(k_ref[ki * tile:(ki + 1) * tile, :], vt_ref[:, ki * tile:(ki + 1) * tile],
                      causal if ki == qi else None)
        o = acc_ref[0] / l_ref[0] - lam * (acc_ref[1] / l_ref[1])
        o = o * lax.rsqrt(jnp.mean(o * o, axis=0, keepdims=True) + EPS) * on_ref[...] * (1.0 - lam_init)
        o_ref[qi * tile:(qi + 1) * tile, :] = o.T.astype(o_ref.dtype)


def _diff_attn(qk, proj, lam_tab, o_norm, bsz, seq, lam_init):
    n = qk.shape[0]
    tile = _tile(ATTN_TILE, seq)
    o_norm_cols = jnp.broadcast_to(o_norm[:, None], (B_V_DIM, tile))
    return pl.pallas_call(
        functools.partial(_diff_attn_kernel, tile=tile, lam_init=lam_init),
        grid=(bsz, B_HEADS),
        in_specs=[
            pl.BlockSpec((seq, LANES), lambda b, h: (b, h)),
            pl.BlockSpec((seq, LANES), lambda b, h: (b, B_HEADS + h)),
            pl.BlockSpec((seq, B_V_DIM), lambda b, h: (b, 2 * B_HEADS + h)),
            pl.BlockSpec((SUBLANES, LANES), lambda b, h: (0, 0)),
            pl.BlockSpec((B_V_DIM, tile), lambda b, h: (0, 0)),
        ],
        out_specs=pl.BlockSpec((seq, B_V_DIM), lambda b, h: (b, h)),
        out_shape=jax.ShapeDtypeStruct((n, D_MODEL), BF16),
        scratch_shapes=[
            pltpu.VMEM((B_V_DIM, seq), BF16),
            pltpu.VMEM((2, tile, LANES), BF16),
            pltpu.VMEM((2, 1, tile), F32),
            pltpu.VMEM((2, 1, tile), F32),
            pltpu.VMEM((2, B_V_DIM, tile), F32),
        ],
        compiler_params=_params(("parallel", "parallel")),
        name="diff_attn",
    )(qk, qk, proj, lam_tab, o_norm_cols)


HALF_D = D_MODEL // 2
HI16 = 0xFFFF0000
PACKED_TILES = HALF_D // LANES


def _pack_bf16_pairs(v):
    bits = lax.bitcast_convert_type(v.astype(BF16).astype(F32), jnp.uint32)
    return (bits[:, HALF_D:] & jnp.uint32(HI16)) | (bits[:, :HALF_D] >> 16)


def _unpack_bf16_pairs(words):
    first = lax.bitcast_convert_type(words << 16, F32)
    second = lax.bitcast_convert_type(words & jnp.uint32(HI16), F32)
    return first.astype(BF16), second.astype(BF16)


def _outproj_kernel(a_ref, w_ref, x_ref, g1_ref, gain_ref, sc_ref, sh_ref, wr_ref, xo_ref, hp_ref, lg_ref):
    y = jnp.dot(a_ref[...], w_ref[...], preferred_element_type=F32)
    xm = x_ref[...] + g1_ref[0] * y
    xo_ref[...] = xm
    h2 = _modulated_rms(xm, gain_ref[...], sc_ref[0], sh_ref[0])
    tm = xm.shape[0]
    packed = _pack_bf16_pairs(h2)
    for j in range(ROW_TILES):
        hp_ref[pl.ds(j, tm, stride=ROW_TILES), :] = (
            packed[:, j * LANES:(j + 1) * LANES] if j < PACKED_TILES else jnp.zeros((tm, LANES), jnp.uint32))
    lg_ref[...] = lax.dot_general(wr_ref[...], h2, (((1,), (1,)), ((), ())),
                                  precision=HIGHEST, preferred_element_type=F32)


def _outproj_norm_router(a, w, x, g1, gain, sc1p, sh, w_router_t, seq):
    n, d = x.shape
    tm = _tile(512, seq)
    per_b = seq // tm
    bmap = lambda i: (i // per_b, 0, 0)
    return pl.pallas_call(
        _outproj_kernel,
        grid=(n // tm,),
        in_specs=[
            pl.BlockSpec((tm, a.shape[1]), lambda i: (i, 0)),
            pl.BlockSpec(w.shape, lambda i: (0, 0)),
            pl.BlockSpec((tm, d), lambda i: (i, 0)),
            pl.BlockSpec((1, 1, d), bmap),
            pl.BlockSpec((1, d), lambda i: (0, 0)),
            pl.BlockSpec((1, 1, d), bmap),
            pl.BlockSpec((1, 1, d), bmap),
            pl.BlockSpec((N_EXPERTS, d), lambda i: (0, 0)),
        ],
        out_specs=[
            pl.BlockSpec((tm, d), lambda i: (i, 0)),
            pl.BlockSpec((tm * ROW_TILES, LANES), lambda i: (i, 0)),
            pl.BlockSpec((N_EXPERTS, tm), lambda i: (0, i)),
        ],
        out_shape=[
            jax.ShapeDtypeStruct((n, d), F32),
            jax.ShapeDtypeStruct((n * ROW_TILES, LANES), jnp.uint32),
            jax.ShapeDtypeStruct((N_EXPERTS, n), F32),
        ],
        compiler_params=_params(("parallel",)),
        name="outproj_norm_router",
    )(a, w, x, g1, gain, sc1p, sh, w_router_t)


PAIR_LO = (0, 0, 0, 1, 1, 2)
PAIR_HI = (1, 2, 3, 2, 3, 3)
PAIRS_PER_GROUP = len(PAIR_LO)
N_CLASSES = N_GROUPS * PAIRS_PER_GROUP
CLASS_ROWS = 32


def _route_kernel(lg_ref, bias_ref, oi_ref, cnt_ref, carry_ref, tri_ref):
    tr = lg_ref.shape[1]
    step = pl.program_id(0)

    @pl.when(step == 0)
    def _():
        carry_ref[...] = jnp.zeros_like(carry_ref)
        r = lax.broadcasted_iota(jnp.int32, (tr, tr), 0)
        c = lax.broadcasted_iota(jnp.int32, (tr, tr), 1)
        tri_ref[...] = jnp.where(r < c, 1.0, 0.0).astype(BF16)

    scores = jax.nn.sigmoid(lg_ref[...])
    biased = scores + bias_ref[...]
    rows = [biased[e:e + 1, :] for e in range(N_EXPERTS)]

    def top2_sum(a, b, c, d):
        m1, n1 = jnp.maximum(a, b), jnp.minimum(a, b)
        m2, n2 = jnp.maximum(c, d), jnp.minimum(c, d)
        return jnp.maximum(m1, m2) + jnp.maximum(jnp.minimum(m1, m2), jnp.maximum(n1, n2))

    gscore = [top2_sum(*rows[g * EXPERTS_PER_GROUP:(g + 1) * EXPERTS_PER_GROUP]) for g in range(N_GROUPS)]
    best = gscore[0]
    gsel = jnp.zeros_like(best, dtype=jnp.int32)
    for g in range(1, N_GROUPS):
        upd = gscore[g] > best
        gsel = jnp.where(upd, g, gsel)
        best = jnp.where(upd, gscore[g], best)

    def pick(table, j):
        out = table[j]
        for g in range(1, N_GROUPS):
            out = jnp.where(gsel == g, table[g * EXPERTS_PER_GROUP + j], out)
        return out

    in_b = [pick(rows, j) for j in range(EXPERTS_PER_GROUP)]
    v1, i1 = in_b[0], jnp.zeros_like(gsel)
    for j in range(1, EXPERTS_PER_GROUP):
        upd = in_b[j] > v1
        v1 = jnp.where(upd, in_b[j], v1)
        i1 = jnp.where(upd, j, i1)
    v2 = jnp.full_like(v1, -jnp.inf)
    i2 = jnp.zeros_like(gsel)
    for j in range(EXPERTS_PER_GROUP):
        upd = (i1 != j) & (in_b[j] > v2)
        v2 = jnp.where(upd, in_b[j], v2)
        i2 = jnp.where(upd, j, i2)
    first_is_lo = i1 < i2
    lo = jnp.where(first_is_lo, i1, i2)
    hi = jnp.where(first_is_lo, i2, i1)
    pair = jnp.where(lo == 0, hi - 1, jnp.where(lo == 1, hi + 1, PAIRS_PER_GROUP - 1))
    cls = gsel * PAIRS_PER_GROUP + pair

    cid = lax.broadcasted_iota(jnp.int32, (CLASS_ROWS, tr), 0)
    onehot = jnp.where(cid == cls, 1.0, 0.0)
    before = jnp.dot(onehot.astype(BF16), tri_ref[...], preferred_element_type=F32) + carry_ref[...]
    oi_ref[0:1, :] = cls
    oi_ref[1:2, :] = jnp.sum(onehot * before, axis=0, keepdims=True).astype(jnp.int32)
    new_carry = carry_ref[...] + jnp.sum(onehot, axis=1, keepdims=True)
    carry_ref[...] = new_carry
    cnt_ref[...] = new_carry.astype(jnp.int32)


def _route(logits_t, router_bias):
    n = logits_t.shape[1]
    tr = _tile(512, n)
    return pl.pallas_call(
        _route_kernel,
        grid=(n // tr,),
        in_specs=[
            pl.BlockSpec((N_EXPERTS, tr), lambda i: (0, i)),
            pl.BlockSpec((N_EXPERTS, 1), lambda i: (0, 0)),
        ],
        out_specs=[
            pl.BlockSpec((2, tr), lambda i: (0, i)),
            pl.BlockSpec((CLASS_ROWS, 1), lambda i: (0, 0)),
        ],
        out_shape=[
            jax.ShapeDtypeStruct((2, n), jnp.int32),
            jax.ShapeDtypeStruct((CLASS_ROWS, 1), jnp.int32),
        ],
        scratch_shapes=[pltpu.VMEM((CLASS_ROWS, 1), F32), pltpu.VMEM((tr, tr), BF16)],
        compiler_params=_params(("arbitrary",)),
        name="route",
    )(logits_t, router_bias.reshape(N_EXPERTS, 1).astype(F32))


SORT_UNROLL = 8


def _sort_order_kernel(pos_ref, tok_ref):
    def body(i, c):
        for u in range(SORT_UNROLL):
            t = i * SORT_UNROLL + u
            tok_ref[pos_ref[t]] = t
        return c

    lax.fori_loop(0, pos_ref.shape[0] // SORT_UNROLL, body, 0)


def _sort_order(pos):
    n = pos.shape[0]
    assert n % SORT_UNROLL == 0
    return pl.pallas_call(
        _sort_order_kernel,
        in_specs=[pl.BlockSpec(memory_space=pltpu.SMEM)],
        out_specs=pl.BlockSpec(memory_space=pltpu.SMEM),
        out_shape=jax.ShapeDtypeStruct((n,), jnp.int32),
        name="moe_sort_order",
    )(pos)


ROW_UNROLL = 8


def _for_rows(n, fn):
    groups = n // ROW_UNROLL

    def group(g, c):
        for u in range(ROW_UNROLL):
            fn(g * ROW_UNROLL + u)
        return c

    def single(r, c):
        fn(r)
        return c

    lax.fori_loop(0, groups, group, 0)
    lax.fori_loop(groups * ROW_UNROLL, n, single, 0)


def _row_copy(src_ref, src_row, dst_ref, dst_row, sem):
    return pltpu.make_async_copy(
        src_ref.at[pl.ds(pl.multiple_of(src_row * ROW_TILES, ROW_TILES), ROW_TILES)],
        dst_ref.at[pl.ds(pl.multiple_of(dst_row * ROW_TILES, ROW_TILES), ROW_TILES)],
        sem)


def _expert_kernel(ea_ref, eb_ref, off_ref, nv_ref, nu_ref, tok_ref,
                   h_hbm, wgu_a_ref, wd_a_ref, wgu_b_ref, wd_b_ref, wr_a_ref, wr_b_ref, y_hbm,
                   xbuf, ybuf, gsem, ssem):
    del ea_ref, eb_ref
    tb = xbuf.shape[1] // ROW_TILES
    i = pl.program_id(0)
    slot = i % 2
    n_used = nu_ref[0]

    def start_gather(block, s):
        base = off_ref[block]
        _for_rows(nv_ref[block],
                  lambda r: _row_copy(h_hbm, tok_ref[base + r], xbuf.at[s], r, gsem.at[s]).start())

    def start_scatter(block, s):
        base = off_ref[block]
        _for_rows(nv_ref[block],
                  lambda r: _row_copy(ybuf.at[s], r, y_hbm, tok_ref[base + r], ssem.at[s]).start())

    def wait_rows(block, hbm, buf, sem):
        rows = nv_ref[block] * ROW_TILES

        @pl.when(rows > 0)
        def _():
            pltpu.make_async_copy(hbm.at[pl.ds(0, rows)], buf.at[pl.ds(0, rows)], sem).wait()

    @pl.when(i == 0)
    def _():
        xbuf[...] = jnp.zeros_like(xbuf)
        start_gather(0, 0)

    @pl.when(i < n_used)
    def _():
        @pl.when(i + 1 < n_used)
        def _():
            start_gather(i + 1, 1 - slot)

        wait_rows(i, h_hbm, xbuf.at[slot], gsem.at[slot])

        @pl.when(i >= 2)
        def _():
            wait_rows(i - 2, y_hbm, ybuf.at[slot], ssem.at[slot])

        words = [xbuf[slot, pl.ds(j, tb, stride=ROW_TILES), :] for j in range(PACKED_TILES)]
        halves = [_unpack_bf16_pairs(w) for w in words]
        x = jnp.concatenate([h[0] for h in halves] + [h[1] for h in halves], axis=1)

        def mlp(wgu_ref, wd_ref):
            gu = jnp.dot(x, wgu_ref[0], preferred_element_type=F32)
            gate = gu[:, :D_EXPERT]
            act = gate * jax.nn.sigmoid(gate) * gu[:, D_EXPERT:]
            return jnp.dot(act.astype(BF16), wd_ref[0], preferred_element_type=F32)

        score_a = jax.nn.sigmoid(jnp.dot(x, wr_a_ref[0], preferred_element_type=F32))
        score_b = jax.nn.sigmoid(jnp.dot(x, wr_b_ref[0], preferred_element_type=F32))
        inv_tot = 1.0 / (score_a + score_b)
        w_a = score_a * inv_tot
        w_b = score_b * inv_tot
        ya = mlp(wgu_a_ref, wd_a_ref)
        yb = mlp(wgu_b_ref, wd_b_ref)
        for j in range(ROW_TILES):
            sl = slice(j * LANES, (j + 1) * LANES)
            ybuf[slot, pl.ds(j, tb, stride=ROW_TILES), :] = w_a * ya[:, sl] + w_b * yb[:, sl]
        start_scatter(i, slot)

        @pl.when(i == n_used - 1)
        def _():
            wait_rows(i, y_hbm, ybuf.at[slot], ssem.at[slot])

            @pl.when(i >= 1)
            def _():
                wait_rows(i - 1, y_hbm, ybuf.at[1 - slot], ssem.at[1 - slot])


def _experts(tables, sorted_tok, h_rows, w_gu, w_down, wr_bcast, tb):
    block_ea, block_eb, src_off, n_valid, n_used = tables
    n_blocks = block_ea.shape[0]
    d, two_f = w_gu.shape[1:]
    ea_map = lambda i, ea, eb, off, nv, nu, tok: (ea[i], 0, 0)
    eb_map = lambda i, ea, eb, off, nv, nu, tok: (eb[i], 0, 0)
    grid_spec = pltpu.PrefetchScalarGridSpec(
        num_scalar_prefetch=6,
        grid=(n_blocks,),
        in_specs=[
            pl.BlockSpec(memory_space=pl.ANY),
            pl.BlockSpec((1, d, two_f), ea_map),
            pl.BlockSpec((1, two_f // 2, d), ea_map),
            pl.BlockSpec((1, d, two_f), eb_map),
            pl.BlockSpec((1, two_f // 2, d), eb_map),
            pl.BlockSpec((1, d, LANES), ea_map),
            pl.BlockSpec((1, d, LANES), eb_map),
        ],
        out_specs=pl.BlockSpec(memory_space=pl.ANY),
        scratch_shapes=[
            pltpu.VMEM((2, tb * ROW_TILES, LANES), jnp.uint32),
            pltpu.VMEM((2, tb * ROW_TILES, LANES), F32),
            pltpu.SemaphoreType.DMA((2,)),
            pltpu.SemaphoreType.DMA((2,)),
        ],
    )
    return pl.pallas_call(
        _expert_kernel,
        grid_spec=grid_spec,
        out_shape=jax.ShapeDtypeStruct(h_rows.shape, F32),
        compiler_params=_params(("arbitrary",)),
        name="moe_experts",
    )(block_ea, block_eb, src_off, n_valid, n_used, sorted_tok, h_rows, w_gu, w_down, w_gu, w_down, wr_bcast, wr_bcast)


def _residual_kernel(y_ref, x_ref, g2_ref, o_ref):
    tc = x_ref.shape[0]
    g2 = g2_ref[0]
    for j in range(ROW_TILES):
        sl = slice(j * LANES, (j + 1) * LANES)
        o_ref[:, sl] = x_ref[:, sl] + g2[:, sl] * y_ref[pl.ds(j, tc, stride=ROW_TILES), :]


def _residual(y_rows, x_mid, g2, seq):
    n, d = x_mid.shape
    tc = _tile(512, seq)
    per_b = seq // tc
    return pl.pallas_call(
        _residual_kernel,
        grid=(n // tc,),
        in_specs=[
            pl.BlockSpec((tc * ROW_TILES, LANES), lambda i: (i, 0)),
            pl.BlockSpec((tc, d), lambda i: (i, 0)),
            pl.BlockSpec((1, 1, d), lambda i: (i // per_b, 0, 0)),
        ],
        out_specs=pl.BlockSpec((tc, d), lambda i: (i, 0)),
        out_shape=jax.ShapeDtypeStruct((n, d), F32),
        compiler_params=_params(("parallel",)),
        name="moe_residual",
    )(y_rows, x_mid, g2)


MOE_ROWS = 256


def _moe(x_mid, h_rows, logits_t, router_bias, w_gu, w_down, wr_bcast, g2, seq):
    n = x_mid.shape[0]
    n_blocks = (n + N_CLASSES * (MOE_ROWS - 1) + MOE_ROWS - 1) // MOE_ROWS
    route_i, counts = _route(logits_t, router_bias)
    counts = counts[:N_CLASSES, 0]
    starts = jnp.cumsum(counts) - counts
    padded = ((counts + MOE_ROWS - 1) // MOE_ROWS) * MOE_ROWS
    pad_ends = jnp.cumsum(padded)
    pad_starts = pad_ends - padded
    pos = route_i[1]
    for c in range(N_CLASSES):
        pos = pos + jnp.where(route_i[0] == c, starts[c], 0)
    sorted_tok = _sort_order(pos)
    blk_start = jnp.arange(n_blocks, dtype=jnp.int32) * MOE_ROWS
    n_used = (pad_ends[-1:] // MOE_ROWS).astype(jnp.int32)
    used = jnp.arange(n_blocks) < n_used[0]
    block_c = jnp.minimum(jnp.sum(blk_start[:, None] >= pad_ends[None, :], axis=1), N_CLASSES - 1)
    block_c = jnp.where(used, block_c, block_c[jnp.maximum(n_used[0] - 1, 0)])
    in_class = blk_start - pad_starts[block_c]
    src_off = jnp.where(used, starts[block_c] + in_class, 0).astype(jnp.int32)
    n_valid = jnp.where(used, jnp.clip(counts[block_c] - in_class, 0, MOE_ROWS), 0).astype(jnp.int32)
    group_base = (block_c // PAIRS_PER_GROUP) * EXPERTS_PER_GROUP
    pair = block_c % PAIRS_PER_GROUP
    block_ea = (group_base + jnp.asarray(PAIR_LO, jnp.int32)[pair]).astype(jnp.int32)
    block_eb = (group_base + jnp.asarray(PAIR_HI, jnp.int32)[pair]).astype(jnp.int32)
    y_rows = _experts((block_ea, block_eb, src_off, n_valid, n_used), sorted_tok, h_rows, w_gu, w_down, wr_bcast,
                      MOE_ROWS)
    return _residual(y_rows, x_mid, g2, seq)


def kernel(x, c, positions, norm1, norm2, w_ada, b_ada, a_w_in, a_b_if, a_h_norm, a_w_out, b_w_in, b_q_norm, b_k_norm, b_lam_q1, b_lam_k1, b_lam_q2, b_lam_k2, b_o_norm, b_w_out, w_router, router_bias, moe_w_gu, moe_w_down):
    bsz, seq, d = x.shape
    depth = w_ada.shape[0]
    n = bsz * seq
    xf = x.reshape(n, d)
    mod = _ada_mod(c, w_ada, b_ada)
    w_router_t = w_router.T
    wr_bcast = jnp.broadcast_to(w_router_t[:, :, None], (N_EXPERTS, d, LANES)).astype(BF16)
    pos_row = positions.reshape(n)

    for l in range(depth):
        sh1, sc1, g1, sh2, sc2, g2 = [mod[l, :, i * d:(i + 1) * d].reshape(bsz, 1, d) for i in range(6)]
        j = l // 2
        if l % 2 == 0:
            w_in = a_w_in[j]
            w_main = w_in[:, :A_MAIN_COLS].astype(BF16)
            w_gate = jnp.pad(w_in[:, A_MAIN_COLS:], ((0, 0), (0, LANES - 2 * A_HEADS))).astype(BF16)
            proj, gates = _norm_matmul(xf, norm1[l].reshape(1, d), 1.0 + sc1, sh1, w_main, w_gate, seq)
            bias_row = jnp.pad(a_b_if[j], (0, LANES - 2 * A_HEADS)).reshape(1, LANES)
            mixed = _mlstm(proj, gates, bias_row, a_h_norm[j].reshape(1, A_HV), bsz, seq)
            w_out = a_w_out[j].astype(BF16)
        else:
            (proj,) = _norm_matmul(xf, norm1[l].reshape(1, d), 1.0 + sc1, sh1, b_w_in[j].astype(BF16), None, seq)
            qk = _qk_prep(proj, pos_row, b_q_norm[j], b_k_norm[j], seq)
            lam_tab = jnp.zeros((SUBLANES, LANES), F32)
            for r, v in enumerate((b_lam_q1[j], b_lam_k1[j], b_lam_q2[j], b_lam_k2[j])):
                lam_tab = lam_tab.at[r, :B_HEAD_DIM].set(v)
            lam_init = 0.8 - 0.6 * math.exp(-0.3 * l)
            mixed = _diff_attn(qk, proj, lam_tab, b_o_norm[j], bsz, seq, lam_init)
            w_out = b_w_out[j].astype(BF16)
        x_mid, h_rows, logits_t = _outproj_norm_router(
            mixed, w_out, xf, g1, norm2[l].reshape(1, d), 1.0 + sc2, sh2, w_router_t, seq)
        xf = _moe(x_mid, h_rows, logits_t, router_bias,
                  moe_w_gu[l].astype(BF16), moe_w_down[l].astype(BF16), wr_bcast, g2, seq)
    return xf.reshape(bsz, seq, d)
```

```python
import functools
import math

import jax
import jax.numpy as jnp
from jax import lax
from jax.experimental import pallas as pl
from jax.experimental.pallas import tpu as pltpu

D_MODEL = 1024
A_HEADS = 4
A_QK_DIM = 128
A_V_DIM = 256
A_CHUNK = 128
A_HQ = A_HEADS * A_QK_DIM
A_HV = A_HEADS * A_V_DIM
A_MAIN_COLS = 2 * A_HQ + 2 * A_HV

B_HEADS = 8
B_HEAD_DIM = 64
B_V_DIM = 128
B_ROT_DIM = 16
ROPE_THETA = 500000.0

N_EXPERTS = 16
N_GROUPS = 4
EXPERTS_PER_GROUP = 4
TOP_K = 2
D_EXPERT = 512
EPS = 1e-6

LANES = 128
SUBLANES = 8
ROW_TILES = D_MODEL // LANES
assert ROW_TILES == SUBLANES
VMEM_LIMIT = 48 * 1024 * 1024

F32 = jnp.float32
BF16 = jnp.bfloat16
HIGHEST = lax.Precision.HIGHEST


def _params(sem):
    return pltpu.CompilerParams(dimension_semantics=sem, vmem_limit_bytes=VMEM_LIMIT)


def _tile(pref, n):
    t = min(pref, n)
    assert n % t == 0, (pref, n)
    return t


def _ada_kernel(c_ref, w_ref, b_ref, o_ref):
    c = c_ref[...]
    c_act = c * jax.nn.sigmoid(c)
    o_ref[0] = jnp.dot(c_act, w_ref[0], precision=HIGHEST, preferred_element_type=F32) + b_ref[0]


def _ada_mod(c, w_ada, b_ada):
    depth, d, six_d = w_ada.shape
    bsz = c.shape[0]
    tn = _tile(1536, six_d)
    return pl.pallas_call(
        _ada_kernel,
        grid=(depth, six_d // tn),
        in_specs=[
            pl.BlockSpec((bsz, d), lambda l, j: (0, 0)),
            pl.BlockSpec((1, d, tn), lambda l, j: (l, 0, j)),
            pl.BlockSpec((1, 1, tn), lambda l, j: (l, 0, j)),
        ],
        out_specs=pl.BlockSpec((1, bsz, tn), lambda l, j: (l, 0, j)),
        out_shape=jax.ShapeDtypeStruct((depth, bsz, six_d), F32),
        compiler_params=_params(("parallel", "parallel")),
        name="ada_mod",
    )(c, w_ada, b_ada.reshape(depth, 1, six_d))


def _modulated_rms(x, g, sc1p, sh):
    y = x * lax.rsqrt(jnp.mean(x * x, axis=-1, keepdims=True) + EPS)
    return (y * g) * sc1p + sh


def _norm_mm_kernel(x_ref, g_ref, sc_ref, sh_ref, w_ref, *rest, col_chunk, has_gates):
    if has_gates:
        wg_ref, o_ref, og_ref = rest
    else:
        (o_ref,) = rest
    hb = _modulated_rms(x_ref[...], g_ref[...], sc_ref[0], sh_ref[0]).astype(BF16)
    for c0 in range(0, o_ref.shape[1], col_chunk):
        o_ref[:, c0:c0 + col_chunk] = jnp.dot(
            hb, w_ref[:, c0:c0 + col_chunk], preferred_element_type=F32).astype(o_ref.dtype)
    if has_gates:
        og_ref[...] = jnp.dot(hb, wg_ref[...], preferred_element_type=F32)


def _norm_matmul(x, gain, sc1p, sh, w, wg, seq):
    n, d = x.shape
    cols = w.shape[1]
    tm = _tile(512, seq)
    per_b = seq // tm
    has_gates = wg is not None
    in_specs = [
        pl.BlockSpec((tm, d), lambda i: (i, 0)),
        pl.BlockSpec((1, d), lambda i: (0, 0)),
        pl.BlockSpec((1, 1, d), lambda i: (i // per_b, 0, 0)),
        pl.BlockSpec((1, 1, d), lambda i: (i // per_b, 0, 0)),
        pl.BlockSpec((d, cols), lambda i: (0, 0)),
    ]
    out_specs = [pl.BlockSpec((tm, cols), lambda i: (i, 0))]
    out_shape = [jax.ShapeDtypeStruct((n, cols), BF16)]
    args = [x, gain, sc1p, sh, w]
    if has_gates:
        in_specs.append(pl.BlockSpec((d, LANES), lambda i: (0, 0)))
        out_specs.append(pl.BlockSpec((tm, LANES), lambda i: (i, 0)))
        out_shape.append(jax.ShapeDtypeStruct((n, LANES), F32))
        args.append(wg)
    return pl.pallas_call(
        functools.partial(_norm_mm_kernel, col_chunk=512, has_gates=has_gates),
        grid=(n // tm,),
        in_specs=in_specs,
        out_specs=out_specs,
        out_shape=out_shape,
        compiler_params=_params(("parallel",)),
        name="norm_inproj",
    )(*args)


def _log_sigmoid(x):
    return jnp.minimum(x, 0.0) - jnp.log1p(jnp.exp(-jnp.abs(x)))


def _mlstm_kernel(p_ref, g_ref, bias_ref, hn_ref, o_ref, ct_ref, n_ref, m_ref):
    L = A_CHUNK

    @pl.when(pl.program_id(1) == 0)
    def _():
        ct_ref[...] = jnp.zeros_like(ct_ref)
        n_ref[...] = jnp.zeros_like(n_ref)
        m_ref[...] = jnp.zeros_like(m_ref)

    gates = g_ref[...] + bias_ref[...]
    src = lax.broadcasted_iota(jnp.int32, (L, L), 0)
    tgt = lax.broadcasted_iota(jnp.int32, (L, L), 1)
    causal = src <= tgt
    bcum = jnp.dot(jnp.where(src >= tgt, 1.0, 0.0), _log_sigmoid(gates), precision=HIGHEST,
                   preferred_element_type=F32)
    bcum_t = bcum.T
    nt = (((1,), (1,)), ((), ()))

    for h in range(A_HEADS):
        q = p_ref[:, h * A_QK_DIM:(h + 1) * A_QK_DIM]
        ksf = p_ref[:, A_HQ + h * A_QK_DIM:A_HQ + (h + 1) * A_QK_DIM].astype(F32) * (A_QK_DIM ** -0.5)
        v = p_ref[:, 2 * A_HQ + h * A_V_DIM:2 * A_HQ + (h + 1) * A_V_DIM]
        og = p_ref[:, 2 * A_HQ + A_HV + h * A_V_DIM:2 * A_HQ + A_HV + (h + 1) * A_V_DIM].astype(F32)
        b_row = bcum_t[A_HEADS + h:A_HEADS + h + 1, :]
        b_last = b_row[:, L - 1:L]
        c_col = gates[:, h:h + 1] - bcum[:, A_HEADS + h:A_HEADS + h + 1]
        m11 = m_ref[h][:, 0:1]
        ct_old = ct_ref[h]
        n_old = n_ref[h]

        log_d = jnp.where(causal, c_col + b_row, -jnp.inf)
        log_inter = b_row + m11
        m_t = jnp.maximum(jnp.max(log_d, axis=0, keepdims=True), log_inter)
        dmat = jnp.exp(log_d - m_t)
        inter = jnp.exp(log_inter - m_t)
        s = lax.dot_general(ksf.astype(BF16), q, nt, preferred_element_type=F32) * dmat
        vt = v.astype(F32).T.astype(BF16)
        num = (jnp.dot(vt, s.astype(BF16), preferred_element_type=F32)
               + inter * lax.dot_general(ct_old.astype(BF16), q, nt, preferred_element_type=F32))
        qn = lax.dot_general(n_old.astype(BF16), q, nt, preferred_element_type=F32)[0:1]
        den = jnp.sum(s, axis=0, keepdims=True) + inter * qn
        hh = num / jnp.maximum(jnp.abs(den), jnp.exp(-m_t))

        lw_col = b_last + c_col
        m_new = jnp.maximum(b_last + m11, jnp.max(lw_col, axis=0, keepdims=True))
        kw = ksf * jnp.exp(lw_col - m_new)
        decay = jnp.exp(b_last + m11 - m_new)
        ct_ref[h] = decay * ct_old + jnp.dot(vt, kw.astype(BF16), preferred_element_type=F32)
        n_ref[h] = decay * n_old + jnp.broadcast_to(jnp.sum(kw, axis=0, keepdims=True), n_old.shape)
        m_ref[h] = jnp.broadcast_to(m_new, (1, LANES))

        hn = hh * lax.rsqrt(jnp.mean(hh * hh, axis=0, keepdims=True) + EPS)
        hn = (hn * hn_ref[h * A_V_DIM:(h + 1) * A_V_DIM, :]).T
        o_ref[:, h * A_V_DIM:(h + 1) * A_V_DIM] = (hn * jax.nn.sigmoid(og)).astype(o_ref.dtype)


def _mlstm(proj, gates, bias_row, h_norm, bsz, seq):
    n = proj.shape[0]
    nc = seq // A_CHUNK
    h_norm_cols = jnp.broadcast_to(h_norm[:, None], (A_HV, A_CHUNK))
    return pl.pallas_call(
        _mlstm_kernel,
        grid=(bsz, nc),
        in_specs=[
            pl.BlockSpec((A_CHUNK, A_MAIN_COLS), lambda b, c: (b * nc + c, 0)),
            pl.BlockSpec((A_CHUNK, LANES), lambda b, c: (b * nc + c, 0)),
            pl.BlockSpec((1, LANES), lambda b, c: (0, 0)),
            pl.BlockSpec((A_HV, A_CHUNK), lambda b, c: (0, 0)),
        ],
        out_specs=pl.BlockSpec((A_CHUNK, A_HV), lambda b, c: (b * nc + c, 0)),
        out_shape=jax.ShapeDtypeStruct((n, A_HV), BF16),
        scratch_shapes=[
            pltpu.VMEM((A_HEADS, A_V_DIM, A_QK_DIM), F32),
            pltpu.VMEM((A_HEADS, SUBLANES, A_QK_DIM), F32),
            pltpu.VMEM((A_HEADS, 1, LANES), F32),
        ],
        compiler_params=_params(("parallel", "arbitrary")),
        name="mlstm",
    )(proj, gates, bias_row, h_norm_cols)


def _qk_prep_kernel(p_ref, pos_ref, seg_ref, segt_ref, gain_ref, freq_ref, o_ref):
    tm = p_ref.shape[0]
    x = p_ref[...].astype(F32)
    xx = x * x
    xx_hi = xx.astype(BF16)
    xx_lo = (xx - xx_hi.astype(F32)).astype(BF16)
    seg = seg_ref[...]
    ss = jnp.dot(xx_hi, seg, preferred_element_type=F32) + jnp.dot(xx_lo, seg, preferred_element_type=F32)
    r = lax.rsqrt(ss * (1.0 / B_HEAD_DIM) + EPS)
    r_hi = r.astype(BF16)
    r_lo = (r - r_hi.astype(F32)).astype(BF16)
    segt = segt_ref[...]
    r_full = jnp.dot(r_hi, segt, preferred_element_type=F32) + jnp.dot(r_lo, segt, preferred_element_type=F32)

    half = B_ROT_DIM // 2
    assert half == SUBLANES
    ang = freq_ref[:, 0:1] * pos_ref[0].astype(F32)
    cos8 = jnp.cos(ang)
    sin8 = jnp.sin(ang)
    one8 = jnp.ones_like(cos8)
    zero8 = jnp.zeros_like(cos8)
    per_seg = B_HEAD_DIM // SUBLANES
    n_seg = LANES // B_HEAD_DIM

    def chunk_pattern(first, second, rest):
        groups = ([first, second] + [rest] * (per_seg - 2)) * n_seg
        return jnp.concatenate(groups, axis=0).T

    c_mul = chunk_pattern(cos8, cos8, one8)
    s_lo = chunk_pattern(-sin8, zero8, zero8)
    s_hi = chunk_pattern(zero8, sin8, zero8)
    for j in range(2 * D_MODEL // LANES):
        sl = slice(j * LANES, (j + 1) * LANES)
        y = x[:, sl] * r_full[:, sl] * gain_ref[:, sl]
        out = y * c_mul + pltpu.roll(y, LANES - half, axis=1) * s_lo + pltpu.roll(y, half, axis=1) * s_hi
        if j < D_MODEL // LANES:
            out = out * (B_HEAD_DIM ** -0.5 * math.log2(math.e))
        o_ref[:, sl] = out.astype(o_ref.dtype)


def _qk_prep(proj, positions_row, q_norm, k_norm, seq):
    n = proj.shape[0]
    two_d = 2 * D_MODEL
    tm = _tile(256, seq)
    seg = (jnp.arange(two_d)[:, None] // B_HEAD_DIM == jnp.arange(LANES)[None, :]).astype(BF16)
    gain = jnp.concatenate([jnp.tile(q_norm, D_MODEL // B_HEAD_DIM), jnp.tile(k_norm, D_MODEL // B_HEAD_DIM)])
    inv_freq = ROPE_THETA ** (-jnp.arange(0, B_ROT_DIM, 2, dtype=F32) / B_ROT_DIM)
    freq_tab = jnp.broadcast_to(inv_freq[:, None], (B_ROT_DIM // 2, LANES))
    return pl.pallas_call(
        _qk_prep_kernel,
        grid=(n // tm,),
        in_specs=[
            pl.BlockSpec((tm, two_d), lambda i: (i, 0)),
            pl.BlockSpec((1, 1, tm), lambda i: (i, 0, 0)),
            pl.BlockSpec((two_d, LANES), lambda i: (0, 0)),
            pl.BlockSpec((LANES, two_d), lambda i: (0, 0)),
            pl.BlockSpec((1, two_d), lambda i: (0, 0)),
            pl.BlockSpec((SUBLANES, LANES), lambda i: (0, 0)),
        ],
        out_specs=pl.BlockSpec((tm, two_d), lambda i: (i, 0)),
        out_shape=jax.ShapeDtypeStruct((n, two_d), BF16),
        compiler_params=_params(("parallel",)),
        name="qk_prep",
    )(proj, positions_row.reshape(n // tm, 1, tm), seg, seg.T, gain.reshape(1, two_d), freq_tab)


ATTN_TILE = 512


def _diff_attn_kernel(q_ref, k_ref, v_ref, lam_ref, on_ref, o_ref, vt_ref, qm_ref, m_ref, l_ref, acc_ref,
                      *, tile, lam_init):
    seq = q_ref.shape[0]
    n_tiles = seq // tile
    lamv = lam_ref[...]
    lam = (jnp.exp(jnp.sum(lamv[0:1] * lamv[1:2], axis=1, keepdims=True))
           - jnp.exp(jnp.sum(lamv[2:3] * lamv[3:4], axis=1, keepdims=True)) + lam_init)
    tchunk = min(256, seq)
    for c in range(seq // tchunk):
        vt_ref[:, c * tchunk:(c + 1) * tchunk] = v_ref[c * tchunk:(c + 1) * tchunk, :].astype(F32).T.astype(BF16)
    first_map = lax.broadcasted_iota(jnp.int32, (tile, LANES), 1) < B_HEAD_DIM
    key_id = lax.broadcasted_iota(jnp.int32, (tile, tile), 0)
    query_id = lax.broadcasted_iota(jnp.int32, (tile, tile), 1)
    causal = query_id >= key_id
    nt = (((1,), (1,)), ((), ()))

    def tile_step(k, vt, mask):
        scores = [lax.dot_general(k, qm_ref[c], nt, preferred_element_type=F32) for c in range(2)]
        for c, s in enumerate(scores):
            if mask is not None:
                s = jnp.where(mask, s, -jnp.inf)
            m_old = m_ref[c]
            m_new = jnp.maximum(m_old, jnp.max(s, axis=0, keepdims=True))
            alpha = jnp.exp2(m_old - m_new)
            p = jnp.exp2(s - m_new)
            l_ref[c] = alpha * l_ref[c] + jnp.sum(p, axis=0, keepdims=True)
            acc_ref[c] = alpha * acc_ref[c] + jnp.dot(vt, p.astype(BF16), preferred_element_type=F32)
            m_ref[c] = m_new

    for qi in range(n_tiles):
        q = q_ref[qi * tile:(qi + 1) * tile, :]
        qm_ref[0] = jnp.where(first_map, q, jnp.zeros_like(q))
        qm_ref[1] = jnp.where(first_map, jnp.zeros_like(q), q)
        m_ref[...] = jnp.full(m_ref.shape, -jnp.inf, F32)
        l_ref[...] = jnp.zeros(l_ref.shape, F32)
        acc_ref[...] = jnp.zeros(acc_ref.shape, F32)
        for ki in range(qi + 1):
            tile_step(k_ref[ki * tile:(ki + 1) * tile, :], vt_ref[:, ki * tile:(ki + 1) * tile],
                      causal if ki == qi else None)
        o = acc_ref[0] / l_ref[0] - lam * (acc_ref[1] / l_ref[1])
        o = o * lax.rsqrt(jnp.mean(o * o, axis=0, keepdims=True) + EPS) * on_ref[...] * (1.0 - lam_init)
        o_ref[qi * tile:(qi + 1) * tile, :] = o.T.astype(o_ref.dtype)


def _diff_attn(qk, proj, lam_tab, o_norm, bsz, seq, lam_init):
    n = qk.shape[0]
    tile = _tile(ATTN_TILE, seq)
    o_norm_cols = jnp.broadcast_to(o_norm[:, None], (B_V_DIM, tile))
    return pl.pallas_call(
        functools.partial(_diff_attn_kernel, tile=tile, lam_init=lam_init),
        grid=(bsz, B_HEADS),
        in_specs=[
            pl.BlockSpec((seq, LANES), lambda b, h: (b, h)),
            pl.BlockSpec((seq, LANES), lambda b, h: (b, B_HEADS + h)),
            pl.BlockSpec((seq, B_V_DIM), lambda b, h: (b, 2 * B_HEADS + h)),
            pl.BlockSpec((SUBLANES, LANES), lambda b, h: (0, 0)),
            pl.BlockSpec((B_V_DIM, tile), lambda b, h: (0, 0)),
        ],
        out_specs=pl.BlockSpec((seq, B_V_DIM), lambda b, h: (b, h)),
        out_shape=jax.ShapeDtypeStruct((n, D_MODEL), BF16),
        scratch_shapes=[
            pltpu.VMEM((B_V_DIM, seq), BF16),
            pltpu.VMEM((2, tile, LANES), BF16),
            pltpu.VMEM((2, 1, tile), F32),
            pltpu.VMEM((2, 1, tile), F32),
            pltpu.VMEM((2, B_V_DIM, tile), F32),
        ],
        compiler_params=_params(("parallel", "parallel")),
        name="diff_attn",
    )(qk, qk, proj, lam_tab, o_norm_cols)


HALF_D = D_MODEL // 2
HI16 = 0xFFFF0000
PACKED_TILES = HALF_D // LANES


def _pack_bf16_pairs(v):
    bits = lax.bitcast_convert_type(v.astype(BF16).astype(F32), jnp.uint32)
    return (bits[:, HALF_D:] & jnp.uint32(HI16)) | (bits[:, :HALF_D] >> 16)


def _unpack_bf16_pairs(words):
    first = lax.bitcast_convert_type(words << 16, F32)
    second = lax.bitcast_convert_type(words & jnp.uint32(HI16), F32)
    return first.astype(BF16), second.astype(BF16)


def _outproj_kernel(a_ref, w_ref, x_ref, g1_ref, gain_ref, sc_ref, sh_ref, wr_ref, xo_ref, hp_ref, lg_ref):
    y = jnp.dot(a_ref[...], w_ref[...], preferred_element_type=F32)
    xm = x_ref[...] + g1_ref[0] * y
    xo_ref[...] = xm
    h2 = _modulated_rms(xm, gain_ref[...], sc_ref[0], sh_ref[0])
    tm = xm.shape[0]
    packed = _pack_bf16_pairs(h2)
    for j in range(ROW_TILES):
        hp_ref[pl.ds(j, tm, stride=ROW_TILES), :] = (
            packed[:, j * LANES:(j + 1) * LANES] if j < PACKED_TILES else jnp.zeros((tm, LANES), jnp.uint32))
    lg_ref[...] = lax.dot_general(wr_ref[...], h2, (((1,), (1,)), ((), ())),
                                  precision=HIGHEST, preferred_element_type=F32)


def _outproj_norm_router(a, w, x, g1, gain, sc1p, sh, w_router_t, seq):
    n, d = x.shape
    tm = _tile(512, seq)
    per_b = seq // tm
    bmap = lambda i: (i // per_b, 0, 0)
    return pl.pallas_call(
        _outproj_kernel,
        grid=(n // tm,),
        in_specs=[
            pl.BlockSpec((tm, a.shape[1]), lambda i: (i, 0)),
            pl.BlockSpec(w.shape, lambda i: (0, 0)),
            pl.BlockSpec((tm, d), lambda i: (i, 0)),
            pl.BlockSpec((1, 1, d), bmap),
            pl.BlockSpec((1, d), lambda i: (0, 0)),
            pl.BlockSpec((1, 1, d), bmap),
            pl.BlockSpec((1, 1, d), bmap),
            pl.BlockSpec((N_EXPERTS, d), lambda i: (0, 0)),
        ],
        out_specs=[
            pl.BlockSpec((tm, d), lambda i: (i, 0)),
            pl.BlockSpec((tm * ROW_TILES, LANES), lambda i: (i, 0)),
            pl.BlockSpec((N_EXPERTS, tm), lambda i: (0, i)),
        ],
        out_shape=[
            jax.ShapeDtypeStruct((n, d), F32),
            jax.ShapeDtypeStruct((n * ROW_TILES, LANES), jnp.uint32),
            jax.ShapeDtypeStruct((N_EXPERTS, n), F32),
        ],
        compiler_params=_params(("parallel",)),
        name="outproj_norm_router",
    )(a, w, x, g1, gain, sc1p, sh, w_router_t)


PAIR_LO = (0, 0, 0, 1, 1, 2)
PAIR_HI = (1, 2, 3, 2, 3, 3)
PAIRS_PER_GROUP = len(PAIR_LO)
N_CLASSES = N_GROUPS * PAIRS_PER_GROUP
CLASS_ROWS = 32


def _route_kernel(lg_ref, bias_ref, oi_ref, cnt_ref, carry_ref, tri_ref):
    tr = lg_ref.shape[1]
    step = pl.program_id(0)

    @pl.when(step == 0)
    def _():
        carry_ref[...] = jnp.zeros_like(carry_ref)
        r = lax.broadcasted_iota(jnp.int32, (tr, tr), 0)
        c = lax.broadcasted_iota(jnp.int32, (tr, tr), 1)
        tri_ref[...] = jnp.where(r < c, 1.0, 0.0).astype(BF16)

    scores = jax.nn.sigmoid(lg_ref[...])
    biased = scores + bias_ref[...]
    rows = [biased[e:e + 1, :] for e in range(N_EXPERTS)]

    def top2_sum(a, b, c, d):
        m1, n1 = jnp.maximum(a, b), jnp.minimum(a, b)
        m2, n2 = jnp.maximum(c, d), jnp.minimum(c, d)
        return jnp.maximum(m1, m2) + jnp.maximum(jnp.minimum(m1, m2), jnp.maximum(n1, n2))

    gscore = [top2_sum(*rows[g * EXPERTS_PER_GROUP:(g + 1) * EXPERTS_PER_GROUP]) for g in range(N_GROUPS)]
    best = gscore[0]
    gsel = jnp.zeros_like(best, dtype=jnp.int32)
    for g in range(1, N_GROUPS):
        upd = gscore[g] > best
        gsel = jnp.where(upd, g, gsel)
        best = jnp.where(upd, gscore[g], best)

    def pick(table, j):
        out = table[j]
        for g in range(1, N_GROUPS):
            out = jnp.where(gsel == g, table[g * EXPERTS_PER_GROUP + j], out)
        return out

    in_b = [pick(rows, j) for j in range(EXPERTS_PER_GROUP)]
    v1, i1 = in_b[0], jnp.zeros_like(gsel)
    for j in range(1, EXPERTS_PER_GROUP):
        upd = in_b[j] > v1
        v1 = jnp.where(upd, in_b[j], v1)
        i1 = jnp.where(upd, j, i1)
    v2 = jnp.full_like(v1, -jnp.inf)
    i2 = jnp.zeros_like(gsel)
    for j in range(EXPERTS_PER_GROUP):
        upd = (i1 != j) & (in_b[j] > v2)
        v2 = jnp.where(upd, in_b[j], v2)
        i2 = jnp.where(upd, j, i2)
    first_is_lo = i1 < i2
    lo = jnp.where(first_is_lo, i1, i2)
    hi = jnp.where(first_is_lo, i2, i1)
    pair = jnp.where(lo == 0, hi - 1, jnp.where(lo == 1, hi + 1, PAIRS_PER_GROUP - 1))
    cls = gsel * PAIRS_PER_GROUP + pair

    cid = lax.broadcasted_iota(jnp.int32, (CLASS_ROWS, tr), 0)
    onehot = jnp.where(cid == cls, 1.0, 0.0)
    before = jnp.dot(onehot.astype(BF16), tri_ref[...], preferred_element_type=F32) + carry_ref[...]
    oi_ref[0:1, :] = cls
    oi_ref[1:2, :] = jnp.sum(onehot * before, axis=0, keepdims=True).astype(jnp.int32)
    new_carry = carry_ref[...] + jnp.sum(onehot, axis=1, keepdims=True)
    carry_ref[...] = new_carry
    cnt_ref[...] = new_carry.astype(jnp.int32)


def _route(logits_t, router_bias):
    n = logits_t.shape[1]
    tr = _tile(512, n)
    return pl.pallas_call(
        _route_kernel,
        grid=(n // tr,),
        in_specs=[
            pl.BlockSpec((N_EXPERTS, tr), lambda i: (0, i)),
            pl.BlockSpec((N_EXPERTS, 1), lambda i: (0, 0)),
        ],
        out_specs=[
            pl.BlockSpec((2, tr), lambda i: (0, i)),
            pl.BlockSpec((CLASS_ROWS, 1), lambda i: (0, 0)),
        ],
        out_shape=[
            jax.ShapeDtypeStruct((2, n), jnp.int32),
            jax.ShapeDtypeStruct((CLASS_ROWS, 1), jnp.int32),
        ],
        scratch_shapes=[pltpu.VMEM((CLASS_ROWS, 1), F32), pltpu.VMEM((tr, tr), BF16)],
        compiler_params=_params(("arbitrary",)),
        name="route",
    )(logits_t, router_bias.reshape(N_EXPERTS, 1).astype(F32))


SORT_UNROLL = 8


def _sort_order_kernel(pos_ref, tok_ref):
    def body(i, c):
        for u in range(SORT_UNROLL):
            t = i * SORT_UNROLL + u
            tok_ref[pos_ref[t]] = t
        return c

    lax.fori_loop(0, pos_ref.shape[0] // SORT_UNROLL, body, 0)


def _sort_order(pos):
    n = pos.shape[0]
    assert n % SORT_UNROLL == 0
    return pl.pallas_call(
        _sort_order_kernel,
        in_specs=[pl.BlockSpec(memory_space=pltpu.SMEM)],
        out_specs=pl.BlockSpec(memory_space=pltpu.SMEM),
        out_shape=jax.ShapeDtypeStruct((n,), jnp.int32),
        name="moe_sort_order",
    )(pos)


ROW_UNROLL = 8


def _for_rows(n, fn):
    groups = n // ROW_UNROLL

    def group(g, c):
        for u in range(ROW_UNROLL):
            fn(g * ROW_UNROLL + u)
        return c

    def single(r, c):
        fn(r)
        return c

    lax.fori_loop(0, groups, group, 0)
    lax.fori_loop(groups * ROW_UNROLL, n, single, 0)


def _row_copy(src_ref, src_row, dst_ref, dst_row, sem):
    return pltpu.make_async_copy(
        src_ref.at[pl.ds(pl.multiple_of(src_row * ROW_TILES, ROW_TILES), ROW_TILES)],
        dst_ref.at[pl.ds(pl.multiple_of(dst_row * ROW_TILES, ROW_TILES), ROW_TILES)],
        sem)


def _expert_kernel(ea_ref, eb_ref, off_ref, nv_ref, nu_ref, tok_ref,
                   h_hbm, wgu_a_ref, wd_a_ref, wgu_b_ref, wd_b_ref, wr_a_ref, wr_b_ref, y_hbm,
                   xbuf, ybuf, gsem, ssem):
    del ea_ref, eb_ref
    tb = xbuf.shape[1] // ROW_TILES
    i = pl.program_id(0)
    slot = i % 2
    n_used = nu_ref[0]

    def start_gather(block, s):
        base = off_ref[block]
        _for_rows(nv_ref[block],
                  lambda r: _row_copy(h_hbm, tok_ref[base + r], xbuf.at[s], r, gsem.at[s]).start())

    def start_scatter(block, s):
        base = off_ref[block]
        _for_rows(nv_ref[block],
                  lambda r: _row_copy(ybuf.at[s], r, y_hbm, tok_ref[base + r], ssem.at[s]).start())

    def wait_rows(block, hbm, buf, sem):
        rows = nv_ref[block] * ROW_TILES

        @pl.when(rows > 0)
        def _():
            pltpu.make_async_copy(hbm.at[pl.ds(0, rows)], buf.at[pl.ds(0, rows)], sem).wait()

    @pl.when(i == 0)
    def _():
        xbuf[...] = jnp.zeros_like(xbuf)
        start_gather(0, 0)

    @pl.when(i < n_used)
    def _():
        @pl.when(i + 1 < n_used)
        def _():
            start_gather(i + 1, 1 - slot)

        wait_rows(i, h_hbm, xbuf.at[slot], gsem.at[slot])

        @pl.when(i >= 2)
        def _():
            wait_rows(i - 2, y_hbm, ybuf.at[slot], ssem.at[slot])

        words = [xbuf[slot, pl.ds(j, tb, stride=ROW_TILES), :] for j in range(PACKED_TILES)]
        halves = [_unpack_bf16_pairs(w) for w in words]
        x = jnp.concatenate([h[0] for h in halves] + [h[1] for h in halves], axis=1)

        def mlp(wgu_ref, wd_ref):
            gu = jnp.dot(x, wgu_ref[0], preferred_element_type=F32)
            gate = gu[:, :D_EXPERT]
            act = gate * jax.nn.sigmoid(gate) * gu[:, D_EXPERT:]
            return jnp.dot(act.astype(BF16), wd_ref[0], preferred_element_type=F32)

        score_a = jax.nn.sigmoid(jnp.dot(x, wr_a_ref[0], preferred_element_type=F32))
        score_b = jax.nn.sigmoid(jnp.dot(x, wr_b_ref[0], preferred_element_type=F32))
        inv_tot = 1.0 / (score_a + score_b)
        w_a = score_a * inv_tot
        w_b = score_b * inv_tot
        ya = mlp(wgu_a_ref, wd_a_ref)
        yb = mlp(wgu_b_ref, wd_b_ref)
        for j in range(ROW_TILES):
            sl = slice(j * LANES, (j + 1) * LANES)
            ybuf[slot, pl.ds(j, tb, stride=ROW_TILES), :] = w_a * ya[:, sl] + w_b * yb[:, sl]
        start_scatter(i, slot)

        @pl.when(i == n_used - 1)
        def _():
            wait_rows(i, y_hbm, ybuf.at[slot], ssem.at[slot])

            @pl.when(i >= 1)
            def _():
                wait_rows(i - 1, y_hbm, ybuf.at[1 - slot], ssem.at[1 - slot])


def _experts(tables, sorted_tok, h_rows, w_gu, w_down, wr_bcast, tb):
    block_ea, block_eb, src_off, n_valid, n_used = tables
    n_blocks = block_ea.shape[0]
    d, two_f = w_gu.shape[1:]
    ea_map = lambda i, ea, eb, off, nv, nu, tok: (ea[i], 0, 0)
    eb_map = lambda i, ea, eb, off, nv, nu, tok: (eb[i], 0, 0)
    grid_spec = pltpu.PrefetchScalarGridSpec(
        num_scalar_prefetch=6,
        grid=(n_blocks,),
        in_specs=[
            pl.BlockSpec(memory_space=pl.ANY),
            pl.BlockSpec((1, d, two_f), ea_map),
            pl.BlockSpec((1, two_f // 2, d), ea_map),
            pl.BlockSpec((1, d, two_f), eb_map),
            pl.BlockSpec((1, two_f // 2, d), eb_map),
            pl.BlockSpec((1, d, LANES), ea_map),
            pl.BlockSpec((1, d, LANES), eb_map),
        ],
        out_specs=pl.BlockSpec(memory_space=pl.ANY),
        scratch_shapes=[
            pltpu.VMEM((2, tb * ROW_TILES, LANES), jnp.uint32),
            pltpu.VMEM((2, tb * ROW_TILES, LANES), F32),
            pltpu.SemaphoreType.DMA((2,)),
            pltpu.SemaphoreType.DMA((2,)),
        ],
    )
    return pl.pallas_call(
        _expert_kernel,
        grid_spec=grid_spec,
        out_shape=jax.ShapeDtypeStruct(h_rows.shape, F32),
        compiler_params=_params(("arbitrary",)),
        name="moe_experts",
    )(block_ea, block_eb, src_off, n_valid, n_used, sorted_tok, h_rows, w_gu, w_down, w_gu, w_down, wr_bcast, wr_bcast)


def _residual_kernel(y_ref, x_ref, g2_ref, o_ref):
    tc = x_ref.shape[0]
    g2 = g2_ref[0]
    for j in range(ROW_TILES):
        sl = slice(j * LANES, (j + 1) * LANES)
        o_ref[:, sl] = x_ref[:, sl] + g2[:, sl] * y_ref[pl.ds(j, tc, stride=ROW_TILES), :]


def _residual(y_rows, x_mid, g2, seq):
    n, d = x_mid.shape
    tc = _tile(512, seq)
    per_b = seq // tc
    return pl.pallas_call(
        _residual_kernel,
        grid=(n // tc,),
        in_specs=[
            pl.BlockSpec((tc * ROW_TILES, LANES), lambda i: (i, 0)),
            pl.BlockSpec((tc, d), lambda i: (i, 0)),
            pl.BlockSpec((1, 1, d), lambda i: (i // per_b, 0, 0)),
        ],
        out_specs=pl.BlockSpec((tc, d), lambda i: (i, 0)),
        out_shape=jax.ShapeDtypeStruct((n, d), F32),
        compiler_params=_params(("parallel",)),
        name="moe_residual",
    )(y_rows, x_mid, g2)


MOE_ROWS = 256


def _moe(x_mid, h_rows, logits_t, router_bias, w_gu, w_down, wr_bcast, g2, seq):
    n = x_mid.shape[0]
    n_blocks = (n + N_CLASSES * (MOE_ROWS - 1) + MOE_ROWS - 1) // MOE_ROWS
    route_i, counts = _route(logits_t, router_bias)
    counts = counts[:N_CLASSES, 0]
    starts = jnp.cumsum(counts) - counts
    padded = ((counts + MOE_ROWS - 1) // MOE_ROWS) * MOE_ROWS
    pad_ends = jnp.cumsum(padded)
    pad_starts = pad_ends - padded
    pos = route_i[1]
    for c in range(N_CLASSES):
        pos = pos + jnp.where(route_i[0] == c, starts[c], 0)
    sorted_tok = _sort_order(pos)
    blk_start = jnp.arange(n_blocks, dtype=jnp.int32) * MOE_ROWS
    n_used = (pad_ends[-1:] // MOE_ROWS).astype(jnp.int32)
    used = jnp.arange(n_blocks) < n_used[0]
    block_c = jnp.minimum(jnp.sum(blk_start[:, None] >= pad_ends[None, :], axis=1), N_CLASSES - 1)
    block_c = jnp.where(used, block_c, block_c[jnp.maximum(n_used[0] - 1, 0)])
    in_class = blk_start - pad_starts[block_c]
    src_off = jnp.where(used, starts[block_c] + in_class, 0).astype(jnp.int32)
    n_valid = jnp.where(used, jnp.clip(counts[block_c] - in_class, 0, MOE_ROWS), 0).astype(jnp.int32)
    group_base = (block_c // PAIRS_PER_GROUP) * EXPERTS_PER_GROUP
    pair = block_c % PAIRS_PER_GROUP
    block_ea = (group_base + jnp.asarray(PAIR_LO, jnp.int32)[pair]).astype(jnp.int32)
    block_eb = (group_base + jnp.asarray(PAIR_HI, jnp.int32)[pair]).astype(jnp.int32)
    y_rows = _experts((block_ea, block_eb, src_off, n_valid, n_used), sorted_tok, h_rows, w_gu, w_down, wr_bcast,
                      MOE_ROWS)
    return _residual(y_rows, x_mid, g2, seq)


def kernel(x, c, positions, norm1, norm2, w_ada, b_ada, a_w_in, a_b_if, a_h_norm, a_w_out, b_w_in, b_q_norm, b_k_norm, b_lam_q1, b_lam_k1, b_lam_q2, b_lam_k2, b_o_norm, b_w_out, w_router, router_bias, moe_w_gu, moe_w_down):
    bsz, seq, d = x.shape
    depth = w_ada.shape[0]
    n = bsz * seq
    xf = x.reshape(n, d)
    mod = _ada_mod(c, w_ada, b_ada)
    w_router_t = w_router.T
    wr_bcast = jnp.broadcast_to(w_router_t[:, :, None], (N_EXPERTS, d, LANES)).astype(BF16)
    pos_row = positions.reshape(n)

    for l in range(depth):
        sh1, sc1, g1, sh2, sc2, g2 = [mod[l, :, i * d:(i + 1) * d].reshape(bsz, 1, d) for i in range(6)]
        j = l // 2
        if l % 2 == 0:
            w_in = a_w_in[j]
            w_main = w_in[:, :A_MAIN_COLS].astype(BF16)
            w_gate = jnp.pad(w_in[:, A_MAIN_COLS:], ((0, 0), (0, LANES - 2 * A_HEADS))).astype(BF16)
            proj, gates = _norm_matmul(xf, norm1[l].reshape(1, d), 1.0 + sc1, sh1, w_main, w_gate, seq)
            bias_row = jnp.pad(a_b_if[j], (0, LANES - 2 * A_HEADS)).reshape(1, LANES)
            mixed = _mlstm(proj, gates, bias_row, a_h_norm[j], bsz, seq)
            w_out = a_w_out[j].astype(BF16)
        else:
            (proj,) = _norm_matmul(xf, norm1[l].reshape(1, d), 1.0 + sc1, sh1, b_w_in[j].astype(BF16), None, seq)
            qk = _qk_prep(proj, pos_row, b_q_norm[j], b_k_norm[j], seq)
            lam_tab = jnp.zeros((SUBLANES, LANES), F32)
            for r, v in enumerate((b_lam_q1[j], b_lam_k1[j], b_lam_q2[j], b_lam_k2[j])):
                lam_tab = lam_tab.at[r, :B_HEAD_DIM].set(v)
            lam_init = 0.8 - 0.6 * math.exp(-0.3 * l)
            mixed = _diff_attn(qk, proj, lam_tab, b_o_norm[j], bsz, seq, lam_init)
            w_out = b_w_out[j].astype(BF16)
        x_mid, h_rows, logits_t = _outproj_norm_router(
            mixed, w_out, xf, g1, norm2[l].reshape(1, d), 1.0 + sc2, sh2, w_router_t, seq)
        xf = _moe(x_mid, h_rows, logits_t, router_bias,
                  moe_w_gu[l].astype(BF16), moe_w_down[l].astype(BF16), wr_bcast, g2, seq)
    return xf.reshape(bsz, seq, d)
```

```python
import functools
import math

import jax
import jax.numpy as jnp
from jax import lax
from jax.experimental import pallas as pl
from jax.experimental.pallas import tpu as pltpu

D_MODEL = 1024
A_HEADS = 4
A_QK_DIM = 128
A_V_DIM = 256
A_CHUNK = 128
A_HQ = A_HEADS * A_QK_DIM
A_HV = A_HEADS * A_V_DIM
A_MAIN_COLS = 2 * A_HQ + 2 * A_HV

B_HEADS = 8
B_HEAD_DIM = 64
B_V_DIM = 128
B_ROT_DIM = 16
ROPE_THETA = 500000.0

N_EXPERTS = 16
N_GROUPS = 4
EXPERTS_PER_GROUP = 4
TOP_K = 2
D_EXPERT = 512
EPS = 1e-6

LANES = 128
SUBLANES = 8
ROW_TILES = D_MODEL // LANES
assert ROW_TILES == SUBLANES
VMEM_LIMIT = 48 * 1024 * 1024

F32 = jnp.float32
BF16 = jnp.bfloat16
HIGHEST = lax.Precision.HIGHEST


def _params(sem):
    return pltpu.CompilerParams(dimension_semantics=sem, vmem_limit_bytes=VMEM_LIMIT)


def _tile(pref, n):
    t = min(pref, n)
    assert n % t == 0, (pref, n)
    return t


def _ada_kernel(c_ref, w_ref, b_ref, o_ref):
    c = c_ref[...]
    c_act = c * jax.nn.sigmoid(c)
    o_ref[0] = jnp.dot(c_act, w_ref[0], precision=HIGHEST, preferred_element_type=F32) + b_ref[0]


def _ada_mod(c, w_ada, b_ada):
    depth, d, six_d = w_ada.shape
    bsz = c.shape[0]
    tn = _tile(1536, six_d)
    return pl.pallas_call(
        _ada_kernel,
        grid=(depth, six_d // tn),
        in_specs=[
            pl.BlockSpec((bsz, d), lambda l, j: (0, 0)),
            pl.BlockSpec((1, d, tn), lambda l, j: (l, 0, j)),
            pl.BlockSpec((1, 1, tn), lambda l, j: (l, 0, j)),
        ],
        out_specs=pl.BlockSpec((1, bsz, tn), lambda l, j: (l, 0, j)),
        out_shape=jax.ShapeDtypeStruct((depth, bsz, six_d), F32),
        compiler_params=_params(("parallel", "parallel")),
        name="ada_mod",
    )(c, w_ada, b_ada.reshape(depth, 1, six_d))


def _modulated_rms(x, g, sc1p, sh):
    y = x * lax.rsqrt(jnp.mean(x * x, axis=-1, keepdims=True) + EPS)
    return (y * g) * sc1p + sh


def _gated_residual(x_ref, y_ref, g_ref):
    rows = x_ref.shape[0]
    g = g_ref[0]
    return jnp.concatenate(
        [x_ref[:, j * LANES:(j + 1) * LANES]
         + g[:, j * LANES:(j + 1) * LANES] * y_ref[pl.ds(j, rows, stride=ROW_TILES), :] for j in range(ROW_TILES)],
        axis=1)


def _norm_mm_kernel(*refs, col_chunk, has_gates, has_residual):
    refs = list(refs)
    x_ref = refs.pop(0)
    if has_residual:
        y_ref, g2_ref = refs.pop(0), refs.pop(0)
    g_ref, sc_ref, sh_ref, w_ref = refs[:4]
    refs = refs[4:]
    wg_ref = refs.pop(0) if has_gates else None
    xo_ref = refs.pop(0) if has_residual else None
    o_ref = refs.pop(0)
    og_ref = refs.pop(0) if has_gates else None

    if has_residual:
        x = _gated_residual(x_ref, y_ref, g2_ref)
        xo_ref[...] = x
    else:
        x = x_ref[...]
    hb = _modulated_rms(x, g_ref[...], sc_ref[0], sh_ref[0]).astype(BF16)
    for c0 in range(0, o_ref.shape[1], col_chunk):
        o_ref[:, c0:c0 + col_chunk] = jnp.dot(
            hb, w_ref[:, c0:c0 + col_chunk], preferred_element_type=F32).astype(o_ref.dtype)
    if has_gates:
        og_ref[...] = jnp.dot(hb, wg_ref[...], preferred_element_type=F32)


def _norm_matmul(x, residual, gain, sc1p, sh, w, wg, seq):
    n, d = x.shape
    cols = w.shape[1]
    tm = _tile(512, seq)
    per_b = seq // tm
    has_gates = wg is not None
    has_residual = residual is not None
    bmap = lambda i: (i // per_b, 0, 0)
    in_specs = [pl.BlockSpec((tm, d), lambda i: (i, 0))]
    args = [x]
    if has_residual:
        in_specs += [pl.BlockSpec((tm * ROW_TILES, LANES), lambda i: (i, 0)), pl.BlockSpec((1, 1, d), bmap)]
        args += list(residual)
    in_specs += [
        pl.BlockSpec((1, d), lambda i: (0, 0)),
        pl.BlockSpec((1, 1, d), bmap),
        pl.BlockSpec((1, 1, d), bmap),
        pl.BlockSpec((d, cols), lambda i: (0, 0)),
    ]
    args += [gain, sc1p, sh, w]
    out_specs, out_shape = [], []
    if has_residual:
        out_specs.append(pl.BlockSpec((tm, d), lambda i: (i, 0)))
        out_shape.append(jax.ShapeDtypeStruct((n, d), F32))
    out_specs.append(pl.BlockSpec((tm, cols), lambda i: (i, 0)))
    out_shape.append(jax.ShapeDtypeStruct((n, cols), BF16))
    if has_gates:
        in_specs.append(pl.BlockSpec((d, LANES), lambda i: (0, 0)))
        out_specs.append(pl.BlockSpec((tm, LANES), lambda i: (i, 0)))
        out_shape.append(jax.ShapeDtypeStruct((n, LANES), F32))
        args.append(wg)
    return pl.pallas_call(
        functools.partial(_norm_mm_kernel, col_chunk=512, has_gates=has_gates, has_residual=has_residual),
        grid=(n // tm,),
        in_specs=in_specs,
        out_specs=out_specs,
        out_shape=out_shape,
        compiler_params=_params(("parallel",)),
        name="norm_inproj",
    )(*args)


def _log_sigmoid(x):
    return jnp.minimum(x, 0.0) - jnp.log1p(jnp.exp(-jnp.abs(x)))


def _mlstm_kernel(p_ref, g_ref, bias_ref, hn_ref, o_ref, ct_ref, n_ref, m_ref):
    L = A_CHUNK

    @pl.when(pl.program_id(1) == 0)
    def _():
        ct_ref[...] = jnp.zeros_like(ct_ref)
        n_ref[...] = jnp.zeros_like(n_ref)
        m_ref[...] = jnp.zeros_like(m_ref)

    gates = g_ref[...] + bias_ref[...]
    src = lax.broadcasted_iota(jnp.int32, (L, L), 0)
    tgt = lax.broadcasted_iota(jnp.int32, (L, L), 1)
    causal = src <= tgt
    bcum = jnp.dot(jnp.where(src >= tgt, 1.0, 0.0), _log_sigmoid(gates), precision=HIGHEST,
                   preferred_element_type=F32)
    bcum_t = bcum.T
    nt = (((1,), (1,)), ((), ()))

    for h in range(A_HEADS):
        q = p_ref[:, h * A_QK_DIM:(h + 1) * A_QK_DIM]
        ksf = p_ref[:, A_HQ + h * A_QK_DIM:A_HQ + (h + 1) * A_QK_DIM].astype(F32) * (A_QK_DIM ** -0.5)
        v = p_ref[:, 2 * A_HQ + h * A_V_DIM:2 * A_HQ + (h + 1) * A_V_DIM]
        og = p_ref[:, 2 * A_HQ + A_HV + h * A_V_DIM:2 * A_HQ + A_HV + (h + 1) * A_V_DIM].astype(F32)
        b_row = bcum_t[A_HEADS + h:A_HEADS + h + 1, :]
        b_last = b_row[:, L - 1:L]
        c_col = gates[:, h:h + 1] - bcum[:, A_HEADS + h:A_HEADS + h + 1]
        m11 = m_ref[h][:, 0:1]
        ct_old = ct_ref[h]
        n_old = n_ref[h]

        log_d = jnp.where(causal, c_col + b_row, -jnp.inf)
        log_inter = b_row + m11
        m_t = jnp.maximum(jnp.max(log_d, axis=0, keepdims=True), log_inter)
        dmat = jnp.exp(log_d - m_t)
        inter = jnp.exp(log_inter - m_t)
        s = lax.dot_general(ksf.astype(BF16), q, nt, preferred_element_type=F32) * dmat
        vt = v.astype(F32).T.astype(BF16)
        num = (jnp.dot(vt, s.astype(BF16), preferred_element_type=F32)
               + inter * lax.dot_general(ct_old.astype(BF16), q, nt, preferred_element_type=F32))
        qn = lax.dot_general(n_old.astype(BF16), q, nt, preferred_element_type=F32)[0:1]
        den = jnp.sum(s, axis=0, keepdims=True) + inter * qn
        hh = num / jnp.maximum(jnp.abs(den), jnp.exp(-m_t))

        lw_col = b_last + c_col
        m_new = jnp.maximum(b_last + m11, jnp.max(lw_col, axis=0, keepdims=True))
        kw = ksf * jnp.exp(lw_col - m_new)
        decay = jnp.exp(b_last + m11 - m_new)
        ct_ref[h] = decay * ct_old + jnp.dot(vt, kw.astype(BF16), preferred_element_type=F32)
        n_ref[h] = decay * n_old + jnp.broadcast_to(jnp.sum(kw, axis=0, keepdims=True), n_old.shape)
        m_ref[h] = jnp.broadcast_to(m_new, (1, LANES))

        hn = hh * lax.rsqrt(jnp.mean(hh * hh, axis=0, keepdims=True) + EPS)
        hn = (hn * hn_ref[h * A_V_DIM:(h + 1) * A_V_DIM, :]).T
        o_ref[:, h * A_V_DIM:(h + 1) * A_V_DIM] = (hn * jax.nn.sigmoid(og)).astype(o_ref.dtype)


def _mlstm(proj, gates, bias_row, h_norm, bsz, seq):
    n = proj.shape[0]
    nc = seq // A_CHUNK
    h_norm_cols = jnp.broadcast_to(h_norm[:, None], (A_HV, A_CHUNK))
    return pl.pallas_call(
        _mlstm_kernel,
        grid=(bsz, nc),
        in_specs=[
            pl.BlockSpec((A_CHUNK, A_MAIN_COLS), lambda b, c: (b * nc + c, 0)),
            pl.BlockSpec((A_CHUNK, LANES), lambda b, c: (b * nc + c, 0)),
            pl.BlockSpec((1, LANES), lambda b, c: (0, 0)),
            pl.BlockSpec((A_HV, A_CHUNK), lambda b, c: (0, 0)),
        ],
        out_specs=pl.BlockSpec((A_CHUNK, A_HV), lambda b, c: (b * nc + c, 0)),
        out_shape=jax.ShapeDtypeStruct((n, A_HV), BF16),
        scratch_shapes=[
            pltpu.VMEM((A_HEADS, A_V_DIM, A_QK_DIM), F32),
            pltpu.VMEM((A_HEADS, SUBLANES, A_QK_DIM), F32),
            pltpu.VMEM((A_HEADS, 1, LANES), F32),
        ],
        compiler_params=_params(("parallel", "arbitrary")),
        name="mlstm",
    )(proj, gates, bias_row, h_norm_cols)


def _qk_prep_kernel(p_ref, pos_ref, seg_ref, segt_ref, gain_ref, freq_ref, o_ref):
    tm = p_ref.shape[0]
    x = p_ref[...].astype(F32)
    xx = x * x
    xx_hi = xx.astype(BF16)
    xx_lo = (xx - xx_hi.astype(F32)).astype(BF16)
    seg = seg_ref[...]
    ss = jnp.dot(xx_hi, seg, preferred_element_type=F32) + jnp.dot(xx_lo, seg, preferred_element_type=F32)
    r = lax.rsqrt(ss * (1.0 / B_HEAD_DIM) + EPS)
    r_hi = r.astype(BF16)
    r_lo = (r - r_hi.astype(F32)).astype(BF16)
    segt = segt_ref[...]
    r_full = jnp.dot(r_hi, segt, preferred_element_type=F32) + jnp.dot(r_lo, segt, preferred_element_type=F32)

    half = B_ROT_DIM // 2
    assert half == SUBLANES
    ang = freq_ref[:, 0:1] * pos_ref[0].astype(F32)
    cos8 = jnp.cos(ang)
    sin8 = jnp.sin(ang)
    one8 = jnp.ones_like(cos8)
    zero8 = jnp.zeros_like(cos8)
    per_seg = B_HEAD_DIM // SUBLANES
    n_seg = LANES // B_HEAD_DIM

    def chunk_pattern(first, second, rest):
        groups = ([first, second] + [rest] * (per_seg - 2)) * n_seg
        return jnp.concatenate(groups, axis=0).T

    c_mul = chunk_pattern(cos8, cos8, one8)
    s_lo = chunk_pattern(-sin8, zero8, zero8)
    s_hi = chunk_pattern(zero8, sin8, zero8)
    for j in range(2 * D_MODEL // LANES):
        sl = slice(j * LANES, (j + 1) * LANES)
        y = x[:, sl] * r_full[:, sl] * gain_ref[:, sl]
        out = y * c_mul + pltpu.roll(y, LANES - half, axis=1) * s_lo + pltpu.roll(y, half, axis=1) * s_hi
        if j < D_MODEL // LANES:
            out = out * (B_HEAD_DIM ** -0.5 * math.log2(math.e))
        o_ref[:, sl] = out.astype(o_ref.dtype)


def _qk_prep(proj, positions_row, q_norm, k_norm, seq):
    n = proj.shape[0]
    two_d = 2 * D_MODEL
    tm = _tile(256, seq)
    seg = (jnp.arange(two_d)[:, None] // B_HEAD_DIM == jnp.arange(LANES)[None, :]).astype(BF16)
    gain = jnp.concatenate([jnp.tile(q_norm, D_MODEL // B_HEAD_DIM), jnp.tile(k_norm, D_MODEL // B_HEAD_DIM)])
    inv_freq = ROPE_THETA ** (-jnp.arange(0, B_ROT_DIM, 2, dtype=F32) / B_ROT_DIM)
    freq_tab = jnp.broadcast_to(inv_freq[:, None], (B_ROT_DIM // 2, LANES))
    return pl.pallas_call(
        _qk_prep_kernel,
        grid=(n // tm,),
        in_specs=[
            pl.BlockSpec((tm, two_d), lambda i: (i, 0)),
            pl.BlockSpec((1, 1, tm), lambda i: (i, 0, 0)),
            pl.BlockSpec((two_d, LANES), lambda i: (0, 0)),
            pl.BlockSpec((LANES, two_d), lambda i: (0, 0)),
            pl.BlockSpec((1, two_d), lambda i: (0, 0)),
            pl.BlockSpec((SUBLANES, LANES), lambda i: (0, 0)),
        ],
        out_specs=pl.BlockSpec((tm, two_d), lambda i: (i, 0)),
        out_shape=jax.ShapeDtypeStruct((n, two_d), BF16),
        compiler_params=_params(("parallel",)),
        name="qk_prep",
    )(proj, positions_row.reshape(n // tm, 1, tm), seg, seg.T, gain.reshape(1, two_d), freq_tab)


ATTN_TILE = 512


def _diff_attn_kernel(q_ref, k_ref, v_ref, lam_ref, on_ref, o_ref, vt_ref, qm_ref, m_ref, l_ref, acc_ref,
                      *, tile, lam_init):
    seq = q_ref.shape[0]
    n_tiles = seq // tile
    lamv = lam_ref[...]
    lam = (jnp.exp(jnp.sum(lamv[0:1] * lamv[1:2], axis=1, keepdims=True))
           - jnp.exp(jnp.sum(lamv[2:3] * lamv[3:4], axis=1, keepdims=True)) + lam_init)
    tchunk = min(256, seq)
    for c in range(seq // tchunk):
        vt_ref[:, c * tchunk:(c + 1) * tchunk] = v_ref[c * tchunk:(c + 1) * tchunk, :].astype(F32).T.astype(BF16)
    first_map = lax.broadcasted_iota(jnp.int32, (tile, LANES), 1) < B_HEAD_DIM
    key_id = lax.broadcasted_iota(jnp.int32, (tile, tile), 0)
    query_id = lax.broadcasted_iota(jnp.int32, (tile, tile), 1)
    causal = query_id >= key_id
    nt = (((1,), (1,)), ((), ()))

    def tile_step(k, vt, mask):
        scores = [lax.dot_general(k, qm_ref[c], nt, preferred_element_type=F32) for c in range(2)]
        for c, s in enumerate(scores):
            if mask is not None:
                s = jnp.where(mask, s, -jnp.inf)
            m_old = m_ref[c]
            m_new = jnp.maximum(m_old, jnp.max(s, axis=0, keepdims=True))
            alpha = jnp.exp2(m_old - m_new)
            p = jnp.exp2(s - m_new)
            l_ref[c] = alpha * l_ref[c] + jnp.sum(p, axis=0, keepdims=True)
            acc_ref[c] = alpha * acc_ref[c] + jnp.dot(vt, p.astype(BF16), preferred_element_type=F32)
            m_ref[c] = m_new

    for qi in range(n_tiles):
        q = q_ref[qi * tile:(qi + 1) * tile, :]
        qm_ref[0] = jnp.where(first_map, q, jnp.zeros_like(q))
        qm_ref[1] = jnp.where(first_map, jnp.zeros_like(q), q)
        m_ref[...] = jnp.full(m_ref.shape, -jnp.inf, F32)
        l_ref[...] = jnp.zeros(l_ref.shape, F32)
        acc_ref[...] = jnp.zeros(acc_ref.shape, F32)
        for ki in range(qi + 1):
            tile_step(k_ref[ki * tile:(ki + 1) * tile, :], vt_ref[:, ki * tile:(ki + 1) * tile],
                      causal if ki == qi else None)
        o = acc_ref[0] / l_ref[0] - lam * (acc_ref[1] / l_ref[1])
        o = o * lax.rsqrt(jnp.mean(o * o, axis=0, keepdims=True) + EPS) * on_ref[...] * (1.0 - lam_init)
        o_ref[qi * tile:(qi + 1) * tile, :] = o.T.astype(o_ref.dtype)


def _diff_attn(qk, proj, lam_tab, o_norm, bsz, seq, lam_init):
    n = qk.shape[0]
    tile = _tile(ATTN_TILE, seq)
    o_norm_cols = jnp.broadcast_to(o_norm[:, None], (B_V_DIM, tile))
    return pl.pallas_call(
        functools.partial(_diff_attn_kernel, tile=tile, lam_init=lam_init),
        grid=(bsz, B_HEADS),
        in_specs=[
            pl.BlockSpec((seq, LANES), lambda b, h: (b, h)),
            pl.BlockSpec((seq, LANES), lambda b, h: (b, B_HEADS + h)),
            pl.BlockSpec((seq, B_V_DIM), lambda b, h: (b, 2 * B_HEADS + h)),
            pl.BlockSpec((SUBLANES, LANES), lambda b, h: (0, 0)),
            pl.BlockSpec((B_V_DIM, tile), lambda b, h: (0, 0)),
        ],
        out_specs=pl.BlockSpec((seq, B_V_DIM), lambda b, h: (b, h)),
        out_shape=jax.ShapeDtypeStruct((n, D_MODEL), BF16),
        scratch_shapes=[
            pltpu.VMEM((B_V_DIM, seq), BF16),
            pltpu.VMEM((2, tile, LANES), BF16),
            pltpu.VMEM((2, 1, tile), F32),
            pltpu.VMEM((2, 1, tile), F32),
            pltpu.VMEM((2, B_V_DIM, tile), F32),
        ],
        compiler_params=_params(("parallel", "parallel")),
        name="diff_attn",
    )(qk, qk, proj, lam_tab, o_norm_cols)


HALF_D = D_MODEL // 2
HI16 = 0xFFFF0000
PACKED_TILES = HALF_D // LANES


def _pack_bf16_pairs(v):
    bits = lax.bitcast_convert_type(v.astype(BF16).astype(F32), jnp.uint32)
    return (bits[:, HALF_D:] & jnp.uint32(HI16)) | (bits[:, :HALF_D] >> 16)


def _unpack_bf16_pairs(words):
    first = lax.bitcast_convert_type(words << 16, F32)
    second = lax.bitcast_convert_type(words & jnp.uint32(HI16), F32)
    return first.astype(BF16), second.astype(BF16)


def _outproj_kernel(a_ref, w_ref, x_ref, g1_ref, gain_ref, sc_ref, sh_ref, wr_ref, xo_ref, hp_ref, lg_ref):
    y = jnp.dot(a_ref[...], w_ref[...], preferred_element_type=F32)
    xm = x_ref[...] + g1_ref[0] * y
    xo_ref[...] = xm
    h2 = _modulated_rms(xm, gain_ref[...], sc_ref[0], sh_ref[0])
    tm = xm.shape[0]
    packed = _pack_bf16_pairs(h2)
    for j in range(ROW_TILES):
        hp_ref[pl.ds(j, tm, stride=ROW_TILES), :] = (
            packed[:, j * LANES:(j + 1) * LANES] if j < PACKED_TILES else jnp.zeros((tm, LANES), jnp.uint32))
    lg_ref[...] = lax.dot_general(wr_ref[...], h2, (((1,), (1,)), ((), ())),
                                  precision=HIGHEST, preferred_element_type=F32)


def _outproj_norm_router(a, w, x, g1, gain, sc1p, sh, w_router_t, seq):
    n, d = x.shape
    tm = _tile(512, seq)
    per_b = seq // tm
    bmap = lambda i: (i // per_b, 0, 0)
    return pl.pallas_call(
        _outproj_kernel,
        grid=(n // tm,),
        in_specs=[
            pl.BlockSpec((tm, a.shape[1]), lambda i: (i, 0)),
            pl.BlockSpec(w.shape, lambda i: (0, 0)),
            pl.BlockSpec((tm, d), lambda i: (i, 0)),
            pl.BlockSpec((1, 1, d), bmap),
            pl.BlockSpec((1, d), lambda i: (0, 0)),
            pl.BlockSpec((1, 1, d), bmap),
            pl.BlockSpec((1, 1, d), bmap),
            pl.BlockSpec((N_EXPERTS, d), lambda i: (0, 0)),
        ],
        out_specs=[
            pl.BlockSpec((tm, d), lambda i: (i, 0)),
            pl.BlockSpec((tm * ROW_TILES, LANES), lambda i: (i, 0)),
            pl.BlockSpec((N_EXPERTS, tm), lambda i: (0, i)),
        ],
        out_shape=[
            jax.ShapeDtypeStruct((n, d), F32),
            jax.ShapeDtypeStruct((n * ROW_TILES, LANES), jnp.uint32),
            jax.ShapeDtypeStruct((N_EXPERTS, n), F32),
        ],
        compiler_params=_params(("parallel",)),
        name="outproj_norm_router",
    )(a, w, x, g1, gain, sc1p, sh, w_router_t)


PAIR_LO = (0, 0, 0, 1, 1, 2)
PAIR_HI = (1, 2, 3, 2, 3, 3)
PAIRS_PER_GROUP = len(PAIR_LO)
N_CLASSES = N_GROUPS * PAIRS_PER_GROUP
CLASS_ROWS = 32


def _route_kernel(lg_ref, bias_ref, oi_ref, cnt_ref, carry_ref, tri_ref):
    tr = lg_ref.shape[1]
    step = pl.program_id(0)

    @pl.when(step == 0)
    def _():
        carry_ref[...] = jnp.zeros_like(carry_ref)
        r = lax.broadcasted_iota(jnp.int32, (tr, tr), 0)
        c = lax.broadcasted_iota(jnp.int32, (tr, tr), 1)
        tri_ref[...] = jnp.where(r < c, 1.0, 0.0).astype(BF16)

    scores = jax.nn.sigmoid(lg_ref[...])
    biased = scores + bias_ref[...]
    rows = [biased[e:e + 1, :] for e in range(N_EXPERTS)]

    def top2_sum(a, b, c, d):
        m1, n1 = jnp.maximum(a, b), jnp.minimum(a, b)
        m2, n2 = jnp.maximum(c, d), jnp.minimum(c, d)
        return jnp.maximum(m1, m2) + jnp.maximum(jnp.minimum(m1, m2), jnp.maximum(n1, n2))

    gscore = [top2_sum(*rows[g * EXPERTS_PER_GROUP:(g + 1) * EXPERTS_PER_GROUP]) for g in range(N_GROUPS)]
    best = gscore[0]
    gsel = jnp.zeros_like(best, dtype=jnp.int32)
    for g in range(1, N_GROUPS):
        upd = gscore[g] > best
        gsel = jnp.where(upd, g, gsel)
        best = jnp.where(upd, gscore[g], best)

    def pick(table, j):
        out = table[j]
        for g in range(1, N_GROUPS):
            out = jnp.where(gsel == g, table[g * EXPERTS_PER_GROUP + j], out)
        return out

    in_b = [pick(rows, j) for j in range(EXPERTS_PER_GROUP)]
    v1, i1 = in_b[0], jnp.zeros_like(gsel)
    for j in range(1, EXPERTS_PER_GROUP):
        upd = in_b[j] > v1
        v1 = jnp.where(upd, in_b[j], v1)
        i1 = jnp.where(upd, j, i1)
    v2 = jnp.full_like(v1, -jnp.inf)
    i2 = jnp.zeros_like(gsel)
    for j in range(EXPERTS_PER_GROUP):
        upd = (i1 != j) & (in_b[j] > v2)
        v2 = jnp.where(upd, in_b[j], v2)
        i2 = jnp.where(upd, j, i2)
    first_is_lo = i1 < i2
    lo = jnp.where(first_is_lo, i1, i2)
    hi = jnp.where(first_is_lo, i2, i1)
    pair = jnp.where(lo == 0, hi - 1, jnp.where(lo == 1, hi + 1, PAIRS_PER_GROUP - 1))
    cls = gsel * PAIRS_PER_GROUP + pair

    cid = lax.broadcasted_iota(jnp.int32, (CLASS_ROWS, tr), 0)
    onehot = jnp.where(cid == cls, 1.0, 0.0)
    before = jnp.dot(onehot.astype(BF16), tri_ref[...], preferred_element_type=F32) + carry_ref[...]
    oi_ref[0:1, :] = cls
    oi_ref[1:2, :] = jnp.sum(onehot * before, axis=0, keepdims=True).astype(jnp.int32)
    new_carry = carry_ref[...] + jnp.sum(onehot, axis=1, keepdims=True)
    carry_ref[...] = new_carry
    cnt_ref[...] = new_carry.astype(jnp.int32)


def _route(logits_t, router_bias):
    n = logits_t.shape[1]
    tr = _tile(512, n)
    return pl.pallas_call(
        _route_kernel,
        grid=(n // tr,),
        in_specs=[
            pl.BlockSpec((N_EXPERTS, tr), lambda i: (0, i)),
            pl.BlockSpec((N_EXPERTS, 1), lambda i: (0, 0)),
        ],
        out_specs=[
            pl.BlockSpec((2, tr), lambda i: (0, i)),
            pl.BlockSpec((CLASS_ROWS, 1), lambda i: (0, 0)),
        ],
        out_shape=[
            jax.ShapeDtypeStruct((2, n), jnp.int32),
            jax.ShapeDtypeStruct((CLASS_ROWS, 1), jnp.int32),
        ],
        scratch_shapes=[pltpu.VMEM((CLASS_ROWS, 1), F32), pltpu.VMEM((tr, tr), BF16)],
        compiler_params=_params(("arbitrary",)),
        name="route",
    )(logits_t, router_bias.reshape(N_EXPERTS, 1).astype(F32))


SORT_UNROLL = 8
MOE_ROWS = 256
MOE_ROWS_LOG2 = MOE_ROWS.bit_length() - 1
assert 1 << MOE_ROWS_LOG2 == MOE_ROWS


def _plan_kernel(cls_ref, rank_ref, cnt_ref, tok_ref, ea_ref, eb_ref, off_ref, nv_ref, nu_ref, start_ref):
    n_blocks = ea_ref.shape[0]
    run = jnp.int32(0)
    blk = jnp.int32(0)
    for c in range(N_CLASSES):
        cnt = cnt_ref[c]
        start_ref[c] = run
        e_lo = (c // PAIRS_PER_GROUP) * EXPERTS_PER_GROUP + PAIR_LO[c % PAIRS_PER_GROUP]
        e_hi = (c // PAIRS_PER_GROUP) * EXPERTS_PER_GROUP + PAIR_HI[c % PAIRS_PER_GROUP]
        n_blk = (cnt + (MOE_ROWS - 1)) >> MOE_ROWS_LOG2

        def fill(b, carry, run=run, blk=blk, cnt=cnt, e_lo=e_lo, e_hi=e_hi):
            ea_ref[blk + b] = e_lo
            eb_ref[blk + b] = e_hi
            off_ref[blk + b] = run + b * MOE_ROWS
            nv_ref[blk + b] = jnp.minimum(cnt - b * MOE_ROWS, MOE_ROWS)
            return carry

        lax.fori_loop(0, n_blk, fill, 0)
        run = run + cnt
        blk = blk + n_blk
    nu_ref[0] = blk

    def fill_unused(b, carry):
        ea_ref[b] = ea_ref[blk - 1]
        eb_ref[b] = eb_ref[blk - 1]
        off_ref[b] = 0
        nv_ref[b] = 0
        return carry

    lax.fori_loop(blk, n_blocks, fill_unused, 0)

    def place(i, carry):
        for u in range(SORT_UNROLL):
            t = i * SORT_UNROLL + u
            tok_ref[start_ref[cls_ref[t]] + rank_ref[t]] = t
        return carry

    lax.fori_loop(0, cls_ref.shape[0] // SORT_UNROLL, place, 0)


def _plan(cls, rank, counts, n_blocks):
    n = cls.shape[0]
    assert n % SORT_UNROLL == 0
    smem = pl.BlockSpec(memory_space=pltpu.SMEM)
    i32 = lambda size: jax.ShapeDtypeStruct((size,), jnp.int32)
    return pl.pallas_call(
        _plan_kernel,
        in_specs=[smem, smem, smem],
        out_specs=[smem] * 6,
        out_shape=[i32(n), i32(n_blocks), i32(n_blocks), i32(n_blocks), i32(n_blocks), i32(1)],
        scratch_shapes=[pltpu.SMEM((N_CLASSES,), jnp.int32)],
        name="moe_plan",
    )(cls, rank, counts)


ROW_UNROLL = 8


def _for_rows(n, fn):
    groups = n // ROW_UNROLL

    def group(g, c):
        for u in range(ROW_UNROLL):
            fn(g * ROW_UNROLL + u)
        return c

    def single(r, c):
        fn(r)
        return c

    lax.fori_loop(0, groups, group, 0)
    lax.fori_loop(groups * ROW_UNROLL, n, single, 0)


def _row_copy(src_ref, src_row, dst_ref, dst_row, sem):
    return pltpu.make_async_copy(
        src_ref.at[pl.ds(pl.multiple_of(src_row * ROW_TILES, ROW_TILES), ROW_TILES)],
        dst_ref.at[pl.ds(pl.multiple_of(dst_row * ROW_TILES, ROW_TILES), ROW_TILES)],
        sem)


def _expert_kernel(ea_ref, eb_ref, off_ref, nv_ref, nu_ref, tok_ref,
                   h_hbm, wgu_a_ref, wd_a_ref, wgu_b_ref, wd_b_ref, wr_a_ref, wr_b_ref, y_hbm,
                   xbuf, ybuf, gsem, ssem):
    del ea_ref, eb_ref
    tb = xbuf.shape[1] // ROW_TILES
    i = pl.program_id(0)
    slot = i % 2
    n_used = nu_ref[0]

    def start_gather(block, s):
        base = off_ref[block]
        _for_rows(nv_ref[block],
                  lambda r: _row_copy(h_hbm, tok_ref[base + r], xbuf.at[s], r, gsem.at[s]).start())

    def start_scatter(block, s):
        base = off_ref[block]
        _for_rows(nv_ref[block],
                  lambda r: _row_copy(ybuf.at[s], r, y_hbm, tok_ref[base + r], ssem.at[s]).start())

    def wait_rows(block, hbm, buf, sem):
        rows = nv_ref[block] * ROW_TILES

        @pl.when(rows > 0)
        def _():
            pltpu.make_async_copy(hbm.at[pl.ds(0, rows)], buf.at[pl.ds(0, rows)], sem).wait()

    @pl.when(i == 0)
    def _():
        xbuf[...] = jnp.zeros_like(xbuf)
        start_gather(0, 0)

    @pl.when(i < n_used)
    def _():
        @pl.when(i + 1 < n_used)
        def _():
            start_gather(i + 1, 1 - slot)

        wait_rows(i, h_hbm, xbuf.at[slot], gsem.at[slot])

        @pl.when(i >= 2)
        def _():
            wait_rows(i - 2, y_hbm, ybuf.at[slot], ssem.at[slot])

        words = [xbuf[slot, pl.ds(j, tb, stride=ROW_TILES), :] for j in range(PACKED_TILES)]
        halves = [_unpack_bf16_pairs(w) for w in words]
        x = jnp.concatenate([h[0] for h in halves] + [h[1] for h in halves], axis=1)

        def mlp(wgu_ref, wd_ref):
            gu = jnp.dot(x, wgu_ref[0], preferred_element_type=F32)
            gate = gu[:, :D_EXPERT]
            act = gate * jax.nn.sigmoid(gate) * gu[:, D_EXPERT:]
            return jnp.dot(act.astype(BF16), wd_ref[0], preferred_element_type=F32)

        score_a = jax.nn.sigmoid(jnp.dot(x, wr_a_ref[0], preferred_element_type=F32))
        score_b = jax.nn.sigmoid(jnp.dot(x, wr_b_ref[0], preferred_element_type=F32))
        inv_tot = 1.0 / (score_a + score_b)
        w_a = score_a * inv_tot
        w_b = score_b * inv_tot
        ya = mlp(wgu_a_ref, wd_a_ref)
        yb = mlp(wgu_b_ref, wd_b_ref)
        for j in range(ROW_TILES):
            sl = slice(j * LANES, (j + 1) * LANES)
            ybuf[slot, pl.ds(j, tb, stride=ROW_TILES), :] = w_a * ya[:, sl] + w_b * yb[:, sl]
        start_scatter(i, slot)

        @pl.when(i == n_used - 1)
        def _():
            wait_rows(i, y_hbm, ybuf.at[slot], ssem.at[slot])

            @pl.when(i >= 1)
            def _():
                wait_rows(i - 1, y_hbm, ybuf.at[1 - slot], ssem.at[1 - slot])


def _experts(tables, sorted_tok, h_rows, layer, w_gu, w_down, wr_bcast, tb):
    block_ea, block_eb, src_off, n_valid, n_used = tables
    n_blocks = block_ea.shape[0]
    d, two_f = w_gu.shape[2:]
    ea_map = lambda i, ea, eb, off, nv, nu, tok: (ea[i], 0, 0)
    eb_map = lambda i, ea, eb, off, nv, nu, tok: (eb[i], 0, 0)
    lea_map = lambda i, ea, eb, off, nv, nu, tok: (layer, ea[i], 0, 0)
    leb_map = lambda i, ea, eb, off, nv, nu, tok: (layer, eb[i], 0, 0)
    grid_spec = pltpu.PrefetchScalarGridSpec(
        num_scalar_prefetch=6,
        grid=(n_blocks,),
        in_specs=[
            pl.BlockSpec(memory_space=pl.ANY),
            pl.BlockSpec((None, 1, d, two_f), lea_map),
            pl.BlockSpec((None, 1, two_f // 2, d), lea_map),
            pl.BlockSpec((None, 1, d, two_f), leb_map),
            pl.BlockSpec((None, 1, two_f // 2, d), leb_map),
            pl.BlockSpec((1, d, LANES), ea_map),
            pl.BlockSpec((1, d, LANES), eb_map),
        ],
        out_specs=pl.BlockSpec(memory_space=pl.ANY),
        scratch_shapes=[
            pltpu.VMEM((2, tb * ROW_TILES, LANES), jnp.uint32),
            pltpu.VMEM((2, tb * ROW_TILES, LANES), F32),
            pltpu.SemaphoreType.DMA((2,)),
            pltpu.SemaphoreType.DMA((2,)),
        ],
    )
    return pl.pallas_call(
        _expert_kernel,
        grid_spec=grid_spec,
        out_shape=jax.ShapeDtypeStruct(h_rows.shape, F32),
        compiler_params=_params(("arbitrary",)),
        name="moe_experts",
    )(block_ea, block_eb, src_off, n_valid, n_used, sorted_tok, h_rows, w_gu, w_down, w_gu, w_down, wr_bcast, wr_bcast)


def _residual_kernel(y_ref, x_ref, g2_ref, o_ref):
    o_ref[...] = _gated_residual(x_ref, y_ref, g2_ref)


def _residual(y_rows, x_mid, g2, seq):
    n, d = x_mid.shape
    tc = _tile(512, seq)
    per_b = seq // tc
    return pl.pallas_call(
        _residual_kernel,
        grid=(n // tc,),
        in_specs=[
            pl.BlockSpec((tc * ROW_TILES, LANES), lambda i: (i, 0)),
            pl.BlockSpec((tc, d), lambda i: (i, 0)),
            pl.BlockSpec((1, 1, d), lambda i: (i // per_b, 0, 0)),
        ],
        out_specs=pl.BlockSpec((tc, d), lambda i: (i, 0)),
        out_shape=jax.ShapeDtypeStruct((n, d), F32),
        compiler_params=_params(("parallel",)),
        name="moe_residual",
    )(y_rows, x_mid, g2)


def _moe(h_rows, logits_t, router_bias, layer, w_gu, w_down, wr_bcast):
    n = logits_t.shape[1]
    n_blocks = (n + N_CLASSES * (MOE_ROWS - 1) + MOE_ROWS - 1) // MOE_ROWS
    route_i, counts = _route(logits_t, router_bias)
    sorted_tok, *tables = _plan(route_i[0], route_i[1], counts.reshape(CLASS_ROWS), n_blocks)
    return _experts(tables, sorted_tok, h_rows, layer, w_gu, w_down, wr_bcast, MOE_ROWS)


def kernel(x, c, positions, norm1, norm2, w_ada, b_ada, a_w_in, a_b_if, a_h_norm, a_w_out, b_w_in, b_q_norm, b_k_norm, b_lam_q1, b_lam_k1, b_lam_q2, b_lam_k2, b_o_norm, b_w_out, w_router, router_bias, moe_w_gu, moe_w_down):
    bsz, seq, d = x.shape
    depth = w_ada.shape[0]
    n = bsz * seq
    xf = x.reshape(n, d)
    mod = _ada_mod(c, w_ada, b_ada)
    w_router_t = w_router.T
    wr_bcast = jnp.broadcast_to(w_router_t[:, :, None], (N_EXPERTS, d, LANES)).astype(BF16)
    pos_row = positions.reshape(n)
    w_gu = moe_w_gu.astype(BF16)
    w_down = moe_w_down.astype(BF16)

    residual = None
    for l in range(depth):
        sh1, sc1, g1, sh2, sc2, g2 = [mod[l, :, i * d:(i + 1) * d].reshape(bsz, 1, d) for i in range(6)]
        j = l // 2
        if l % 2 == 0:
            w_in = a_w_in[j]
            w_main = w_in[:, :A_MAIN_COLS].astype(BF16)
            w_gate = jnp.pad(w_in[:, A_MAIN_COLS:], ((0, 0), (0, LANES - 2 * A_HEADS))).astype(BF16)
        else:
            w_main, w_gate = b_w_in[j].astype(BF16), None
        outs = _norm_matmul(xf, residual, norm1[l].reshape(1, d), 1.0 + sc1, sh1, w_main, w_gate, seq)
        if residual is not None:
            xf, *outs = outs
        if l % 2 == 0:
            proj, gates = outs
            bias_row = jnp.pad(a_b_if[j], (0, LANES - 2 * A_HEADS)).reshape(1, LANES)
            mixed = _mlstm(proj, gates, bias_row, a_h_norm[j], bsz, seq)
            w_out = a_w_out[j].astype(BF16)
        else:
            (proj,) = outs
            qk = _qk_prep(proj, pos_row, b_q_norm[j], b_k_norm[j], seq)
            lam_tab = jnp.zeros((SUBLANES, LANES), F32)
            for r, v in enumerate((b_lam_q1[j], b_lam_k1[j], b_lam_q2[j], b_lam_k2[j])):
                lam_tab = lam_tab.at[r, :B_HEAD_DIM].set(v)
            lam_init = 0.8 - 0.6 * math.exp(-0.3 * l)
            mixed = _diff_attn(qk, proj, lam_tab, b_o_norm[j], bsz, seq, lam_init)
            w_out = b_w_out[j].astype(BF16)
        xf, h_rows, logits_t = _outproj_norm_router(
            mixed, w_out, xf, g1, norm2[l].reshape(1, d), 1.0 + sc2, sh2, w_router_t, seq)
        residual = (_moe(h_rows, logits_t, router_bias, l, w_gu, w_down, wr_bcast), g2)
    return _residual(residual[0], xf, residual[1], seq).reshape(bsz, seq, d)
```

```python
import functools
import math

import jax
import jax.numpy as jnp
from jax import lax
from jax.experimental import pallas as pl
from jax.experimental.pallas import tpu as pltpu

D_MODEL = 1024
A_HEADS = 4
A_QK_DIM = 128
A_V_DIM = 256
A_CHUNK = 128
A_HQ = A_HEADS * A_QK_DIM
A_HV = A_HEADS * A_V_DIM
A_MAIN_COLS = 2 * A_HQ + 2 * A_HV

B_HEADS = 8
B_HEAD_DIM = 64
B_V_DIM = 128
B_ROT_DIM = 16
ROPE_THETA = 500000.0

N_EXPERTS = 16
N_GROUPS = 4
EXPERTS_PER_GROUP = 4
TOP_K = 2
D_EXPERT = 512
EPS = 1e-6

LANES = 128
SUBLANES = 8
ROW_TILES = D_MODEL // LANES
assert ROW_TILES == SUBLANES
VMEM_LIMIT = 48 * 1024 * 1024

F32 = jnp.float32
BF16 = jnp.bfloat16
HIGHEST = lax.Precision.HIGHEST


def _params(sem):
    return pltpu.CompilerParams(dimension_semantics=sem, vmem_limit_bytes=VMEM_LIMIT)


def _tile(pref, n):
    t = min(pref, n)
    assert n % t == 0, (pref, n)
    return t


def _ada_kernel(c_ref, w_ref, b_ref, o_ref):
    c = c_ref[...]
    c_act = c * jax.nn.sigmoid(c)
    o_ref[0] = jnp.dot(c_act, w_ref[0], precision=HIGHEST, preferred_element_type=F32) + b_ref[0]


def _ada_mod(c, w_ada, b_ada):
    depth, d, six_d = w_ada.shape
    bsz = c.shape[0]
    tn = _tile(1536, six_d)
    return pl.pallas_call(
        _ada_kernel,
        grid=(depth, six_d // tn),
        in_specs=[
            pl.BlockSpec((bsz, d), lambda l, j: (0, 0)),
            pl.BlockSpec((1, d, tn), lambda l, j: (l, 0, j)),
            pl.BlockSpec((1, 1, tn), lambda l, j: (l, 0, j)),
        ],
        out_specs=pl.BlockSpec((1, bsz, tn), lambda l, j: (l, 0, j)),
        out_shape=jax.ShapeDtypeStruct((depth, bsz, six_d), F32),
        compiler_params=_params(("parallel", "parallel")),
        name="ada_mod",
    )(c, w_ada, b_ada.reshape(depth, 1, six_d))


def _modulated_rms(x, g, sc1p, sh):
    y = x * lax.rsqrt(jnp.mean(x * x, axis=-1, keepdims=True) + EPS)
    return (y * g) * sc1p + sh


def _gated_residual(x_ref, y_ref, g_ref):
    rows = x_ref.shape[0]
    g = g_ref[0]
    return jnp.concatenate(
        [x_ref[:, j * LANES:(j + 1) * LANES]
         + g[:, j * LANES:(j + 1) * LANES] * y_ref[pl.ds(j, rows, stride=ROW_TILES), :] for j in range(ROW_TILES)],
        axis=1)


def _norm_mm_kernel(*refs, col_chunk, has_gates, has_residual):
    refs = list(refs)
    x_ref = refs.pop(0)
    if has_residual:
        y_ref, g2_ref = refs.pop(0), refs.pop(0)
    g_ref, sc_ref, sh_ref, w_ref = refs[:4]
    refs = refs[4:]
    wg_ref = refs.pop(0) if has_gates else None
    xo_ref = refs.pop(0) if has_residual else None
    o_ref = refs.pop(0)
    og_ref = refs.pop(0) if has_gates else None

    if has_residual:
        x = _gated_residual(x_ref, y_ref, g2_ref)
        xo_ref[...] = x
    else:
        x = x_ref[...]
    hb = _modulated_rms(x, g_ref[...], sc_ref[0], sh_ref[0]).astype(BF16)
    for c0 in range(0, o_ref.shape[1], col_chunk):
        o_ref[:, c0:c0 + col_chunk] = jnp.dot(
            hb, w_ref[:, c0:c0 + col_chunk], preferred_element_type=F32).astype(o_ref.dtype)
    if has_gates:
        og_ref[...] = jnp.dot(hb, wg_ref[...], preferred_element_type=F32)


def _norm_matmul(x, residual, gain, sc1p, sh, w, wg, seq):
    n, d = x.shape
    cols = w.shape[1]
    tm = _tile(512, seq)
    per_b = seq // tm
    has_gates = wg is not None
    has_residual = residual is not None
    bmap = lambda i: (i // per_b, 0, 0)
    in_specs = [pl.BlockSpec((tm, d), lambda i: (i, 0))]
    args = [x]
    if has_residual:
        in_specs += [pl.BlockSpec((tm * ROW_TILES, LANES), lambda i: (i, 0)), pl.BlockSpec((1, 1, d), bmap)]
        args += list(residual)
    in_specs += [
        pl.BlockSpec((1, d), lambda i: (0, 0)),
        pl.BlockSpec((1, 1, d), bmap),
        pl.BlockSpec((1, 1, d), bmap),
        pl.BlockSpec((d, cols), lambda i: (0, 0)),
    ]
    args += [gain, sc1p, sh, w]
    out_specs, out_shape = [], []
    if has_residual:
        out_specs.append(pl.BlockSpec((tm, d), lambda i: (i, 0)))
        out_shape.append(jax.ShapeDtypeStruct((n, d), F32))
    out_specs.append(pl.BlockSpec((tm, cols), lambda i: (i, 0)))
    out_shape.append(jax.ShapeDtypeStruct((n, cols), BF16))
    if has_gates:
        in_specs.append(pl.BlockSpec((d, LANES), lambda i: (0, 0)))
        out_specs.append(pl.BlockSpec((tm, LANES), lambda i: (i, 0)))
        out_shape.append(jax.ShapeDtypeStruct((n, LANES), F32))
        args.append(wg)
    return pl.pallas_call(
        functools.partial(_norm_mm_kernel, col_chunk=512, has_gates=has_gates, has_residual=has_residual),
        grid=(n // tm,),
        in_specs=in_specs,
        out_specs=out_specs,
        out_shape=out_shape,
        compiler_params=_params(("parallel",)),
        name="norm_inproj",
    )(*args)


def _log_sigmoid(x):
    return jnp.minimum(x, 0.0) - jnp.log1p(jnp.exp(-jnp.abs(x)))


def _mlstm_kernel(p_ref, g_ref, bias_ref, hn_ref, o_ref, ct_ref, n_ref, m_ref):
    L = A_CHUNK

    @pl.when(pl.program_id(1) == 0)
    def _():
        ct_ref[...] = jnp.zeros_like(ct_ref)
        n_ref[...] = jnp.zeros_like(n_ref)
        m_ref[...] = jnp.zeros_like(m_ref)

    gates = g_ref[...] + bias_ref[...]
    src = lax.broadcasted_iota(jnp.int32, (L, L), 0)
    tgt = lax.broadcasted_iota(jnp.int32, (L, L), 1)
    causal = src <= tgt
    bcum = jnp.dot(jnp.where(src >= tgt, 1.0, 0.0), _log_sigmoid(gates), precision=HIGHEST,
                   preferred_element_type=F32)
    bcum_t = bcum.T
    nt = (((1,), (1,)), ((), ()))

    for h in range(A_HEADS):
        q = p_ref[:, h * A_QK_DIM:(h + 1) * A_QK_DIM]
        ksf = p_ref[:, A_HQ + h * A_QK_DIM:A_HQ + (h + 1) * A_QK_DIM].astype(F32) * (A_QK_DIM ** -0.5)
        v = p_ref[:, 2 * A_HQ + h * A_V_DIM:2 * A_HQ + (h + 1) * A_V_DIM]
        og = p_ref[:, 2 * A_HQ + A_HV + h * A_V_DIM:2 * A_HQ + A_HV + (h + 1) * A_V_DIM].astype(F32)
        b_row = bcum_t[A_HEADS + h:A_HEADS + h + 1, :]
        b_last = b_row[:, L - 1:L]
        c_col = gates[:, h:h + 1] - bcum[:, A_HEADS + h:A_HEADS + h + 1]
        m11 = m_ref[h][:, 0:1]
        ct_old = ct_ref[h]
        n_old = n_ref[h]

        log_d = jnp.where(causal, c_col + b_row, -jnp.inf)
        log_inter = b_row + m11
        m_t = jnp.maximum(jnp.max(log_d, axis=0, keepdims=True), log_inter)
        dmat = jnp.exp(log_d - m_t)
        inter = jnp.exp(log_inter - m_t)
        s = lax.dot_general(ksf.astype(BF16), q, nt, preferred_element_type=F32) * dmat
        vt = v.astype(F32).T.astype(BF16)
        num = (jnp.dot(vt, s.astype(BF16), preferred_element_type=F32)
               + inter * lax.dot_general(ct_old.astype(BF16), q, nt, preferred_element_type=F32))
        qn = lax.dot_general(n_old.astype(BF16), q, nt, preferred_element_type=F32)[0:1]
        den = jnp.sum(s, axis=0, keepdims=True) + inter * qn
        hh = num / jnp.maximum(jnp.abs(den), jnp.exp(-m_t))

        lw_col = b_last + c_col
        m_new = jnp.maximum(b_last + m11, jnp.max(lw_col, axis=0, keepdims=True))
        kw = ksf * jnp.exp(lw_col - m_new)
        decay = jnp.exp(b_last + m11 - m_new)
        ct_ref[h] = decay * ct_old + jnp.dot(vt, kw.astype(BF16), preferred_element_type=F32)
        n_ref[h] = decay * n_old + jnp.broadcast_to(jnp.sum(kw, axis=0, keepdims=True), n_old.shape)
        m_ref[h] = jnp.broadcast_to(m_new, (1, LANES))

        hn = hh * lax.rsqrt(jnp.mean(hh * hh, axis=0, keepdims=True) + EPS)
        hn = (hn * hn_ref[h * A_V_DIM:(h + 1) * A_V_DIM, :]).T
        o_ref[:, h * A_V_DIM:(h + 1) * A_V_DIM] = (hn * jax.nn.sigmoid(og)).astype(o_ref.dtype)


def _mlstm(proj, gates, bias_row, h_norm, bsz, seq):
    n = proj.shape[0]
    nc = seq // A_CHUNK
    h_norm_cols = jnp.broadcast_to(h_norm[:, None], (A_HV, A_CHUNK))
    return pl.pallas_call(
        _mlstm_kernel,
        grid=(bsz, nc),
        in_specs=[
            pl.BlockSpec((A_CHUNK, A_MAIN_COLS), lambda b, c: (b * nc + c, 0)),
            pl.BlockSpec((A_CHUNK, LANES), lambda b, c: (b * nc + c, 0)),
            pl.BlockSpec((1, LANES), lambda b, c: (0, 0)),
            pl.BlockSpec((A_HV, A_CHUNK), lambda b, c: (0, 0)),
        ],
        out_specs=pl.BlockSpec((A_CHUNK, A_HV), lambda b, c: (b * nc + c, 0)),
        out_shape=jax.ShapeDtypeStruct((n, A_HV), BF16),
        scratch_shapes=[
            pltpu.VMEM((A_HEADS, A_V_DIM, A_QK_DIM), F32),
            pltpu.VMEM((A_HEADS, SUBLANES, A_QK_DIM), F32),
            pltpu.VMEM((A_HEADS, 1, LANES), F32),
        ],
        compiler_params=_params(("parallel", "arbitrary")),
        name="mlstm",
    )(proj, gates, bias_row, h_norm_cols)


def _qk_prep_kernel(p_ref, pos_ref, seg_ref, segt_ref, gain_ref, freq_ref, o_ref):
    tm = p_ref.shape[0]
    x = p_ref[...].astype(F32)
    xx = x * x
    xx_hi = xx.astype(BF16)
    xx_lo = (xx - xx_hi.astype(F32)).astype(BF16)
    seg = seg_ref[...]
    ss = jnp.dot(xx_hi, seg, preferred_element_type=F32) + jnp.dot(xx_lo, seg, preferred_element_type=F32)
    r = lax.rsqrt(ss * (1.0 / B_HEAD_DIM) + EPS)
    r_hi = r.astype(BF16)
    r_lo = (r - r_hi.astype(F32)).astype(BF16)
    segt = segt_ref[...]
    r_full = jnp.dot(r_hi, segt, preferred_element_type=F32) + jnp.dot(r_lo, segt, preferred_element_type=F32)

    half = B_ROT_DIM // 2
    assert half == SUBLANES
    ang = freq_ref[:, 0:1] * pos_ref[0].astype(F32)
    cos8 = jnp.cos(ang)
    sin8 = jnp.sin(ang)
    one8 = jnp.ones_like(cos8)
    zero8 = jnp.zeros_like(cos8)
    per_seg = B_HEAD_DIM // SUBLANES
    n_seg = LANES // B_HEAD_DIM

    def chunk_pattern(first, second, rest):
        groups = ([first, second] + [rest] * (per_seg - 2)) * n_seg
        return jnp.concatenate(groups, axis=0).T

    c_mul = chunk_pattern(cos8, cos8, one8)
    s_lo = chunk_pattern(-sin8, zero8, zero8)
    s_hi = chunk_pattern(zero8, sin8, zero8)
    for j in range(2 * D_MODEL // LANES):
        sl = slice(j * LANES, (j + 1) * LANES)
        y = x[:, sl] * r_full[:, sl] * gain_ref[:, sl]
        out = y * c_mul + pltpu.roll(y, LANES - half, axis=1) * s_lo + pltpu.roll(y, half, axis=1) * s_hi
        if j < D_MODEL // LANES:
            out = out * (B_HEAD_DIM ** -0.5 * math.log2(math.e))
        o_ref[:, sl] = out.astype(o_ref.dtype)


def _qk_prep(proj, positions_row, q_norm, k_norm, seq):
    n = proj.shape[0]
    two_d = 2 * D_MODEL
    tm = _tile(256, seq)
    seg = (jnp.arange(two_d)[:, None] // B_HEAD_DIM == jnp.arange(LANES)[None, :]).astype(BF16)
    gain = jnp.concatenate([jnp.tile(q_norm, D_MODEL // B_HEAD_DIM), jnp.tile(k_norm, D_MODEL // B_HEAD_DIM)])
    inv_freq = ROPE_THETA ** (-jnp.arange(0, B_ROT_DIM, 2, dtype=F32) / B_ROT_DIM)
    freq_tab = jnp.broadcast_to(inv_freq[:, None], (B_ROT_DIM // 2, LANES))
    return pl.pallas_call(
        _qk_prep_kernel,
        grid=(n // tm,),
        in_specs=[
            pl.BlockSpec((tm, two_d), lambda i: (i, 0)),
            pl.BlockSpec((1, 1, tm), lambda i: (i, 0, 0)),
            pl.BlockSpec((two_d, LANES), lambda i: (0, 0)),
            pl.BlockSpec((LANES, two_d), lambda i: (0, 0)),
            pl.BlockSpec((1, two_d), lambda i: (0, 0)),
            pl.BlockSpec((SUBLANES, LANES), lambda i: (0, 0)),
        ],
        out_specs=pl.BlockSpec((tm, two_d), lambda i: (i, 0)),
        out_shape=jax.ShapeDtypeStruct((n, two_d), BF16),
        compiler_params=_params(("parallel",)),
        name="qk_prep",
    )(proj, positions_row.reshape(n // tm, 1, tm), seg, seg.T, gain.reshape(1, two_d), freq_tab)


ATTN_TILE = 512


def _diff_attn_kernel(q_ref, k_ref, v_ref, lam_ref, on_ref, o_ref, vt_ref, qm_ref, m_ref, l_ref, acc_ref,
                      *, tile, lam_init):
    seq = q_ref.shape[0]
    n_tiles = seq // tile
    lamv = lam_ref[...]
    lam = (jnp.exp(jnp.sum(lamv[0:1] * lamv[1:2], axis=1, keepdims=True))
           - jnp.exp(jnp.sum(lamv[2:3] * lamv[3:4], axis=1, keepdims=True)) + lam_init)
    tchunk = min(256, seq)
    for c in range(seq // tchunk):
        vt_ref[:, c * tchunk:(c + 1) * tchunk] = v_ref[c * tchunk:(c + 1) * tchunk, :].astype(F32).T.astype(BF16)
    first_map = lax.broadcasted_iota(jnp.int32, (tile, LANES), 1) < B_HEAD_DIM
    key_id = lax.broadcasted_iota(jnp.int32, (tile, tile), 0)
    query_id = lax.broadcasted_iota(jnp.int32, (tile, tile), 1)
    causal = query_id >= key_id
    nt = (((1,), (1,)), ((), ()))

    def tile_step(k, vt, mask):
        scores = [lax.dot_general(k, qm_ref[c], nt, preferred_element_type=F32) for c in range(2)]
        for c, s in enumerate(scores):
            if mask is not None:
                s = jnp.where(mask, s, -jnp.inf)
            m_old = m_ref[c]
            m_new = jnp.maximum(m_old, jnp.max(s, axis=0, keepdims=True))
            alpha = jnp.exp2(m_old - m_new)
            p = jnp.exp2(s - m_new)
            l_ref[c] = alpha * l_ref[c] + jnp.sum(p, axis=0, keepdims=True)
            acc_ref[c] = alpha * acc_ref[c] + jnp.dot(vt, p.astype(BF16), preferred_element_type=F32)
            m_ref[c] = m_new

    for qi in range(n_tiles):
        q = q_ref[qi * tile:(qi + 1) * tile, :]
        qm_ref[0] = jnp.where(first_map, q, jnp.zeros_like(q))
        qm_ref[1] = jnp.where(first_map, jnp.zeros_like(q), q)
        m_ref[...] = jnp.full(m_ref.shape, -jnp.inf, F32)
        l_ref[...] = jnp.zeros(l_ref.shape, F32)
        acc_ref[...] = jnp.zeros(acc_ref.shape, F32)
        for ki in range(qi + 1):
            tile_step(k_ref[ki * tile:(ki + 1) * tile, :], vt_ref[:, ki * tile:(ki + 1) * tile],
                      causal if ki == qi else None)
        o = acc_ref[0] / l_ref[0] - lam * (acc_ref[1] / l_ref[1])
        o = o * lax.rsqrt(jnp.mean(o * o, axis=0, keepdims=True) + EPS) * on_ref[...] * (1.0 - lam_init)
        o_ref[qi * tile:(qi + 1) * tile, :] = o.T.astype(o_ref.dtype)


def _diff_attn(qk, proj, lam_tab, o_norm, bsz, seq, lam_init):
    n = qk.shape[0]
    tile = _tile(ATTN_TILE, seq)
    o_norm_cols = jnp.broadcast_to(o_norm[:, None], (B_V_DIM, tile))
    return pl.pallas_call(
        functools.partial(_diff_attn_kernel, tile=tile, lam_init=lam_init),
        grid=(bsz, B_HEADS),
        in_specs=[
            pl.BlockSpec((seq, LANES), lambda b, h: (b, h)),
            pl.BlockSpec((seq, LANES), lambda b, h: (b, B_HEADS + h)),
            pl.BlockSpec((seq, B_V_DIM), lambda b, h: (b, 2 * B_HEADS + h)),
            pl.BlockSpec((SUBLANES, LANES), lambda b, h: (0, 0)),
            pl.BlockSpec((B_V_DIM, tile), lambda b, h: (0, 0)),
        ],
        out_specs=pl.BlockSpec((seq, B_V_DIM), lambda b, h: (b, h)),
        out_shape=jax.ShapeDtypeStruct((n, D_MODEL), BF16),
        scratch_shapes=[
            pltpu.VMEM((B_V_DIM, seq), BF16),
            pltpu.VMEM((2, tile, LANES), BF16),
            pltpu.VMEM((2, 1, tile), F32),
            pltpu.VMEM((2, 1, tile), F32),
            pltpu.VMEM((2, B_V_DIM, tile), F32),
        ],
        compiler_params=_params(("parallel", "parallel")),
        name="diff_attn",
    )(qk, qk, proj, lam_tab, o_norm_cols)


HALF_D = D_MODEL // 2
HI16 = 0xFFFF0000
PACKED_TILES = HALF_D // LANES


def _pack_bf16_pairs(v):
    bits = lax.bitcast_convert_type(v.astype(BF16).astype(F32), jnp.uint32)
    return (bits[:, HALF_D:] & jnp.uint32(HI16)) | (bits[:, :HALF_D] >> 16)


def _unpack_bf16_pairs(words):
    first = lax.bitcast_convert_type(words << 16, F32)
    second = lax.bitcast_convert_type(words & jnp.uint32(HI16), F32)
    return first.astype(BF16), second.astype(BF16)


def _outproj_kernel(a_ref, w_ref, x_ref, g1_ref, gain_ref, sc_ref, sh_ref, wr_ref, xo_ref, hp_ref, lg_ref):
    y = jnp.dot(a_ref[...], w_ref[...], preferred_element_type=F32)
    xm = x_ref[...] + g1_ref[0] * y
    xo_ref[...] = xm
    h2 = _modulated_rms(xm, gain_ref[...], sc_ref[0], sh_ref[0])
    tm = xm.shape[0]
    packed = _pack_bf16_pairs(h2)
    for j in range(ROW_TILES):
        hp_ref[pl.ds(j, tm, stride=ROW_TILES), :] = (
            packed[:, j * LANES:(j + 1) * LANES] if j < PACKED_TILES else jnp.zeros((tm, LANES), jnp.uint32))
    lg_ref[...] = lax.dot_general(wr_ref[...], h2, (((1,), (1,)), ((), ())),
                                  precision=HIGHEST, preferred_element_type=F32)


def _outproj_norm_router(a, w, x, g1, gain, sc1p, sh, w_router_t, seq):
    n, d = x.shape
    tm = _tile(512, seq)
    per_b = seq // tm
    bmap = lambda i: (i // per_b, 0, 0)
    return pl.pallas_call(
        _outproj_kernel,
        grid=(n // tm,),
        in_specs=[
            pl.BlockSpec((tm, a.shape[1]), lambda i: (i, 0)),
            pl.BlockSpec(w.shape, lambda i: (0, 0)),
            pl.BlockSpec((tm, d), lambda i: (i, 0)),
            pl.BlockSpec((1, 1, d), bmap),
            pl.BlockSpec((1, d), lambda i: (0, 0)),
            pl.BlockSpec((1, 1, d), bmap),
            pl.BlockSpec((1, 1, d), bmap),
            pl.BlockSpec((N_EXPERTS, d), lambda i: (0, 0)),
        ],
        out_specs=[
            pl.BlockSpec((tm, d), lambda i: (i, 0)),
            pl.BlockSpec((tm * ROW_TILES, LANES), lambda i: (i, 0)),
            pl.BlockSpec((N_EXPERTS, tm), lambda i: (0, i)),
        ],
        out_shape=[
            jax.ShapeDtypeStruct((n, d), F32),
            jax.ShapeDtypeStruct((n * ROW_TILES, LANES), jnp.uint32),
            jax.ShapeDtypeStruct((N_EXPERTS, n), F32),
        ],
        compiler_params=_params(("parallel",)),
        name="outproj_norm_router",
    )(a, w, x, g1, gain, sc1p, sh, w_router_t)


PAIR_LO = (0, 0, 0, 1, 1, 2)
PAIR_HI = (1, 2, 3, 2, 3, 3)
PAIRS_PER_GROUP = len(PAIR_LO)
N_CLASSES = N_GROUPS * PAIRS_PER_GROUP
CLASS_ROWS = 32
RANK_BITS = 20
RANK_SPAN = 1 << RANK_BITS


def _route_kernel(lg_ref, bias_ref, oi_ref, cnt_ref, carry_ref, tri_ref):
    tr = lg_ref.shape[1]
    step = pl.program_id(0)

    @pl.when(step == 0)
    def _():
        carry_ref[...] = jnp.zeros_like(carry_ref)
        r = lax.broadcasted_iota(jnp.int32, (tr, tr), 0)
        c = lax.broadcasted_iota(jnp.int32, (tr, tr), 1)
        tri_ref[...] = jnp.where(r < c, 1.0, 0.0).astype(BF16)

    scores = jax.nn.sigmoid(lg_ref[...])
    biased = scores + bias_ref[...]
    rows = [biased[e:e + 1, :] for e in range(N_EXPERTS)]

    def top2_sum(a, b, c, d):
        m1, n1 = jnp.maximum(a, b), jnp.minimum(a, b)
        m2, n2 = jnp.maximum(c, d), jnp.minimum(c, d)
        return jnp.maximum(m1, m2) + jnp.maximum(jnp.minimum(m1, m2), jnp.maximum(n1, n2))

    gscore = [top2_sum(*rows[g * EXPERTS_PER_GROUP:(g + 1) * EXPERTS_PER_GROUP]) for g in range(N_GROUPS)]
    best = gscore[0]
    gsel = jnp.zeros_like(best, dtype=jnp.int32)
    for g in range(1, N_GROUPS):
        upd = gscore[g] > best
        gsel = jnp.where(upd, g, gsel)
        best = jnp.where(upd, gscore[g], best)

    def pick(table, j):
        out = table[j]
        for g in range(1, N_GROUPS):
            out = jnp.where(gsel == g, table[g * EXPERTS_PER_GROUP + j], out)
        return out

    in_b = [pick(rows, j) for j in range(EXPERTS_PER_GROUP)]
    v1, i1 = in_b[0], jnp.zeros_like(gsel)
    for j in range(1, EXPERTS_PER_GROUP):
        upd = in_b[j] > v1
        v1 = jnp.where(upd, in_b[j], v1)
        i1 = jnp.where(upd, j, i1)
    v2 = jnp.full_like(v1, -jnp.inf)
    i2 = jnp.zeros_like(gsel)
    for j in range(EXPERTS_PER_GROUP):
        upd = (i1 != j) & (in_b[j] > v2)
        v2 = jnp.where(upd, in_b[j], v2)
        i2 = jnp.where(upd, j, i2)
    first_is_lo = i1 < i2
    lo = jnp.where(first_is_lo, i1, i2)
    hi = jnp.where(first_is_lo, i2, i1)
    pair = jnp.where(lo == 0, hi - 1, jnp.where(lo == 1, hi + 1, PAIRS_PER_GROUP - 1))
    cls = gsel * PAIRS_PER_GROUP + pair

    cid = lax.broadcasted_iota(jnp.int32, (CLASS_ROWS, tr), 0)
    onehot = jnp.where(cid == cls, 1.0, 0.0)
    before = jnp.dot(onehot.astype(BF16), tri_ref[...], preferred_element_type=F32) + carry_ref[...]
    rank = jnp.sum(onehot * before, axis=0, keepdims=True).astype(jnp.int32)
    oi_ref[...] = cls * RANK_SPAN + rank
    new_carry = carry_ref[...] + jnp.sum(onehot, axis=1, keepdims=True)
    carry_ref[...] = new_carry
    cnt_ref[...] = new_carry.astype(jnp.int32)


def _route(logits_t, router_bias):
    n = logits_t.shape[1]
    tr = _tile(512, n)
    return pl.pallas_call(
        _route_kernel,
        grid=(n // tr,),
        in_specs=[
            pl.BlockSpec((N_EXPERTS, tr), lambda i: (0, i)),
            pl.BlockSpec((N_EXPERTS, 1), lambda i: (0, 0)),
        ],
        out_specs=[
            pl.BlockSpec((1, tr), lambda i: (0, i)),
            pl.BlockSpec((CLASS_ROWS, 1), lambda i: (0, 0)),
        ],
        out_shape=[
            jax.ShapeDtypeStruct((1, n), jnp.int32),
            jax.ShapeDtypeStruct((CLASS_ROWS, 1), jnp.int32),
        ],
        scratch_shapes=[pltpu.VMEM((CLASS_ROWS, 1), F32), pltpu.VMEM((tr, tr), BF16)],
        compiler_params=_params(("arbitrary",)),
        name="route",
    )(logits_t, router_bias.reshape(N_EXPERTS, 1).astype(F32))


SORT_UNROLL = 8
MOE_ROWS = 256
MOE_ROWS_LOG2 = MOE_ROWS.bit_length() - 1
assert 1 << MOE_ROWS_LOG2 == MOE_ROWS


def _plan_kernel(code_ref, cnt_ref, tok_ref, ea_ref, eb_ref, off_ref, nv_ref, nu_ref, start_ref):
    n_blocks = ea_ref.shape[0]
    run = jnp.int32(0)
    blk = jnp.int32(0)
    for c in range(N_CLASSES):
        cnt = cnt_ref[c]
        start_ref[c] = run - c * RANK_SPAN
        e_lo = (c // PAIRS_PER_GROUP) * EXPERTS_PER_GROUP + PAIR_LO[c % PAIRS_PER_GROUP]
        e_hi = (c // PAIRS_PER_GROUP) * EXPERTS_PER_GROUP + PAIR_HI[c % PAIRS_PER_GROUP]
        n_blk = (cnt + (MOE_ROWS - 1)) >> MOE_ROWS_LOG2

        def fill(b, carry, run=run, blk=blk, cnt=cnt, e_lo=e_lo, e_hi=e_hi):
            ea_ref[blk + b] = e_lo
            eb_ref[blk + b] = e_hi
            off_ref[blk + b] = run + b * MOE_ROWS
            nv_ref[blk + b] = jnp.minimum(cnt - b * MOE_ROWS, MOE_ROWS)
            return carry

        lax.fori_loop(0, n_blk, fill, 0)
        run = run + cnt
        blk = blk + n_blk
    nu_ref[0] = blk

    def fill_unused(b, carry):
        ea_ref[b] = ea_ref[blk - 1]
        eb_ref[b] = eb_ref[blk - 1]
        off_ref[b] = 0
        nv_ref[b] = 0
        return carry

    lax.fori_loop(blk, n_blocks, fill_unused, 0)

    def place(i, carry):
        toks = [i * SORT_UNROLL + u for u in range(SORT_UNROLL)]
        codes = [code_ref[t] for t in toks]
        slots = [start_ref[code >> RANK_BITS] + code for code in codes]
        for t, p in zip(toks, slots):
            tok_ref[p] = t
        return carry

    lax.fori_loop(0, code_ref.shape[0] // SORT_UNROLL, place, 0)


def _plan(codes, counts, n_blocks):
    n = codes.shape[0]
    assert n % SORT_UNROLL == 0 and n <= RANK_SPAN
    smem = pl.BlockSpec(memory_space=pltpu.SMEM)
    i32 = lambda size: jax.ShapeDtypeStruct((size,), jnp.int32)
    return pl.pallas_call(
        _plan_kernel,
        in_specs=[smem, smem],
        out_specs=[smem] * 6,
        out_shape=[i32(n), i32(n_blocks), i32(n_blocks), i32(n_blocks), i32(n_blocks), i32(1)],
        scratch_shapes=[pltpu.SMEM((N_CLASSES,), jnp.int32)],
        name="moe_plan",
    )(codes, counts)


ROW_UNROLL = 8


def _for_rows(n, fn):
    groups = n // ROW_UNROLL

    def group(g, c):
        for u in range(ROW_UNROLL):
            fn(g * ROW_UNROLL + u)
        return c

    def single(r, c):
        fn(r)
        return c

    lax.fori_loop(0, groups, group, 0)
    lax.fori_loop(groups * ROW_UNROLL, n, single, 0)


def _row_copy(src_ref, src_row, dst_ref, dst_row, sem):
    return pltpu.make_async_copy(
        src_ref.at[pl.ds(pl.multiple_of(src_row * ROW_TILES, ROW_TILES), ROW_TILES)],
        dst_ref.at[pl.ds(pl.multiple_of(dst_row * ROW_TILES, ROW_TILES), ROW_TILES)],
        sem)


def _expert_kernel(ea_ref, eb_ref, off_ref, nv_ref, nu_ref, tok_ref,
                   h_hbm, wgu_a_ref, wd_a_ref, wgu_b_ref, wd_b_ref, wr_a_ref, wr_b_ref, y_hbm,
                   xbuf, ybuf, gsem, ssem):
    del ea_ref, eb_ref
    tb = xbuf.shape[1] // ROW_TILES
    i = pl.program_id(0)
    slot = i % 2
    n_used = nu_ref[0]

    def start_gather(block, s):
        base = off_ref[block]
        _for_rows(nv_ref[block],
                  lambda r: _row_copy(h_hbm, tok_ref[base + r], xbuf.at[s], r, gsem.at[s]).start())

    def start_scatter(block, s):
        base = off_ref[block]
        _for_rows(nv_ref[block],
                  lambda r: _row_copy(ybuf.at[s], r, y_hbm, tok_ref[base + r], ssem.at[s]).start())

    def wait_rows(block, hbm, buf, sem):
        rows = nv_ref[block] * ROW_TILES

        @pl.when(rows > 0)
        def _():
            pltpu.make_async_copy(hbm.at[pl.ds(0, rows)], buf.at[pl.ds(0, rows)], sem).wait()

    @pl.when(i == 0)
    def _():
        xbuf[...] = jnp.zeros_like(xbuf)
        start_gather(0, 0)

    @pl.when(i < n_used)
    def _():
        @pl.when(i + 1 < n_used)
        def _():
            start_gather(i + 1, 1 - slot)

        wait_rows(i, h_hbm, xbuf.at[slot], gsem.at[slot])

        @pl.when(i >= 2)
        def _():
            wait_rows(i - 2, y_hbm, ybuf.at[slot], ssem.at[slot])

        words = [xbuf[slot, pl.ds(j, tb, stride=ROW_TILES), :] for j in range(PACKED_TILES)]
        halves = [_unpack_bf16_pairs(w) for w in words]
        x = jnp.concatenate([h[0] for h in halves] + [h[1] for h in halves], axis=1)

        def mlp(wgu_ref, wd_ref):
            gu = jnp.dot(x, wgu_ref[0], preferred_element_type=F32)
            gate = gu[:, :D_EXPERT]
            act = gate * jax.nn.sigmoid(gate) * gu[:, D_EXPERT:]
            return jnp.dot(act.astype(BF16), wd_ref[0], preferred_element_type=F32)

        score_a = jax.nn.sigmoid(jnp.dot(x, wr_a_ref[0], preferred_element_type=F32))
        score_b = jax.nn.sigmoid(jnp.dot(x, wr_b_ref[0], preferred_element_type=F32))
        inv_tot = 1.0 / (score_a + score_b)
        w_a = score_a * inv_tot
        w_b = score_b * inv_tot
        ya = mlp(wgu_a_ref, wd_a_ref)
        yb = mlp(wgu_b_ref, wd_b_ref)
        for j in range(ROW_TILES):
            sl = slice(j * LANES, (j + 1) * LANES)
            ybuf[slot, pl.ds(j, tb, stride=ROW_TILES), :] = w_a * ya[:, sl] + w_b * yb[:, sl]
        start_scatter(i, slot)

        @pl.when(i == n_used - 1)
        def _():
            wait_rows(i, y_hbm, ybuf.at[slot], ssem.at[slot])

            @pl.when(i >= 1)
            def _():
                wait_rows(i - 1, y_hbm, ybuf.at[1 - slot], ssem.at[1 - slot])


def _experts(tables, sorted_tok, h_rows, layer, w_gu, w_down, wr_bcast, tb):
    block_ea, block_eb, src_off, n_valid, n_used = tables
    n_blocks = block_ea.shape[0]
    d, two_f = w_gu.shape[2:]
    ea_map = lambda i, ea, eb, off, nv, nu, tok: (ea[i], 0, 0)
    eb_map = lambda i, ea, eb, off, nv, nu, tok: (eb[i], 0, 0)
    lea_map = lambda i, ea, eb, off, nv, nu, tok: (layer, ea[i], 0, 0)
    leb_map = lambda i, ea, eb, off, nv, nu, tok: (layer, eb[i], 0, 0)
    grid_spec = pltpu.PrefetchScalarGridSpec(
        num_scalar_prefetch=6,
        grid=(n_blocks,),
        in_specs=[
            pl.BlockSpec(memory_space=pl.ANY),
            pl.BlockSpec((None, 1, d, two_f), lea_map),
            pl.BlockSpec((None, 1, two_f // 2, d), lea_map),
            pl.BlockSpec((None, 1, d, two_f), leb_map),
            pl.BlockSpec((None, 1, two_f // 2, d), leb_map),
            pl.BlockSpec((1, d, LANES), ea_map),
            pl.BlockSpec((1, d, LANES), eb_map),
        ],
        out_specs=pl.BlockSpec(memory_space=pl.ANY),
        scratch_shapes=[
            pltpu.VMEM((2, tb * ROW_TILES, LANES), jnp.uint32),
            pltpu.VMEM((2, tb * ROW_TILES, LANES), F32),
            pltpu.SemaphoreType.DMA((2,)),
            pltpu.SemaphoreType.DMA((2,)),
        ],
    )
    return pl.pallas_call(
        _expert_kernel,
        grid_spec=grid_spec,
        out_shape=jax.ShapeDtypeStruct(h_rows.shape, F32),
        compiler_params=_params(("arbitrary",)),
        name="moe_experts",
    )(block_ea, block_eb, src_off, n_valid, n_used, sorted_tok, h_rows, w_gu, w_down, w_gu, w_down, wr_bcast, wr_bcast)


def _residual_kernel(y_ref, x_ref, g2_ref, o_ref):
    o_ref[...] = _gated_residual(x_ref, y_ref, g2_ref)


def _residual(y_rows, x_mid, g2, seq):
    n, d = x_mid.shape
    tc = _tile(512, seq)
    per_b = seq // tc
    return pl.pallas_call(
        _residual_kernel,
        grid=(n // tc,),
        in_specs=[
            pl.BlockSpec((tc * ROW_TILES, LANES), lambda i: (i, 0)),
            pl.BlockSpec((tc, d), lambda i: (i, 0)),
            pl.BlockSpec((1, 1, d), lambda i: (i // per_b, 0, 0)),
        ],
        out_specs=pl.BlockSpec((tc, d), lambda i: (i, 0)),
        out_shape=jax.ShapeDtypeStruct((n, d), F32),
        compiler_params=_params(("parallel",)),
        name="moe_residual",
    )(y_rows, x_mid, g2)


def _moe(h_rows, logits_t, router_bias, layer, w_gu, w_down, wr_bcast):
    n = logits_t.shape[1]
    n_blocks = (n + N_CLASSES * (MOE_ROWS - 1) + MOE_ROWS - 1) // MOE_ROWS
    codes, counts = _route(logits_t, router_bias)
    sorted_tok, *tables = _plan(codes.reshape(n), counts.reshape(CLASS_ROWS), n_blocks)
    return _experts(tables, sorted_tok, h_rows, layer, w_gu, w_down, wr_bcast, MOE_ROWS)


def kernel(x, c, positions, norm1, norm2, w_ada, b_ada, a_w_in, a_b_if, a_h_norm, a_w_out, b_w_in, b_q_norm, b_k_norm, b_lam_q1, b_lam_k1, b_lam_q2, b_lam_k2, b_o_norm, b_w_out, w_router, router_bias, moe_w_gu, moe_w_down):
    bsz, seq, d = x.shape
    depth = w_ada.shape[0]
    n = bsz * seq
    xf = x.reshape(n, d)
    mod = _ada_mod(c, w_ada, b_ada)
    w_router_t = w_router.T
    wr_bcast = jnp.broadcast_to(w_router_t[:, :, None], (N_EXPERTS, d, LANES)).astype(BF16)
    pos_row = positions.reshape(n)
    w_gu = moe_w_gu.astype(BF16)
    w_down = moe_w_down.astype(BF16)

    residual = None
    for l in range(depth):
        sh1, sc1, g1, sh2, sc2, g2 = [mod[l, :, i * d:(i + 1) * d].reshape(bsz, 1, d) for i in range(6)]
        j = l // 2
        if l % 2 == 0:
            w_in = a_w_in[j]
            w_main = w_in[:, :A_MAIN_COLS].astype(BF16)
            w_gate = jnp.pad(w_in[:, A_MAIN_COLS:], ((0, 0), (0, LANES - 2 * A_HEADS))).astype(BF16)
        else:
            w_main, w_gate = b_w_in[j].astype(BF16), None
        outs = _norm_matmul(xf, residual, norm1[l].reshape(1, d), 1.0 + sc1, sh1, w_main, w_gate, seq)
        if residual is not None:
            xf, *outs = outs
        if l % 2 == 0:
            proj, gates = outs
            bias_row = jnp.pad(a_b_if[j], (0, LANES - 2 * A_HEADS)).reshape(1, LANES)
            mixed = _mlstm(proj, gates, bias_row, a_h_norm[j], bsz, seq)
            w_out = a_w_out[j].astype(BF16)
        else:
            (proj,) = outs
            qk = _qk_prep(proj, pos_row, b_q_norm[j], b_k_norm[j], seq)
            lam_tab = jnp.zeros((SUBLANES, LANES), F32)
            for r, v in enumerate((b_lam_q1[j], b_lam_k1[j], b_lam_q2[j], b_lam_k2[j])):
                lam_tab = lam_tab.at[r, :B_HEAD_DIM].set(v)
            lam_init = 0.8 - 0.6 * math.exp(-0.3 * l)
            mixed = _diff_attn(qk, proj, lam_tab, b_o_norm[j], bsz, seq, lam_init)
            w_out = b_w_out[j].astype(BF16)
        xf, h_rows, logits_t = _outproj_norm_router(
            mixed, w_out, xf, g1, norm2[l].reshape(1, d), 1.0 + sc2, sh2, w_router_t, seq)
        residual = (_moe(h_rows, logits_t, router_bias, l, w_gu, w_down, wr_bcast), g2)
    return _residual(residual[0], xf, residual[1], seq).reshape(bsz, seq, d)
```

```python
import functools
import math

import jax
import jax.numpy as jnp
from jax import lax
from jax.experimental import pallas as pl
from jax.experimental.pallas import tpu as pltpu

D_MODEL = 1024
A_HEADS = 4
A_QK_DIM = 128
A_V_DIM = 256
A_CHUNK = 128
A_HQ = A_HEADS * A_QK_DIM
A_HV = A_HEADS * A_V_DIM
A_MAIN_COLS = 2 * A_HQ + 2 * A_HV

B_HEADS = 8
B_HEAD_DIM = 64
B_V_DIM = 128
B_ROT_DIM = 16
ROPE_THETA = 500000.0

N_EXPERTS = 16
N_GROUPS = 4
EXPERTS_PER_GROUP = 4
TOP_K = 2
D_EXPERT = 512
EPS = 1e-6

LANES = 128
SUBLANES = 8
ROW_TILES = D_MODEL // LANES
assert ROW_TILES == SUBLANES
VMEM_LIMIT = 48 * 1024 * 1024

F32 = jnp.float32
BF16 = jnp.bfloat16
HIGHEST = lax.Precision.HIGHEST


def _params(sem):
    return pltpu.CompilerParams(dimension_semantics=sem, vmem_limit_bytes=VMEM_LIMIT)


def _tile(pref, n):
    t = min(pref, n)
    assert n % t == 0, (pref, n)
    return t


def _ada_kernel(c_ref, w_ref, b_ref, o_ref):
    c = c_ref[...]
    c_act = c * jax.nn.sigmoid(c)
    o_ref[0] = jnp.dot(c_act, w_ref[0], precision=HIGHEST, preferred_element_type=F32) + b_ref[0]


def _ada_mod(c, w_ada, b_ada):
    depth, d, six_d = w_ada.shape
    bsz = c.shape[0]
    tn = _tile(1536, six_d)
    return pl.pallas_call(
        _ada_kernel,
        grid=(depth, six_d // tn),
        in_specs=[
            pl.BlockSpec((bsz, d), lambda l, j: (0, 0)),
            pl.BlockSpec((1, d, tn), lambda l, j: (l, 0, j)),
            pl.BlockSpec((1, 1, tn), lambda l, j: (l, 0, j)),
        ],
        out_specs=pl.BlockSpec((1, bsz, tn), lambda l, j: (l, 0, j)),
        out_shape=jax.ShapeDtypeStruct((depth, bsz, six_d), F32),
        compiler_params=_params(("parallel", "parallel")),
        name="ada_mod",
    )(c, w_ada, b_ada.reshape(depth, 1, six_d))


def _modulated_rms(x, g, sc1p, sh):
    y = x * lax.rsqrt(jnp.mean(x * x, axis=-1, keepdims=True) + EPS)
    return (y * g) * sc1p + sh


def _gated_residual(x_ref, y_ref, g_ref):
    rows = x_ref.shape[0]
    g = g_ref[0]
    return jnp.concatenate(
        [x_ref[:, j * LANES:(j + 1) * LANES]
         + g[:, j * LANES:(j + 1) * LANES] * y_ref[pl.ds(j, rows, stride=ROW_TILES), :] for j in range(ROW_TILES)],
        axis=1)


def _norm_mm_kernel(*refs, col_chunk, has_gates, has_residual):
    refs = list(refs)
    x_ref = refs.pop(0)
    if has_residual:
        y_ref, g2_ref = refs.pop(0), refs.pop(0)
    g_ref, sc_ref, sh_ref, w_ref = refs[:4]
    refs = refs[4:]
    wg_ref = refs.pop(0) if has_gates else None
    xo_ref = refs.pop(0) if has_residual else None
    o_ref = refs.pop(0)
    og_ref = refs.pop(0) if has_gates else None

    if has_residual:
        x = _gated_residual(x_ref, y_ref, g2_ref)
        xo_ref[...] = x
    else:
        x = x_ref[...]
    hb = _modulated_rms(x, g_ref[...], sc_ref[0], sh_ref[0]).astype(BF16)
    for c0 in range(0, o_ref.shape[1], col_chunk):
        o_ref[:, c0:c0 + col_chunk] = jnp.dot(
            hb, w_ref[:, c0:c0 + col_chunk], preferred_element_type=F32).astype(o_ref.dtype)
    if has_gates:
        og_ref[...] = jnp.dot(hb, wg_ref[...], preferred_element_type=F32)


def _norm_matmul(x, residual, gain, sc1p, sh, w, wg, seq):
    n, d = x.shape
    cols = w.shape[1]
    tm = _tile(512, seq)
    per_b = seq // tm
    has_gates = wg is not None
    has_residual = residual is not None
    bmap = lambda i: (i // per_b, 0, 0)
    in_specs = [pl.BlockSpec((tm, d), lambda i: (i, 0))]
    args = [x]
    if has_residual:
        in_specs += [pl.BlockSpec((tm * ROW_TILES, LANES), lambda i: (i, 0)), pl.BlockSpec((1, 1, d), bmap)]
        args += list(residual)
    in_specs += [
        pl.BlockSpec((1, d), lambda i: (0, 0)),
        pl.BlockSpec((1, 1, d), bmap),
        pl.BlockSpec((1, 1, d), bmap),
        pl.BlockSpec((d, cols), lambda i: (0, 0)),
    ]
    args += [gain, sc1p, sh, w]
    out_specs, out_shape = [], []
    if has_residual:
        out_specs.append(pl.BlockSpec((tm, d), lambda i: (i, 0)))
        out_shape.append(jax.ShapeDtypeStruct((n, d), F32))
    out_specs.append(pl.BlockSpec((tm, cols), lambda i: (i, 0)))
    out_shape.append(jax.ShapeDtypeStruct((n, cols), BF16))
    if has_gates:
        in_specs.append(pl.BlockSpec((d, LANES), lambda i: (0, 0)))
        out_specs.append(pl.BlockSpec((tm, LANES), lambda i: (i, 0)))
        out_shape.append(jax.ShapeDtypeStruct((n, LANES), F32))
        args.append(wg)
    return pl.pallas_call(
        functools.partial(_norm_mm_kernel, col_chunk=512, has_gates=has_gates, has_residual=has_residual),
        grid=(n // tm,),
        in_specs=in_specs,
        out_specs=out_specs,
        out_shape=out_shape,
        compiler_params=_params(("parallel",)),
        name="norm_inproj",
    )(*args)


def _log_sigmoid(x):
    return jnp.minimum(x, 0.0) - jnp.log1p(jnp.exp(-jnp.abs(x)))


def _mlstm_kernel(p_ref, g_ref, bias_ref, hn_ref, o_ref, ct_ref, n_ref, m_ref):
    L = A_CHUNK

    @pl.when(pl.program_id(1) == 0)
    def _():
        ct_ref[...] = jnp.zeros_like(ct_ref)
        n_ref[...] = jnp.zeros_like(n_ref)
        m_ref[...] = jnp.zeros_like(m_ref)

    gates = g_ref[...] + bias_ref[...]
    src = lax.broadcasted_iota(jnp.int32, (L, L), 0)
    tgt = lax.broadcasted_iota(jnp.int32, (L, L), 1)
    causal = src <= tgt
    bcum = jnp.dot(jnp.where(src >= tgt, 1.0, 0.0), _log_sigmoid(gates), precision=HIGHEST,
                   preferred_element_type=F32)
    bcum_t = bcum.T
    nt = (((1,), (1,)), ((), ()))

    for h in range(A_HEADS):
        q = p_ref[:, h * A_QK_DIM:(h + 1) * A_QK_DIM]
        ksf = p_ref[:, A_HQ + h * A_QK_DIM:A_HQ + (h + 1) * A_QK_DIM].astype(F32) * (A_QK_DIM ** -0.5)
        v = p_ref[:, 2 * A_HQ + h * A_V_DIM:2 * A_HQ + (h + 1) * A_V_DIM]
        og = p_ref[:, 2 * A_HQ + A_HV + h * A_V_DIM:2 * A_HQ + A_HV + (h + 1) * A_V_DIM].astype(F32)
        b_row = bcum_t[A_HEADS + h:A_HEADS + h + 1, :]
        b_last = b_row[:, L - 1:L]
        c_col = gates[:, h:h + 1] - bcum[:, A_HEADS + h:A_HEADS + h + 1]
        m11 = m_ref[h][:, 0:1]
        ct_old = ct_ref[h]
        n_old = n_ref[h]

        log_d = jnp.where(causal, c_col + b_row, -jnp.inf)
        log_inter = b_row + m11
        m_t = jnp.maximum(jnp.max(log_d, axis=0, keepdims=True), log_inter)
        dmat = jnp.exp(log_d - m_t)
        inter = jnp.exp(log_inter - m_t)
        s = lax.dot_general(ksf.astype(BF16), q, nt, preferred_element_type=F32) * dmat
        vt = v.astype(F32).T.astype(BF16)
        num = (jnp.dot(vt, s.astype(BF16), preferred_element_type=F32)
               + inter * lax.dot_general(ct_old.astype(BF16), q, nt, preferred_element_type=F32))
        qn = lax.dot_general(n_old.astype(BF16), q, nt, preferred_element_type=F32)[0:1]
        den = jnp.sum(s, axis=0, keepdims=True) + inter * qn
        hh = num / jnp.maximum(jnp.abs(den), jnp.exp(-m_t))

        lw_col = b_last + c_col
        m_new = jnp.maximum(b_last + m11, jnp.max(lw_col, axis=0, keepdims=True))
        kw = ksf * jnp.exp(lw_col - m_new)
        decay = jnp.exp(b_last + m11 - m_new)
        ct_ref[h] = decay * ct_old + jnp.dot(vt, kw.astype(BF16), preferred_element_type=F32)
        n_ref[h] = decay * n_old + jnp.broadcast_to(jnp.sum(kw, axis=0, keepdims=True), n_old.shape)
        m_ref[h] = jnp.broadcast_to(m_new, (1, LANES))

        hn = hh * lax.rsqrt(jnp.mean(hh * hh, axis=0, keepdims=True) + EPS)
        hn = (hn * hn_ref[h * A_V_DIM:(h + 1) * A_V_DIM, :]).T
        o_ref[:, h * A_V_DIM:(h + 1) * A_V_DIM] = (hn * jax.nn.sigmoid(og)).astype(o_ref.dtype)


def _mlstm(proj, gates, bias_row, h_norm, bsz, seq):
    n = proj.shape[0]
    nc = seq // A_CHUNK
    h_norm_cols = jnp.broadcast_to(h_norm[:, None], (A_HV, A_CHUNK))
    return pl.pallas_call(
        _mlstm_kernel,
        grid=(bsz, nc),
        in_specs=[
            pl.BlockSpec((A_CHUNK, A_MAIN_COLS), lambda b, c: (b * nc + c, 0)),
            pl.BlockSpec((A_CHUNK, LANES), lambda b, c: (b * nc + c, 0)),
            pl.BlockSpec((1, LANES), lambda b, c: (0, 0)),
            pl.BlockSpec((A_HV, A_CHUNK), lambda b, c: (0, 0)),
        ],
        out_specs=pl.BlockSpec((A_CHUNK, A_HV), lambda b, c: (b * nc + c, 0)),
        out_shape=jax.ShapeDtypeStruct((n, A_HV), BF16),
        scratch_shapes=[
            pltpu.VMEM((A_HEADS, A_V_DIM, A_QK_DIM), F32),
            pltpu.VMEM((A_HEADS, SUBLANES, A_QK_DIM), F32),
            pltpu.VMEM((A_HEADS, 1, LANES), F32),
        ],
        compiler_params=_params(("parallel", "arbitrary")),
        name="mlstm",
    )(proj, gates, bias_row, h_norm_cols)


def _qk_prep_kernel(p_ref, pos_ref, seg_ref, segt_ref, gain_ref, freq_ref, o_ref):
    tm = p_ref.shape[0]
    x = p_ref[...].astype(F32)
    xx = x * x
    xx_hi = xx.astype(BF16)
    xx_lo = (xx - xx_hi.astype(F32)).astype(BF16)
    seg = seg_ref[...]
    ss = jnp.dot(xx_hi, seg, preferred_element_type=F32) + jnp.dot(xx_lo, seg, preferred_element_type=F32)
    r = lax.rsqrt(ss * (1.0 / B_HEAD_DIM) + EPS)
    r_hi = r.astype(BF16)
    r_lo = (r - r_hi.astype(F32)).astype(BF16)
    segt = segt_ref[...]
    r_full = jnp.dot(r_hi, segt, preferred_element_type=F32) + jnp.dot(r_lo, segt, preferred_element_type=F32)

    half = B_ROT_DIM // 2
    assert half == SUBLANES
    ang = freq_ref[:, 0:1] * pos_ref[0].astype(F32)
    cos8 = jnp.cos(ang)
    sin8 = jnp.sin(ang)
    one8 = jnp.ones_like(cos8)
    zero8 = jnp.zeros_like(cos8)
    per_seg = B_HEAD_DIM // SUBLANES
    n_seg = LANES // B_HEAD_DIM

    def chunk_pattern(first, second, rest):
        groups = ([first, second] + [rest] * (per_seg - 2)) * n_seg
        return jnp.concatenate(groups, axis=0).T

    c_mul = chunk_pattern(cos8, cos8, one8)
    s_lo = chunk_pattern(-sin8, zero8, zero8)
    s_hi = chunk_pattern(zero8, sin8, zero8)
    for j in range(2 * D_MODEL // LANES):
        sl = slice(j * LANES, (j + 1) * LANES)
        y = x[:, sl] * r_full[:, sl] * gain_ref[:, sl]
        out = y * c_mul + pltpu.roll(y, LANES - half, axis=1) * s_lo + pltpu.roll(y, half, axis=1) * s_hi
        if j < D_MODEL // LANES:
            out = out * (B_HEAD_DIM ** -0.5 * math.log2(math.e))
        o_ref[:, sl] = out.astype(o_ref.dtype)


def _qk_prep(proj, positions_row, q_norm, k_norm, seq):
    n = proj.shape[0]
    two_d = 2 * D_MODEL
    tm = _tile(256, seq)
    seg = (jnp.arange(two_d)[:, None] // B_HEAD_DIM == jnp.arange(LANES)[None, :]).astype(BF16)
    gain = jnp.concatenate([jnp.tile(q_norm, D_MODEL // B_HEAD_DIM), jnp.tile(k_norm, D_MODEL // B_HEAD_DIM)])
    inv_freq = ROPE_THETA ** (-jnp.arange(0, B_ROT_DIM, 2, dtype=F32) / B_ROT_DIM)
    freq_tab = jnp.broadcast_to(inv_freq[:, None], (B_ROT_DIM // 2, LANES))
    return pl.pallas_call(
        _qk_prep_kernel,
        grid=(n // tm,),
        in_specs=[
            pl.BlockSpec((tm, two_d), lambda i: (i, 0)),
            pl.BlockSpec((1, 1, tm), lambda i: (i, 0, 0)),
            pl.BlockSpec((two_d, LANES), lambda i: (0, 0)),
            pl.BlockSpec((LANES, two_d), lambda i: (0, 0)),
            pl.BlockSpec((1, two_d), lambda i: (0, 0)),
            pl.BlockSpec((SUBLANES, LANES), lambda i: (0, 0)),
        ],
        out_specs=pl.BlockSpec((tm, two_d), lambda i: (i, 0)),
        out_shape=jax.ShapeDtypeStruct((n, two_d), BF16),
        compiler_params=_params(("parallel",)),
        name="qk_prep",
    )(proj, positions_row.reshape(n // tm, 1, tm), seg, seg.T, gain.reshape(1, two_d), freq_tab)


ATTN_TILE = 512


def _diff_attn_kernel(q_ref, k_ref, v_ref, lam_ref, on_ref, o_ref, vt_ref, qm_ref, m_ref, l_ref, acc_ref,
                      *, tile, lam_init):
    seq = q_ref.shape[0]
    n_tiles = seq // tile
    lamv = lam_ref[...]
    lam = (jnp.exp(jnp.sum(lamv[0:1] * lamv[1:2], axis=1, keepdims=True))
           - jnp.exp(jnp.sum(lamv[2:3] * lamv[3:4], axis=1, keepdims=True)) + lam_init)
    tchunk = min(256, seq)
    for c in range(seq // tchunk):
        vt_ref[:, c * tchunk:(c + 1) * tchunk] = v_ref[c * tchunk:(c + 1) * tchunk, :].astype(F32).T.astype(BF16)
    first_map = lax.broadcasted_iota(jnp.int32, (tile, LANES), 1) < B_HEAD_DIM
    key_id = lax.broadcasted_iota(jnp.int32, (tile, tile), 0)
    query_id = lax.broadcasted_iota(jnp.int32, (tile, tile), 1)
    causal = query_id >= key_id
    nt = (((1,), (1,)), ((), ()))

    def tile_step(k, vt, mask):
        scores = [lax.dot_general(k, qm_ref[c], nt, preferred_element_type=F32) for c in range(2)]
        for c, s in enumerate(scores):
            if mask is not None:
                s = jnp.where(mask, s, -jnp.inf)
            m_old = m_ref[c]
            m_new = jnp.maximum(m_old, jnp.max(s, axis=0, keepdims=True))
            alpha = jnp.exp2(m_old - m_new)
            p = jnp.exp2(s - m_new)
            l_ref[c] = alpha * l_ref[c] + jnp.sum(p, axis=0, keepdims=True)
            acc_ref[c] = alpha * acc_ref[c] + jnp.dot(vt, p.astype(BF16), preferred_element_type=F32)
            m_ref[c] = m_new

    for qi in range(n_tiles):
        q = q_ref[qi * tile:(qi + 1) * tile, :]
        qm_ref[0] = jnp.where(first_map, q, jnp.zeros_like(q))
        qm_ref[1] = jnp.where(first_map, jnp.zeros_like(q), q)
        m_ref[...] = jnp.full(m_ref.shape, -jnp.inf, F32)
        l_ref[...] = jnp.zeros(l_ref.shape, F32)
        acc_ref[...] = jnp.zeros(acc_ref.shape, F32)
        for ki in range(qi + 1):
            tile_step(k_ref[ki * tile:(ki + 1) * tile, :], vt_ref[:, ki * tile:(ki + 1) * tile],
                      causal if ki == qi else None)
        o = acc_ref[0] / l_ref[0] - lam * (acc_ref[1] / l_ref[1])
        o = o * lax.rsqrt(jnp.mean(o * o, axis=0, keepdims=True) + EPS) * on_ref[...] * (1.0 - lam_init)
        o_ref[qi * tile:(qi + 1) * tile, :] = o.T.astype(o_ref.dtype)


def _diff_attn(qk, proj, lam_tab, o_norm, bsz, seq, lam_init):
    n = qk.shape[0]
    tile = _tile(ATTN_TILE, seq)
    o_norm_cols = jnp.broadcast_to(o_norm[:, None], (B_V_DIM, tile))
    return pl.pallas_call(
        functools.partial(_diff_attn_kernel, tile=tile, lam_init=lam_init),
        grid=(bsz, B_HEADS),
        in_specs=[
            pl.BlockSpec((seq, LANES), lambda b, h: (b, h)),
            pl.BlockSpec((seq, LANES), lambda b, h: (b, B_HEADS + h)),
            pl.BlockSpec((seq, B_V_DIM), lambda b, h: (b, 2 * B_HEADS + h)),
            pl.BlockSpec((SUBLANES, LANES), lambda b, h: (0, 0)),
            pl.BlockSpec((B_V_DIM, tile), lambda b, h: (0, 0)),
        ],
        out_specs=pl.BlockSpec((seq, B_V_DIM), lambda b, h: (b, h)),
        out_shape=jax.ShapeDtypeStruct((n, D_MODEL), BF16),
        scratch_shapes=[
            pltpu.VMEM((B_V_DIM, seq), BF16),
            pltpu.VMEM((2, tile, LANES), BF16),
            pltpu.VMEM((2, 1, tile), F32),
            pltpu.VMEM((2, 1, tile), F32),
            pltpu.VMEM((2, B_V_DIM, tile), F32),
        ],
        compiler_params=_params(("parallel", "parallel")),
        name="diff_attn",
    )(qk, qk, proj, lam_tab, o_norm_cols)


HALF_D = D_MODEL // 2
HI16 = 0xFFFF0000
PACKED_TILES = HALF_D // LANES


def _pack_bf16_pairs(v):
    bits = lax.bitcast_convert_type(v.astype(BF16).astype(F32), jnp.uint32)
    return (bits[:, HALF_D:] & jnp.uint32(HI16)) | (bits[:, :HALF_D] >> 16)


def _unpack_bf16_pairs(words):
    first = lax.bitcast_convert_type(words << 16, F32)
    second = lax.bitcast_convert_type(words & jnp.uint32(HI16), F32)
    return first.astype(BF16), second.astype(BF16)


def _outproj_kernel(a_ref, w_ref, x_ref, g1_ref, gain_ref, sc_ref, sh_ref, wr_ref, xo_ref, hp_ref, lg_ref):
    y = jnp.dot(a_ref[...], w_ref[...], preferred_element_type=F32)
    xm = x_ref[...] + g1_ref[0] * y
    xo_ref[...] = xm
    h2 = _modulated_rms(xm, gain_ref[...], sc_ref[0], sh_ref[0])
    tm = xm.shape[0]
    packed = _pack_bf16_pairs(h2)
    for j in range(ROW_TILES):
        hp_ref[pl.ds(j, tm, stride=ROW_TILES), :] = (
            packed[:, j * LANES:(j + 1) * LANES] if j < PACKED_TILES else jnp.zeros((tm, LANES), jnp.uint32))
    lg_ref[...] = lax.dot_general(wr_ref[...], h2, (((1,), (1,)), ((), ())),
                                  precision=HIGHEST, preferred_element_type=F32)


def _outproj_norm_router(a, w, x, g1, gain, sc1p, sh, w_router_t, seq):
    n, d = x.shape
    tm = _tile(512, seq)
    per_b = seq // tm
    bmap = lambda i: (i // per_b, 0, 0)
    return pl.pallas_call(
        _outproj_kernel,
        grid=(n // tm,),
        in_specs=[
            pl.BlockSpec((tm, a.shape[1]), lambda i: (i, 0)),
            pl.BlockSpec(w.shape, lambda i: (0, 0)),
            pl.BlockSpec((tm, d), lambda i: (i, 0)),
            pl.BlockSpec((1, 1, d), bmap),
            pl.BlockSpec((1, d), lambda i: (0, 0)),
            pl.BlockSpec((1, 1, d), bmap),
            pl.BlockSpec((1, 1, d), bmap),
            pl.BlockSpec((N_EXPERTS, d), lambda i: (0, 0)),
        ],
        out_specs=[
            pl.BlockSpec((tm, d), lambda i: (i, 0)),
            pl.BlockSpec((tm * ROW_TILES, LANES), lambda i: (i, 0)),
            pl.BlockSpec((N_EXPERTS, tm), lambda i: (0, i)),
        ],
        out_shape=[
            jax.ShapeDtypeStruct((n, d), F32),
            jax.ShapeDtypeStruct((n * ROW_TILES, LANES), jnp.uint32),
            jax.ShapeDtypeStruct((N_EXPERTS, n), F32),
        ],
        compiler_params=_params(("parallel",)),
        name="outproj_norm_router",
    )(a, w, x, g1, gain, sc1p, sh, w_router_t)


PAIR_LO = (0, 0, 0, 1, 1, 2)
PAIR_HI = (1, 2, 3, 2, 3, 3)
PAIRS_PER_GROUP = len(PAIR_LO)
N_CLASSES = N_GROUPS * PAIRS_PER_GROUP
CLASS_ROWS = 32
RANK_BITS = 20
RANK_SPAN = 1 << RANK_BITS


def _route_kernel(lg_ref, bias_ref, oi_ref, cnt_ref, carry_ref, tri_ref):
    tr = lg_ref.shape[1]
    step = pl.program_id(0)

    @pl.when(step == 0)
    def _():
        carry_ref[...] = jnp.zeros_like(carry_ref)
        r = lax.broadcasted_iota(jnp.int32, (tr, tr), 0)
        c = lax.broadcasted_iota(jnp.int32, (tr, tr), 1)
        tri_ref[...] = jnp.where(r < c, 1.0, 0.0).astype(BF16)

    scores = jax.nn.sigmoid(lg_ref[...])
    biased = scores + bias_ref[...]
    rows = [biased[e:e + 1, :] for e in range(N_EXPERTS)]

    def top2_sum(a, b, c, d):
        m1, n1 = jnp.maximum(a, b), jnp.minimum(a, b)
        m2, n2 = jnp.maximum(c, d), jnp.minimum(c, d)
        return jnp.maximum(m1, m2) + jnp.maximum(jnp.minimum(m1, m2), jnp.maximum(n1, n2))

    gscore = [top2_sum(*rows[g * EXPERTS_PER_GROUP:(g + 1) * EXPERTS_PER_GROUP]) for g in range(N_GROUPS)]
    best = gscore[0]
    gsel = jnp.zeros_like(best, dtype=jnp.int32)
    for g in range(1, N_GROUPS):
        upd = gscore[g] > best
        gsel = jnp.where(upd, g, gsel)
        best = jnp.where(upd, gscore[g], best)

    def pick(table, j):
        out = table[j]
        for g in range(1, N_GROUPS):
            out = jnp.where(gsel == g, table[g * EXPERTS_PER_GROUP + j], out)
        return out

    in_b = [pick(rows, j) for j in range(EXPERTS_PER_GROUP)]
    v1, i1 = in_b[0], jnp.zeros_like(gsel)
    for j in range(1, EXPERTS_PER_GROUP):
        upd = in_b[j] > v1
        v1 = jnp.where(upd, in_b[j], v1)
        i1 = jnp.where(upd, j, i1)
    v2 = jnp.full_like(v1, -jnp.inf)
    i2 = jnp.zeros_like(gsel)
    for j in range(EXPERTS_PER_GROUP):
        upd = (i1 != j) & (in_b[j] > v2)
        v2 = jnp.where(upd, in_b[j], v2)
        i2 = jnp.where(upd, j, i2)
    first_is_lo = i1 < i2
    lo = jnp.where(first_is_lo, i1, i2)
    hi = jnp.where(first_is_lo, i2, i1)
    pair = jnp.where(lo == 0, hi - 1, jnp.where(lo == 1, hi + 1, PAIRS_PER_GROUP - 1))
    cls = gsel * PAIRS_PER_GROUP + pair

    cid = lax.broadcasted_iota(jnp.int32, (CLASS_ROWS, tr), 0)
    onehot = jnp.where(cid == cls, 1.0, 0.0)
    before = jnp.dot(onehot.astype(BF16), tri_ref[...], preferred_element_type=F32) + carry_ref[...]
    rank = jnp.sum(onehot * before, axis=0, keepdims=True).astype(jnp.int32)
    oi_ref[...] = cls * RANK_SPAN + rank
    new_carry = carry_ref[...] + jnp.sum(onehot, axis=1, keepdims=True)
    carry_ref[...] = new_carry
    cnt_ref[...] = new_carry.astype(jnp.int32)


def _route(logits_t, router_bias):
    n = logits_t.shape[1]
    tr = _tile(512, n)
    return pl.pallas_call(
        _route_kernel,
        grid=(n // tr,),
        in_specs=[
            pl.BlockSpec((N_EXPERTS, tr), lambda i: (0, i)),
            pl.BlockSpec((N_EXPERTS, 1), lambda i: (0, 0)),
        ],
        out_specs=[
            pl.BlockSpec((1, tr), lambda i: (0, i)),
            pl.BlockSpec((CLASS_ROWS, 1), lambda i: (0, 0)),
        ],
        out_shape=[
            jax.ShapeDtypeStruct((1, n), jnp.int32),
            jax.ShapeDtypeStruct((CLASS_ROWS, 1), jnp.int32),
        ],
        scratch_shapes=[pltpu.VMEM((CLASS_ROWS, 1), F32), pltpu.VMEM((tr, tr), BF16)],
        compiler_params=_params(("arbitrary",)),
        name="route",
    )(logits_t, router_bias.reshape(N_EXPERTS, 1).astype(F32))


SORT_UNROLL = 8
MOE_ROWS = 256
MOE_ROWS_LOG2 = MOE_ROWS.bit_length() - 1
assert 1 << MOE_ROWS_LOG2 == MOE_ROWS


def _plan_kernel(code_ref, cnt_ref, row_ref, ea_ref, eb_ref, nu_ref, start_ref):
    n_tok = code_ref.shape[0]
    n_blocks = ea_ref.shape[0]
    cap = n_blocks * MOE_ROWS

    def spare_rows(lo, hi, first_row):
        def body(s, carry):
            row_ref[MOE_ROWS + s] = first_row + (s - lo)
            return carry

        lax.fori_loop(lo, hi, body, 0)

    spare_rows(-MOE_ROWS, 0, cap)
    blk = jnp.int32(0)
    run = jnp.int32(0)
    for c in range(N_CLASSES):
        cnt = cnt_ref[c]
        first = blk * MOE_ROWS
        start_ref[c] = first + MOE_ROWS - c * RANK_SPAN
        e_lo = (c // PAIRS_PER_GROUP) * EXPERTS_PER_GROUP + PAIR_LO[c % PAIRS_PER_GROUP]
        e_hi = (c // PAIRS_PER_GROUP) * EXPERTS_PER_GROUP + PAIR_HI[c % PAIRS_PER_GROUP]
        n_blk = (cnt + (MOE_ROWS - 1)) >> MOE_ROWS_LOG2

        def fill(b, carry, blk=blk, e_lo=e_lo, e_hi=e_hi):
            ea_ref[blk + b] = e_lo
            eb_ref[blk + b] = e_hi
            return carry

        lax.fori_loop(0, n_blk, fill, 0)
        blk = blk + n_blk
        run = run + cnt
        spare_rows(first + cnt, blk * MOE_ROWS, n_tok + (first + cnt - run))
    nu_ref[0] = blk
    spare_rows(blk * MOE_ROWS, cap, blk * MOE_ROWS)

    def fill_unused(b, carry):
        ea_ref[b] = ea_ref[blk - 1]
        eb_ref[b] = eb_ref[blk - 1]
        return carry

    lax.fori_loop(blk, n_blocks, fill_unused, 0)

    def place(i, carry):
        toks = [i * SORT_UNROLL + u for u in range(SORT_UNROLL)]
        codes = [code_ref[t] for t in toks]
        slots = [start_ref[code >> RANK_BITS] + code for code in codes]
        for t, p in zip(toks, slots):
            row_ref[p] = t
        return carry

    lax.fori_loop(0, n_tok // SORT_UNROLL, place, 0)


def _plan(codes, counts, n_blocks):
    n = codes.shape[0]
    assert n % SORT_UNROLL == 0 and n <= RANK_SPAN
    smem = pl.BlockSpec(memory_space=pltpu.SMEM)
    i32 = lambda size: jax.ShapeDtypeStruct((size,), jnp.int32)
    return pl.pallas_call(
        _plan_kernel,
        in_specs=[smem, smem],
        out_specs=[smem] * 4,
        out_shape=[i32((n_blocks + 1) * MOE_ROWS), i32(n_blocks), i32(n_blocks), i32(1)],
        scratch_shapes=[pltpu.SMEM((N_CLASSES,), jnp.int32)],
        name="moe_plan",
    )(codes, counts)


def _row_copy(src_ref, src_row, dst_ref, dst_row, sem):
    return pltpu.make_async_copy(
        src_ref.at[pl.ds(pl.multiple_of(src_row * ROW_TILES, ROW_TILES), ROW_TILES)],
        dst_ref.at[pl.ds(pl.multiple_of(dst_row * ROW_TILES, ROW_TILES), ROW_TILES)],
        sem)


def _expert_kernel(ea_ref, eb_ref, nu_ref, row_ref,
                   h_hbm, wgu_a_ref, wd_a_ref, wgu_b_ref, wd_b_ref, wr_a_ref, wr_b_ref, y_hbm,
                   xbuf, ybuf, zbuf, gsem, ssem, zsem, *, n_tok):
    del ea_ref, eb_ref
    tb = xbuf.shape[1] // ROW_TILES
    i = pl.program_id(0)
    slot = i % 2
    n_used = nu_ref[0]

    def start_gather(block, s):
        base = (block + 1) * tb
        for r in range(tb):
            _row_copy(h_hbm, jnp.minimum(row_ref[base + r], n_tok - 1), xbuf.at[s], r, gsem.at[s]).start()

    def start_scatter(block, s):
        base = (block + 1) * tb
        for r in range(tb):
            _row_copy(ybuf.at[s], r, y_hbm, row_ref[base + r], ssem.at[s]).start()

    def wait_block(hbm, buf, sem):
        pltpu.make_async_copy(hbm.at[pl.ds(0, tb * ROW_TILES)], buf, sem).wait()

    @pl.when(i == 0)
    def _():
        ybuf[...] = jnp.zeros_like(ybuf)
        zbuf[...] = jnp.zeros_like(zbuf)
        start_gather(0, 0)

    @pl.when(i < n_used)
    def _():
        wait_block(h_hbm, xbuf.at[slot], gsem.at[slot])
        start_gather(jnp.minimum(i + 1, n_used - 1), 1 - slot)
        start_scatter(i - 1, 1 - slot)

        words = [xbuf[slot, pl.ds(j, tb, stride=ROW_TILES), :] for j in range(PACKED_TILES)]
        halves = [_unpack_bf16_pairs(w) for w in words]
        x = jnp.concatenate([h[0] for h in halves] + [h[1] for h in halves], axis=1)

        def mlp(wgu_ref, wd_ref):
            gu = jnp.dot(x, wgu_ref[0], preferred_element_type=F32)
            gate = gu[:, :D_EXPERT]
            act = gate * jax.nn.sigmoid(gate) * gu[:, D_EXPERT:]
            return jnp.dot(act.astype(BF16), wd_ref[0], preferred_element_type=F32)

        score_a = jax.nn.sigmoid(jnp.dot(x, wr_a_ref[0], preferred_element_type=F32))
        score_b = jax.nn.sigmoid(jnp.dot(x, wr_b_ref[0], preferred_element_type=F32))
        inv_tot = 1.0 / (score_a + score_b)
        w_a = score_a * inv_tot
        w_b = score_b * inv_tot
        ya = mlp(wgu_a_ref, wd_a_ref)
        yb = mlp(wgu_b_ref, wd_b_ref)

        @pl.when(i >= 1)
        def _():
            wait_block(y_hbm, ybuf.at[slot], ssem.at[slot])

        for j in range(ROW_TILES):
            sl = slice(j * LANES, (j + 1) * LANES)
            ybuf[slot, pl.ds(j, tb, stride=ROW_TILES), :] = w_a * ya[:, sl] + w_b * yb[:, sl]

        @pl.when(i == n_used - 1)
        def _():
            start_scatter(i, slot)

            def unused_block_copy(b):
                first = pl.multiple_of(b * (tb * ROW_TILES), tb * ROW_TILES)
                return pltpu.make_async_copy(zbuf, y_hbm.at[pl.ds(first, tb * ROW_TILES)], zsem)

            def start_zero(b, carry):
                unused_block_copy(b).start()
                return carry

            def wait_zero(b, carry):
                unused_block_copy(b).wait()
                return carry

            lax.fori_loop(n_used, pl.num_programs(0), start_zero, 0)
            wait_block(y_hbm, ybuf.at[slot], ssem.at[slot])
            wait_block(y_hbm, ybuf.at[1 - slot], ssem.at[1 - slot])
            wait_block(h_hbm, xbuf.at[1 - slot], gsem.at[1 - slot])
            lax.fori_loop(n_used, pl.num_programs(0), wait_zero, 0)


def _experts(tables, slot_rows, h_rows, layer, w_gu, w_down, wr_bcast, tb):
    block_ea, block_eb, n_used = tables
    n_blocks = block_ea.shape[0]
    n_tok = h_rows.shape[0] // ROW_TILES
    d, two_f = w_gu.shape[2:]
    ea_map = lambda i, ea, eb, nu, row: (ea[i], 0, 0)
    eb_map = lambda i, ea, eb, nu, row: (eb[i], 0, 0)
    lea_map = lambda i, ea, eb, nu, row: (layer, ea[i], 0, 0)
    leb_map = lambda i, ea, eb, nu, row: (layer, eb[i], 0, 0)
    grid_spec = pltpu.PrefetchScalarGridSpec(
        num_scalar_prefetch=4,
        grid=(n_blocks,),
        in_specs=[
            pl.BlockSpec(memory_space=pl.ANY),
            pl.BlockSpec((None, 1, d, two_f), lea_map),
            pl.BlockSpec((None, 1, two_f // 2, d), lea_map),
            pl.BlockSpec((None, 1, d, two_f), leb_map),
            pl.BlockSpec((None, 1, two_f // 2, d), leb_map),
            pl.BlockSpec((1, d, LANES), ea_map),
            pl.BlockSpec((1, d, LANES), eb_map),
        ],
        out_specs=pl.BlockSpec(memory_space=pl.ANY),
        scratch_shapes=[
            pltpu.VMEM((2, tb * ROW_TILES, LANES), jnp.uint32),
            pltpu.VMEM((2, tb * ROW_TILES, LANES), F32),
            pltpu.VMEM((tb * ROW_TILES, LANES), F32),
            pltpu.SemaphoreType.DMA((2,)),
            pltpu.SemaphoreType.DMA((2,)),
            pltpu.SemaphoreType.DMA,
        ],
    )
    out_rows = slot_rows.shape[0]
    return pl.pallas_call(
        functools.partial(_expert_kernel, n_tok=n_tok),
        grid_spec=grid_spec,
        out_shape=jax.ShapeDtypeStruct((out_rows * ROW_TILES, LANES), F32),
        compiler_params=_params(("arbitrary",)),
        name="moe_experts",
    )(block_ea, block_eb, n_used, slot_rows, h_rows, w_gu, w_down, w_gu, w_down, wr_bcast, wr_bcast)


def _residual_kernel(y_ref, x_ref, g2_ref, o_ref):
    o_ref[...] = _gated_residual(x_ref, y_ref, g2_ref)


def _residual(y_rows, x_mid, g2, seq):
    n, d = x_mid.shape
    tc = _tile(512, seq)
    per_b = seq // tc
    return pl.pallas_call(
        _residual_kernel,
        grid=(n // tc,),
        in_specs=[
            pl.BlockSpec((tc * ROW_TILES, LANES), lambda i: (i, 0)),
            pl.BlockSpec((tc, d), lambda i: (i, 0)),
            pl.BlockSpec((1, 1, d), lambda i: (i // per_b, 0, 0)),
        ],
        out_specs=pl.BlockSpec((tc, d), lambda i: (i, 0)),
        out_shape=jax.ShapeDtypeStruct((n, d), F32),
        compiler_params=_params(("parallel",)),
        name="moe_residual",
    )(y_rows, x_mid, g2)


def _moe(h_rows, logits_t, router_bias, layer, w_gu, w_down, wr_bcast):
    n = logits_t.shape[1]
    n_blocks = (n + N_CLASSES * (MOE_ROWS - 1) + MOE_ROWS - 1) // MOE_ROWS
    codes, counts = _route(logits_t, router_bias)
    slot_rows, *tables = _plan(codes.reshape(n), counts.reshape(CLASS_ROWS), n_blocks)
    return _experts(tables, slot_rows, h_rows, layer, w_gu, w_down, wr_bcast, MOE_ROWS)


def kernel(x, c, positions, norm1, norm2, w_ada, b_ada, a_w_in, a_b_if, a_h_norm, a_w_out, b_w_in, b_q_norm, b_k_norm, b_lam_q1, b_lam_k1, b_lam_q2, b_lam_k2, b_o_norm, b_w_out, w_router, router_bias, moe_w_gu, moe_w_down):
    bsz, seq, d = x.shape
    depth = w_ada.shape[0]
    n = bsz * seq
    xf = x.reshape(n, d)
    mod = _ada_mod(c, w_ada, b_ada)
    w_router_t = w_router.T
    wr_bcast = jnp.broadcast_to(w_router_t[:, :, None], (N_EXPERTS, d, LANES)).astype(BF16)
    pos_row = positions.reshape(n)
    w_gu = moe_w_gu.astype(BF16)
    w_down = moe_w_down.astype(BF16)

    residual = None
    for l in range(depth):
        sh1, sc1, g1, sh2, sc2, g2 = [mod[l, :, i * d:(i + 1) * d].reshape(bsz, 1, d) for i in range(6)]
        j = l // 2
        if l % 2 == 0:
            w_in = a_w_in[j]
            w_main = w_in[:, :A_MAIN_COLS].astype(BF16)
            w_gate = jnp.pad(w_in[:, A_MAIN_COLS:], ((0, 0), (0, LANES - 2 * A_HEADS))).astype(BF16)
        else:
            w_main, w_gate = b_w_in[j].astype(BF16), None
        outs = _norm_matmul(xf, residual, norm1[l].reshape(1, d), 1.0 + sc1, sh1, w_main, w_gate, seq)
        if residual is not None:
            xf, *outs = outs
        if l % 2 == 0:
            proj, gates = outs
            bias_row = jnp.pad(a_b_if[j], (0, LANES - 2 * A_HEADS)).reshape(1, LANES)
            mixed = _mlstm(proj, gates, bias_row, a_h_norm[j], bsz, seq)
            w_out = a_w_out[j].astype(BF16)
        else:
            (proj,) = outs
            qk = _qk_prep(proj, pos_row, b_q_norm[j], b_k_norm[j], seq)
            lam_tab = jnp.zeros((SUBLANES, LANES), F32)
            for r, v in enumerate((b_lam_q1[j], b_lam_k1[j], b_lam_q2[j], b_lam_k2[j])):
                lam_tab = lam_tab.at[r, :B_HEAD_DIM].set(v)
            lam_init = 0.8 - 0.6 * math.exp(-0.3 * l)
            mixed = _diff_attn(qk, proj, lam_tab, b_o_norm[j], bsz, seq, lam_init)
            w_out = b_w_out[j].astype(BF16)
        xf, h_rows, logits_t = _outproj_norm_router(
            mixed, w_out, xf, g1, norm2[l].reshape(1, d), 1.0 + sc2, sh2, w_router_t, seq)
        residual = (_moe(h_rows, logits_t, router_bias, l, w_gu, w_down, wr_bcast), g2)
    return _residual(residual[0], xf, residual[1], seq).reshape(bsz, seq, d)
```

```python
import functools
import math

import jax
import jax.numpy as jnp
from jax import lax
from jax.experimental import pallas as pl
from jax.experimental.pallas import tpu as pltpu

D_MODEL = 1024
A_HEADS = 4
A_QK_DIM = 128
A_V_DIM = 256
A_CHUNK = 128
A_HQ = A_HEADS * A_QK_DIM
A_HV = A_HEADS * A_V_DIM
A_MAIN_COLS = 2 * A_HQ + 2 * A_HV

B_HEADS = 8
B_HEAD_DIM = 64
B_V_DIM = 128
B_ROT_DIM = 16
ROPE_THETA = 500000.0

N_EXPERTS = 16
N_GROUPS = 4
EXPERTS_PER_GROUP = 4
TOP_K = 2
D_EXPERT = 512
EPS = 1e-6

LANES = 128
SUBLANES = 8
ROW_TILES = D_MODEL // LANES
assert ROW_TILES == SUBLANES
VMEM_LIMIT = 48 * 1024 * 1024

F32 = jnp.float32
BF16 = jnp.bfloat16
HIGHEST = lax.Precision.HIGHEST


def _params(sem):
    return pltpu.CompilerParams(dimension_semantics=sem, vmem_limit_bytes=VMEM_LIMIT)


def _tile(pref, n):
    t = min(pref, n)
    assert n % t == 0, (pref, n)
    return t


def _ada_kernel(c_ref, w_ref, b_ref, o_ref):
    c = c_ref[...]
    c_act = c * jax.nn.sigmoid(c)
    o_ref[0] = jnp.dot(c_act, w_ref[0], precision=HIGHEST, preferred_element_type=F32) + b_ref[0]


def _ada_mod(c, w_ada, b_ada):
    depth, d, six_d = w_ada.shape
    bsz = c.shape[0]
    tn = _tile(1536, six_d)
    return pl.pallas_call(
        _ada_kernel,
        grid=(depth, six_d // tn),
        in_specs=[
            pl.BlockSpec((bsz, d), lambda l, j: (0, 0)),
            pl.BlockSpec((1, d, tn), lambda l, j: (l, 0, j)),
            pl.BlockSpec((1, 1, tn), lambda l, j: (l, 0, j)),
        ],
        out_specs=pl.BlockSpec((1, bsz, tn), lambda l, j: (l, 0, j)),
        out_shape=jax.ShapeDtypeStruct((depth, bsz, six_d), F32),
        compiler_params=_params(("parallel", "parallel")),
        name="ada_mod",
    )(c, w_ada, b_ada.reshape(depth, 1, six_d))


def _modulated_rms(x, g, sc1p, sh):
    y = x * lax.rsqrt(jnp.mean(x * x, axis=-1, keepdims=True) + EPS)
    return (y * g) * sc1p + sh


def _gated_residual(x_ref, y_ref, g_ref):
    rows = x_ref.shape[0]
    g = g_ref[0]
    return jnp.concatenate(
        [x_ref[:, j * LANES:(j + 1) * LANES]
         + g[:, j * LANES:(j + 1) * LANES] * y_ref[pl.ds(j, rows, stride=ROW_TILES), :] for j in range(ROW_TILES)],
        axis=1)


def _norm_mm_kernel(*refs, col_chunk, has_gates, has_residual):
    refs = list(refs)
    x_ref = refs.pop(0)
    if has_residual:
        y_ref, g2_ref = refs.pop(0), refs.pop(0)
    g_ref, sc_ref, sh_ref, w_ref = refs[:4]
    refs = refs[4:]
    wg_ref = refs.pop(0) if has_gates else None
    xo_ref = refs.pop(0) if has_residual else None
    o_ref = refs.pop(0)
    og_ref = refs.pop(0) if has_gates else None

    if has_residual:
        x = _gated_residual(x_ref, y_ref, g2_ref)
        xo_ref[...] = x
    else:
        x = x_ref[...]
    hb = _modulated_rms(x, g_ref[...], sc_ref[0], sh_ref[0]).astype(BF16)
    for c0 in range(0, o_ref.shape[1], col_chunk):
        o_ref[:, c0:c0 + col_chunk] = jnp.dot(
            hb, w_ref[:, c0:c0 + col_chunk], preferred_element_type=F32).astype(o_ref.dtype)
    if has_gates:
        og_ref[...] = jnp.dot(hb, wg_ref[...], preferred_element_type=F32)


def _norm_matmul(x, residual, gain, sc1p, sh, w, wg, seq):
    n, d = x.shape
    cols = w.shape[1]
    tm = _tile(512, seq)
    per_b = seq // tm
    has_gates = wg is not None
    has_residual = residual is not None
    bmap = lambda i: (i // per_b, 0, 0)
    in_specs = [pl.BlockSpec((tm, d), lambda i: (i, 0))]
    args = [x]
    if has_residual:
        in_specs += [pl.BlockSpec((tm * ROW_TILES, LANES), lambda i: (i, 0)), pl.BlockSpec((1, 1, d), bmap)]
        args += list(residual)
    in_specs += [
        pl.BlockSpec((1, d), lambda i: (0, 0)),
        pl.BlockSpec((1, 1, d), bmap),
        pl.BlockSpec((1, 1, d), bmap),
        pl.BlockSpec((d, cols), lambda i: (0, 0)),
    ]
    args += [gain, sc1p, sh, w]
    out_specs, out_shape = [], []
    if has_residual:
        out_specs.append(pl.BlockSpec((tm, d), lambda i: (i, 0)))
        out_shape.append(jax.ShapeDtypeStruct((n, d), F32))
    out_specs.append(pl.BlockSpec((tm, cols), lambda i: (i, 0)))
    out_shape.append(jax.ShapeDtypeStruct((n, cols), BF16))
    if has_gates:
        in_specs.append(pl.BlockSpec((d, LANES), lambda i: (0, 0)))
        out_specs.append(pl.BlockSpec((tm, LANES), lambda i: (i, 0)))
        out_shape.append(jax.ShapeDtypeStruct((n, LANES), F32))
        args.append(wg)
    return pl.pallas_call(
        functools.partial(_norm_mm_kernel, col_chunk=512, has_gates=has_gates, has_residual=has_residual),
        grid=(n // tm,),
        in_specs=in_specs,
        out_specs=out_specs,
        out_shape=out_shape,
        compiler_params=_params(("parallel",)),
        name="norm_inproj",
    )(*args)


def _log_sigmoid(x):
    return jnp.minimum(x, 0.0) - jnp.log1p(jnp.exp(-jnp.abs(x)))


def _mlstm_kernel(p_ref, g_ref, bias_ref, hn_ref, o_ref, ct_ref, n_ref, m_ref):
    L = A_CHUNK

    @pl.when(pl.program_id(1) == 0)
    def _():
        ct_ref[...] = jnp.zeros_like(ct_ref)
        n_ref[...] = jnp.zeros_like(n_ref)
        m_ref[...] = jnp.zeros_like(m_ref)

    gates = g_ref[...] + bias_ref[...]
    src = lax.broadcasted_iota(jnp.int32, (L, L), 0)
    tgt = lax.broadcasted_iota(jnp.int32, (L, L), 1)
    causal = src <= tgt
    bcum = jnp.dot(jnp.where(src >= tgt, 1.0, 0.0), _log_sigmoid(gates), precision=HIGHEST,
                   preferred_element_type=F32)
    bcum_t = bcum.T
    nt = (((1,), (1,)), ((), ()))

    for h in range(A_HEADS):
        q = p_ref[:, h * A_QK_DIM:(h + 1) * A_QK_DIM]
        ksf = p_ref[:, A_HQ + h * A_QK_DIM:A_HQ + (h + 1) * A_QK_DIM].astype(F32) * (A_QK_DIM ** -0.5)
        v = p_ref[:, 2 * A_HQ + h * A_V_DIM:2 * A_HQ + (h + 1) * A_V_DIM]
        og = p_ref[:, 2 * A_HQ + A_HV + h * A_V_DIM:2 * A_HQ + A_HV + (h + 1) * A_V_DIM].astype(F32)
        b_row = bcum_t[A_HEADS + h:A_HEADS + h + 1, :]
        b_last = b_row[:, L - 1:L]
        c_col = gates[:, h:h + 1] - bcum[:, A_HEADS + h:A_HEADS + h + 1]
        m11 = m_ref[h][:, 0:1]
        ct_old = ct_ref[h]
        n_old = n_ref[h]

        log_d = jnp.where(causal, c_col + b_row, -jnp.inf)
        log_inter = b_row + m11
        m_t = jnp.maximum(jnp.max(log_d, axis=0, keepdims=True), log_inter)
        dmat = jnp.exp(log_d - m_t)
        inter = jnp.exp(log_inter - m_t)
        s = lax.dot_general(ksf.astype(BF16), q, nt, preferred_element_type=F32) * dmat
        vt = v.astype(F32).T.astype(BF16)
        num = (jnp.dot(vt, s.astype(BF16), preferred_element_type=F32)
               + inter * lax.dot_general(ct_old.astype(BF16), q, nt, preferred_element_type=F32))
        qn = lax.dot_general(n_old.astype(BF16), q, nt, preferred_element_type=F32)[0:1]
        den = jnp.sum(s, axis=0, keepdims=True) + inter * qn
        hh = num / jnp.maximum(jnp.abs(den), jnp.exp(-m_t))

        lw_col = b_last + c_col
        m_new = jnp.maximum(b_last + m11, jnp.max(lw_col, axis=0, keepdims=True))
        kw = ksf * jnp.exp(lw_col - m_new)
        decay = jnp.exp(b_last + m11 - m_new)
        ct_ref[h] = decay * ct_old + jnp.dot(vt, kw.astype(BF16), preferred_element_type=F32)
        n_ref[h] = decay * n_old + jnp.broadcast_to(jnp.sum(kw, axis=0, keepdims=True), n_old.shape)
        m_ref[h] = jnp.broadcast_to(m_new, (1, LANES))

        hn = hh * lax.rsqrt(jnp.mean(hh * hh, axis=0, keepdims=True) + EPS)
        hn = (hn * hn_ref[h * A_V_DIM:(h + 1) * A_V_DIM, :]).T
        o_ref[:, h * A_V_DIM:(h + 1) * A_V_DIM] = (hn * jax.nn.sigmoid(og)).astype(o_ref.dtype)


def _mlstm(proj, gates, bias_row, h_norm, bsz, seq):
    n = proj.shape[0]
    nc = seq // A_CHUNK
    h_norm_cols = jnp.broadcast_to(h_norm[:, None], (A_HV, A_CHUNK))
    return pl.pallas_call(
        _mlstm_kernel,
        grid=(bsz, nc),
        in_specs=[
            pl.BlockSpec((A_CHUNK, A_MAIN_COLS), lambda b, c: (b * nc + c, 0)),
            pl.BlockSpec((A_CHUNK, LANES), lambda b, c: (b * nc + c, 0)),
            pl.BlockSpec((1, LANES), lambda b, c: (0, 0)),
            pl.BlockSpec((A_HV, A_CHUNK), lambda b, c: (0, 0)),
        ],
        out_specs=pl.BlockSpec((A_CHUNK, A_HV), lambda b, c: (b * nc + c, 0)),
        out_shape=jax.ShapeDtypeStruct((n, A_HV), BF16),
        scratch_shapes=[
            pltpu.VMEM((A_HEADS, A_V_DIM, A_QK_DIM), F32),
            pltpu.VMEM((A_HEADS, SUBLANES, A_QK_DIM), F32),
            pltpu.VMEM((A_HEADS, 1, LANES), F32),
        ],
        compiler_params=_params(("parallel", "arbitrary")),
        name="mlstm",
    )(proj, gates, bias_row, h_norm_cols)


def _qk_prep_kernel(p_ref, pos_ref, seg_ref, segt_ref, gain_ref, freq_ref, o_ref):
    tm = p_ref.shape[0]
    x = p_ref[...].astype(F32)
    xx = x * x
    xx_hi = xx.astype(BF16)
    xx_lo = (xx - xx_hi.astype(F32)).astype(BF16)
    seg = seg_ref[...]
    ss = jnp.dot(xx_hi, seg, preferred_element_type=F32) + jnp.dot(xx_lo, seg, preferred_element_type=F32)
    r = lax.rsqrt(ss * (1.0 / B_HEAD_DIM) + EPS)
    r_hi = r.astype(BF16)
    r_lo = (r - r_hi.astype(F32)).astype(BF16)
    segt = segt_ref[...]
    r_full = jnp.dot(r_hi, segt, preferred_element_type=F32) + jnp.dot(r_lo, segt, preferred_element_type=F32)

    half = B_ROT_DIM // 2
    assert half == SUBLANES
    ang = freq_ref[:, 0:1] * pos_ref[0].astype(F32)
    cos8 = jnp.cos(ang)
    sin8 = jnp.sin(ang)
    one8 = jnp.ones_like(cos8)
    zero8 = jnp.zeros_like(cos8)
    per_seg = B_HEAD_DIM // SUBLANES
    n_seg = LANES // B_HEAD_DIM

    def chunk_pattern(first, second, rest):
        groups = ([first, second] + [rest] * (per_seg - 2)) * n_seg
        return jnp.concatenate(groups, axis=0).T

    c_mul = chunk_pattern(cos8, cos8, one8)
    s_lo = chunk_pattern(-sin8, zero8, zero8)
    s_hi = chunk_pattern(zero8, sin8, zero8)
    for j in range(2 * D_MODEL // LANES):
        sl = slice(j * LANES, (j + 1) * LANES)
        y = x[:, sl] * r_full[:, sl] * gain_ref[:, sl]
        out = y * c_mul + pltpu.roll(y, LANES - half, axis=1) * s_lo + pltpu.roll(y, half, axis=1) * s_hi
        if j < D_MODEL // LANES:
            out = out * (B_HEAD_DIM ** -0.5 * math.log2(math.e))
        o_ref[:, sl] = out.astype(o_ref.dtype)


def _qk_prep(proj, positions_row, q_norm, k_norm, seq):
    n = proj.shape[0]
    two_d = 2 * D_MODEL
    tm = _tile(256, seq)
    seg = (jnp.arange(two_d)[:, None] // B_HEAD_DIM == jnp.arange(LANES)[None, :]).astype(BF16)
    gain = jnp.concatenate([jnp.tile(q_norm, D_MODEL // B_HEAD_DIM), jnp.tile(k_norm, D_MODEL // B_HEAD_DIM)])
    inv_freq = ROPE_THETA ** (-jnp.arange(0, B_ROT_DIM, 2, dtype=F32) / B_ROT_DIM)
    freq_tab = jnp.broadcast_to(inv_freq[:, None], (B_ROT_DIM // 2, LANES))
    return pl.pallas_call(
        _qk_prep_kernel,
        grid=(n // tm,),
        in_specs=[
            pl.BlockSpec((tm, two_d), lambda i: (i, 0)),
            pl.BlockSpec((1, 1, tm), lambda i: (i, 0, 0)),
            pl.BlockSpec((two_d, LANES), lambda i: (0, 0)),
            pl.BlockSpec((LANES, two_d), lambda i: (0, 0)),
            pl.BlockSpec((1, two_d), lambda i: (0, 0)),
            pl.BlockSpec((SUBLANES, LANES), lambda i: (0, 0)),
        ],
        out_specs=pl.BlockSpec((tm, two_d), lambda i: (i, 0)),
        out_shape=jax.ShapeDtypeStruct((n, two_d), BF16),
        compiler_params=_params(("parallel",)),
        name="qk_prep",
    )(proj, positions_row.reshape(n // tm, 1, tm), seg, seg.T, gain.reshape(1, two_d), freq_tab)


ATTN_TILE = 512


def _diff_attn_kernel(q_ref, k_ref, v_ref, lam_ref, on_ref, o_ref, vt_ref, qm_ref, m_ref, l_ref, acc_ref,
                      *, tile, lam_init):
    seq = q_ref.shape[0]
    n_tiles = seq // tile
    lamv = lam_ref[...]
    lam = (jnp.exp(jnp.sum(lamv[0:1] * lamv[1:2], axis=1, keepdims=True))
           - jnp.exp(jnp.sum(lamv[2:3] * lamv[3:4], axis=1, keepdims=True)) + lam_init)
    tchunk = min(256, seq)
    for c in range(seq // tchunk):
        vt_ref[:, c * tchunk:(c + 1) * tchunk] = v_ref[c * tchunk:(c + 1) * tchunk, :].astype(F32).T.astype(BF16)
    first_map = lax.broadcasted_iota(jnp.int32, (tile, LANES), 1) < B_HEAD_DIM
    key_id = lax.broadcasted_iota(jnp.int32, (tile, tile), 0)
    query_id = lax.broadcasted_iota(jnp.int32, (tile, tile), 1)
    causal = query_id >= key_id
    nt = (((1,), (1,)), ((), ()))

    def tile_step(k, vt, mask):
        scores = [lax.dot_general(k, qm_ref[c], nt, preferred_element_type=F32) for c in range(2)]
        for c, s in enumerate(scores):
            if mask is not None:
                s = jnp.where(mask, s, -jnp.inf)
            m_old = m_ref[c]
            m_new = jnp.maximum(m_old, jnp.max(s, axis=0, keepdims=True))
            alpha = jnp.exp2(m_old - m_new)
            p = jnp.exp2(s - m_new)
            l_ref[c] = alpha * l_ref[c] + jnp.sum(p, axis=0, keepdims=True)
            acc_ref[c] = alpha * acc_ref[c] + jnp.dot(vt, p.astype(BF16), preferred_element_type=F32)
            m_ref[c] = m_new

    for qi in range(n_tiles):
        q = q_ref[qi * tile:(qi + 1) * tile, :]
        qm_ref[0] = jnp.where(first_map, q, jnp.zeros_like(q))
        qm_ref[1] = jnp.where(first_map, jnp.zeros_like(q), q)
        m_ref[...] = jnp.full(m_ref.shape, -jnp.inf, F32)
        l_ref[...] = jnp.zeros(l_ref.shape, F32)
        acc_ref[...] = jnp.zeros(acc_ref.shape, F32)
        for ki in range(qi + 1):
            tile_step(k_ref[ki * tile:(ki + 1) * tile, :], vt_ref[:, ki * tile:(ki + 1) * tile],
                      causal if ki == qi else None)
        o = acc_ref[0] / l_ref[0] - lam * (acc_ref[1] / l_ref[1])
        o = o * lax.rsqrt(jnp.mean(o * o, axis=0, keepdims=True) + EPS) * on_ref[...] * (1.0 - lam_init)
        o_ref[qi * tile:(qi + 1) * tile, :] = o.T.astype(o_ref.dtype)


def _diff_attn(qk, proj, lam_tab, o_norm, bsz, seq, lam_init):
    n = qk.shape[0]
    tile = _tile(ATTN_TILE, seq)
    o_norm_cols = jnp.broadcast_to(o_norm[:, None], (B_V_DIM, tile))
    return pl.pallas_call(
        functools.partial(_diff_attn_kernel, tile=tile, lam_init=lam_init),
        grid=(bsz, B_HEADS),
        in_specs=[
            pl.BlockSpec((seq, LANES), lambda b, h: (b, h)),
            pl.BlockSpec((seq, LANES), lambda b, h: (b, B_HEADS + h)),
            pl.BlockSpec((seq, B_V_DIM), lambda b, h: (b, 2 * B_HEADS + h)),
            pl.BlockSpec((SUBLANES, LANES), lambda b, h: (0, 0)),
            pl.BlockSpec((B_V_DIM, tile), lambda b, h: (0, 0)),
        ],
        out_specs=pl.BlockSpec((seq, B_V_DIM), lambda b, h: (b, h)),
        out_shape=jax.ShapeDtypeStruct((n, D_MODEL), BF16),
        scratch_shapes=[
            pltpu.VMEM((B_V_DIM, seq), BF16),
            pltpu.VMEM((2, tile, LANES), BF16),
            pltpu.VMEM((2, 1, tile), F32),
            pltpu.VMEM((2, 1, tile), F32),
            pltpu.VMEM((2, B_V_DIM, tile), F32),
        ],
        compiler_params=_params(("parallel", "parallel")),
        name="diff_attn",
    )(qk, qk, proj, lam_tab, o_norm_cols)


HALF_D = D_MODEL // 2
HI16 = 0xFFFF0000
PACKED_TILES = HALF_D // LANES


def _pack_bf16_pairs(v):
    bits = lax.bitcast_convert_type(v.astype(BF16).astype(F32), jnp.uint32)
    return (bits[:, HALF_D:] & jnp.uint32(HI16)) | (bits[:, :HALF_D] >> 16)


def _unpack_bf16_pairs(words):
    first = lax.bitcast_convert_type(words << 16, F32)
    second = lax.bitcast_convert_type(words & jnp.uint32(HI16), F32)
    return first.astype(BF16), second.astype(BF16)


def _outproj_kernel(a_ref, w_ref, x_ref, g1_ref, gain_ref, sc_ref, sh_ref, wr_ref, xo_ref, hp_ref, lg_ref):
    y = jnp.dot(a_ref[...], w_ref[...], preferred_element_type=F32)
    xm = x_ref[...] + g1_ref[0] * y
    xo_ref[...] = xm
    h2 = _modulated_rms(xm, gain_ref[...], sc_ref[0], sh_ref[0])
    tm = xm.shape[0]
    packed = _pack_bf16_pairs(h2)
    for j in range(ROW_TILES):
        hp_ref[pl.ds(j, tm, stride=ROW_TILES), :] = (
            packed[:, j * LANES:(j + 1) * LANES] if j < PACKED_TILES else jnp.zeros((tm, LANES), jnp.uint32))
    lg_ref[...] = lax.dot_general(wr_ref[...], h2, (((1,), (1,)), ((), ())),
                                  precision=HIGHEST, preferred_element_type=F32)


def _outproj_norm_router(a, w, x, g1, gain, sc1p, sh, w_router_t, seq):
    n, d = x.shape
    tm = _tile(512, seq)
    per_b = seq // tm
    bmap = lambda i: (i // per_b, 0, 0)
    return pl.pallas_call(
        _outproj_kernel,
        grid=(n // tm,),
        in_specs=[
            pl.BlockSpec((tm, a.shape[1]), lambda i: (i, 0)),
            pl.BlockSpec(w.shape, lambda i: (0, 0)),
            pl.BlockSpec((tm, d), lambda i: (i, 0)),
            pl.BlockSpec((1, 1, d), bmap),
            pl.BlockSpec((1, d), lambda i: (0, 0)),
            pl.BlockSpec((1, 1, d), bmap),
            pl.BlockSpec((1, 1, d), bmap),
            pl.BlockSpec((N_EXPERTS, d), lambda i: (0, 0)),
        ],
        out_specs=[
            pl.BlockSpec((tm, d), lambda i: (i, 0)),
            pl.BlockSpec((tm * ROW_TILES, LANES), lambda i: (i, 0)),
            pl.BlockSpec((N_EXPERTS, tm), lambda i: (0, i)),
        ],
        out_shape=[
            jax.ShapeDtypeStruct((n, d), F32),
            jax.ShapeDtypeStruct((n * ROW_TILES, LANES), jnp.uint32),
            jax.ShapeDtypeStruct((N_EXPERTS, n), F32),
        ],
        compiler_params=_params(("parallel",)),
        name="outproj_norm_router",
    )(a, w, x, g1, gain, sc1p, sh, w_router_t)


PAIR_LO = (0, 0, 0, 1, 1, 2)
PAIR_HI = (1, 2, 3, 2, 3, 3)
PAIRS_PER_GROUP = len(PAIR_LO)
N_CLASSES = N_GROUPS * PAIRS_PER_GROUP
CLASS_ROWS = 32
RANK_BITS = 20
RANK_SPAN = 1 << RANK_BITS


def _route_kernel(lg_ref, bias_ref, oi_ref, cnt_ref, carry_ref, tri_ref):
    tr = lg_ref.shape[1]
    step = pl.program_id(0)

    @pl.when(step == 0)
    def _():
        carry_ref[...] = jnp.zeros_like(carry_ref)
        r = lax.broadcasted_iota(jnp.int32, (tr, tr), 0)
        c = lax.broadcasted_iota(jnp.int32, (tr, tr), 1)
        tri_ref[...] = jnp.where(r < c, 1.0, 0.0).astype(BF16)

    scores = jax.nn.sigmoid(lg_ref[...])
    biased = scores + bias_ref[...]
    rows = [biased[e:e + 1, :] for e in range(N_EXPERTS)]

    def top2_sum(a, b, c, d):
        m1, n1 = jnp.maximum(a, b), jnp.minimum(a, b)
        m2, n2 = jnp.maximum(c, d), jnp.minimum(c, d)
        return jnp.maximum(m1, m2) + jnp.maximum(jnp.minimum(m1, m2), jnp.maximum(n1, n2))

    gscore = [top2_sum(*rows[g * EXPERTS_PER_GROUP:(g + 1) * EXPERTS_PER_GROUP]) for g in range(N_GROUPS)]
    best = gscore[0]
    gsel = jnp.zeros_like(best, dtype=jnp.int32)
    for g in range(1, N_GROUPS):
        upd = gscore[g] > best
        gsel = jnp.where(upd, g, gsel)
        best = jnp.where(upd, gscore[g], best)

    def pick(table, j):
        out = table[j]
        for g in range(1, N_GROUPS):
            out = jnp.where(gsel == g, table[g * EXPERTS_PER_GROUP + j], out)
        return out

    in_b = [pick(rows, j) for j in range(EXPERTS_PER_GROUP)]
    v1, i1 = in_b[0], jnp.zeros_like(gsel)
    for j in range(1, EXPERTS_PER_GROUP):
        upd = in_b[j] > v1
        v1 = jnp.where(upd, in_b[j], v1)
        i1 = jnp.where(upd, j, i1)
    v2 = jnp.full_like(v1, -jnp.inf)
    i2 = jnp.zeros_like(gsel)
    for j in range(EXPERTS_PER_GROUP):
        upd = (i1 != j) & (in_b[j] > v2)
        v2 = jnp.where(upd, in_b[j], v2)
        i2 = jnp.where(upd, j, i2)
    first_is_lo = i1 < i2
    lo = jnp.where(first_is_lo, i1, i2)
    hi = jnp.where(first_is_lo, i2, i1)
    pair = jnp.where(lo == 0, hi - 1, jnp.where(lo == 1, hi + 1, PAIRS_PER_GROUP - 1))
    cls = gsel * PAIRS_PER_GROUP + pair

    cid = lax.broadcasted_iota(jnp.int32, (CLASS_ROWS, tr), 0)
    onehot = jnp.where(cid == cls, 1.0, 0.0)
    before = jnp.dot(onehot.astype(BF16), tri_ref[...], preferred_element_type=F32) + carry_ref[...]
    rank = jnp.sum(onehot * before, axis=0, keepdims=True).astype(jnp.int32)
    oi_ref[...] = cls * RANK_SPAN + rank
    new_carry = carry_ref[...] + jnp.sum(onehot, axis=1, keepdims=True)
    carry_ref[...] = new_carry
    cnt_ref[...] = new_carry.astype(jnp.int32)


def _route(logits_t, router_bias):
    n = logits_t.shape[1]
    tr = _tile(512, n)
    return pl.pallas_call(
        _route_kernel,
        grid=(n // tr,),
        in_specs=[
            pl.BlockSpec((N_EXPERTS, tr), lambda i: (0, i)),
            pl.BlockSpec((N_EXPERTS, 1), lambda i: (0, 0)),
        ],
        out_specs=[
            pl.BlockSpec((1, tr), lambda i: (0, i)),
            pl.BlockSpec((CLASS_ROWS, 1), lambda i: (0, 0)),
        ],
        out_shape=[
            jax.ShapeDtypeStruct((1, n), jnp.int32),
            jax.ShapeDtypeStruct((CLASS_ROWS, 1), jnp.int32),
        ],
        scratch_shapes=[pltpu.VMEM((CLASS_ROWS, 1), F32), pltpu.VMEM((tr, tr), BF16)],
        compiler_params=_params(("arbitrary",)),
        name="route",
    )(logits_t, router_bias.reshape(N_EXPERTS, 1).astype(F32))


SORT_UNROLL = 8
MOE_ROWS = 256
MOE_ROWS_LOG2 = MOE_ROWS.bit_length() - 1
assert 1 << MOE_ROWS_LOG2 == MOE_ROWS


def _plan_kernel(code_ref, cnt_ref, tok_ref, ea_ref, eb_ref, off_ref, nv_ref, nu_ref, start_ref):
    n_blocks = ea_ref.shape[0]
    run = jnp.int32(0)
    blk = jnp.int32(0)
    for c in range(N_CLASSES):
        cnt = cnt_ref[c]
        start_ref[c] = run - c * RANK_SPAN
        e_lo = (c // PAIRS_PER_GROUP) * EXPERTS_PER_GROUP + PAIR_LO[c % PAIRS_PER_GROUP]
        e_hi = (c // PAIRS_PER_GROUP) * EXPERTS_PER_GROUP + PAIR_HI[c % PAIRS_PER_GROUP]
        n_blk = (cnt + (MOE_ROWS - 1)) >> MOE_ROWS_LOG2

        def fill(b, carry, run=run, blk=blk, cnt=cnt, e_lo=e_lo, e_hi=e_hi):
            ea_ref[blk + b] = e_lo
            eb_ref[blk + b] = e_hi
            off_ref[blk + b] = run + b * MOE_ROWS
            nv_ref[blk + b] = jnp.minimum(cnt - b * MOE_ROWS, MOE_ROWS)
            return carry

        lax.fori_loop(0, n_blk, fill, 0)
        run = run + cnt
        blk = blk + n_blk
    nu_ref[0] = blk

    def fill_unused(b, carry):
        ea_ref[b] = ea_ref[blk - 1]
        eb_ref[b] = eb_ref[blk - 1]
        off_ref[b] = 0
        nv_ref[b] = 0
        return carry

    lax.fori_loop(blk, n_blocks, fill_unused, 0)

    def place(i, carry):
        toks = [i * SORT_UNROLL + u for u in range(SORT_UNROLL)]
        codes = [code_ref[t] for t in toks]
        slots = [start_ref[code >> RANK_BITS] + code for code in codes]
        for t, p in zip(toks, slots):
            tok_ref[p] = t
        return carry

    lax.fori_loop(0, code_ref.shape[0] // SORT_UNROLL, place, 0)


def _plan(codes, counts, n_blocks):
    n = codes.shape[0]
    assert n % SORT_UNROLL == 0 and n <= RANK_SPAN
    smem = pl.BlockSpec(memory_space=pltpu.SMEM)
    i32 = lambda size: jax.ShapeDtypeStruct((size,), jnp.int32)
    return pl.pallas_call(
        _plan_kernel,
        in_specs=[smem, smem],
        out_specs=[smem] * 6,
        out_shape=[i32(n), i32(n_blocks), i32(n_blocks), i32(n_blocks), i32(n_blocks), i32(1)],
        scratch_shapes=[pltpu.SMEM((N_CLASSES,), jnp.int32)],
        name="moe_plan",
    )(codes, counts)


ROW_UNROLL = 8


def _for_rows(n, fn):
    groups = n // ROW_UNROLL

    def group(g, c):
        for u in range(ROW_UNROLL):
            fn(g * ROW_UNROLL + u)
        return c

    def single(r, c):
        fn(r)
        return c

    lax.fori_loop(0, groups, group, 0)
    lax.fori_loop(groups * ROW_UNROLL, n, single, 0)


def _row_copy(src_ref, src_row, dst_ref, dst_row, sem):
    return pltpu.make_async_copy(
        src_ref.at[pl.ds(pl.multiple_of(src_row * ROW_TILES, ROW_TILES), ROW_TILES)],
        dst_ref.at[pl.ds(pl.multiple_of(dst_row * ROW_TILES, ROW_TILES), ROW_TILES)],
        sem)


SCATTER_DMA_PRIORITY = 1


def _expert_kernel(ea_ref, eb_ref, off_ref, nv_ref, nu_ref, tok_ref,
                   h_hbm, wgu_a_ref, wd_a_ref, wgu_b_ref, wd_b_ref, wr_a_ref, wr_b_ref, y_hbm,
                   xbuf, ybuf, gsem, ssem):
    del ea_ref, eb_ref
    tb = xbuf.shape[1] // ROW_TILES
    i = pl.program_id(0)
    slot = i % 2
    n_used = nu_ref[0]

    def start_gather(block, s):
        base = off_ref[block]
        _for_rows(nv_ref[block],
                  lambda r: _row_copy(h_hbm, tok_ref[base + r], xbuf.at[s], r, gsem.at[s]).start())

    def start_scatter(block, s):
        base = off_ref[block]
        _for_rows(nv_ref[block],
                  lambda r: _row_copy(ybuf.at[s], r, y_hbm, tok_ref[base + r], ssem.at[s]).start(
                      priority=SCATTER_DMA_PRIORITY))

    def wait_rows(block, hbm, buf, sem):
        rows = nv_ref[block] * ROW_TILES

        @pl.when(rows > 0)
        def _():
            pltpu.make_async_copy(hbm.at[pl.ds(0, rows)], buf.at[pl.ds(0, rows)], sem).wait()

    @pl.when(i == 0)
    def _():
        xbuf[...] = jnp.zeros_like(xbuf)
        start_gather(0, 0)

    @pl.when(i < n_used)
    def _():
        @pl.when(i + 1 < n_used)
        def _():
            start_gather(i + 1, 1 - slot)

        wait_rows(i, h_hbm, xbuf.at[slot], gsem.at[slot])

        @pl.when(i >= 2)
        def _():
            wait_rows(i - 2, y_hbm, ybuf.at[slot], ssem.at[slot])

        words = [xbuf[slot, pl.ds(j, tb, stride=ROW_TILES), :] for j in range(PACKED_TILES)]
        halves = [_unpack_bf16_pairs(w) for w in words]
        x = jnp.concatenate([h[0] for h in halves] + [h[1] for h in halves], axis=1)

        def mlp(wgu_ref, wd_ref):
            gu = jnp.dot(x, wgu_ref[0], preferred_element_type=F32)
            gate = gu[:, :D_EXPERT]
            act = gate * jax.nn.sigmoid(gate) * gu[:, D_EXPERT:]
            return jnp.dot(act.astype(BF16), wd_ref[0], preferred_element_type=F32)

        score_a = jax.nn.sigmoid(jnp.dot(x, wr_a_ref[0], preferred_element_type=F32))
        score_b = jax.nn.sigmoid(jnp.dot(x, wr_b_ref[0], preferred_element_type=F32))
        inv_tot = 1.0 / (score_a + score_b)
        w_a = score_a * inv_tot
        w_b = score_b * inv_tot
        ya = mlp(wgu_a_ref, wd_a_ref)
        yb = mlp(wgu_b_ref, wd_b_ref)
        for j in range(ROW_TILES):
            sl = slice(j * LANES, (j + 1) * LANES)
            ybuf[slot, pl.ds(j, tb, stride=ROW_TILES), :] = w_a * ya[:, sl] + w_b * yb[:, sl]
        start_scatter(i, slot)

        @pl.when(i == n_used - 1)
        def _():
            wait_rows(i, y_hbm, ybuf.at[slot], ssem.at[slot])

            @pl.when(i >= 1)
            def _():
                wait_rows(i - 1, y_hbm, ybuf.at[1 - slot], ssem.at[1 - slot])


def _experts(tables, sorted_tok, h_rows, layer, w_gu, w_down, wr_bcast, tb):
    block_ea, block_eb, src_off, n_valid, n_used = tables
    n_blocks = block_ea.shape[0]
    d, two_f = w_gu.shape[2:]
    ea_map = lambda i, ea, eb, off, nv, nu, tok: (ea[i], 0, 0)
    eb_map = lambda i, ea, eb, off, nv, nu, tok: (eb[i], 0, 0)
    lea_map = lambda i, ea, eb, off, nv, nu, tok: (layer, ea[i], 0, 0)
    leb_map = lambda i, ea, eb, off, nv, nu, tok: (layer, eb[i], 0, 0)
    grid_spec = pltpu.PrefetchScalarGridSpec(
        num_scalar_prefetch=6,
        grid=(n_blocks,),
        in_specs=[
            pl.BlockSpec(memory_space=pl.ANY),
            pl.BlockSpec((None, 1, d, two_f), lea_map),
            pl.BlockSpec((None, 1, two_f // 2, d), lea_map),
            pl.BlockSpec((None, 1, d, two_f), leb_map),
            pl.BlockSpec((None, 1, two_f // 2, d), leb_map),
            pl.BlockSpec((1, d, LANES), ea_map),
            pl.BlockSpec((1, d, LANES), eb_map),
        ],
        out_specs=pl.BlockSpec(memory_space=pl.ANY),
        scratch_shapes=[
            pltpu.VMEM((2, tb * ROW_TILES, LANES), jnp.uint32),
            pltpu.VMEM((2, tb * ROW_TILES, LANES), F32),
            pltpu.SemaphoreType.DMA((2,)),
            pltpu.SemaphoreType.DMA((2,)),
        ],
    )
    return pl.pallas_call(
        _expert_kernel,
        grid_spec=grid_spec,
        out_shape=jax.ShapeDtypeStruct(h_rows.shape, F32),
        compiler_params=_params(("arbitrary",)),
        name="moe_experts",
    )(block_ea, block_eb, src_off, n_valid, n_used, sorted_tok, h_rows, w_gu, w_down, w_gu, w_down, wr_bcast, wr_bcast)


def _residual_kernel(y_ref, x_ref, g2_ref, o_ref):
    o_ref[...] = _gated_residual(x_ref, y_ref, g2_ref)


def _residual(y_rows, x_mid, g2, seq):
    n, d = x_mid.shape
    tc = _tile(512, seq)
    per_b = seq // tc
    return pl.pallas_call(
        _residual_kernel,
        grid=(n // tc,),
        in_specs=[
            pl.BlockSpec((tc * ROW_TILES, LANES), lambda i: (i, 0)),
            pl.BlockSpec((tc, d), lambda i: (i, 0)),
            pl.BlockSpec((1, 1, d), lambda i: (i // per_b, 0, 0)),
        ],
        out_specs=pl.BlockSpec((tc, d), lambda i: (i, 0)),
        out_shape=jax.ShapeDtypeStruct((n, d), F32),
        compiler_params=_params(("parallel",)),
        name="moe_residual",
    )(y_rows, x_mid, g2)


def _moe(h_rows, logits_t, router_bias, layer, w_gu, w_down, wr_bcast):
    n = logits_t.shape[1]
    n_blocks = (n + N_CLASSES * (MOE_ROWS - 1) + MOE_ROWS - 1) // MOE_ROWS
    codes, counts = _route(logits_t, router_bias)
    sorted_tok, *tables = _plan(codes.reshape(n), counts.reshape(CLASS_ROWS), n_blocks)
    return _experts(tables, sorted_tok, h_rows, layer, w_gu, w_down, wr_bcast, MOE_ROWS)


def kernel(x, c, positions, norm1, norm2, w_ada, b_ada, a_w_in, a_b_if, a_h_norm, a_w_out, b_w_in, b_q_norm, b_k_norm, b_lam_q1, b_lam_k1, b_lam_q2, b_lam_k2, b_o_norm, b_w_out, w_router, router_bias, moe_w_gu, moe_w_down):
    bsz, seq, d = x.shape
    depth = w_ada.shape[0]
    n = bsz * seq
    xf = x.reshape(n, d)
    mod = _ada_mod(c, w_ada, b_ada)
    w_router_t = w_router.T
    wr_bcast = jnp.broadcast_to(w_router_t[:, :, None], (N_EXPERTS, d, LANES)).astype(BF16)
    pos_row = positions.reshape(n)
    w_gu = moe_w_gu.astype(BF16)
    w_down = moe_w_down.astype(BF16)

    residual = None
    for l in range(depth):
        sh1, sc1, g1, sh2, sc2, g2 = [mod[l, :, i * d:(i + 1) * d].reshape(bsz, 1, d) for i in range(6)]
        j = l // 2
        if l % 2 == 0:
            w_in = a_w_in[j]
            w_main = w_in[:, :A_MAIN_COLS].astype(BF16)
            w_gate = jnp.pad(w_in[:, A_MAIN_COLS:], ((0, 0), (0, LANES - 2 * A_HEADS))).astype(BF16)
        else:
            w_main, w_gate = b_w_in[j].astype(BF16), None
        outs = _norm_matmul(xf, residual, norm1[l].reshape(1, d), 1.0 + sc1, sh1, w_main, w_gate, seq)
        if residual is not None:
            xf, *outs = outs
        if l % 2 == 0:
            proj, gates = outs
            bias_row = jnp.pad(a_b_if[j], (0, LANES - 2 * A_HEADS)).reshape(1, LANES)
            mixed = _mlstm(proj, gates, bias_row, a_h_norm[j], bsz, seq)
            w_out = a_w_out[j].astype(BF16)
        else:
            (proj,) = outs
            qk = _qk_prep(proj, pos_row, b_q_norm[j], b_k_norm[j], seq)
            lam_tab = jnp.zeros((SUBLANES, LANES), F32)
            for r, v in enumerate((b_lam_q1[j], b_lam_k1[j], b_lam_q2[j], b_lam_k2[j])):
                lam_tab = lam_tab.at[r, :B_HEAD_DIM].set(v)
            lam_init = 0.8 - 0.6 * math.exp(-0.3 * l)
            mixed = _diff_attn(qk, proj, lam_tab, b_o_norm[j], bsz, seq, lam_init)
            w_out = b_w_out[j].astype(BF16)
        xf, h_rows, logits_t = _outproj_norm_router(
            mixed, w_out, xf, g1, norm2[l].reshape(1, d), 1.0 + sc2, sh2, w_router_t, seq)
        residual = (_moe(h_rows, logits_t, router_bias, l, w_gu, w_down, wr_bcast), g2)
    return _residual(residual[0], xf, residual[1], seq).reshape(bsz, seq, d)
```

```python
import functools
import math

import jax
import jax.numpy as jnp
from jax import lax
from jax.experimental import pallas as pl
from jax.experimental.pallas import tpu as pltpu

D_MODEL = 1024
A_HEADS = 4
A_QK_DIM = 128
A_V_DIM = 256
A_CHUNK = 128
A_HQ = A_HEADS * A_QK_DIM
A_HV = A_HEADS * A_V_DIM
A_MAIN_COLS = 2 * A_HQ + 2 * A_HV

B_HEADS = 8
B_HEAD_DIM = 64
B_V_DIM = 128
B_ROT_DIM = 16
ROPE_THETA = 500000.0

N_EXPERTS = 16
N_GROUPS = 4
EXPERTS_PER_GROUP = 4
TOP_K = 2
D_EXPERT = 512
EPS = 1e-6

LANES = 128
SUBLANES = 8
ROW_TILES = D_MODEL // LANES
assert ROW_TILES == SUBLANES
VMEM_LIMIT = 48 * 1024 * 1024

F32 = jnp.float32
BF16 = jnp.bfloat16
HIGHEST = lax.Precision.HIGHEST


def _params(sem):
    return pltpu.CompilerParams(dimension_semantics=sem, vmem_limit_bytes=VMEM_LIMIT)


def _tile(pref, n):
    t = min(pref, n)
    assert n % t == 0, (pref, n)
    return t


def _ada_kernel(c_ref, w_ref, b_ref, o_ref):
    c = c_ref[...]
    c_act = c * jax.nn.sigmoid(c)
    o_ref[0] = jnp.dot(c_act, w_ref[0], precision=HIGHEST, preferred_element_type=F32) + b_ref[0]


def _ada_mod(c, w_ada, b_ada):
    depth, d, six_d = w_ada.shape
    bsz = c.shape[0]
    tn = _tile(1536, six_d)
    return pl.pallas_call(
        _ada_kernel,
        grid=(depth, six_d // tn),
        in_specs=[
            pl.BlockSpec((bsz, d), lambda l, j: (0, 0)),
            pl.BlockSpec((1, d, tn), lambda l, j: (l, 0, j)),
            pl.BlockSpec((1, 1, tn), lambda l, j: (l, 0, j)),
        ],
        out_specs=pl.BlockSpec((1, bsz, tn), lambda l, j: (l, 0, j)),
        out_shape=jax.ShapeDtypeStruct((depth, bsz, six_d), F32),
        compiler_params=_params(("parallel", "parallel")),
        name="ada_mod",
    )(c, w_ada, b_ada.reshape(depth, 1, six_d))


def _modulated_rms(x, g, sc1p, sh):
    y = x * lax.rsqrt(jnp.mean(x * x, axis=-1, keepdims=True) + EPS)
    return (y * g) * sc1p + sh


def _gated_residual(x_ref, y_ref, g_ref):
    rows = x_ref.shape[0]
    g = g_ref[0]
    return jnp.concatenate(
        [x_ref[:, j * LANES:(j + 1) * LANES]
         + g[:, j * LANES:(j + 1) * LANES] * y_ref[pl.ds(j, rows, stride=ROW_TILES), :] for j in range(ROW_TILES)],
        axis=1)


def _norm_mm_kernel(*refs, col_chunk, has_gates, has_residual):
    refs = list(refs)
    x_ref = refs.pop(0)
    if has_residual:
        y_ref, g2_ref = refs.pop(0), refs.pop(0)
    g_ref, sc_ref, sh_ref, w_ref = refs[:4]
    refs = refs[4:]
    wg_ref = refs.pop(0) if has_gates else None
    xo_ref = refs.pop(0) if has_residual else None
    o_ref = refs.pop(0)
    og_ref = refs.pop(0) if has_gates else None

    if has_residual:
        x = _gated_residual(x_ref, y_ref, g2_ref)
        xo_ref[...] = x
    else:
        x = x_ref[...]
    hb = _modulated_rms(x, g_ref[...], sc_ref[0], sh_ref[0]).astype(BF16)
    for c0 in range(0, o_ref.shape[1], col_chunk):
        o_ref[:, c0:c0 + col_chunk] = jnp.dot(
            hb, w_ref[:, c0:c0 + col_chunk], preferred_element_type=F32).astype(o_ref.dtype)
    if has_gates:
        og_ref[...] = jnp.dot(hb, wg_ref[...], preferred_element_type=F32)


def _norm_matmul(x, residual, gain, sc1p, sh, w, wg, seq):
    n, d = x.shape
    cols = w.shape[1]
    tm = _tile(512, seq)
    per_b = seq // tm
    has_gates = wg is not None
    has_residual = residual is not None
    bmap = lambda i: (i // per_b, 0, 0)
    in_specs = [pl.BlockSpec((tm, d), lambda i: (i, 0))]
    args = [x]
    if has_residual:
        in_specs += [pl.BlockSpec((tm * ROW_TILES, LANES), lambda i: (i, 0)), pl.BlockSpec((1, 1, d), bmap)]
        args += list(residual)
    in_specs += [
        pl.BlockSpec((1, d), lambda i: (0, 0)),
        pl.BlockSpec((1, 1, d), bmap),
        pl.BlockSpec((1, 1, d), bmap),
        pl.BlockSpec((d, cols), lambda i: (0, 0)),
    ]
    args += [gain, sc1p, sh, w]
    out_specs, out_shape = [], []
    if has_residual:
        out_specs.append(pl.BlockSpec((tm, d), lambda i: (i, 0)))
        out_shape.append(jax.ShapeDtypeStruct((n, d), F32))
    out_specs.append(pl.BlockSpec((tm, cols), lambda i: (i, 0)))
    out_shape.append(jax.ShapeDtypeStruct((n, cols), BF16))
    if has_gates:
        in_specs.append(pl.BlockSpec((d, LANES), lambda i: (0, 0)))
        out_specs.append(pl.BlockSpec((tm, LANES), lambda i: (i, 0)))
        out_shape.append(jax.ShapeDtypeStruct((n, LANES), F32))
        args.append(wg)
    return pl.pallas_call(
        functools.partial(_norm_mm_kernel, col_chunk=512, has_gates=has_gates, has_residual=has_residual),
        grid=(n // tm,),
        in_specs=in_specs,
        out_specs=out_specs,
        out_shape=out_shape,
        compiler_params=_params(("parallel",)),
        name="norm_inproj",
    )(*args)


def _log_sigmoid(x):
    return jnp.minimum(x, 0.0) - jnp.log1p(jnp.exp(-jnp.abs(x)))


def _mlstm_kernel(p_ref, g_ref, bias_ref, hn_ref, o_ref, ct_ref, n_ref, m_ref):
    L = A_CHUNK

    @pl.when(pl.program_id(1) == 0)
    def _():
        ct_ref[...] = jnp.zeros_like(ct_ref)
        n_ref[...] = jnp.zeros_like(n_ref)
        m_ref[...] = jnp.zeros_like(m_ref)

    gates = g_ref[...] + bias_ref[...]
    src = lax.broadcasted_iota(jnp.int32, (L, L), 0)
    tgt = lax.broadcasted_iota(jnp.int32, (L, L), 1)
    causal = src <= tgt
    bcum = jnp.dot(jnp.where(src >= tgt, 1.0, 0.0), _log_sigmoid(gates), precision=HIGHEST,
                   preferred_element_type=F32)
    bcum_t = bcum.T
    nt = (((1,), (1,)), ((), ()))

    pre = []
    for h in range(A_HEADS):
        q = p_ref[:, h * A_QK_DIM:(h + 1) * A_QK_DIM]
        ksf = p_ref[:, A_HQ + h * A_QK_DIM:A_HQ + (h + 1) * A_QK_DIM].astype(F32) * (A_QK_DIM ** -0.5)
        v = p_ref[:, 2 * A_HQ + h * A_V_DIM:2 * A_HQ + (h + 1) * A_V_DIM]
        ct_old = ct_ref[h]
        n_old = n_ref[h]
        qk = lax.dot_general(ksf.astype(BF16), q, nt, preferred_element_type=F32)
        qc = lax.dot_general(ct_old.astype(BF16), q, nt, preferred_element_type=F32)
        qn = lax.dot_general(n_old.astype(BF16), q, nt, preferred_element_type=F32)[0:1]
        vt = v.astype(F32).T.astype(BF16)
        pre.append((ksf, vt, ct_old, n_old, qk, qc, qn))

    for h in range(A_HEADS):
        ksf, vt, ct_old, n_old, qk, qc, qn = pre[h]
        og = p_ref[:, 2 * A_HQ + A_HV + h * A_V_DIM:2 * A_HQ + A_HV + (h + 1) * A_V_DIM].astype(F32)
        b_row = bcum_t[A_HEADS + h:A_HEADS + h + 1, :]
        b_last = b_row[:, L - 1:L]
        c_col = gates[:, h:h + 1] - bcum[:, A_HEADS + h:A_HEADS + h + 1]
        m11 = m_ref[h][:, 0:1]

        log_d = jnp.where(causal, c_col + b_row, -jnp.inf)
        log_inter = b_row + m11
        m_t = jnp.maximum(jnp.max(log_d, axis=0, keepdims=True), log_inter)
        dmat = jnp.exp(log_d - m_t)
        inter = jnp.exp(log_inter - m_t)
        s = qk * dmat
        num = jnp.dot(vt, s.astype(BF16), preferred_element_type=F32) + inter * qc
        den = jnp.sum(s, axis=0, keepdims=True) + inter * qn
        hh = num / jnp.maximum(jnp.abs(den), jnp.exp(-m_t))

        lw_col = b_last + c_col
        m_new = jnp.maximum(b_last + m11, jnp.max(lw_col, axis=0, keepdims=True))
        kw = ksf * jnp.exp(lw_col - m_new)
        decay = jnp.exp(b_last + m11 - m_new)
        ct_ref[h] = decay * ct_old + jnp.dot(vt, kw.astype(BF16), preferred_element_type=F32)
        n_ref[h] = decay * n_old + jnp.broadcast_to(jnp.sum(kw, axis=0, keepdims=True), n_old.shape)
        m_ref[h] = jnp.broadcast_to(m_new, (1, LANES))

        hn = hh * lax.rsqrt(jnp.mean(hh * hh, axis=0, keepdims=True) + EPS)
        hn = (hn * hn_ref[h * A_V_DIM:(h + 1) * A_V_DIM, :]).T
        o_ref[:, h * A_V_DIM:(h + 1) * A_V_DIM] = (hn * jax.nn.sigmoid(og)).astype(o_ref.dtype)


def _mlstm(proj, gates, bias_row, h_norm, bsz, seq):
    n = proj.shape[0]
    nc = seq // A_CHUNK
    h_norm_cols = jnp.broadcast_to(h_norm[:, None], (A_HV, A_CHUNK))
    return pl.pallas_call(
        _mlstm_kernel,
        grid=(bsz, nc),
        in_specs=[
            pl.BlockSpec((A_CHUNK, A_MAIN_COLS), lambda b, c: (b * nc + c, 0)),
            pl.BlockSpec((A_CHUNK, LANES), lambda b, c: (b * nc + c, 0)),
            pl.BlockSpec((1, LANES), lambda b, c: (0, 0)),
            pl.BlockSpec((A_HV, A_CHUNK), lambda b, c: (0, 0)),
        ],
        out_specs=pl.BlockSpec((A_CHUNK, A_HV), lambda b, c: (b * nc + c, 0)),
        out_shape=jax.ShapeDtypeStruct((n, A_HV), BF16),
        scratch_shapes=[
            pltpu.VMEM((A_HEADS, A_V_DIM, A_QK_DIM), F32),
            pltpu.VMEM((A_HEADS, SUBLANES, A_QK_DIM), F32),
            pltpu.VMEM((A_HEADS, 1, LANES), F32),
        ],
        compiler_params=_params(("parallel", "arbitrary")),
        name="mlstm",
    )(proj, gates, bias_row, h_norm_cols)


def _qk_prep_kernel(p_ref, pos_ref, seg_ref, segt_ref, gain_ref, freq_ref, o_ref):
    tm = p_ref.shape[0]
    x = p_ref[...].astype(F32)
    xx = x * x
    xx_hi = xx.astype(BF16)
    xx_lo = (xx - xx_hi.astype(F32)).astype(BF16)
    seg = seg_ref[...]
    ss = jnp.dot(xx_hi, seg, preferred_element_type=F32) + jnp.dot(xx_lo, seg, preferred_element_type=F32)
    r = lax.rsqrt(ss * (1.0 / B_HEAD_DIM) + EPS)
    r_hi = r.astype(BF16)
    r_lo = (r - r_hi.astype(F32)).astype(BF16)
    segt = segt_ref[...]
    r_full = jnp.dot(r_hi, segt, preferred_element_type=F32) + jnp.dot(r_lo, segt, preferred_element_type=F32)

    half = B_ROT_DIM // 2
    assert half == SUBLANES
    ang = freq_ref[:, 0:1] * pos_ref[0].astype(F32)
    cos8 = jnp.cos(ang)
    sin8 = jnp.sin(ang)
    one8 = jnp.ones_like(cos8)
    zero8 = jnp.zeros_like(cos8)
    per_seg = B_HEAD_DIM // SUBLANES
    n_seg = LANES // B_HEAD_DIM

    def chunk_pattern(first, second, rest):
        groups = ([first, second] + [rest] * (per_seg - 2)) * n_seg
        return jnp.concatenate(groups, axis=0).T

    c_mul = chunk_pattern(cos8, cos8, one8)
    s_lo = chunk_pattern(-sin8, zero8, zero8)
    s_hi = chunk_pattern(zero8, sin8, zero8)
    for j in range(2 * D_MODEL // LANES):
        sl = slice(j * LANES, (j + 1) * LANES)
        y = x[:, sl] * r_full[:, sl] * gain_ref[:, sl]
        out = y * c_mul + pltpu.roll(y, LANES - half, axis=1) * s_lo + pltpu.roll(y, half, axis=1) * s_hi
        if j < D_MODEL // LANES:
            out = out * (B_HEAD_DIM ** -0.5 * math.log2(math.e))
        o_ref[:, sl] = out.astype(o_ref.dtype)


def _qk_prep(proj, positions_row, q_norm, k_norm, seq):
    n = proj.shape[0]
    two_d = 2 * D_MODEL
    tm = _tile(256, seq)
    seg = (jnp.arange(two_d)[:, None] // B_HEAD_DIM == jnp.arange(LANES)[None, :]).astype(BF16)
    gain = jnp.concatenate([jnp.tile(q_norm, D_MODEL // B_HEAD_DIM), jnp.tile(k_norm, D_MODEL // B_HEAD_DIM)])
    inv_freq = ROPE_THETA ** (-jnp.arange(0, B_ROT_DIM, 2, dtype=F32) / B_ROT_DIM)
    freq_tab = jnp.broadcast_to(inv_freq[:, None], (B_ROT_DIM // 2, LANES))
    return pl.pallas_call(
        _qk_prep_kernel,
        grid=(n // tm,),
        in_specs=[
            pl.BlockSpec((tm, two_d), lambda i: (i, 0)),
            pl.BlockSpec((1, 1, tm), lambda i: (i, 0, 0)),
            pl.BlockSpec((two_d, LANES), lambda i: (0, 0)),
            pl.BlockSpec((LANES, two_d), lambda i: (0, 0)),
            pl.BlockSpec((1, two_d), lambda i: (0, 0)),
            pl.BlockSpec((SUBLANES, LANES), lambda i: (0, 0)),
        ],
        out_specs=pl.BlockSpec((tm, two_d), lambda i: (i, 0)),
        out_shape=jax.ShapeDtypeStruct((n, two_d), BF16),
        compiler_params=_params(("parallel",)),
        name="qk_prep",
    )(proj, positions_row.reshape(n // tm, 1, tm), seg, seg.T, gain.reshape(1, two_d), freq_tab)


ATTN_TILE = 512


def _diff_attn_kernel(q_ref, k_ref, v_ref, lam_ref, on_ref, o_ref, vt_ref, qm_ref, m_ref, l_ref, acc_ref,
                      *, tile, lam_init):
    seq = q_ref.shape[0]
    n_tiles = seq // tile
    lamv = lam_ref[...]
    lam = (jnp.exp(jnp.sum(lamv[0:1] * lamv[1:2], axis=1, keepdims=True))
           - jnp.exp(jnp.sum(lamv[2:3] * lamv[3:4], axis=1, keepdims=True)) + lam_init)
    tchunk = min(256, seq)
    for c in range(seq // tchunk):
        vt_ref[:, c * tchunk:(c + 1) * tchunk] = v_ref[c * tchunk:(c + 1) * tchunk, :].astype(F32).T.astype(BF16)
    first_map = lax.broadcasted_iota(jnp.int32, (tile, LANES), 1) < B_HEAD_DIM
    key_id = lax.broadcasted_iota(jnp.int32, (tile, tile), 0)
    query_id = lax.broadcasted_iota(jnp.int32, (tile, tile), 1)
    causal = query_id >= key_id
    nt = (((1,), (1,)), ((), ()))

    def qk_score(qi, ki, c):
        return lax.dot_general(k_ref[ki * tile:(ki + 1) * tile, :], qm_ref[2 * qi + c], nt,
                               preferred_element_type=F32)

    def softmax_pv(c, s, ki, mask):
        vt = vt_ref[:, ki * tile:(ki + 1) * tile]
        if mask is not None:
            s = jnp.where(mask, s, -jnp.inf)
        m_old = m_ref[c]
        m_new = jnp.maximum(m_old, jnp.max(s, axis=0, keepdims=True))
        alpha = jnp.exp2(m_old - m_new)
        p = jnp.exp2(s - m_new)
        l_ref[c] = alpha * l_ref[c] + jnp.sum(p, axis=0, keepdims=True)
        acc_ref[c] = alpha * acc_ref[c] + jnp.dot(vt, p.astype(BF16), preferred_element_type=F32)
        m_ref[c] = m_new

    def reset():
        m_ref[...] = jnp.full(m_ref.shape, -jnp.inf, F32)
        l_ref[...] = jnp.zeros(l_ref.shape, F32)
        acc_ref[...] = jnp.zeros(acc_ref.shape, F32)

    for qi in range(n_tiles):
        q = q_ref[qi * tile:(qi + 1) * tile, :]
        qm_ref[2 * qi] = jnp.where(first_map, q, jnp.zeros_like(q))
        qm_ref[2 * qi + 1] = jnp.where(first_map, jnp.zeros_like(q), q)
    steps = [(qi, ki) for qi in range(n_tiles) for ki in range(qi + 1)]
    reset()
    scores_next = [qk_score(*steps[0], c) for c in range(2)]
    for n, (qi, ki) in enumerate(steps):
        scores = scores_next
        scores_next = [None, None]
        for c in range(2):
            if n + 1 < len(steps):
                scores_next[c] = qk_score(*steps[n + 1], c)
            softmax_pv(c, scores[c], ki, causal if ki == qi else None)
        if ki == qi:
            o = acc_ref[0] / l_ref[0] - lam * (acc_ref[1] / l_ref[1])
            o = o * lax.rsqrt(jnp.mean(o * o, axis=0, keepdims=True) + EPS) * on_ref[...] * (1.0 - lam_init)
            o_ref[qi * tile:(qi + 1) * tile, :] = o.T.astype(o_ref.dtype)
            if n + 1 < len(steps):
                reset()


def _diff_attn(qk, proj, lam_tab, o_norm, bsz, seq, lam_init):
    n = qk.shape[0]
    tile = _tile(ATTN_TILE, seq)
    o_norm_cols = jnp.broadcast_to(o_norm[:, None], (B_V_DIM, tile))
    return pl.pallas_call(
        functools.partial(_diff_attn_kernel, tile=tile, lam_init=lam_init),
        grid=(bsz, B_HEADS),
        in_specs=[
            pl.BlockSpec((seq, LANES), lambda b, h: (b, h)),
            pl.BlockSpec((seq, LANES), lambda b, h: (b, B_HEADS + h)),
            pl.BlockSpec((seq, B_V_DIM), lambda b, h: (b, 2 * B_HEADS + h)),
            pl.BlockSpec((SUBLANES, LANES), lambda b, h: (0, 0)),
            pl.BlockSpec((B_V_DIM, tile), lambda b, h: (0, 0)),
        ],
        out_specs=pl.BlockSpec((seq, B_V_DIM), lambda b, h: (b, h)),
        out_shape=jax.ShapeDtypeStruct((n, D_MODEL), BF16),
        scratch_shapes=[
            pltpu.VMEM((B_V_DIM, seq), BF16),
            pltpu.VMEM((2 * (seq // tile), tile, LANES), BF16),
            pltpu.VMEM((2, 1, tile), F32),
            pltpu.VMEM((2, 1, tile), F32),
            pltpu.VMEM((2, B_V_DIM, tile), F32),
        ],
        compiler_params=_params(("parallel", "parallel")),
        name="diff_attn",
    )(qk, qk, proj, lam_tab, o_norm_cols)


HALF_D = D_MODEL // 2
HI16 = 0xFFFF0000
PACKED_TILES = HALF_D // LANES


def _pack_bf16_pairs(v):
    bits = lax.bitcast_convert_type(v.astype(BF16).astype(F32), jnp.uint32)
    return (bits[:, HALF_D:] & jnp.uint32(HI16)) | (bits[:, :HALF_D] >> 16)


def _unpack_bf16_pairs(words):
    first = lax.bitcast_convert_type(words << 16, F32)
    second = lax.bitcast_convert_type(words & jnp.uint32(HI16), F32)
    return first.astype(BF16), second.astype(BF16)


def _outproj_kernel(a_ref, w_ref, x_ref, g1_ref, gain_ref, sc_ref, sh_ref, wr_ref, xo_ref, hp_ref, lg_ref):
    y = jnp.dot(a_ref[...], w_ref[...], preferred_element_type=F32)
    xm = x_ref[...] + g1_ref[0] * y
    xo_ref[...] = xm
    h2 = _modulated_rms(xm, gain_ref[...], sc_ref[0], sh_ref[0])
    tm = xm.shape[0]
    packed = _pack_bf16_pairs(h2)
    for j in range(ROW_TILES):
        hp_ref[pl.ds(j, tm, stride=ROW_TILES), :] = (
            packed[:, j * LANES:(j + 1) * LANES] if j < PACKED_TILES else jnp.zeros((tm, LANES), jnp.uint32))
    lg_ref[...] = lax.dot_general(wr_ref[...], h2, (((1,), (1,)), ((), ())),
                                  precision=HIGHEST, preferred_element_type=F32)


def _outproj_norm_router(a, w, x, g1, gain, sc1p, sh, w_router_t, seq):
    n, d = x.shape
    tm = _tile(512, seq)
    per_b = seq // tm
    bmap = lambda i: (i // per_b, 0, 0)
    return pl.pallas_call(
        _outproj_kernel,
        grid=(n // tm,),
        in_specs=[
            pl.BlockSpec((tm, a.shape[1]), lambda i: (i, 0)),
            pl.BlockSpec(w.shape, lambda i: (0, 0)),
            pl.BlockSpec((tm, d), lambda i: (i, 0)),
            pl.BlockSpec((1, 1, d), bmap),
            pl.BlockSpec((1, d), lambda i: (0, 0)),
            pl.BlockSpec((1, 1, d), bmap),
            pl.BlockSpec((1, 1, d), bmap),
            pl.BlockSpec((N_EXPERTS, d), lambda i: (0, 0)),
        ],
        out_specs=[
            pl.BlockSpec((tm, d), lambda i: (i, 0)),
            pl.BlockSpec((tm * ROW_TILES, LANES), lambda i: (i, 0)),
            pl.BlockSpec((N_EXPERTS, tm), lambda i: (0, i)),
        ],
        out_shape=[
            jax.ShapeDtypeStruct((n, d), F32),
            jax.ShapeDtypeStruct((n * ROW_TILES, LANES), jnp.uint32),
            jax.ShapeDtypeStruct((N_EXPERTS, n), F32),
        ],
        compiler_params=_params(("parallel",)),
        name="outproj_norm_router",
    )(a, w, x, g1, gain, sc1p, sh, w_router_t)


PAIR_LO = (0, 0, 0, 1, 1, 2)
PAIR_HI = (1, 2, 3, 2, 3, 3)
PAIRS_PER_GROUP = len(PAIR_LO)
N_CLASSES = N_GROUPS * PAIRS_PER_GROUP
CLASS_ROWS = 32
RANK_BITS = 20
RANK_SPAN = 1 << RANK_BITS


def _route_kernel(lg_ref, bias_ref, oi_ref, cnt_ref, carry_ref, tri_ref):
    tr = lg_ref.shape[1]
    step = pl.program_id(0)

    @pl.when(step == 0)
    def _():
        carry_ref[...] = jnp.zeros_like(carry_ref)
        r = lax.broadcasted_iota(jnp.int32, (tr, tr), 0)
        c = lax.broadcasted_iota(jnp.int32, (tr, tr), 1)
        tri_ref[...] = jnp.where(r < c, 1.0, 0.0).astype(BF16)

    scores = jax.nn.sigmoid(lg_ref[...])
    biased = scores + bias_ref[...]
    rows = [biased[e:e + 1, :] for e in range(N_EXPERTS)]

    def top2_sum(a, b, c, d):
        m1, n1 = jnp.maximum(a, b), jnp.minimum(a, b)
        m2, n2 = jnp.maximum(c, d), jnp.minimum(c, d)
        return jnp.maximum(m1, m2) + jnp.maximum(jnp.minimum(m1, m2), jnp.maximum(n1, n2))

    gscore = [top2_sum(*rows[g * EXPERTS_PER_GROUP:(g + 1) * EXPERTS_PER_GROUP]) for g in range(N_GROUPS)]
    best = gscore[0]
    gsel = jnp.zeros_like(best, dtype=jnp.int32)
    for g in range(1, N_GROUPS):
        upd = gscore[g] > best
        gsel = jnp.where(upd, g, gsel)
        best = jnp.where(upd, gscore[g], best)

    def pick(table, j):
        out = table[j]
        for g in range(1, N_GROUPS):
            out = jnp.where(gsel == g, table[g * EXPERTS_PER_GROUP + j], out)
        return out

    in_b = [pick(rows, j) for j in range(EXPERTS_PER_GROUP)]
    v1, i1 = in_b[0], jnp.zeros_like(gsel)
    for j in range(1, EXPERTS_PER_GROUP):
        upd = in_b[j] > v1
        v1 = jnp.where(upd, in_b[j], v1)
        i1 = jnp.where(upd, j, i1)
    v2 = jnp.full_like(v1, -jnp.inf)
    i2 = jnp.zeros_like(gsel)
    for j in range(EXPERTS_PER_GROUP):
        upd = (i1 != j) & (in_b[j] > v2)
        v2 = jnp.where(upd, in_b[j], v2)
        i2 = jnp.where(upd, j, i2)
    first_is_lo = i1 < i2
    lo = jnp.where(first_is_lo, i1, i2)
    hi = jnp.where(first_is_lo, i2, i1)
    pair = jnp.where(lo == 0, hi - 1, jnp.where(lo == 1, hi + 1, PAIRS_PER_GROUP - 1))
    cls = gsel * PAIRS_PER_GROUP + pair

    cid = lax.broadcasted_iota(jnp.int32, (CLASS_ROWS, tr), 0)
    onehot = jnp.where(cid == cls, 1.0, 0.0)
    before = jnp.dot(onehot.astype(BF16), tri_ref[...], preferred_element_type=F32) + carry_ref[...]
    rank = jnp.sum(onehot * before, axis=0, keepdims=True).astype(jnp.int32)
    oi_ref[...] = cls * RANK_SPAN + rank
    new_carry = carry_ref[...] + jnp.sum(onehot, axis=1, keepdims=True)
    carry_ref[...] = new_carry
    cnt_ref[...] = new_carry.astype(jnp.int32)


def _route(logits_t, router_bias):
    n = logits_t.shape[1]
    tr = _tile(512, n)
    return pl.pallas_call(
        _route_kernel,
        grid=(n // tr,),
        in_specs=[
            pl.BlockSpec((N_EXPERTS, tr), lambda i: (0, i)),
            pl.BlockSpec((N_EXPERTS, 1), lambda i: (0, 0)),
        ],
        out_specs=[
            pl.BlockSpec((1, tr), lambda i: (0, i)),
            pl.BlockSpec((CLASS_ROWS, 1), lambda i: (0, 0)),
        ],
        out_shape=[
            jax.ShapeDtypeStruct((1, n), jnp.int32),
            jax.ShapeDtypeStruct((CLASS_ROWS, 1), jnp.int32),
        ],
        scratch_shapes=[pltpu.VMEM((CLASS_ROWS, 1), F32), pltpu.VMEM((tr, tr), BF16)],
        compiler_params=_params(("arbitrary",)),
        name="route",
    )(logits_t, router_bias.reshape(N_EXPERTS, 1).astype(F32))


SORT_UNROLL = 8
MOE_ROWS = 256
MOE_ROWS_LOG2 = MOE_ROWS.bit_length() - 1
assert 1 << MOE_ROWS_LOG2 == MOE_ROWS


def _plan_kernel(code_ref, cnt_ref, tok_ref, ea_ref, eb_ref, off_ref, nv_ref, nu_ref, start_ref):
    n_blocks = ea_ref.shape[0]
    run = jnp.int32(0)
    blk = jnp.int32(0)
    for c in range(N_CLASSES):
        cnt = cnt_ref[c]
        start_ref[c] = run - c * RANK_SPAN
        e_lo = (c // PAIRS_PER_GROUP) * EXPERTS_PER_GROUP + PAIR_LO[c % PAIRS_PER_GROUP]
        e_hi = (c // PAIRS_PER_GROUP) * EXPERTS_PER_GROUP + PAIR_HI[c % PAIRS_PER_GROUP]
        n_blk = (cnt + (MOE_ROWS - 1)) >> MOE_ROWS_LOG2

        def fill(b, carry, run=run, blk=blk, cnt=cnt, e_lo=e_lo, e_hi=e_hi):
            ea_ref[blk + b] = e_lo
            eb_ref[blk + b] = e_hi
            off_ref[blk + b] = run + b * MOE_ROWS
            nv_ref[blk + b] = jnp.minimum(cnt - b * MOE_ROWS, MOE_ROWS)
            return carry

        lax.fori_loop(0, n_blk, fill, 0)
        run = run + cnt
        blk = blk + n_blk
    nu_ref[0] = blk

    def fill_unused(b, carry):
        ea_ref[b] = ea_ref[blk - 1]
        eb_ref[b] = eb_ref[blk - 1]
        off_ref[b] = 0
        nv_ref[b] = 0
        return carry

    lax.fori_loop(blk, n_blocks, fill_unused, 0)

    def place(i, carry):
        toks = [i * SORT_UNROLL + u for u in range(SORT_UNROLL)]
        codes = [code_ref[t] for t in toks]
        slots = [start_ref[code >> RANK_BITS] + code for code in codes]
        for t, p in zip(toks, slots):
            tok_ref[p] = t
        return carry

    lax.fori_loop(0, code_ref.shape[0] // SORT_UNROLL, place, 0)


def _plan(codes, counts, n_blocks):
    n = codes.shape[0]
    assert n % SORT_UNROLL == 0 and n <= RANK_SPAN
    smem = pl.BlockSpec(memory_space=pltpu.SMEM)
    i32 = lambda size: jax.ShapeDtypeStruct((size,), jnp.int32)
    return pl.pallas_call(
        _plan_kernel,
        in_specs=[smem, smem],
        out_specs=[smem] * 6,
        out_shape=[i32(n), i32(n_blocks), i32(n_blocks), i32(n_blocks), i32(n_blocks), i32(1)],
        scratch_shapes=[pltpu.SMEM((N_CLASSES,), jnp.int32)],
        name="moe_plan",
    )(codes, counts)


ROW_UNROLL = 8


def _for_rows(n, fn):
    groups = n // ROW_UNROLL

    def group(g, c):
        for u in range(ROW_UNROLL):
            fn(g * ROW_UNROLL + u)
        return c

    def single(r, c):
        fn(r)
        return c

    lax.fori_loop(0, groups, group, 0)
    lax.fori_loop(groups * ROW_UNROLL, n, single, 0)


def _row_copy(src_ref, src_row, dst_ref, dst_row, sem):
    return pltpu.make_async_copy(
        src_ref.at[pl.ds(pl.multiple_of(src_row * ROW_TILES, ROW_TILES), ROW_TILES)],
        dst_ref.at[pl.ds(pl.multiple_of(dst_row * ROW_TILES, ROW_TILES), ROW_TILES)],
        sem)


def _expert_kernel(ea_ref, eb_ref, off_ref, nv_ref, nu_ref, tok_ref,
                   h_hbm, wgu_a_ref, wd_a_ref, wgu_b_ref, wd_b_ref, wr_a_ref, wr_b_ref, y_hbm,
                   xbuf, ybuf, gsem, ssem):
    del ea_ref, eb_ref
    tb = xbuf.shape[1] // ROW_TILES
    i = pl.program_id(0)
    slot = i % 2
    n_used = nu_ref[0]

    def start_gather(block, s):
        base = off_ref[block]
        _for_rows(nv_ref[block],
                  lambda r: _row_copy(h_hbm, tok_ref[base + r], xbuf.at[s], r, gsem.at[s]).start())

    def start_scatter(block, s):
        base = off_ref[block]
        _for_rows(nv_ref[block],
                  lambda r: _row_copy(ybuf.at[s], r, y_hbm, tok_ref[base + r], ssem.at[s]).start())

    def wait_rows(block, hbm, buf, sem):
        rows = nv_ref[block] * ROW_TILES

        @pl.when(rows > 0)
        def _():
            pltpu.make_async_copy(hbm.at[pl.ds(0, rows)], buf.at[pl.ds(0, rows)], sem).wait()

    @pl.when(i == 0)
    def _():
        xbuf[...] = jnp.zeros_like(xbuf)
        start_gather(0, 0)

    @pl.when(i < n_used)
    def _():
        @pl.when(i + 1 < n_used)
        def _():
            start_gather(i + 1, 1 - slot)

        wait_rows(i, h_hbm, xbuf.at[slot], gsem.at[slot])

        @pl.when(i >= 2)
        def _():
            wait_rows(i - 2, y_hbm, ybuf.at[slot], ssem.at[slot])

        words = [xbuf[slot, pl.ds(j, tb, stride=ROW_TILES), :] for j in range(PACKED_TILES)]
        halves = [_unpack_bf16_pairs(w) for w in words]
        x = jnp.concatenate([h[0] for h in halves] + [h[1] for h in halves], axis=1)

        def mlp(wgu_ref, wd_ref):
            gu = jnp.dot(x, wgu_ref[0], preferred_element_type=F32)
            gate = gu[:, :D_EXPERT]
            act = gate * jax.nn.sigmoid(gate) * gu[:, D_EXPERT:]
            return jnp.dot(act.astype(BF16), wd_ref[0], preferred_element_type=F32)

        score_a = jax.nn.sigmoid(jnp.dot(x, wr_a_ref[0], preferred_element_type=F32))
        score_b = jax.nn.sigmoid(jnp.dot(x, wr_b_ref[0], preferred_element_type=F32))
        inv_tot = 1.0 / (score_a + score_b)
        w_a = score_a * inv_tot
        w_b = score_b * inv_tot
        ya = mlp(wgu_a_ref, wd_a_ref)
        yb = mlp(wgu_b_ref, wd_b_ref)
        for j in range(ROW_TILES):
            sl = slice(j * LANES, (j + 1) * LANES)
            ybuf[slot, pl.ds(j, tb, stride=ROW_TILES), :] = w_a * ya[:, sl] + w_b * yb[:, sl]
        start_scatter(i, slot)

        @pl.when(i == n_used - 1)
        def _():
            wait_rows(i, y_hbm, ybuf.at[slot], ssem.at[slot])

            @pl.when(i >= 1)
            def _():
                wait_rows(i - 1, y_hbm, ybuf.at[1 - slot], ssem.at[1 - slot])


def _experts(tables, sorted_tok, h_rows, layer, w_gu, w_down, wr_bcast, tb):
    block_ea, block_eb, src_off, n_valid, n_used = tables
    n_blocks = block_ea.shape[0]
    d, two_f = w_gu.shape[2:]
    ea_map = lambda i, ea, eb, off, nv, nu, tok: (ea[i], 0, 0)
    eb_map = lambda i, ea, eb, off, nv, nu, tok: (eb[i], 0, 0)
    lea_map = lambda i, ea, eb, off, nv, nu, tok: (layer, ea[i], 0, 0)
    leb_map = lambda i, ea, eb, off, nv, nu, tok: (layer, eb[i], 0, 0)
    grid_spec = pltpu.PrefetchScalarGridSpec(
        num_scalar_prefetch=6,
        grid=(n_blocks,),
        in_specs=[
            pl.BlockSpec(memory_space=pl.ANY),
            pl.BlockSpec((None, 1, d, two_f), lea_map),
            pl.BlockSpec((None, 1, two_f // 2, d), lea_map),
            pl.BlockSpec((None, 1, d, two_f), leb_map),
            pl.BlockSpec((None, 1, two_f // 2, d), leb_map),
            pl.BlockSpec((1, d, LANES), ea_map),
            pl.BlockSpec((1, d, LANES), eb_map),
        ],
        out_specs=pl.BlockSpec(memory_space=pl.ANY),
        scratch_shapes=[
            pltpu.VMEM((2, tb * ROW_TILES, LANES), jnp.uint32),
            pltpu.VMEM((2, tb * ROW_TILES, LANES), F32),
            pltpu.SemaphoreType.DMA((2,)),
            pltpu.SemaphoreType.DMA((2,)),
        ],
    )
    return pl.pallas_call(
        _expert_kernel,
        grid_spec=grid_spec,
        out_shape=jax.ShapeDtypeStruct(h_rows.shape, F32),
        compiler_params=_params(("arbitrary",)),
        name="moe_experts",
    )(block_ea, block_eb, src_off, n_valid, n_used, sorted_tok, h_rows, w_gu, w_down, w_gu, w_down, wr_bcast, wr_bcast)


def _residual_kernel(y_ref, x_ref, g2_ref, o_ref):
    o_ref[...] = _gated_residual(x_ref, y_ref, g2_ref)


def _residual(y_rows, x_mid, g2, seq):
    n, d = x_mid.shape
    tc = _tile(512, seq)
    per_b = seq // tc
    return pl.pallas_call(
        _residual_kernel,
        grid=(n // tc,),
        in_specs=[
            pl.BlockSpec((tc * ROW_TILES, LANES), lambda i: (i, 0)),
            pl.BlockSpec((tc, d), lambda i: (i, 0)),
            pl.BlockSpec((1, 1, d), lambda i: (i // per_b, 0, 0)),
        ],
        out_specs=pl.BlockSpec((tc, d), lambda i: (i, 0)),
        out_shape=jax.ShapeDtypeStruct((n, d), F32),
        compiler_params=_params(("parallel",)),
        name="moe_residual",
    )(y_rows, x_mid, g2)


def _moe(h_rows, logits_t, router_bias, layer, w_gu, w_down, wr_bcast):
    n = logits_t.shape[1]
    n_blocks = (n + N_CLASSES * (MOE_ROWS - 1) + MOE_ROWS - 1) // MOE_ROWS
    codes, counts = _route(logits_t, router_bias)
    sorted_tok, *tables = _plan(codes.reshape(n), counts.reshape(CLASS_ROWS), n_blocks)
    return _experts(tables, sorted_tok, h_rows, layer, w_gu, w_down, wr_bcast, MOE_ROWS)


def kernel(x, c, positions, norm1, norm2, w_ada, b_ada, a_w_in, a_b_if, a_h_norm, a_w_out, b_w_in, b_q_norm, b_k_norm, b_lam_q1, b_lam_k1, b_lam_q2, b_lam_k2, b_o_norm, b_w_out, w_router, router_bias, moe_w_gu, moe_w_down):
    bsz, seq, d = x.shape
    depth = w_ada.shape[0]
    n = bsz * seq
    xf = x.reshape(n, d)
    mod = _ada_mod(c, w_ada, b_ada)
    w_router_t = w_router.T
    wr_bcast = jnp.broadcast_to(w_router_t[:, :, None], (N_EXPERTS, d, LANES)).astype(BF16)
    pos_row = positions.reshape(n)
    w_gu = moe_w_gu.astype(BF16)
    w_down = moe_w_down.astype(BF16)

    residual = None
    for l in range(depth):
        sh1, sc1, g1, sh2, sc2, g2 = [mod[l, :, i * d:(i + 1) * d].reshape(bsz, 1, d) for i in range(6)]
        j = l // 2
        if l % 2 == 0:
            w_in = a_w_in[j]
            w_main = w_in[:, :A_MAIN_COLS].astype(BF16)
            w_gate = jnp.pad(w_in[:, A_MAIN_COLS:], ((0, 0), (0, LANES - 2 * A_HEADS))).astype(BF16)
        else:
            w_main, w_gate = b_w_in[j].astype(BF16), None
        outs = _norm_matmul(xf, residual, norm1[l].reshape(1, d), 1.0 + sc1, sh1, w_main, w_gate, seq)
        if residual is not None:
            xf, *outs = outs
        if l % 2 == 0:
            proj, gates = outs
            bias_row = jnp.pad(a_b_if[j], (0, LANES - 2 * A_HEADS)).reshape(1, LANES)
            mixed = _mlstm(proj, gates, bias_row, a_h_norm[j], bsz, seq)
            w_out = a_w_out[j].astype(BF16)
        else:
            (proj,) = outs
            qk = _qk_prep(proj, pos_row, b_q_norm[j], b_k_norm[j], seq)
            lam_tab = jnp.zeros((SUBLANES, LANES), F32)
            for r, v in enumerate((b_lam_q1[j], b_lam_k1[j], b_lam_q2[j], b_lam_k2[j])):
                lam_tab = lam_tab.at[r, :B_HEAD_DIM].set(v)
            lam_init = 0.8 - 0.6 * math.exp(-0.3 * l)
            mixed = _diff_attn(qk, proj, lam_tab, b_o_norm[j], bsz, seq, lam_init)
            w_out = b_w_out[j].astype(BF16)
        xf, h_rows, logits_t = _outproj_norm_router(
            mixed, w_out, xf, g1, norm2[l].reshape(1, d), 1.0 + sc2, sh2, w_router_t, seq)
        residual = (_moe(h_rows, logits_t, router_bias, l, w_gu, w_down, wr_bcast), g2)
    return _residual(residual[0], xf, residual[1], seq).reshape(bsz, seq, d)
```

```python
import functools
import math

import jax
import jax.numpy as jnp
from jax import lax
from jax.experimental import pallas as pl
from jax.experimental.pallas import tpu as pltpu

D_MODEL = 1024
A_HEADS = 4
A_QK_DIM = 128
A_V_DIM = 256
A_CHUNK = 128
A_HQ = A_HEADS * A_QK_DIM
A_HV = A_HEADS * A_V_DIM
A_MAIN_COLS = 2 * A_HQ + 2 * A_HV

B_HEADS = 8
B_HEAD_DIM = 64
B_V_DIM = 128
B_ROT_DIM = 16
ROPE_THETA = 500000.0

N_EXPERTS = 16
N_GROUPS = 4
EXPERTS_PER_GROUP = 4
TOP_K = 2
D_EXPERT = 512
EPS = 1e-6

LANES = 128
SUBLANES = 8
ROW_TILES = D_MODEL // LANES
assert ROW_TILES == SUBLANES
VMEM_LIMIT = 48 * 1024 * 1024

F32 = jnp.float32
BF16 = jnp.bfloat16
HIGHEST = lax.Precision.HIGHEST


def _params(sem):
    return pltpu.CompilerParams(dimension_semantics=sem, vmem_limit_bytes=VMEM_LIMIT)


def _tile(pref, n):
    t = min(pref, n)
    assert n % t == 0, (pref, n)
    return t


def _ada_kernel(c_ref, w_ref, b_ref, o_ref):
    c = c_ref[...]
    c_act = c * jax.nn.sigmoid(c)
    o_ref[0] = jnp.dot(c_act, w_ref[0], precision=HIGHEST, preferred_element_type=F32) + b_ref[0]


def _ada_mod(c, w_ada, b_ada):
    depth, d, six_d = w_ada.shape
    bsz = c.shape[0]
    tn = _tile(1536, six_d)
    return pl.pallas_call(
        _ada_kernel,
        grid=(depth, six_d // tn),
        in_specs=[
            pl.BlockSpec((bsz, d), lambda l, j: (0, 0)),
            pl.BlockSpec((1, d, tn), lambda l, j: (l, 0, j)),
            pl.BlockSpec((1, 1, tn), lambda l, j: (l, 0, j)),
        ],
        out_specs=pl.BlockSpec((1, bsz, tn), lambda l, j: (l, 0, j)),
        out_shape=jax.ShapeDtypeStruct((depth, bsz, six_d), F32),
        compiler_params=_params(("parallel", "parallel")),
        name="ada_mod",
    )(c, w_ada, b_ada.reshape(depth, 1, six_d))


def _modulated_rms(x, g, sc1p, sh):
    y = x * lax.rsqrt(jnp.mean(x * x, axis=-1, keepdims=True) + EPS)
    return (y * g) * sc1p + sh


def _gated_residual(x_ref, y_ref, g_ref):
    rows = x_ref.shape[0]
    g = g_ref[0]
    return jnp.concatenate(
        [x_ref[:, j * LANES:(j + 1) * LANES]
         + g[:, j * LANES:(j + 1) * LANES] * y_ref[pl.ds(j, rows, stride=ROW_TILES), :] for j in range(ROW_TILES)],
        axis=1)


def _norm_mm_kernel(*refs, col_chunk, has_gates, has_residual):
    refs = list(refs)
    x_ref = refs.pop(0)
    if has_residual:
        y_ref, g2_ref = refs.pop(0), refs.pop(0)
    g_ref, sc_ref, sh_ref, w_ref = refs[:4]
    refs = refs[4:]
    wg_ref = refs.pop(0) if has_gates else None
    xo_ref = refs.pop(0) if has_residual else None
    o_ref = refs.pop(0)
    og_ref = refs.pop(0) if has_gates else None

    if has_residual:
        x = _gated_residual(x_ref, y_ref, g2_ref)
        xo_ref[...] = x
    else:
        x = x_ref[...]
    hb = _modulated_rms(x, g_ref[...], sc_ref[0], sh_ref[0]).astype(BF16)
    for c0 in range(0, o_ref.shape[1], col_chunk):
        o_ref[:, c0:c0 + col_chunk] = jnp.dot(
            hb, w_ref[:, c0:c0 + col_chunk], preferred_element_type=F32).astype(o_ref.dtype)
    if has_gates:
        og_ref[...] = jnp.dot(hb, wg_ref[...], preferred_element_type=F32)


def _norm_matmul(x, residual, gain, sc1p, sh, w, wg, seq):
    n, d = x.shape
    cols = w.shape[1]
    tm = _tile(512, seq)
    per_b = seq // tm
    has_gates = wg is not None
    has_residual = residual is not None
    bmap = lambda i: (i // per_b, 0, 0)
    in_specs = [pl.BlockSpec((tm, d), lambda i: (i, 0))]
    args = [x]
    if has_residual:
        in_specs += [pl.BlockSpec((tm * ROW_TILES, LANES), lambda i: (i, 0)), pl.BlockSpec((1, 1, d), bmap)]
        args += list(residual)
    in_specs += [
        pl.BlockSpec((1, d), lambda i: (0, 0)),
        pl.BlockSpec((1, 1, d), bmap),
        pl.BlockSpec((1, 1, d), bmap),
        pl.BlockSpec((d, cols), lambda i: (0, 0)),
    ]
    args += [gain, sc1p, sh, w]
    out_specs, out_shape = [], []
    if has_residual:
        out_specs.append(pl.BlockSpec((tm, d), lambda i: (i, 0)))
        out_shape.append(jax.ShapeDtypeStruct((n, d), F32))
    out_specs.append(pl.BlockSpec((tm, cols), lambda i: (i, 0)))
    out_shape.append(jax.ShapeDtypeStruct((n, cols), BF16))
    if has_gates:
        in_specs.append(pl.BlockSpec((d, LANES), lambda i: (0, 0)))
        out_specs.append(pl.BlockSpec((tm, LANES), lambda i: (i, 0)))
        out_shape.append(jax.ShapeDtypeStruct((n, LANES), F32))
        args.append(wg)
    return pl.pallas_call(
        functools.partial(_norm_mm_kernel, col_chunk=512, has_gates=has_gates, has_residual=has_residual),
        grid=(n // tm,),
        in_specs=in_specs,
        out_specs=out_specs,
        out_shape=out_shape,
        compiler_params=_params(("parallel",)),
        name="norm_inproj",
    )(*args)


def _log_sigmoid(x):
    return jnp.minimum(x, 0.0) - jnp.log1p(jnp.exp(-jnp.abs(x)))


def _mlstm_kernel(p_ref, g_ref, bias_ref, hn_ref, o_ref, ct_ref, n_ref, m_ref):
    L = A_CHUNK

    @pl.when(pl.program_id(1) == 0)
    def _():
        ct_ref[...] = jnp.zeros_like(ct_ref)
        n_ref[...] = jnp.zeros_like(n_ref)
        m_ref[...] = jnp.zeros_like(m_ref)

    gates = g_ref[...] + bias_ref[...]
    src = lax.broadcasted_iota(jnp.int32, (L, L), 0)
    tgt = lax.broadcasted_iota(jnp.int32, (L, L), 1)
    causal = src <= tgt
    bcum = jnp.dot(jnp.where(src >= tgt, 1.0, 0.0), _log_sigmoid(gates), precision=HIGHEST,
                   preferred_element_type=F32)
    bcum_t = bcum.T
    nt = (((1,), (1,)), ((), ()))

    pre = []
    for h in range(A_HEADS):
        q = p_ref[:, h * A_QK_DIM:(h + 1) * A_QK_DIM]
        ksf = p_ref[:, A_HQ + h * A_QK_DIM:A_HQ + (h + 1) * A_QK_DIM].astype(F32) * (A_QK_DIM ** -0.5)
        v = p_ref[:, 2 * A_HQ + h * A_V_DIM:2 * A_HQ + (h + 1) * A_V_DIM]
        ct_old = ct_ref[h]
        n_old = n_ref[h]
        qk = lax.dot_general(ksf.astype(BF16), q, nt, preferred_element_type=F32)
        qc = lax.dot_general(ct_old.astype(BF16), q, nt, preferred_element_type=F32)
        qn = lax.dot_general(n_old.astype(BF16), q, nt, preferred_element_type=F32)[0:1]
        vt = v.astype(F32).T.astype(BF16)
        pre.append((ksf, vt, ct_old, n_old, qk, qc, qn))

    for h in range(A_HEADS):
        ksf, vt, ct_old, n_old, qk, qc, qn = pre[h]
        og = p_ref[:, 2 * A_HQ + A_HV + h * A_V_DIM:2 * A_HQ + A_HV + (h + 1) * A_V_DIM].astype(F32)
        b_row = bcum_t[A_HEADS + h:A_HEADS + h + 1, :]
        b_last = b_row[:, L - 1:L]
        c_col = gates[:, h:h + 1] - bcum[:, A_HEADS + h:A_HEADS + h + 1]
        m11 = m_ref[h][:, 0:1]

        log_d = jnp.where(causal, c_col + b_row, -jnp.inf)
        log_inter = b_row + m11
        m_t = jnp.maximum(jnp.max(log_d, axis=0, keepdims=True), log_inter)
        dmat = jnp.exp(log_d - m_t)
        inter = jnp.exp(log_inter - m_t)
        s = qk * dmat
        num = jnp.dot(vt, s.astype(BF16), preferred_element_type=F32) + inter * qc
        den = jnp.sum(s, axis=0, keepdims=True) + inter * qn
        hh = num / jnp.maximum(jnp.abs(den), jnp.exp(-m_t))

        lw_col = b_last + c_col
        m_new = jnp.maximum(b_last + m11, jnp.max(lw_col, axis=0, keepdims=True))
        kw = ksf * jnp.exp(lw_col - m_new)
        decay = jnp.exp(b_last + m11 - m_new)
        ct_ref[h] = decay * ct_old + jnp.dot(vt, kw.astype(BF16), preferred_element_type=F32)
        n_ref[h] = decay * n_old + jnp.broadcast_to(jnp.sum(kw, axis=0, keepdims=True), n_old.shape)
        m_ref[h] = jnp.broadcast_to(m_new, (1, LANES))

        hn = hh * lax.rsqrt(jnp.mean(hh * hh, axis=0, keepdims=True) + EPS)
        hn = (hn * hn_ref[h * A_V_DIM:(h + 1) * A_V_DIM, :]).T
        o_ref[:, h * A_V_DIM:(h + 1) * A_V_DIM] = (hn * jax.nn.sigmoid(og)).astype(o_ref.dtype)


def _mlstm(proj, gates, bias_row, h_norm, bsz, seq):
    n = proj.shape[0]
    nc = seq // A_CHUNK
    h_norm_cols = jnp.broadcast_to(h_norm[:, None], (A_HV, A_CHUNK))
    return pl.pallas_call(
        _mlstm_kernel,
        grid=(bsz, nc),
        in_specs=[
            pl.BlockSpec((A_CHUNK, A_MAIN_COLS), lambda b, c: (b * nc + c, 0)),
            pl.BlockSpec((A_CHUNK, LANES), lambda b, c: (b * nc + c, 0)),
            pl.BlockSpec((1, LANES), lambda b, c: (0, 0)),
            pl.BlockSpec((A_HV, A_CHUNK), lambda b, c: (0, 0)),
        ],
        out_specs=pl.BlockSpec((A_CHUNK, A_HV), lambda b, c: (b * nc + c, 0)),
        out_shape=jax.ShapeDtypeStruct((n, A_HV), BF16),
        scratch_shapes=[
            pltpu.VMEM((A_HEADS, A_V_DIM, A_QK_DIM), F32),
            pltpu.VMEM((A_HEADS, SUBLANES, A_QK_DIM), F32),
            pltpu.VMEM((A_HEADS, 1, LANES), F32),
        ],
        compiler_params=_params(("parallel", "arbitrary")),
        name="mlstm",
    )(proj, gates, bias_row, h_norm_cols)


def _qk_prep_kernel(p_ref, pos_ref, seg_ref, segt_ref, gain_ref, freq_ref, o_ref):
    tm = p_ref.shape[0]
    x = p_ref[...].astype(F32)
    xx = x * x
    xx_hi = xx.astype(BF16)
    xx_lo = (xx - xx_hi.astype(F32)).astype(BF16)
    seg = seg_ref[...]
    ss = jnp.dot(xx_hi, seg, preferred_element_type=F32) + jnp.dot(xx_lo, seg, preferred_element_type=F32)
    r = lax.rsqrt(ss * (1.0 / B_HEAD_DIM) + EPS)
    r_hi = r.astype(BF16)
    r_lo = (r - r_hi.astype(F32)).astype(BF16)
    segt = segt_ref[...]
    r_full = jnp.dot(r_hi, segt, preferred_element_type=F32) + jnp.dot(r_lo, segt, preferred_element_type=F32)

    assert B_ROT_DIM // 2 == SUBLANES
    ang = freq_ref[:, 0:1] * pos_ref[0].astype(F32)
    cos8 = jnp.cos(ang)
    sin8 = jnp.sin(ang)
    one8 = jnp.ones_like(cos8)
    zero8 = jnp.zeros_like(cos8)
    groups_per_quarter = MAP_QUARTER // SUBLANES

    def chunk_pattern(lower, upper, rest):
        quarter = lambda lead: [lead] + [rest] * (groups_per_quarter - 1)
        return jnp.concatenate(quarter(lower) * 2 + quarter(upper) * 2, axis=0).T

    c_mul = chunk_pattern(cos8, cos8, one8)
    s_mul = chunk_pattern(-sin8, sin8, zero8)
    for j in range(2 * D_MODEL // LANES):
        sl = slice(j * LANES, (j + 1) * LANES)
        y = x[:, sl] * r_full[:, sl] * gain_ref[:, sl]
        out = y * c_mul + pltpu.roll(y, LANES // 2, axis=1) * s_mul
        if j < D_MODEL // LANES:
            out = out * (B_HEAD_DIM ** -0.5 * math.log2(math.e))
        o_ref[:, sl] = out.astype(o_ref.dtype)


MAP_QUARTER = B_HEAD_DIM // 2


def _interleave_maps(t):
    half = B_ROT_DIM // 2
    rest = MAP_QUARTER - half
    lead = t.shape[:-1]
    t = t.reshape(*lead, 2 * B_HEADS, 2, B_HEAD_DIM)
    lower = jnp.concatenate([t[..., 0:half], t[..., B_ROT_DIM:B_ROT_DIM + rest]], axis=-1)
    upper = jnp.concatenate([t[..., half:B_ROT_DIM], t[..., B_ROT_DIM + rest:]], axis=-1)
    out = jnp.stack([lower, upper], axis=-3)
    return out.reshape(*lead, 2 * D_MODEL)


def _qk_prep(proj, positions_row, q_norm, k_norm, seq):
    n = proj.shape[0]
    two_d = 2 * D_MODEL
    tm = _tile(256, seq)
    seg_of_col = _interleave_maps(jnp.arange(two_d) // B_HEAD_DIM)
    seg = (seg_of_col[:, None] == jnp.arange(LANES)[None, :]).astype(BF16)
    gain = _interleave_maps(
        jnp.concatenate([jnp.tile(q_norm, D_MODEL // B_HEAD_DIM), jnp.tile(k_norm, D_MODEL // B_HEAD_DIM)]))
    inv_freq = ROPE_THETA ** (-jnp.arange(0, B_ROT_DIM, 2, dtype=F32) / B_ROT_DIM)
    freq_tab = jnp.broadcast_to(inv_freq[:, None], (B_ROT_DIM // 2, LANES))
    return pl.pallas_call(
        _qk_prep_kernel,
        grid=(n // tm,),
        in_specs=[
            pl.BlockSpec((tm, two_d), lambda i: (i, 0)),
            pl.BlockSpec((1, 1, tm), lambda i: (i, 0, 0)),
            pl.BlockSpec((two_d, LANES), lambda i: (0, 0)),
            pl.BlockSpec((LANES, two_d), lambda i: (0, 0)),
            pl.BlockSpec((1, two_d), lambda i: (0, 0)),
            pl.BlockSpec((SUBLANES, LANES), lambda i: (0, 0)),
        ],
        out_specs=pl.BlockSpec((tm, two_d), lambda i: (i, 0)),
        out_shape=jax.ShapeDtypeStruct((n, two_d), BF16),
        compiler_params=_params(("parallel",)),
        name="qk_prep",
    )(proj, positions_row.reshape(n // tm, 1, tm), seg, seg.T, gain.reshape(1, two_d), freq_tab)


ATTN_TILE = 512


def _diff_attn_kernel(q_ref, k_ref, v_ref, lam_ref, on_ref, o_ref, vt_ref, qm_ref, m_ref, l_ref, acc_ref,
                      *, tile, lam_init):
    seq = q_ref.shape[0]
    n_tiles = seq // tile
    lamv = lam_ref[...]
    lam = (jnp.exp(jnp.sum(lamv[0:1] * lamv[1:2], axis=1, keepdims=True))
           - jnp.exp(jnp.sum(lamv[2:3] * lamv[3:4], axis=1, keepdims=True)) + lam_init)
    tchunk = min(256, seq)
    for c in range(seq // tchunk):
        vt_ref[:, c * tchunk:(c + 1) * tchunk] = v_ref[c * tchunk:(c + 1) * tchunk, :].astype(F32).T.astype(BF16)
    first_map = (lax.broadcasted_iota(jnp.int32, (tile, LANES), 1) // MAP_QUARTER) % 2 == 0
    key_id = lax.broadcasted_iota(jnp.int32, (tile, tile), 0)
    query_id = lax.broadcasted_iota(jnp.int32, (tile, tile), 1)
    causal = query_id >= key_id
    nt = (((1,), (1,)), ((), ()))

    def qk_score(qi, ki, c):
        return lax.dot_general(k_ref[ki * tile:(ki + 1) * tile, :], qm_ref[2 * qi + c], nt,
                               preferred_element_type=F32)

    def softmax_pv(c, s, ki, mask):
        vt = vt_ref[:, ki * tile:(ki + 1) * tile]
        if mask is not None:
            s = jnp.where(mask, s, -jnp.inf)
        m_old = m_ref[c]
        m_new = jnp.maximum(m_old, jnp.max(s, axis=0, keepdims=True))
        alpha = jnp.exp2(m_old - m_new)
        p = jnp.exp2(s - m_new)
        l_ref[c] = alpha * l_ref[c] + jnp.sum(p, axis=0, keepdims=True)
        acc_ref[c] = alpha * acc_ref[c] + jnp.dot(vt, p.astype(BF16), preferred_element_type=F32)
        m_ref[c] = m_new

    def reset():
        m_ref[...] = jnp.full(m_ref.shape, -jnp.inf, F32)
        l_ref[...] = jnp.zeros(l_ref.shape, F32)
        acc_ref[...] = jnp.zeros(acc_ref.shape, F32)

    for qi in range(n_tiles):
        q = q_ref[qi * tile:(qi + 1) * tile, :]
        qm_ref[2 * qi] = jnp.where(first_map, q, jnp.zeros_like(q))
        qm_ref[2 * qi + 1] = jnp.where(first_map, jnp.zeros_like(q), q)
    steps = [(qi, ki) for qi in range(n_tiles) for ki in range(qi + 1)]
    reset()
    scores_next = [qk_score(*steps[0], c) for c in range(2)]
    for n, (qi, ki) in enumerate(steps):
        scores = scores_next
        scores_next = [None, None]
        for c in range(2):
            if n + 1 < len(steps):
                scores_next[c] = qk_score(*steps[n + 1], c)
            softmax_pv(c, scores[c], ki, causal if ki == qi else None)
        if ki == qi:
            o = acc_ref[0] / l_ref[0] - lam * (acc_ref[1] / l_ref[1])
            o = o * lax.rsqrt(jnp.mean(o * o, axis=0, keepdims=True) + EPS) * on_ref[...] * (1.0 - lam_init)
            o_ref[qi * tile:(qi + 1) * tile, :] = o.T.astype(o_ref.dtype)
            if n + 1 < len(steps):
                reset()


def _diff_attn(qk, proj, lam_tab, o_norm, bsz, seq, lam_init):
    n = qk.shape[0]
    tile = _tile(ATTN_TILE, seq)
    o_norm_cols = jnp.broadcast_to(o_norm[:, None], (B_V_DIM, tile))
    return pl.pallas_call(
        functools.partial(_diff_attn_kernel, tile=tile, lam_init=lam_init),
        grid=(bsz, B_HEADS),
        in_specs=[
            pl.BlockSpec((seq, LANES), lambda b, h: (b, h)),
            pl.BlockSpec((seq, LANES), lambda b, h: (b, B_HEADS + h)),
            pl.BlockSpec((seq, B_V_DIM), lambda b, h: (b, 2 * B_HEADS + h)),
            pl.BlockSpec((SUBLANES, LANES), lambda b, h: (0, 0)),
            pl.BlockSpec((B_V_DIM, tile), lambda b, h: (0, 0)),
        ],
        out_specs=pl.BlockSpec((seq, B_V_DIM), lambda b, h: (b, h)),
        out_shape=jax.ShapeDtypeStruct((n, D_MODEL), BF16),
        scratch_shapes=[
            pltpu.VMEM((B_V_DIM, seq), BF16),
            pltpu.VMEM((2 * (seq // tile), tile, LANES), BF16),
            pltpu.VMEM((2, 1, tile), F32),
            pltpu.VMEM((2, 1, tile), F32),
            pltpu.VMEM((2, B_V_DIM, tile), F32),
        ],
        compiler_params=_params(("parallel", "parallel")),
        name="diff_attn",
    )(qk, qk, proj, lam_tab, o_norm_cols)


HALF_D = D_MODEL // 2
HI16 = 0xFFFF0000
PACKED_TILES = HALF_D // LANES


def _pack_bf16_pairs(v):
    bits = lax.bitcast_convert_type(v.astype(BF16).astype(F32), jnp.uint32)
    return (bits[:, HALF_D:] & jnp.uint32(HI16)) | (bits[:, :HALF_D] >> 16)


def _unpack_bf16_pairs(words):
    first = lax.bitcast_convert_type(words << 16, F32)
    second = lax.bitcast_convert_type(words & jnp.uint32(HI16), F32)
    return first.astype(BF16), second.astype(BF16)


def _outproj_kernel(a_ref, w_ref, x_ref, g1_ref, gain_ref, sc_ref, sh_ref, wr_ref, xo_ref, hp_ref, lg_ref):
    y = jnp.dot(a_ref[...], w_ref[...], preferred_element_type=F32)
    xm = x_ref[...] + g1_ref[0] * y
    xo_ref[...] = xm
    h2 = _modulated_rms(xm, gain_ref[...], sc_ref[0], sh_ref[0])
    tm = xm.shape[0]
    packed = _pack_bf16_pairs(h2)
    for j in range(ROW_TILES):
        hp_ref[pl.ds(j, tm, stride=ROW_TILES), :] = (
            packed[:, j * LANES:(j + 1) * LANES] if j < PACKED_TILES else jnp.zeros((tm, LANES), jnp.uint32))
    lg_ref[...] = lax.dot_general(wr_ref[...], h2, (((1,), (1,)), ((), ())),
                                  precision=HIGHEST, preferred_element_type=F32)


def _outproj_norm_router(a, w, x, g1, gain, sc1p, sh, w_router_t, seq):
    n, d = x.shape
    tm = _tile(512, seq)
    per_b = seq // tm
    bmap = lambda i: (i // per_b, 0, 0)
    return pl.pallas_call(
        _outproj_kernel,
        grid=(n // tm,),
        in_specs=[
            pl.BlockSpec((tm, a.shape[1]), lambda i: (i, 0)),
            pl.BlockSpec(w.shape, lambda i: (0, 0)),
            pl.BlockSpec((tm, d), lambda i: (i, 0)),
            pl.BlockSpec((1, 1, d), bmap),
            pl.BlockSpec((1, d), lambda i: (0, 0)),
            pl.BlockSpec((1, 1, d), bmap),
            pl.BlockSpec((1, 1, d), bmap),
            pl.BlockSpec((N_EXPERTS, d), lambda i: (0, 0)),
        ],
        out_specs=[
            pl.BlockSpec((tm, d), lambda i: (i, 0)),
            pl.BlockSpec((tm * ROW_TILES, LANES), lambda i: (i, 0)),
            pl.BlockSpec((N_EXPERTS, tm), lambda i: (0, i)),
        ],
        out_shape=[
            jax.ShapeDtypeStruct((n, d), F32),
            jax.ShapeDtypeStruct((n * ROW_TILES, LANES), jnp.uint32),
            jax.ShapeDtypeStruct((N_EXPERTS, n), F32),
        ],
        compiler_params=_params(("parallel",)),
        name="outproj_norm_router",
    )(a, w, x, g1, gain, sc1p, sh, w_router_t)


PAIR_LO = (0, 0, 0, 1, 1, 2)
PAIR_HI = (1, 2, 3, 2, 3, 3)
PAIRS_PER_GROUP = len(PAIR_LO)
N_CLASSES = N_GROUPS * PAIRS_PER_GROUP
CLASS_ROWS = 32
RANK_BITS = 20
RANK_SPAN = 1 << RANK_BITS


def _route_kernel(lg_ref, bias_ref, oi_ref, cnt_ref, carry_ref, tri_ref):
    tr = lg_ref.shape[1]
    step = pl.program_id(0)

    @pl.when(step == 0)
    def _():
        carry_ref[...] = jnp.zeros_like(carry_ref)
        r = lax.broadcasted_iota(jnp.int32, (tr, tr), 0)
        c = lax.broadcasted_iota(jnp.int32, (tr, tr), 1)
        tri_ref[...] = jnp.where(r < c, 1.0, 0.0).astype(BF16)

    scores = jax.nn.sigmoid(lg_ref[...])
    biased = scores + bias_ref[...]
    rows = [biased[e:e + 1, :] for e in range(N_EXPERTS)]

    def top2_sum(a, b, c, d):
        m1, n1 = jnp.maximum(a, b), jnp.minimum(a, b)
        m2, n2 = jnp.maximum(c, d), jnp.minimum(c, d)
        return jnp.maximum(m1, m2) + jnp.maximum(jnp.minimum(m1, m2), jnp.maximum(n1, n2))

    gscore = [top2_sum(*rows[g * EXPERTS_PER_GROUP:(g + 1) * EXPERTS_PER_GROUP]) for g in range(N_GROUPS)]
    best = gscore[0]
    gsel = jnp.zeros_like(best, dtype=jnp.int32)
    for g in range(1, N_GROUPS):
        upd = gscore[g] > best
        gsel = jnp.where(upd, g, gsel)
        best = jnp.where(upd, gscore[g], best)

    def pick(table, j):
        out = table[j]
        for g in range(1, N_GROUPS):
            out = jnp.where(gsel == g, table[g * EXPERTS_PER_GROUP + j], out)
        return out

    in_b = [pick(rows, j) for j in range(EXPERTS_PER_GROUP)]
    v1, i1 = in_b[0], jnp.zeros_like(gsel)
    for j in range(1, EXPERTS_PER_GROUP):
        upd = in_b[j] > v1
        v1 = jnp.where(upd, in_b[j], v1)
        i1 = jnp.where(upd, j, i1)
    v2 = jnp.full_like(v1, -jnp.inf)
    i2 = jnp.zeros_like(gsel)
    for j in range(EXPERTS_PER_GROUP):
        upd = (i1 != j) & (in_b[j] > v2)
        v2 = jnp.where(upd, in_b[j], v2)
        i2 = jnp.where(upd, j, i2)
    first_is_lo = i1 < i2
    lo = jnp.where(first_is_lo, i1, i2)
    hi = jnp.where(first_is_lo, i2, i1)
    pair = jnp.where(lo == 0, hi - 1, jnp.where(lo == 1, hi + 1, PAIRS_PER_GROUP - 1))
    cls = gsel * PAIRS_PER_GROUP + pair

    cid = lax.broadcasted_iota(jnp.int32, (CLASS_ROWS, tr), 0)
    onehot = jnp.where(cid == cls, 1.0, 0.0)
    before = jnp.dot(onehot.astype(BF16), tri_ref[...], preferred_element_type=F32) + carry_ref[...]
    rank = jnp.sum(onehot * before, axis=0, keepdims=True).astype(jnp.int32)
    oi_ref[...] = cls * RANK_SPAN + rank
    new_carry = carry_ref[...] + jnp.sum(onehot, axis=1, keepdims=True)
    carry_ref[...] = new_carry
    cnt_ref[...] = new_carry.astype(jnp.int32)


def _route(logits_t, router_bias):
    n = logits_t.shape[1]
    tr = _tile(512, n)
    return pl.pallas_call(
        _route_kernel,
        grid=(n // tr,),
        in_specs=[
            pl.BlockSpec((N_EXPERTS, tr), lambda i: (0, i)),
            pl.BlockSpec((N_EXPERTS, 1), lambda i: (0, 0)),
        ],
        out_specs=[
            pl.BlockSpec((1, tr), lambda i: (0, i)),
            pl.BlockSpec((CLASS_ROWS, 1), lambda i: (0, 0)),
        ],
        out_shape=[
            jax.ShapeDtypeStruct((1, n), jnp.int32),
            jax.ShapeDtypeStruct((CLASS_ROWS, 1), jnp.int32),
        ],
        scratch_shapes=[pltpu.VMEM((CLASS_ROWS, 1), F32), pltpu.VMEM((tr, tr), BF16)],
        compiler_params=_params(("arbitrary",)),
        name="route",
    )(logits_t, router_bias.reshape(N_EXPERTS, 1).astype(F32))


SORT_UNROLL = 8
MOE_ROWS = 256
MOE_ROWS_LOG2 = MOE_ROWS.bit_length() - 1
assert 1 << MOE_ROWS_LOG2 == MOE_ROWS


def _plan_kernel(code_ref, cnt_ref, tok_ref, ea_ref, eb_ref, off_ref, nv_ref, nu_ref, start_ref):
    n_blocks = ea_ref.shape[0]
    run = jnp.int32(0)
    blk = jnp.int32(0)
    for c in range(N_CLASSES):
        cnt = cnt_ref[c]
        start_ref[c] = run - c * RANK_SPAN
        e_lo = (c // PAIRS_PER_GROUP) * EXPERTS_PER_GROUP + PAIR_LO[c % PAIRS_PER_GROUP]
        e_hi = (c // PAIRS_PER_GROUP) * EXPERTS_PER_GROUP + PAIR_HI[c % PAIRS_PER_GROUP]
        n_blk = (cnt + (MOE_ROWS - 1)) >> MOE_ROWS_LOG2

        def fill(b, carry, run=run, blk=blk, cnt=cnt, e_lo=e_lo, e_hi=e_hi):
            ea_ref[blk + b] = e_lo
            eb_ref[blk + b] = e_hi
            off_ref[blk + b] = run + b * MOE_ROWS
            nv_ref[blk + b] = jnp.minimum(cnt - b * MOE_ROWS, MOE_ROWS)
            return carry

        lax.fori_loop(0, n_blk, fill, 0)
        run = run + cnt
        blk = blk + n_blk
    nu_ref[0] = blk

    def fill_unused(b, carry):
        ea_ref[b] = ea_ref[blk - 1]
        eb_ref[b] = eb_ref[blk - 1]
        off_ref[b] = 0
        nv_ref[b] = 0
        return carry

    lax.fori_loop(blk, n_blocks, fill_unused, 0)

    def place(i, carry):
        toks = [i * SORT_UNROLL + u for u in range(SORT_UNROLL)]
        codes = [code_ref[t] for t in toks]
        slots = [start_ref[code >> RANK_BITS] + code for code in codes]
        for t, p in zip(toks, slots):
            tok_ref[p] = t
        return carry

    lax.fori_loop(0, code_ref.shape[0] // SORT_UNROLL, place, 0)


def _plan(codes, counts, n_blocks):
    n = codes.shape[0]
    assert n % SORT_UNROLL == 0 and n <= RANK_SPAN
    smem = pl.BlockSpec(memory_space=pltpu.SMEM)
    i32 = lambda size: jax.ShapeDtypeStruct((size,), jnp.int32)
    return pl.pallas_call(
        _plan_kernel,
        in_specs=[smem, smem],
        out_specs=[smem] * 6,
        out_shape=[i32(n), i32(n_blocks), i32(n_blocks), i32(n_blocks), i32(n_blocks), i32(1)],
        scratch_shapes=[pltpu.SMEM((N_CLASSES,), jnp.int32)],
        name="moe_plan",
    )(codes, counts)


ROW_UNROLL = 8


def _for_rows(n, fn):
    groups = n // ROW_UNROLL

    def group(g, c):
        for u in range(ROW_UNROLL):
            fn(g * ROW_UNROLL + u)
        return c

    def single(r, c):
        fn(r)
        return c

    lax.fori_loop(0, groups, group, 0)
    lax.fori_loop(groups * ROW_UNROLL, n, single, 0)


def _row_copy(src_ref, src_row, dst_ref, dst_row, sem):
    return pltpu.make_async_copy(
        src_ref.at[pl.ds(pl.multiple_of(src_row * ROW_TILES, ROW_TILES), ROW_TILES)],
        dst_ref.at[pl.ds(pl.multiple_of(dst_row * ROW_TILES, ROW_TILES), ROW_TILES)],
        sem)


def _expert_kernel(ea_ref, eb_ref, off_ref, nv_ref, nu_ref, tok_ref,
                   h_hbm, wgu_a_ref, wd_a_ref, wgu_b_ref, wd_b_ref, wr_a_ref, wr_b_ref, y_hbm,
                   xbuf, ybuf, gsem, ssem):
    del ea_ref, eb_ref
    tb = xbuf.shape[1] // ROW_TILES
    i = pl.program_id(0)
    slot = i % 2
    n_used = nu_ref[0]

    def start_gather(block, s):
        base = off_ref[block]
        _for_rows(nv_ref[block],
                  lambda r: _row_copy(h_hbm, tok_ref[base + r], xbuf.at[s], r, gsem.at[s]).start())

    def start_scatter(block, s):
        base = off_ref[block]
        _for_rows(nv_ref[block],
                  lambda r: _row_copy(ybuf.at[s], r, y_hbm, tok_ref[base + r], ssem.at[s]).start())

    def wait_rows(block, hbm, buf, sem):
        rows = nv_ref[block] * ROW_TILES

        @pl.when(rows > 0)
        def _():
            pltpu.make_async_copy(hbm.at[pl.ds(0, rows)], buf.at[pl.ds(0, rows)], sem).wait()

    @pl.when(i == 0)
    def _():
        xbuf[...] = jnp.zeros_like(xbuf)
        start_gather(0, 0)

    @pl.when(i < n_used)
    def _():
        @pl.when(i + 1 < n_used)
        def _():
            start_gather(i + 1, 1 - slot)

        wait_rows(i, h_hbm, xbuf.at[slot], gsem.at[slot])

        @pl.when(i >= 2)
        def _():
            wait_rows(i - 2, y_hbm, ybuf.at[slot], ssem.at[slot])

        words = [xbuf[slot, pl.ds(j, tb, stride=ROW_TILES), :] for j in range(PACKED_TILES)]
        halves = [_unpack_bf16_pairs(w) for w in words]
        x = jnp.concatenate([h[0] for h in halves] + [h[1] for h in halves], axis=1)

        def mlp(wgu_ref, wd_ref):
            gu = jnp.dot(x, wgu_ref[0], preferred_element_type=F32)
            gate = gu[:, :D_EXPERT]
            act = gate * jax.nn.sigmoid(gate) * gu[:, D_EXPERT:]
            return jnp.dot(act.astype(BF16), wd_ref[0], preferred_element_type=F32)

        score_a = jax.nn.sigmoid(jnp.dot(x, wr_a_ref[0], preferred_element_type=F32))
        score_b = jax.nn.sigmoid(jnp.dot(x, wr_b_ref[0], preferred_element_type=F32))
        inv_tot = 1.0 / (score_a + score_b)
        w_a = score_a * inv_tot
        w_b = score_b * inv_tot
        ya = mlp(wgu_a_ref, wd_a_ref)
        yb = mlp(wgu_b_ref, wd_b_ref)
        for j in range(ROW_TILES):
            sl = slice(j * LANES, (j + 1) * LANES)
            ybuf[slot, pl.ds(j, tb, stride=ROW_TILES), :] = w_a * ya[:, sl] + w_b * yb[:, sl]
        start_scatter(i, slot)

        @pl.when(i == n_used - 1)
        def _():
            wait_rows(i, y_hbm, ybuf.at[slot], ssem.at[slot])

            @pl.when(i >= 1)
            def _():
                wait_rows(i - 1, y_hbm, ybuf.at[1 - slot], ssem.at[1 - slot])


def _experts(tables, sorted_tok, h_rows, layer, w_gu, w_down, wr_bcast, tb):
    block_ea, block_eb, src_off, n_valid, n_used = tables
    n_blocks = block_ea.shape[0]
    d, two_f = w_gu.shape[2:]
    ea_map = lambda i, ea, eb, off, nv, nu, tok: (ea[i], 0, 0)
    eb_map = lambda i, ea, eb, off, nv, nu, tok: (eb[i], 0, 0)
    lea_map = lambda i, ea, eb, off, nv, nu, tok: (layer, ea[i], 0, 0)
    leb_map = lambda i, ea, eb, off, nv, nu, tok: (layer, eb[i], 0, 0)
    grid_spec = pltpu.PrefetchScalarGridSpec(
        num_scalar_prefetch=6,
        grid=(n_blocks,),
        in_specs=[
            pl.BlockSpec(memory_space=pl.ANY),
            pl.BlockSpec((None, 1, d, two_f), lea_map),
            pl.BlockSpec((None, 1, two_f // 2, d), lea_map),
            pl.BlockSpec((None, 1, d, two_f), leb_map),
            pl.BlockSpec((None, 1, two_f // 2, d), leb_map),
            pl.BlockSpec((1, d, LANES), ea_map),
            pl.BlockSpec((1, d, LANES), eb_map),
        ],
        out_specs=pl.BlockSpec(memory_space=pl.ANY),
        scratch_shapes=[
            pltpu.VMEM((2, tb * ROW_TILES, LANES), jnp.uint32),
            pltpu.VMEM((2, tb * ROW_TILES, LANES), F32),
            pltpu.SemaphoreType.DMA((2,)),
            pltpu.SemaphoreType.DMA((2,)),
        ],
    )
    return pl.pallas_call(
        _expert_kernel,
        grid_spec=grid_spec,
        out_shape=jax.ShapeDtypeStruct(h_rows.shape, F32),
        compiler_params=_params(("arbitrary",)),
        name="moe_experts",
    )(block_ea, block_eb, src_off, n_valid, n_used, sorted_tok, h_rows, w_gu, w_down, w_gu, w_down, wr_bcast, wr_bcast)


def _residual_kernel(y_ref, x_ref, g2_ref, o_ref):
    o_ref[...] = _gated_residual(x_ref, y_ref, g2_ref)


def _residual(y_rows, x_mid, g2, seq):
    n, d = x_mid.shape
    tc = _tile(512, seq)
    per_b = seq // tc
    return pl.pallas_call(
        _residual_kernel,
        grid=(n // tc,),
        in_specs=[
            pl.BlockSpec((tc * ROW_TILES, LANES), lambda i: (i, 0)),
            pl.BlockSpec((tc, d), lambda i: (i, 0)),
            pl.BlockSpec((1, 1, d), lambda i: (i // per_b, 0, 0)),
        ],
        out_specs=pl.BlockSpec((tc, d), lambda i: (i, 0)),
        out_shape=jax.ShapeDtypeStruct((n, d), F32),
        compiler_params=_params(("parallel",)),
        name="moe_residual",
    )(y_rows, x_mid, g2)


def _moe(h_rows, logits_t, router_bias, layer, w_gu, w_down, wr_bcast):
    n = logits_t.shape[1]
    n_blocks = (n + N_CLASSES * (MOE_ROWS - 1) + MOE_ROWS - 1) // MOE_ROWS
    codes, counts = _route(logits_t, router_bias)
    sorted_tok, *tables = _plan(codes.reshape(n), counts.reshape(CLASS_ROWS), n_blocks)
    return _experts(tables, sorted_tok, h_rows, layer, w_gu, w_down, wr_bcast, MOE_ROWS)


def kernel(x, c, positions, norm1, norm2, w_ada, b_ada, a_w_in, a_b_if, a_h_norm, a_w_out, b_w_in, b_q_norm, b_k_norm, b_lam_q1, b_lam_k1, b_lam_q2, b_lam_k2, b_o_norm, b_w_out, w_router, router_bias, moe_w_gu, moe_w_down):
    bsz, seq, d = x.shape
    depth = w_ada.shape[0]
    n = bsz * seq
    xf = x.reshape(n, d)
    mod = _ada_mod(c, w_ada, b_ada)
    w_router_t = w_router.T
    wr_bcast = jnp.broadcast_to(w_router_t[:, :, None], (N_EXPERTS, d, LANES)).astype(BF16)
    pos_row = positions.reshape(n)
    w_gu = moe_w_gu.astype(BF16)
    w_down = moe_w_down.astype(BF16)

    residual = None
    for l in range(depth):
        sh1, sc1, g1, sh2, sc2, g2 = [mod[l, :, i * d:(i + 1) * d].reshape(bsz, 1, d) for i in range(6)]
        j = l // 2
        if l % 2 == 0:
            w_in = a_w_in[j]
            w_main = w_in[:, :A_MAIN_COLS].astype(BF16)
            w_gate = jnp.pad(w_in[:, A_MAIN_COLS:], ((0, 0), (0, LANES - 2 * A_HEADS))).astype(BF16)
        else:
            w_in = b_w_in[j]
            w_qk = _interleave_maps(w_in[:, :2 * D_MODEL])
            w_main, w_gate = jnp.concatenate([w_qk, w_in[:, 2 * D_MODEL:]], axis=1).astype(BF16), None
        outs = _norm_matmul(xf, residual, norm1[l].reshape(1, d), 1.0 + sc1, sh1, w_main, w_gate, seq)
        if residual is not None:
            xf, *outs = outs
        if l % 2 == 0:
            proj, gates = outs
            bias_row = jnp.pad(a_b_if[j], (0, LANES - 2 * A_HEADS)).reshape(1, LANES)
            mixed = _mlstm(proj, gates, bias_row, a_h_norm[j], bsz, seq)
            w_out = a_w_out[j].astype(BF16)
        else:
            (proj,) = outs
            qk = _qk_prep(proj, pos_row, b_q_norm[j], b_k_norm[j], seq)
            lam_tab = jnp.zeros((SUBLANES, LANES), F32)
            for r, v in enumerate((b_lam_q1[j], b_lam_k1[j], b_lam_q2[j], b_lam_k2[j])):
                lam_tab = lam_tab.at[r, :B_HEAD_DIM].set(v)
            lam_init = 0.8 - 0.6 * math.exp(-0.3 * l)
            mixed = _diff_attn(qk, proj, lam_tab, b_o_norm[j], bsz, seq, lam_init)
            w_out = b_w_out[j].astype(BF16)
        xf, h_rows, logits_t = _outproj_norm_router(
            mixed, w_out, xf, g1, norm2[l].reshape(1, d), 1.0 + sc2, sh2, w_router_t, seq)
        residual = (_moe(h_rows, logits_t, router_bias, l, w_gu, w_down, wr_bcast), g2)
    return _residual(residual[0], xf, residual[1], seq).reshape(bsz, seq, d)
```

```python
import functools
import math

import jax
import jax.numpy as jnp
from jax import lax
from jax.experimental import pallas as pl
from jax.experimental.pallas import tpu as pltpu

D_MODEL = 1024
A_HEADS = 4
A_QK_DIM = 128
A_V_DIM = 256
A_CHUNK = 128
A_HQ = A_HEADS * A_QK_DIM
A_HV = A_HEADS * A_V_DIM
A_MAIN_COLS = 2 * A_HQ + 2 * A_HV

B_HEADS = 8
B_HEAD_DIM = 64
B_V_DIM = 128
B_ROT_DIM = 16
ROPE_THETA = 500000.0

N_EXPERTS = 16
N_GROUPS = 4
EXPERTS_PER_GROUP = 4
TOP_K = 2
D_EXPERT = 512
EPS = 1e-6

LANES = 128
SUBLANES = 8
ROW_TILES = D_MODEL // LANES
assert ROW_TILES == SUBLANES
VMEM_LIMIT = 48 * 1024 * 1024

F32 = jnp.float32
BF16 = jnp.bfloat16
HIGHEST = lax.Precision.HIGHEST


def _params(sem):
    return pltpu.CompilerParams(dimension_semantics=sem, vmem_limit_bytes=VMEM_LIMIT)


def _tile(pref, n):
    t = min(pref, n)
    assert n % t == 0, (pref, n)
    return t


def _ada_kernel(c_ref, w_ref, b_ref, o_ref):
    c = c_ref[...]
    c_act = c * jax.nn.sigmoid(c)
    o_ref[0] = jnp.dot(c_act, w_ref[0], precision=HIGHEST, preferred_element_type=F32) + b_ref[0]


def _ada_mod(c, w_ada, b_ada):
    depth, d, six_d = w_ada.shape
    bsz = c.shape[0]
    tn = _tile(1536, six_d)
    return pl.pallas_call(
        _ada_kernel,
        grid=(depth, six_d // tn),
        in_specs=[
            pl.BlockSpec((bsz, d), lambda l, j: (0, 0)),
            pl.BlockSpec((1, d, tn), lambda l, j: (l, 0, j)),
            pl.BlockSpec((1, 1, tn), lambda l, j: (l, 0, j)),
        ],
        out_specs=pl.BlockSpec((1, bsz, tn), lambda l, j: (l, 0, j)),
        out_shape=jax.ShapeDtypeStruct((depth, bsz, six_d), F32),
        compiler_params=_params(("parallel", "parallel")),
        name="ada_mod",
    )(c, w_ada, b_ada.reshape(depth, 1, six_d))


def _modulated_rms(x, g, sc1p, sh):
    y = x * lax.rsqrt(jnp.mean(x * x, axis=-1, keepdims=True) + EPS)
    return (y * g) * sc1p + sh


def _gated_residual(x_ref, y_ref, g_ref):
    rows = x_ref.shape[0]
    g = g_ref[0]
    return jnp.concatenate(
        [x_ref[:, j * LANES:(j + 1) * LANES]
         + g[:, j * LANES:(j + 1) * LANES] * y_ref[pl.ds(j, rows, stride=ROW_TILES), :] for j in range(ROW_TILES)],
        axis=1)


def _norm_mm_kernel(*refs, col_chunk, has_gates, has_residual):
    refs = list(refs)
    x_ref = refs.pop(0)
    if has_residual:
        y_ref, g2_ref = refs.pop(0), refs.pop(0)
    g_ref, sc_ref, sh_ref, w_ref = refs[:4]
    refs = refs[4:]
    wg_ref = refs.pop(0) if has_gates else None
    xo_ref = refs.pop(0) if has_residual else None
    o_ref = refs.pop(0)
    og_ref = refs.pop(0) if has_gates else None

    if has_residual:
        x = _gated_residual(x_ref, y_ref, g2_ref)
        xo_ref[...] = x
    else:
        x = x_ref[...]
    hb = _modulated_rms(x, g_ref[...], sc_ref[0], sh_ref[0]).astype(BF16)
    for c0 in range(0, o_ref.shape[1], col_chunk):
        o_ref[:, c0:c0 + col_chunk] = jnp.dot(
            hb, w_ref[:, c0:c0 + col_chunk], preferred_element_type=F32).astype(o_ref.dtype)
    if has_gates:
        og_ref[...] = jnp.dot(hb, wg_ref[...], preferred_element_type=F32)


def _norm_matmul(x, residual, gain, sc1p, sh, w, wg, seq):
    n, d = x.shape
    cols = w.shape[1]
    tm = _tile(512, seq)
    per_b = seq // tm
    has_gates = wg is not None
    has_residual = residual is not None
    bmap = lambda i: (i // per_b, 0, 0)
    in_specs = [pl.BlockSpec((tm, d), lambda i: (i, 0))]
    args = [x]
    if has_residual:
        in_specs += [pl.BlockSpec((tm * ROW_TILES, LANES), lambda i: (i, 0)), pl.BlockSpec((1, 1, d), bmap)]
        args += list(residual)
    in_specs += [
        pl.BlockSpec((1, d), lambda i: (0, 0)),
        pl.BlockSpec((1, 1, d), bmap),
        pl.BlockSpec((1, 1, d), bmap),
        pl.BlockSpec((d, cols), lambda i: (0, 0)),
    ]
    args += [gain, sc1p, sh, w]
    out_specs, out_shape = [], []
    if has_residual:
        out_specs.append(pl.BlockSpec((tm, d), lambda i: (i, 0)))
        out_shape.append(jax.ShapeDtypeStruct((n, d), F32))
    out_specs.append(pl.BlockSpec((tm, cols), lambda i: (i, 0)))
    out_shape.append(jax.ShapeDtypeStruct((n, cols), BF16))
    if has_gates:
        in_specs.append(pl.BlockSpec((d, LANES), lambda i: (0, 0)))
        out_specs.append(pl.BlockSpec((tm, LANES), lambda i: (i, 0)))
        out_shape.append(jax.ShapeDtypeStruct((n, LANES), F32))
        args.append(wg)
    return pl.pallas_call(
        functools.partial(_norm_mm_kernel, col_chunk=512, has_gates=has_gates, has_residual=has_residual),
        grid=(n // tm,),
        in_specs=in_specs,
        out_specs=out_specs,
        out_shape=out_shape,
        compiler_params=_params(("parallel",)),
        name="norm_inproj",
    )(*args)


def _log_sigmoid(x):
    return jnp.minimum(x, 0.0) - jnp.log1p(jnp.exp(-jnp.abs(x)))


def _mlstm_chunk(r0, p_ref, g_ref, bias_ref, hn_ref, o_ref, ct_ref, n_ref, m_ref):
    L = A_CHUNK
    rows = slice(r0, r0 + L)
    gates = g_ref[rows, :] + bias_ref[...]
    src = lax.broadcasted_iota(jnp.int32, (L, L), 0)
    tgt = lax.broadcasted_iota(jnp.int32, (L, L), 1)
    causal = src <= tgt
    bcum = jnp.dot(jnp.where(src >= tgt, 1.0, 0.0), _log_sigmoid(gates), precision=HIGHEST,
                   preferred_element_type=F32)
    bcum_t = bcum.T
    nt = (((1,), (1,)), ((), ()))

    pre = []
    for h in range(A_HEADS):
        q = p_ref[rows, h * A_QK_DIM:(h + 1) * A_QK_DIM]
        ksf = p_ref[rows, A_HQ + h * A_QK_DIM:A_HQ + (h + 1) * A_QK_DIM].astype(F32) * (A_QK_DIM ** -0.5)
        v = p_ref[rows, 2 * A_HQ + h * A_V_DIM:2 * A_HQ + (h + 1) * A_V_DIM]
        ct_old = ct_ref[h]
        n_old = n_ref[h]
        qk = lax.dot_general(ksf.astype(BF16), q, nt, preferred_element_type=F32)
        qc = lax.dot_general(ct_old.astype(BF16), q, nt, preferred_element_type=F32)
        qn = lax.dot_general(n_old.astype(BF16), q, nt, preferred_element_type=F32)[0:1]
        vt = v.astype(F32).T.astype(BF16)
        pre.append((ksf, vt, ct_old, n_old, qk, qc, qn))

    for h in range(A_HEADS):
        ksf, vt, ct_old, n_old, qk, qc, qn = pre[h]
        og = p_ref[rows, 2 * A_HQ + A_HV + h * A_V_DIM:2 * A_HQ + A_HV + (h + 1) * A_V_DIM].astype(F32)
        b_row = bcum_t[A_HEADS + h:A_HEADS + h + 1, :]
        b_last = b_row[:, L - 1:L]
        c_col = gates[:, h:h + 1] - bcum[:, A_HEADS + h:A_HEADS + h + 1]
        m11 = m_ref[h][:, 0:1]

        log_d = jnp.where(causal, c_col + b_row, -jnp.inf)
        log_inter = b_row + m11
        m_t = jnp.maximum(jnp.max(log_d, axis=0, keepdims=True), log_inter)
        dmat = jnp.exp(log_d - m_t)
        inter = jnp.exp(log_inter - m_t)
        s = qk * dmat
        num = jnp.dot(vt, s.astype(BF16), preferred_element_type=F32) + inter * qc
        den = jnp.sum(s, axis=0, keepdims=True) + inter * qn
        hh = num / jnp.maximum(jnp.abs(den), jnp.exp(-m_t))

        lw_col = b_last + c_col
        m_new = jnp.maximum(b_last + m11, jnp.max(lw_col, axis=0, keepdims=True))
        kw = ksf * jnp.exp(lw_col - m_new)
        decay = jnp.exp(b_last + m11 - m_new)
        ct_ref[h] = decay * ct_old + jnp.dot(vt, kw.astype(BF16), preferred_element_type=F32)
        n_ref[h] = decay * n_old + jnp.broadcast_to(jnp.sum(kw, axis=0, keepdims=True), n_old.shape)
        m_ref[h] = jnp.broadcast_to(m_new, (1, LANES))

        hn = hh * lax.rsqrt(jnp.mean(hh * hh, axis=0, keepdims=True) + EPS)
        hn = (hn * hn_ref[h * A_V_DIM:(h + 1) * A_V_DIM, :]).T
        o_ref[rows, h * A_V_DIM:(h + 1) * A_V_DIM] = (hn * jax.nn.sigmoid(og)).astype(o_ref.dtype)


MLSTM_CHUNKS_PER_STEP = 2


def _mlstm_kernel(p_ref, g_ref, bias_ref, hn_ref, o_ref, ct_ref, n_ref, m_ref):
    @pl.when(pl.program_id(1) == 0)
    def _():
        ct_ref[...] = jnp.zeros_like(ct_ref)
        n_ref[...] = jnp.zeros_like(n_ref)
        m_ref[...] = jnp.zeros_like(m_ref)

    for r0 in range(0, p_ref.shape[0], A_CHUNK):
        _mlstm_chunk(r0, p_ref, g_ref, bias_ref, hn_ref, o_ref, ct_ref, n_ref, m_ref)


def _mlstm(proj, gates, bias_row, h_norm, bsz, seq):
    n = proj.shape[0]
    rows = _tile(MLSTM_CHUNKS_PER_STEP * A_CHUNK, seq)
    nc = seq // rows
    h_norm_cols = jnp.broadcast_to(h_norm[:, None], (A_HV, A_CHUNK))
    return pl.pallas_call(
        _mlstm_kernel,
        grid=(bsz, nc),
        in_specs=[
            pl.BlockSpec((rows, A_MAIN_COLS), lambda b, c: (b * nc + c, 0)),
            pl.BlockSpec((rows, LANES), lambda b, c: (b * nc + c, 0)),
            pl.BlockSpec((1, LANES), lambda b, c: (0, 0)),
            pl.BlockSpec((A_HV, A_CHUNK), lambda b, c: (0, 0)),
        ],
        out_specs=pl.BlockSpec((rows, A_HV), lambda b, c: (b * nc + c, 0)),
        out_shape=jax.ShapeDtypeStruct((n, A_HV), BF16),
        scratch_shapes=[
            pltpu.VMEM((A_HEADS, A_V_DIM, A_QK_DIM), F32),
            pltpu.VMEM((A_HEADS, SUBLANES, A_QK_DIM), F32),
            pltpu.VMEM((A_HEADS, 1, LANES), F32),
        ],
        compiler_params=_params(("parallel", "arbitrary")),
        name="mlstm",
    )(proj, gates, bias_row, h_norm_cols)


def _qk_prep_kernel(p_ref, pos_ref, seg_ref, segt_ref, gain_ref, freq_ref, o_ref):
    tm = p_ref.shape[0]
    x = p_ref[...].astype(F32)
    ss = jnp.dot((x * x).astype(BF16), seg_ref[...], preferred_element_type=F32)
    r = lax.rsqrt(ss * (1.0 / B_HEAD_DIM) + EPS)
    r_hi = r.astype(BF16)
    r_lo = (r - r_hi.astype(F32)).astype(BF16)
    r_full = jnp.dot(jnp.concatenate([r_hi, r_lo], axis=1), segt_ref[...], preferred_element_type=F32)

    assert B_ROT_DIM // 2 == SUBLANES
    ang = freq_ref[:, 0:1] * pos_ref[0].astype(F32)
    cos8 = jnp.cos(ang)
    sin8 = jnp.sin(ang)
    one8 = jnp.ones_like(cos8)
    zero8 = jnp.zeros_like(cos8)
    groups_per_quarter = MAP_QUARTER // SUBLANES

    def chunk_pattern(lower, upper, rest):
        quarter = lambda lead: [lead] + [rest] * (groups_per_quarter - 1)
        return jnp.concatenate(quarter(lower) * 2 + quarter(upper) * 2, axis=0).T

    c_mul = chunk_pattern(cos8, cos8, one8)
    s_mul = chunk_pattern(-sin8, sin8, zero8)
    for j in range(2 * D_MODEL // LANES):
        sl = slice(j * LANES, (j + 1) * LANES)
        y = x[:, sl] * r_full[:, sl] * gain_ref[:, sl]
        out = y * c_mul + pltpu.roll(y, LANES // 2, axis=1) * s_mul
        if j < D_MODEL // LANES:
            out = out * (B_HEAD_DIM ** -0.5 * math.log2(math.e))
        o_ref[:, sl] = out.astype(o_ref.dtype)


MAP_QUARTER = B_HEAD_DIM // 2


def _interleave_maps(t):
    half = B_ROT_DIM // 2
    rest = MAP_QUARTER - half
    lead = t.shape[:-1]
    t = t.reshape(*lead, 2 * B_HEADS, 2, B_HEAD_DIM)
    lower = jnp.concatenate([t[..., 0:half], t[..., B_ROT_DIM:B_ROT_DIM + rest]], axis=-1)
    upper = jnp.concatenate([t[..., half:B_ROT_DIM], t[..., B_ROT_DIM + rest:]], axis=-1)
    out = jnp.stack([lower, upper], axis=-3)
    return out.reshape(*lead, 2 * D_MODEL)


def _qk_prep(proj, positions_row, q_norm, k_norm, seq):
    n = proj.shape[0]
    two_d = 2 * D_MODEL
    tm = _tile(256, seq)
    seg_of_col = _interleave_maps(jnp.arange(two_d) // B_HEAD_DIM)
    seg = (seg_of_col[:, None] == jnp.arange(LANES)[None, :]).astype(BF16)
    gain = _interleave_maps(
        jnp.concatenate([jnp.tile(q_norm, D_MODEL // B_HEAD_DIM), jnp.tile(k_norm, D_MODEL // B_HEAD_DIM)]))
    inv_freq = ROPE_THETA ** (-jnp.arange(0, B_ROT_DIM, 2, dtype=F32) / B_ROT_DIM)
    freq_tab = jnp.broadcast_to(inv_freq[:, None], (B_ROT_DIM // 2, LANES))
    return pl.pallas_call(
        _qk_prep_kernel,
        grid=(n // tm,),
        in_specs=[
            pl.BlockSpec((tm, two_d), lambda i: (i, 0)),
            pl.BlockSpec((1, 1, tm), lambda i: (i, 0, 0)),
            pl.BlockSpec((two_d, LANES), lambda i: (0, 0)),
            pl.BlockSpec((2 * LANES, two_d), lambda i: (0, 0)),
            pl.BlockSpec((1, two_d), lambda i: (0, 0)),
            pl.BlockSpec((SUBLANES, LANES), lambda i: (0, 0)),
        ],
        out_specs=pl.BlockSpec((tm, two_d), lambda i: (i, 0)),
        out_shape=jax.ShapeDtypeStruct((n, two_d), BF16),
        compiler_params=_params(("parallel",)),
        name="qk_prep",
    )(proj, positions_row.reshape(n // tm, 1, tm), seg, jnp.concatenate([seg.T, seg.T], axis=0),
      gain.reshape(1, two_d), freq_tab)


ATTN_TILE = 512


def _diff_attn_kernel(q_ref, k_ref, v_ref, lam_ref, on_ref, o_ref, vt_ref, qm_ref, m_ref, l_ref, acc_ref,
                      *, tile, lam_init):
    seq = q_ref.shape[0]
    n_tiles = seq // tile
    lamv = lam_ref[...]
    lam = (jnp.exp(jnp.sum(lamv[0:1] * lamv[1:2], axis=1, keepdims=True))
           - jnp.exp(jnp.sum(lamv[2:3] * lamv[3:4], axis=1, keepdims=True)) + lam_init)
    tchunk = min(256, seq)
    for c in range(seq // tchunk):
        vt_ref[:, c * tchunk:(c + 1) * tchunk] = v_ref[c * tchunk:(c + 1) * tchunk, :].astype(F32).T.astype(BF16)
    first_map = (lax.broadcasted_iota(jnp.int32, (tile, LANES), 1) // MAP_QUARTER) % 2 == 0
    key_id = lax.broadcasted_iota(jnp.int32, (tile, tile), 0)
    query_id = lax.broadcasted_iota(jnp.int32, (tile, tile), 1)
    causal = query_id >= key_id
    nt = (((1,), (1,)), ((), ()))

    def qk_score(qi, ki, c):
        return lax.dot_general(k_ref[ki * tile:(ki + 1) * tile, :], qm_ref[2 * qi + c], nt,
                               preferred_element_type=F32)

    def softmax_pv(c, s, ki, mask):
        vt = vt_ref[:, ki * tile:(ki + 1) * tile]
        if mask is not None:
            s = jnp.where(mask, s, -jnp.inf)
        m_old = m_ref[c]
        m_new = jnp.maximum(m_old, jnp.max(s, axis=0, keepdims=True))
        alpha = jnp.exp2(m_old - m_new)
        p = jnp.exp2(s - m_new)
        l_ref[c] = alpha * l_ref[c] + jnp.sum(p, axis=0, keepdims=True)
        acc_ref[c] = alpha * acc_ref[c] + jnp.dot(vt, p.astype(BF16), preferred_element_type=F32)
        m_ref[c] = m_new

    def reset():
        m_ref[...] = jnp.full(m_ref.shape, -jnp.inf, F32)
        l_ref[...] = jnp.zeros(l_ref.shape, F32)
        acc_ref[...] = jnp.zeros(acc_ref.shape, F32)

    for qi in range(n_tiles):
        q = q_ref[qi * tile:(qi + 1) * tile, :]
        qm_ref[2 * qi] = jnp.where(first_map, q, jnp.zeros_like(q))
        qm_ref[2 * qi + 1] = jnp.where(first_map, jnp.zeros_like(q), q)
    steps = [(qi, ki) for qi in range(n_tiles) for ki in range(qi + 1)]
    reset()
    scores_next = [qk_score(*steps[0], c) for c in range(2)]
    for n, (qi, ki) in enumerate(steps):
        scores = scores_next
        scores_next = [None, None]
        for c in range(2):
            if n + 1 < len(steps):
                scores_next[c] = qk_score(*steps[n + 1], c)
            softmax_pv(c, scores[c], ki, causal if ki == qi else None)
        if ki == qi:
            o = acc_ref[0] / l_ref[0] - lam * (acc_ref[1] / l_ref[1])
            o = o * lax.rsqrt(jnp.mean(o * o, axis=0, keepdims=True) + EPS) * on_ref[...] * (1.0 - lam_init)
            o_ref[qi * tile:(qi + 1) * tile, :] = o.T.astype(o_ref.dtype)
            if n + 1 < len(steps):
                reset()


def _diff_attn(qk, proj, lam_tab, o_norm, bsz, seq, lam_init):
    n = qk.shape[0]
    tile = _tile(ATTN_TILE, seq)
    o_norm_cols = jnp.broadcast_to(o_norm[:, None], (B_V_DIM, tile))
    return pl.pallas_call(
        functools.partial(_diff_attn_kernel, tile=tile, lam_init=lam_init),
        grid=(bsz, B_HEADS),
        in_specs=[
            pl.BlockSpec((seq, LANES), lambda b, h: (b, h)),
            pl.BlockSpec((seq, LANES), lambda b, h: (b, B_HEADS + h)),
            pl.BlockSpec((seq, B_V_DIM), lambda b, h: (b, 2 * B_HEADS + h)),
            pl.BlockSpec((SUBLANES, LANES), lambda b, h: (0, 0)),
            pl.BlockSpec((B_V_DIM, tile), lambda b, h: (0, 0)),
        ],
        out_specs=pl.BlockSpec((seq, B_V_DIM), lambda b, h: (b, h)),
        out_shape=jax.ShapeDtypeStruct((n, D_MODEL), BF16),
        scratch_shapes=[
            pltpu.VMEM((B_V_DIM, seq), BF16),
            pltpu.VMEM((2 * (seq // tile), tile, LANES), BF16),
            pltpu.VMEM((2, 1, tile), F32),
            pltpu.VMEM((2, 1, tile), F32),
            pltpu.VMEM((2, B_V_DIM, tile), F32),
        ],
        compiler_params=_params(("parallel", "parallel")),
        name="diff_attn",
    )(qk, qk, proj, lam_tab, o_norm_cols)


HALF_D = D_MODEL // 2
HI16 = 0xFFFF0000
PACKED_TILES = HALF_D // LANES


def _pack_bf16_pairs(v):
    bits = lax.bitcast_convert_type(v.astype(BF16).astype(F32), jnp.uint32)
    return (bits[:, HALF_D:] & jnp.uint32(HI16)) | (bits[:, :HALF_D] >> 16)


def _unpack_bf16_pairs(words):
    first = lax.bitcast_convert_type(words << 16, F32)
    second = lax.bitcast_convert_type(words & jnp.uint32(HI16), F32)
    return first.astype(BF16), second.astype(BF16)


def _outproj_kernel(a_ref, w_ref, x_ref, g1_ref, gain_ref, sc_ref, sh_ref, wr_ref, xo_ref, hp_ref, lg_ref):
    y = jnp.dot(a_ref[...], w_ref[...], preferred_element_type=F32)
    xm = x_ref[...] + g1_ref[0] * y
    xo_ref[...] = xm
    h2 = _modulated_rms(xm, gain_ref[...], sc_ref[0], sh_ref[0])
    tm = xm.shape[0]
    packed = _pack_bf16_pairs(h2)
    for j in range(ROW_TILES):
        hp_ref[pl.ds(j, tm, stride=ROW_TILES), :] = (
            packed[:, j * LANES:(j + 1) * LANES] if j < PACKED_TILES else jnp.zeros((tm, LANES), jnp.uint32))
    lg_ref[...] = lax.dot_general(wr_ref[...], h2, (((1,), (1,)), ((), ())),
                                  precision=HIGHEST, preferred_element_type=F32)


def _outproj_norm_router(a, w, x, g1, gain, sc1p, sh, w_router_t, seq):
    n, d = x.shape
    tm = _tile(512, seq)
    per_b = seq // tm
    bmap = lambda i: (i // per_b, 0, 0)
    return pl.pallas_call(
        _outproj_kernel,
        grid=(n // tm,),
        in_specs=[
            pl.BlockSpec((tm, a.shape[1]), lambda i: (i, 0)),
            pl.BlockSpec(w.shape, lambda i: (0, 0)),
            pl.BlockSpec((tm, d), lambda i: (i, 0)),
            pl.BlockSpec((1, 1, d), bmap),
            pl.BlockSpec((1, d), lambda i: (0, 0)),
            pl.BlockSpec((1, 1, d), bmap),
            pl.BlockSpec((1, 1, d), bmap),
            pl.BlockSpec((N_EXPERTS, d), lambda i: (0, 0)),
        ],
        out_specs=[
            pl.BlockSpec((tm, d), lambda i: (i, 0)),
            pl.BlockSpec((tm * ROW_TILES, LANES), lambda i: (i, 0)),
            pl.BlockSpec((N_EXPERTS, tm), lambda i: (0, i)),
        ],
        out_shape=[
            jax.ShapeDtypeStruct((n, d), F32),
            jax.ShapeDtypeStruct((n * ROW_TILES, LANES), jnp.uint32),
            jax.ShapeDtypeStruct((N_EXPERTS, n), F32),
        ],
        compiler_params=_params(("parallel",)),
        name="outproj_norm_router",
    )(a, w, x, g1, gain, sc1p, sh, w_router_t)


PAIR_LO = (0, 0, 0, 1, 1, 2)
PAIR_HI = (1, 2, 3, 2, 3, 3)
PAIRS_PER_GROUP = len(PAIR_LO)
N_CLASSES = N_GROUPS * PAIRS_PER_GROUP
CLASS_ROWS = 32
RANK_BITS = 20
RANK_SPAN = 1 << RANK_BITS


def _route_kernel(lg_ref, bias_ref, oi_ref, cnt_ref, carry_ref, tri_ref):
    tr = lg_ref.shape[1]
    step = pl.program_id(0)

    @pl.when(step == 0)
    def _():
        carry_ref[...] = jnp.zeros_like(carry_ref)
        r = lax.broadcasted_iota(jnp.int32, (tr, tr), 0)
        c = lax.broadcasted_iota(jnp.int32, (tr, tr), 1)
        tri_ref[...] = jnp.where(r < c, 1.0, 0.0).astype(BF16)

    scores = jax.nn.sigmoid(lg_ref[...])
    biased = scores + bias_ref[...]
    rows = [biased[e:e + 1, :] for e in range(N_EXPERTS)]

    def top2_sum(a, b, c, d):
        m1, n1 = jnp.maximum(a, b), jnp.minimum(a, b)
        m2, n2 = jnp.maximum(c, d), jnp.minimum(c, d)
        return jnp.maximum(m1, m2) + jnp.maximum(jnp.minimum(m1, m2), jnp.maximum(n1, n2))

    gscore = [top2_sum(*rows[g * EXPERTS_PER_GROUP:(g + 1) * EXPERTS_PER_GROUP]) for g in range(N_GROUPS)]
    best = gscore[0]
    gsel = jnp.zeros_like(best, dtype=jnp.int32)
    for g in range(1, N_GROUPS):
        upd = gscore[g] > best
        gsel = jnp.where(upd, g, gsel)
        best = jnp.where(upd, gscore[g], best)

    def pick(table, j):
        out = table[j]
        for g in range(1, N_GROUPS):
            out = jnp.where(gsel == g, table[g * EXPERTS_PER_GROUP + j], out)
        return out

    in_b = [pick(rows, j) for j in range(EXPERTS_PER_GROUP)]
    v1, i1 = in_b[0], jnp.zeros_like(gsel)
    for j in range(1, EXPERTS_PER_GROUP):
        upd = in_b[j] > v1
        v1 = jnp.where(upd, in_b[j], v1)
        i1 = jnp.where(upd, j, i1)
    v2 = jnp.full_like(v1, -jnp.inf)
    i2 = jnp.zeros_like(gsel)
    for j in range(EXPERTS_PER_GROUP):
        upd = (i1 != j) & (in_b[j] > v2)
        v2 = jnp.where(upd, in_b[j], v2)
        i2 = jnp.where(upd, j, i2)
    first_is_lo = i1 < i2
    lo = jnp.where(first_is_lo, i1, i2)
    hi = jnp.where(first_is_lo, i2, i1)
    pair = jnp.where(lo == 0, hi - 1, jnp.where(lo == 1, hi + 1, PAIRS_PER_GROUP - 1))
    cls = gsel * PAIRS_PER_GROUP + pair

    cid = lax.broadcasted_iota(jnp.int32, (CLASS_ROWS, tr), 0)
    onehot = jnp.where(cid == cls, 1.0, 0.0)
    before = jnp.dot(onehot.astype(BF16), tri_ref[...], preferred_element_type=F32) + carry_ref[...]
    rank = jnp.sum(onehot * before, axis=0, keepdims=True).astype(jnp.int32)
    oi_ref[...] = cls * RANK_SPAN + rank
    new_carry = carry_ref[...] + jnp.sum(onehot, axis=1, keepdims=True)
    carry_ref[...] = new_carry
    cnt_ref[...] = new_carry.astype(jnp.int32)


def _route(logits_t, router_bias):
    n = logits_t.shape[1]
    tr = _tile(512, n)
    return pl.pallas_call(
        _route_kernel,
        grid=(n // tr,),
        in_specs=[
            pl.BlockSpec((N_EXPERTS, tr), lambda i: (0, i)),
            pl.BlockSpec((N_EXPERTS, 1), lambda i: (0, 0)),
        ],
        out_specs=[
            pl.BlockSpec((1, tr), lambda i: (0, i)),
            pl.BlockSpec((CLASS_ROWS, 1), lambda i: (0, 0)),
        ],
        out_shape=[
            jax.ShapeDtypeStruct((1, n), jnp.int32),
            jax.ShapeDtypeStruct((CLASS_ROWS, 1), jnp.int32),
        ],
        scratch_shapes=[pltpu.VMEM((CLASS_ROWS, 1), F32), pltpu.VMEM((tr, tr), BF16)],
        compiler_params=_params(("arbitrary",)),
        name="route",
    )(logits_t, router_bias.reshape(N_EXPERTS, 1).astype(F32))


SORT_UNROLL = 8
MOE_ROWS = 256
MOE_ROWS_LOG2 = MOE_ROWS.bit_length() - 1
assert 1 << MOE_ROWS_LOG2 == MOE_ROWS


def _plan_kernel(code_ref, cnt_ref, tok_ref, ea_ref, eb_ref, off_ref, nv_ref, nu_ref, start_ref):
    n_blocks = ea_ref.shape[0]
    run = jnp.int32(0)
    blk = jnp.int32(0)
    for c in range(N_CLASSES):
        cnt = cnt_ref[c]
        start_ref[c] = run - c * RANK_SPAN
        e_lo = (c // PAIRS_PER_GROUP) * EXPERTS_PER_GROUP + PAIR_LO[c % PAIRS_PER_GROUP]
        e_hi = (c // PAIRS_PER_GROUP) * EXPERTS_PER_GROUP + PAIR_HI[c % PAIRS_PER_GROUP]
        n_blk = (cnt + (MOE_ROWS - 1)) >> MOE_ROWS_LOG2

        def fill(b, carry, run=run, blk=blk, cnt=cnt, e_lo=e_lo, e_hi=e_hi):
            ea_ref[blk + b] = e_lo
            eb_ref[blk + b] = e_hi
            off_ref[blk + b] = run + b * MOE_ROWS
            nv_ref[blk + b] = jnp.minimum(cnt - b * MOE_ROWS, MOE_ROWS)
            return carry

        lax.fori_loop(0, n_blk, fill, 0)
        run = run + cnt
        blk = blk + n_blk
    nu_ref[0] = blk

    def fill_unused(b, carry):
        ea_ref[b] = ea_ref[blk - 1]
        eb_ref[b] = eb_ref[blk - 1]
        off_ref[b] = 0
        nv_ref[b] = 0
        return carry

    lax.fori_loop(blk, n_blocks, fill_unused, 0)

    def place(i, carry):
        toks = [i * SORT_UNROLL + u for u in range(SORT_UNROLL)]
        codes = [code_ref[t] for t in toks]
        slots = [start_ref[code >> RANK_BITS] + code for code in codes]
        for t, p in zip(toks, slots):
            tok_ref[p] = t
        return carry

    lax.fori_loop(0, code_ref.shape[0] // SORT_UNROLL, place, 0)


def _plan(codes, counts, n_blocks):
    n = codes.shape[0]
    assert n % SORT_UNROLL == 0 and n <= RANK_SPAN
    smem = pl.BlockSpec(memory_space=pltpu.SMEM)
    i32 = lambda size: jax.ShapeDtypeStruct((size,), jnp.int32)
    return pl.pallas_call(
        _plan_kernel,
        in_specs=[smem, smem],
        out_specs=[smem] * 6,
        out_shape=[i32(n), i32(n_blocks), i32(n_blocks), i32(n_blocks), i32(n_blocks), i32(1)],
        scratch_shapes=[pltpu.SMEM((N_CLASSES,), jnp.int32)],
        name="moe_plan",
    )(codes, counts)


ROW_UNROLL = 8


def _for_rows(n, fn):
    groups = n // ROW_UNROLL

    def group(g, c):
        for u in range(ROW_UNROLL):
            fn(g * ROW_UNROLL + u)
        return c

    def single(r, c):
        fn(r)
        return c

    lax.fori_loop(0, groups, group, 0)
    lax.fori_loop(groups * ROW_UNROLL, n, single, 0)


def _row_copy(src_ref, src_row, dst_ref, dst_row, sem):
    return pltpu.make_async_copy(
        src_ref.at[pl.ds(pl.multiple_of(src_row * ROW_TILES, ROW_TILES), ROW_TILES)],
        dst_ref.at[pl.ds(pl.multiple_of(dst_row * ROW_TILES, ROW_TILES), ROW_TILES)],
        sem)


def _expert_kernel(ea_ref, eb_ref, off_ref, nv_ref, nu_ref, tok_ref,
                   h_hbm, wgu_a_ref, wd_a_ref, wgu_b_ref, wd_b_ref, wr_a_ref, wr_b_ref, y_hbm,
                   xbuf, ybuf, gsem, ssem):
    del ea_ref, eb_ref
    tb = xbuf.shape[1] // ROW_TILES
    i = pl.program_id(0)
    slot = i % 2
    n_used = nu_ref[0]

    def start_gather(block, s):
        base = off_ref[block]
        _for_rows(nv_ref[block],
                  lambda r: _row_copy(h_hbm, tok_ref[base + r], xbuf.at[s], r, gsem.at[s]).start())

    def start_scatter(block, s):
        base = off_ref[block]
        _for_rows(nv_ref[block],
                  lambda r: _row_copy(ybuf.at[s], r, y_hbm, tok_ref[base + r], ssem.at[s]).start())

    def wait_rows(block, hbm, buf, sem):
        rows = nv_ref[block] * ROW_TILES

        @pl.when(rows > 0)
        def _():
            pltpu.make_async_copy(hbm.at[pl.ds(0, rows)], buf.at[pl.ds(0, rows)], sem).wait()

    @pl.when(i == 0)
    def _():
        xbuf[...] = jnp.zeros_like(xbuf)
        start_gather(0, 0)

    @pl.when(i < n_used)
    def _():
        @pl.when(i + 1 < n_used)
        def _():
            start_gather(i + 1, 1 - slot)

        wait_rows(i, h_hbm, xbuf.at[slot], gsem.at[slot])

        @pl.when(i >= 2)
        def _():
            wait_rows(i - 2, y_hbm, ybuf.at[slot], ssem.at[slot])

        words = [xbuf[slot, pl.ds(j, tb, stride=ROW_TILES), :] for j in range(PACKED_TILES)]
        halves = [_unpack_bf16_pairs(w) for w in words]
        x = jnp.concatenate([h[0] for h in halves] + [h[1] for h in halves], axis=1)

        def mlp(wgu_ref, wd_ref):
            gu = jnp.dot(x, wgu_ref[0], preferred_element_type=F32)
            gate = gu[:, :D_EXPERT]
            act = gate * jax.nn.sigmoid(gate) * gu[:, D_EXPERT:]
            return jnp.dot(act.astype(BF16), wd_ref[0], preferred_element_type=F32)

        score_a = jax.nn.sigmoid(jnp.dot(x, wr_a_ref[0], preferred_element_type=F32))
        score_b = jax.nn.sigmoid(jnp.dot(x, wr_b_ref[0], preferred_element_type=F32))
        inv_tot = 1.0 / (score_a + score_b)
        w_a = score_a * inv_tot
        w_b = score_b * inv_tot
        ya = mlp(wgu_a_ref, wd_a_ref)
        yb = mlp(wgu_b_ref, wd_b_ref)
        for j in range(ROW_TILES):
            sl = slice(j * LANES, (j + 1) * LANES)
            ybuf[slot, pl.ds(j, tb, stride=ROW_TILES), :] = w_a * ya[:, sl] + w_b * yb[:, sl]
        start_scatter(i, slot)

        @pl.when(i == n_used - 1)
        def _():
            wait_rows(i, y_hbm, ybuf.at[slot], ssem.at[slot])

            @pl.when(i >= 1)
            def _():
                wait_rows(i - 1, y_hbm, ybuf.at[1 - slot], ssem.at[1 - slot])


def _experts(tables, sorted_tok, h_rows, layer, w_gu, w_down, wr_bcast, tb):
    block_ea, block_eb, src_off, n_valid, n_used = tables
    n_blocks = block_ea.shape[0]
    d, two_f = w_gu.shape[2:]
    ea_map = lambda i, ea, eb, off, nv, nu, tok: (ea[i], 0, 0)
    eb_map = lambda i, ea, eb, off, nv, nu, tok: (eb[i], 0, 0)
    lea_map = lambda i, ea, eb, off, nv, nu, tok: (layer, ea[i], 0, 0)
    leb_map = lambda i, ea, eb, off, nv, nu, tok: (layer, eb[i], 0, 0)
    grid_spec = pltpu.PrefetchScalarGridSpec(
        num_scalar_prefetch=6,
        grid=(n_blocks,),
        in_specs=[
            pl.BlockSpec(memory_space=pl.ANY),
            pl.BlockSpec((None, 1, d, two_f), lea_map),
            pl.BlockSpec((None, 1, two_f // 2, d), lea_map),
            pl.BlockSpec((None, 1, d, two_f), leb_map),
            pl.BlockSpec((None, 1, two_f // 2, d), leb_map),
            pl.BlockSpec((1, d, LANES), ea_map),
            pl.BlockSpec((1, d, LANES), eb_map),
        ],
        out_specs=pl.BlockSpec(memory_space=pl.ANY),
        scratch_shapes=[
            pltpu.VMEM((2, tb * ROW_TILES, LANES), jnp.uint32),
            pltpu.VMEM((2, tb * ROW_TILES, LANES), F32),
            pltpu.SemaphoreType.DMA((2,)),
            pltpu.SemaphoreType.DMA((2,)),
        ],
    )
    return pl.pallas_call(
        _expert_kernel,
        grid_spec=grid_spec,
        out_shape=jax.ShapeDtypeStruct(h_rows.shape, F32),
        compiler_params=_params(("arbitrary",)),
        name="moe_experts",
    )(block_ea, block_eb, src_off, n_valid, n_used, sorted_tok, h_rows, w_gu, w_down, w_gu, w_down, wr_bcast, wr_bcast)


def _residual_kernel(y_ref, x_ref, g2_ref, o_ref):
    o_ref[...] = _gated_residual(x_ref, y_ref, g2_ref)


def _residual(y_rows, x_mid, g2, seq):
    n, d = x_mid.shape
    tc = _tile(512, seq)
    per_b = seq // tc
    return pl.pallas_call(
        _residual_kernel,
        grid=(n // tc,),
        in_specs=[
            pl.BlockSpec((tc * ROW_TILES, LANES), lambda i: (i, 0)),
            pl.BlockSpec((tc, d), lambda i: (i, 0)),
            pl.BlockSpec((1, 1, d), lambda i: (i // per_b, 0, 0)),
        ],
        out_specs=pl.BlockSpec((tc, d), lambda i: (i, 0)),
        out_shape=jax.ShapeDtypeStruct((n, d), F32),
        compiler_params=_params(("parallel",)),
        name="moe_residual",
    )(y_rows, x_mid, g2)


def _moe(h_rows, logits_t, router_bias, layer, w_gu, w_down, wr_bcast):
    n = logits_t.shape[1]
    n_blocks = (n + N_CLASSES * (MOE_ROWS - 1) + MOE_ROWS - 1) // MOE_ROWS
    codes, counts = _route(logits_t, router_bias)
    sorted_tok, *tables = _plan(codes.reshape(n), counts.reshape(CLASS_ROWS), n_blocks)
    return _experts(tables, sorted_tok, h_rows, layer, w_gu, w_down, wr_bcast, MOE_ROWS)


def kernel(x, c, positions, norm1, norm2, w_ada, b_ada, a_w_in, a_b_if, a_h_norm, a_w_out, b_w_in, b_q_norm, b_k_norm, b_lam_q1, b_lam_k1, b_lam_q2, b_lam_k2, b_o_norm, b_w_out, w_router, router_bias, moe_w_gu, moe_w_down):
    bsz, seq, d = x.shape
    depth = w_ada.shape[0]
    n = bsz * seq
    xf = x.reshape(n, d)
    mod = _ada_mod(c, w_ada, b_ada)
    w_router_t = w_router.T
    wr_bcast = jnp.broadcast_to(w_router_t[:, :, None], (N_EXPERTS, d, LANES)).astype(BF16)
    pos_row = positions.reshape(n)
    w_gu = moe_w_gu.astype(BF16)
    w_down = moe_w_down.astype(BF16)

    residual = None
    for l in range(depth):
        sh1, sc1, g1, sh2, sc2, g2 = [mod[l, :, i * d:(i + 1) * d].reshape(bsz, 1, d) for i in range(6)]
        j = l // 2
        if l % 2 == 0:
            w_in = a_w_in[j]
            w_main = w_in[:, :A_MAIN_COLS].astype(BF16)
            w_gate = jnp.pad(w_in[:, A_MAIN_COLS:], ((0, 0), (0, LANES - 2 * A_HEADS))).astype(BF16)
        else:
            w_in = b_w_in[j]
            w_qk = _interleave_maps(w_in[:, :2 * D_MODEL])
            w_main, w_gate = jnp.concatenate([w_qk, w_in[:, 2 * D_MODEL:]], axis=1).astype(BF16), None
        outs = _norm_matmul(xf, residual, norm1[l].reshape(1, d), 1.0 + sc1, sh1, w_main, w_gate, seq)
        if residual is not None:
            xf, *outs = outs
        if l % 2 == 0:
            proj, gates = outs
            bias_row = jnp.pad(a_b_if[j], (0, LANES - 2 * A_HEADS)).reshape(1, LANES)
            mixed = _mlstm(proj, gates, bias_row, a_h_norm[j], bsz, seq)
            w_out = a_w_out[j].astype(BF16)
        else:
            (proj,) = outs
            qk = _qk_prep(proj, pos_row, b_q_norm[j], b_k_norm[j], seq)
            lam_tab = jnp.zeros((SUBLANES, LANES), F32)
            for r, v in enumerate((b_lam_q1[j], b_lam_k1[j], b_lam_q2[j], b_lam_k2[j])):
                lam_tab = lam_tab.at[r, :B_HEAD_DIM].set(v)
            lam_init = 0.8 - 0.6 * math.exp(-0.3 * l)
            mixed = _diff_attn(qk, proj, lam_tab, b_o_norm[j], bsz, seq, lam_init)
            w_out = b_w_out[j].astype(BF16)
        xf, h_rows, logits_t = _outproj_norm_router(
            mixed, w_out, xf, g1, norm2[l].reshape(1, d), 1.0 + sc2, sh2, w_router_t, seq)
        residual = (_moe(h_rows, logits_t, router_bias, l, w_gu, w_down, wr_bcast), g2)
    return _residual(residual[0], xf, residual[1], seq).reshape(bsz, seq, d)
```

```python
import functools
import math

import jax
import jax.numpy as jnp
from jax import lax
from jax.experimental import pallas as pl
from jax.experimental.pallas import tpu as pltpu

D_MODEL = 1024
A_HEADS = 4
A_QK_DIM = 128
A_V_DIM = 256
A_CHUNK = 128
A_HQ = A_HEADS * A_QK_DIM
A_HV = A_HEADS * A_V_DIM
A_MAIN_COLS = 2 * A_HQ + 2 * A_HV

B_HEADS = 8
B_HEAD_DIM = 64
B_V_DIM = 128
B_ROT_DIM = 16
ROPE_THETA = 500000.0

N_EXPERTS = 16
N_GROUPS = 4
EXPERTS_PER_GROUP = 4
TOP_K = 2
D_EXPERT = 512
EPS = 1e-6

LANES = 128
SUBLANES = 8
ROW_TILES = D_MODEL // LANES
assert ROW_TILES == SUBLANES
VMEM_LIMIT = 48 * 1024 * 1024

F32 = jnp.float32
BF16 = jnp.bfloat16
HIGHEST = lax.Precision.HIGHEST


def _params(sem):
    return pltpu.CompilerParams(dimension_semantics=sem, vmem_limit_bytes=VMEM_LIMIT)


def _tile(pref, n):
    t = min(pref, n)
    assert n % t == 0, (pref, n)
    return t


def _ada_kernel(c_ref, w_ref, b_ref, o_ref):
    c = c_ref[...]
    c_act = c * jax.nn.sigmoid(c)
    o_ref[0] = jnp.dot(c_act, w_ref[0], precision=HIGHEST, preferred_element_type=F32) + b_ref[0]


def _ada_mod(c, w_ada, b_ada):
    depth, d, six_d = w_ada.shape
    bsz = c.shape[0]
    tn = _tile(1536, six_d)
    return pl.pallas_call(
        _ada_kernel,
        grid=(depth, six_d // tn),
        in_specs=[
            pl.BlockSpec((bsz, d), lambda l, j: (0, 0)),
            pl.BlockSpec((1, d, tn), lambda l, j: (l, 0, j)),
            pl.BlockSpec((1, 1, tn), lambda l, j: (l, 0, j)),
        ],
        out_specs=pl.BlockSpec((1, bsz, tn), lambda l, j: (l, 0, j)),
        out_shape=jax.ShapeDtypeStruct((depth, bsz, six_d), F32),
        compiler_params=_params(("parallel", "parallel")),
        name="ada_mod",
    )(c, w_ada, b_ada.reshape(depth, 1, six_d))


def _modulated_rms(x, g, sc1p, sh):
    y = x * lax.rsqrt(jnp.mean(x * x, axis=-1, keepdims=True) + EPS)
    return (y * g) * sc1p + sh


def _gated_residual(x_ref, y_ref, g_ref):
    rows = x_ref.shape[0]
    g = g_ref[0]
    return jnp.concatenate(
        [x_ref[:, j * LANES:(j + 1) * LANES]
         + g[:, j * LANES:(j + 1) * LANES] * y_ref[pl.ds(j, rows, stride=ROW_TILES), :] for j in range(ROW_TILES)],
        axis=1)


def _norm_mm_kernel(*refs, col_chunk, has_gates, has_residual):
    refs = list(refs)
    x_ref = refs.pop(0)
    if has_residual:
        y_ref, g2_ref = refs.pop(0), refs.pop(0)
    g_ref, sc_ref, sh_ref, w_ref = refs[:4]
    refs = refs[4:]
    wg_ref = refs.pop(0) if has_gates else None
    xo_ref = refs.pop(0) if has_residual else None
    o_ref = refs.pop(0)
    og_ref = refs.pop(0) if has_gates else None

    if has_residual:
        x = _gated_residual(x_ref, y_ref, g2_ref)
        xo_ref[...] = x
    else:
        x = x_ref[...]
    hb = _modulated_rms(x, g_ref[...], sc_ref[0], sh_ref[0]).astype(BF16)
    for c0 in range(0, o_ref.shape[1], col_chunk):
        o_ref[:, c0:c0 + col_chunk] = jnp.dot(
            hb, w_ref[:, c0:c0 + col_chunk], preferred_element_type=F32).astype(o_ref.dtype)
    if has_gates:
        og_ref[...] = jnp.dot(hb, wg_ref[...], preferred_element_type=F32)


def _norm_matmul(x, residual, gain, sc1p, sh, w, wg, seq):
    n, d = x.shape
    cols = w.shape[1]
    tm = _tile(512, seq)
    per_b = seq // tm
    has_gates = wg is not None
    has_residual = residual is not None
    bmap = lambda i: (i // per_b, 0, 0)
    in_specs = [pl.BlockSpec((tm, d), lambda i: (i, 0))]
    args = [x]
    if has_residual:
        in_specs += [pl.BlockSpec((tm * ROW_TILES, LANES), lambda i: (i, 0)), pl.BlockSpec((1, 1, d), bmap)]
        args += list(residual)
    in_specs += [
        pl.BlockSpec((1, d), lambda i: (0, 0)),
        pl.BlockSpec((1, 1, d), bmap),
        pl.BlockSpec((1, 1, d), bmap),
        pl.BlockSpec((d, cols), lambda i: (0, 0)),
    ]
    args += [gain, sc1p, sh, w]
    out_specs, out_shape = [], []
    if has_residual:
        out_specs.append(pl.BlockSpec((tm, d), lambda i: (i, 0)))
        out_shape.append(jax.ShapeDtypeStruct((n, d), F32))
    out_specs.append(pl.BlockSpec((tm, cols), lambda i: (i, 0)))
    out_shape.append(jax.ShapeDtypeStruct((n, cols), BF16))
    if has_gates:
        in_specs.append(pl.BlockSpec((d, LANES), lambda i: (0, 0)))
        out_specs.append(pl.BlockSpec((tm, LANES), lambda i: (i, 0)))
        out_shape.append(jax.ShapeDtypeStruct((n, LANES), F32))
        args.append(wg)
    return pl.pallas_call(
        functools.partial(_norm_mm_kernel, col_chunk=512, has_gates=has_gates, has_residual=has_residual),
        grid=(n // tm,),
        in_specs=in_specs,
        out_specs=out_specs,
        out_shape=out_shape,
        compiler_params=_params(("parallel",)),
        name="norm_inproj",
    )(*args)


def _log_sigmoid(x):
    return jnp.minimum(x, 0.0) - jnp.log1p(jnp.exp(-jnp.abs(x)))


def _mlstm_chunk(r0, p_ref, g_ref, bias_ref, hn_ref, o_ref, ct_ref, n_ref, m_ref):
    L = A_CHUNK
    rows = slice(r0, r0 + L)
    gates = g_ref[rows, :] + bias_ref[...]
    src = lax.broadcasted_iota(jnp.int32, (L, L), 0)
    tgt = lax.broadcasted_iota(jnp.int32, (L, L), 1)
    causal = src <= tgt
    bcum = jnp.dot(jnp.where(src >= tgt, 1.0, 0.0), _log_sigmoid(gates), precision=HIGHEST,
                   preferred_element_type=F32)
    bcum_t = bcum.T
    nt = (((1,), (1,)), ((), ()))

    pre = []
    for h in range(A_HEADS):
        q = p_ref[rows, h * A_QK_DIM:(h + 1) * A_QK_DIM]
        ksf = p_ref[rows, A_HQ + h * A_QK_DIM:A_HQ + (h + 1) * A_QK_DIM].astype(F32) * (A_QK_DIM ** -0.5)
        v = p_ref[rows, 2 * A_HQ + h * A_V_DIM:2 * A_HQ + (h + 1) * A_V_DIM]
        ct_old = ct_ref[h]
        n_old = n_ref[h]
        qk = lax.dot_general(ksf.astype(BF16), q, nt, preferred_element_type=F32)
        qc = lax.dot_general(ct_old.astype(BF16), q, nt, preferred_element_type=F32)
        qn = lax.dot_general(n_old.astype(BF16), q, nt, preferred_element_type=F32)[0:1]
        vt = v.astype(F32).T.astype(BF16)
        pre.append((ksf, vt, ct_old, n_old, qk, qc, qn))

    for h in range(A_HEADS):
        ksf, vt, ct_old, n_old, qk, qc, qn = pre[h]
        og = p_ref[rows, 2 * A_HQ + A_HV + h * A_V_DIM:2 * A_HQ + A_HV + (h + 1) * A_V_DIM].astype(F32)
        b_row = bcum_t[A_HEADS + h:A_HEADS + h + 1, :]
        b_last = b_row[:, L - 1:L]
        c_col = gates[:, h:h + 1] - bcum[:, A_HEADS + h:A_HEADS + h + 1]
        m11 = m_ref[h][:, 0:1]

        log_d = jnp.where(causal, c_col + b_row, -jnp.inf)
        log_inter = b_row + m11
        m_t = jnp.maximum(jnp.max(log_d, axis=0, keepdims=True), log_inter)
        dmat = jnp.exp(log_d - m_t)
        inter = jnp.exp(log_inter - m_t)
        s = qk * dmat
        num = jnp.dot(vt, s.astype(BF16), preferred_element_type=F32) + inter * qc
        den = jnp.sum(s, axis=0, keepdims=True) + inter * qn
        hh = num / jnp.maximum(jnp.abs(den), jnp.exp(-m_t))

        lw_col = b_last + c_col
        m_new = jnp.maximum(b_last + m11, jnp.max(lw_col, axis=0, keepdims=True))
        kw = ksf * jnp.exp(lw_col - m_new)
        decay = jnp.exp(b_last + m11 - m_new)
        ct_ref[h] = decay * ct_old + jnp.dot(vt, kw.astype(BF16), preferred_element_type=F32)
        n_ref[h] = decay * n_old + jnp.broadcast_to(jnp.sum(kw, axis=0, keepdims=True), n_old.shape)
        m_ref[h] = jnp.broadcast_to(m_new, (1, LANES))

        hn = hh * lax.rsqrt(jnp.mean(hh * hh, axis=0, keepdims=True) + EPS)
        hn = (hn * hn_ref[h * A_V_DIM:(h + 1) * A_V_DIM, :]).T
        o_ref[rows, h * A_V_DIM:(h + 1) * A_V_DIM] = (hn * jax.nn.sigmoid(og)).astype(o_ref.dtype)


MLSTM_CHUNKS_PER_STEP = 2


def _mlstm_kernel(p_ref, g_ref, bias_ref, hn_ref, o_ref, ct_ref, n_ref, m_ref):
    @pl.when(pl.program_id(1) == 0)
    def _():
        ct_ref[...] = jnp.zeros_like(ct_ref)
        n_ref[...] = jnp.zeros_like(n_ref)
        m_ref[...] = jnp.zeros_like(m_ref)

    for r0 in range(0, p_ref.shape[0], A_CHUNK):
        _mlstm_chunk(r0, p_ref, g_ref, bias_ref, hn_ref, o_ref, ct_ref, n_ref, m_ref)


def _mlstm(proj, gates, bias_row, h_norm, bsz, seq):
    n = proj.shape[0]
    rows = _tile(MLSTM_CHUNKS_PER_STEP * A_CHUNK, seq)
    nc = seq // rows
    h_norm_cols = jnp.broadcast_to(h_norm[:, None], (A_HV, A_CHUNK))
    return pl.pallas_call(
        _mlstm_kernel,
        grid=(bsz, nc),
        in_specs=[
            pl.BlockSpec((rows, A_MAIN_COLS), lambda b, c: (b * nc + c, 0)),
            pl.BlockSpec((rows, LANES), lambda b, c: (b * nc + c, 0)),
            pl.BlockSpec((1, LANES), lambda b, c: (0, 0)),
            pl.BlockSpec((A_HV, A_CHUNK), lambda b, c: (0, 0)),
        ],
        out_specs=pl.BlockSpec((rows, A_HV), lambda b, c: (b * nc + c, 0)),
        out_shape=jax.ShapeDtypeStruct((n, A_HV), BF16),
        scratch_shapes=[
            pltpu.VMEM((A_HEADS, A_V_DIM, A_QK_DIM), F32),
            pltpu.VMEM((A_HEADS, SUBLANES, A_QK_DIM), F32),
            pltpu.VMEM((A_HEADS, 1, LANES), F32),
        ],
        compiler_params=_params(("parallel", "arbitrary")),
        name="mlstm",
    )(proj, gates, bias_row, h_norm_cols)


def _qk_prep_kernel(p_ref, pos_ref, seg_ref, segt_ref, gain_ref, freq_ref, o_ref):
    tm = p_ref.shape[0]
    x = p_ref[...].astype(F32)
    ss = jnp.dot((x * x).astype(BF16), seg_ref[...], preferred_element_type=F32)
    r = lax.rsqrt(ss * (1.0 / B_HEAD_DIM) + EPS)
    r_hi = r.astype(BF16)
    r_lo = (r - r_hi.astype(F32)).astype(BF16)
    r_full = jnp.dot(jnp.concatenate([r_hi, r_lo], axis=1), segt_ref[...], preferred_element_type=F32)

    assert B_ROT_DIM // 2 == SUBLANES
    ang = freq_ref[:, 0:1] * pos_ref[0].astype(F32)
    cos8 = jnp.cos(ang)
    sin8 = jnp.sin(ang)
    one8 = jnp.ones_like(cos8)
    zero8 = jnp.zeros_like(cos8)
    groups_per_quarter = MAP_QUARTER // SUBLANES

    def chunk_pattern(lower, upper, rest):
        quarter = lambda lead: [lead] + [rest] * (groups_per_quarter - 1)
        return jnp.concatenate(quarter(lower) * 2 + quarter(upper) * 2, axis=0).T

    c_mul = chunk_pattern(cos8, cos8, one8)
    s_mul = chunk_pattern(-sin8, sin8, zero8)
    for j in range(2 * D_MODEL // LANES):
        sl = slice(j * LANES, (j + 1) * LANES)
        y = x[:, sl] * r_full[:, sl] * gain_ref[:, sl]
        out = y * c_mul + pltpu.roll(y, LANES // 2, axis=1) * s_mul
        if j < D_MODEL // LANES:
            out = out * (B_HEAD_DIM ** -0.5 * math.log2(math.e))
        o_ref[:, sl] = out.astype(o_ref.dtype)


MAP_QUARTER = B_HEAD_DIM // 2


def _interleave_maps(t):
    half = B_ROT_DIM // 2
    rest = MAP_QUARTER - half
    lead = t.shape[:-1]
    t = t.reshape(*lead, 2 * B_HEADS, 2, B_HEAD_DIM)
    lower = jnp.concatenate([t[..., 0:half], t[..., B_ROT_DIM:B_ROT_DIM + rest]], axis=-1)
    upper = jnp.concatenate([t[..., half:B_ROT_DIM], t[..., B_ROT_DIM + rest:]], axis=-1)
    out = jnp.stack([lower, upper], axis=-3)
    return out.reshape(*lead, 2 * D_MODEL)


def _qk_prep(proj, positions_row, q_norm, k_norm, seq):
    n = proj.shape[0]
    two_d = 2 * D_MODEL
    tm = _tile(256, seq)
    seg_of_col = _interleave_maps(jnp.arange(two_d) // B_HEAD_DIM)
    seg = (seg_of_col[:, None] == jnp.arange(LANES)[None, :]).astype(BF16)
    gain = _interleave_maps(
        jnp.concatenate([jnp.tile(q_norm, D_MODEL // B_HEAD_DIM), jnp.tile(k_norm, D_MODEL // B_HEAD_DIM)]))
    inv_freq = ROPE_THETA ** (-jnp.arange(0, B_ROT_DIM, 2, dtype=F32) / B_ROT_DIM)
    freq_tab = jnp.broadcast_to(inv_freq[:, None], (B_ROT_DIM // 2, LANES))
    return pl.pallas_call(
        _qk_prep_kernel,
        grid=(n // tm,),
        in_specs=[
            pl.BlockSpec((tm, two_d), lambda i: (i, 0)),
            pl.BlockSpec((1, 1, tm), lambda i: (i, 0, 0)),
            pl.BlockSpec((two_d, LANES), lambda i: (0, 0)),
            pl.BlockSpec((2 * LANES, two_d), lambda i: (0, 0)),
            pl.BlockSpec((1, two_d), lambda i: (0, 0)),
            pl.BlockSpec((SUBLANES, LANES), lambda i: (0, 0)),
        ],
        out_specs=pl.BlockSpec((tm, two_d), lambda i: (i, 0)),
        out_shape=jax.ShapeDtypeStruct((n, two_d), BF16),
        compiler_params=_params(("parallel",)),
        name="qk_prep",
    )(proj, positions_row.reshape(n // tm, 1, tm), seg, jnp.concatenate([seg.T, seg.T], axis=0),
      gain.reshape(1, two_d), freq_tab)


ATTN_TILE = 512
ATTN_KEY_TILE = 256


def _diff_attn_kernel(q_ref, k_ref, v_ref, lam_ref, on_ref, o_ref, vt_ref, qm_ref, m_ref, l_ref, acc_ref,
                      *, tile, tk, lam_init):
    seq = q_ref.shape[0]
    n_tiles = seq // tile
    k_per_q = tile // tk
    lamv = lam_ref[...]
    lam = (jnp.exp(jnp.sum(lamv[0:1] * lamv[1:2], axis=1, keepdims=True))
           - jnp.exp(jnp.sum(lamv[2:3] * lamv[3:4], axis=1, keepdims=True)) + lam_init)
    tchunk = min(256, seq)
    for c in range(seq // tchunk):
        vt_ref[:, c * tchunk:(c + 1) * tchunk] = v_ref[c * tchunk:(c + 1) * tchunk, :].astype(F32).T.astype(BF16)
    first_map = (lax.broadcasted_iota(jnp.int32, (tile, LANES), 1) // MAP_QUARTER) % 2 == 0
    key_id = lax.broadcasted_iota(jnp.int32, (tk, tile), 0)
    query_id = lax.broadcasted_iota(jnp.int32, (tk, tile), 1)
    nt = (((1,), (1,)), ((), ()))

    def qk_score(qi, ki, c):
        return lax.dot_general(k_ref[ki * tk:(ki + 1) * tk, :], qm_ref[2 * qi + c], nt,
                               preferred_element_type=F32)

    def softmax_pv(c, s, ki, mask):
        vt = vt_ref[:, ki * tk:(ki + 1) * tk]
        if mask is not None:
            s = jnp.where(mask, s, -jnp.inf)
        m_old = m_ref[c]
        m_new = jnp.maximum(m_old, jnp.max(s, axis=0, keepdims=True))
        alpha = jnp.exp2(m_old - m_new)
        p = jnp.exp2(s - m_new)
        l_ref[c] = alpha * l_ref[c] + jnp.sum(p, axis=0, keepdims=True)
        acc_ref[c] = alpha * acc_ref[c] + jnp.dot(vt, p.astype(BF16), preferred_element_type=F32)
        m_ref[c] = m_new

    def reset():
        m_ref[...] = jnp.full(m_ref.shape, -jnp.inf, F32)
        l_ref[...] = jnp.zeros(l_ref.shape, F32)
        acc_ref[...] = jnp.zeros(acc_ref.shape, F32)

    for qi in range(n_tiles):
        q = q_ref[qi * tile:(qi + 1) * tile, :]
        qm_ref[2 * qi] = jnp.where(first_map, q, jnp.zeros_like(q))
        qm_ref[2 * qi + 1] = jnp.where(first_map, jnp.zeros_like(q), q)
    steps = [(qi, ki) for qi in range(n_tiles) for ki in range((qi + 1) * k_per_q)]
    reset()
    scores_next = [qk_score(*steps[0], c) for c in range(2)]
    for n, (qi, ki) in enumerate(steps):
        scores = scores_next
        scores_next = [None, None]
        on_diagonal = ki >= qi * k_per_q
        mask = (query_id + qi * tile >= key_id + ki * tk) if on_diagonal else None
        for c in range(2):
            if n + 1 < len(steps):
                scores_next[c] = qk_score(*steps[n + 1], c)
            softmax_pv(c, scores[c], ki, mask)
        if ki == (qi + 1) * k_per_q - 1:
            o = acc_ref[0] / l_ref[0] - lam * (acc_ref[1] / l_ref[1])
            o = o * lax.rsqrt(jnp.mean(o * o, axis=0, keepdims=True) + EPS) * on_ref[...] * (1.0 - lam_init)
            o_ref[qi * tile:(qi + 1) * tile, :] = o.T.astype(o_ref.dtype)
            if n + 1 < len(steps):
                reset()


def _diff_attn(qk, proj, lam_tab, o_norm, bsz, seq, lam_init):
    n = qk.shape[0]
    tile = _tile(ATTN_TILE, seq)
    o_norm_cols = jnp.broadcast_to(o_norm[:, None], (B_V_DIM, tile))
    return pl.pallas_call(
        functools.partial(_diff_attn_kernel, tile=tile, tk=_tile(ATTN_KEY_TILE, tile), lam_init=lam_init),
        grid=(bsz, B_HEADS),
        in_specs=[
            pl.BlockSpec((seq, LANES), lambda b, h: (b, h)),
            pl.BlockSpec((seq, LANES), lambda b, h: (b, B_HEADS + h)),
            pl.BlockSpec((seq, B_V_DIM), lambda b, h: (b, 2 * B_HEADS + h)),
            pl.BlockSpec((SUBLANES, LANES), lambda b, h: (0, 0)),
            pl.BlockSpec((B_V_DIM, tile), lambda b, h: (0, 0)),
        ],
        out_specs=pl.BlockSpec((seq, B_V_DIM), lambda b, h: (b, h)),
        out_shape=jax.ShapeDtypeStruct((n, D_MODEL), BF16),
        scratch_shapes=[
            pltpu.VMEM((B_V_DIM, seq), BF16),
            pltpu.VMEM((2 * (seq // tile), tile, LANES), BF16),
            pltpu.VMEM((2, 1, tile), F32),
            pltpu.VMEM((2, 1, tile), F32),
            pltpu.VMEM((2, B_V_DIM, tile), F32),
        ],
        compiler_params=_params(("parallel", "parallel")),
        name="diff_attn",
    )(qk, qk, proj, lam_tab, o_norm_cols)


HALF_D = D_MODEL // 2
HI16 = 0xFFFF0000
PACKED_TILES = HALF_D // LANES


def _pack_bf16_pairs(v):
    bits = lax.bitcast_convert_type(v.astype(BF16).astype(F32), jnp.uint32)
    return (bits[:, HALF_D:] & jnp.uint32(HI16)) | (bits[:, :HALF_D] >> 16)


def _unpack_bf16_pairs(words):
    first = lax.bitcast_convert_type(words << 16, F32)
    second = lax.bitcast_convert_type(words & jnp.uint32(HI16), F32)
    return first.astype(BF16), second.astype(BF16)


def _outproj_kernel(a_ref, w_ref, x_ref, g1_ref, gain_ref, sc_ref, sh_ref, wr_ref, xo_ref, hp_ref, lg_ref):
    y = jnp.dot(a_ref[...], w_ref[...], preferred_element_type=F32)
    xm = x_ref[...] + g1_ref[0] * y
    xo_ref[...] = xm
    h2 = _modulated_rms(xm, gain_ref[...], sc_ref[0], sh_ref[0])
    tm = xm.shape[0]
    packed = _pack_bf16_pairs(h2)
    for j in range(ROW_TILES):
        hp_ref[pl.ds(j, tm, stride=ROW_TILES), :] = (
            packed[:, j * LANES:(j + 1) * LANES] if j < PACKED_TILES else jnp.zeros((tm, LANES), jnp.uint32))
    lg_ref[...] = lax.dot_general(wr_ref[...], h2, (((1,), (1,)), ((), ())),
                                  precision=HIGHEST, preferred_element_type=F32)


def _outproj_norm_router(a, w, x, g1, gain, sc1p, sh, w_router_t, seq):
    n, d = x.shape
    tm = _tile(512, seq)
    per_b = seq // tm
    bmap = lambda i: (i // per_b, 0, 0)
    return pl.pallas_call(
        _outproj_kernel,
        grid=(n // tm,),
        in_specs=[
            pl.BlockSpec((tm, a.shape[1]), lambda i: (i, 0)),
            pl.BlockSpec(w.shape, lambda i: (0, 0)),
            pl.BlockSpec((tm, d), lambda i: (i, 0)),
            pl.BlockSpec((1, 1, d), bmap),
            pl.BlockSpec((1, d), lambda i: (0, 0)),
            pl.BlockSpec((1, 1, d), bmap),
            pl.BlockSpec((1, 1, d), bmap),
            pl.BlockSpec((N_EXPERTS, d), lambda i: (0, 0)),
        ],
        out_specs=[
            pl.BlockSpec((tm, d), lambda i: (i, 0)),
            pl.BlockSpec((tm * ROW_TILES, LANES), lambda i: (i, 0)),
            pl.BlockSpec((N_EXPERTS, tm), lambda i: (0, i)),
        ],
        out_shape=[
            jax.ShapeDtypeStruct((n, d), F32),
            jax.ShapeDtypeStruct((n * ROW_TILES, LANES), jnp.uint32),
            jax.ShapeDtypeStruct((N_EXPERTS, n), F32),
        ],
        compiler_params=_params(("parallel",)),
        name="outproj_norm_router",
    )(a, w, x, g1, gain, sc1p, sh, w_router_t)


PAIR_LO = (0, 0, 0, 1, 1, 2)
PAIR_HI = (1, 2, 3, 2, 3, 3)
PAIRS_PER_GROUP = len(PAIR_LO)
N_CLASSES = N_GROUPS * PAIRS_PER_GROUP
CLASS_ROWS = 32
RANK_BITS = 20
RANK_SPAN = 1 << RANK_BITS


def _route_kernel(lg_ref, bias_ref, oi_ref, cnt_ref, carry_ref, tri_ref):
    tr = lg_ref.shape[1]
    step = pl.program_id(0)

    @pl.when(step == 0)
    def _():
        carry_ref[...] = jnp.zeros_like(carry_ref)
        r = lax.broadcasted_iota(jnp.int32, (tr, tr), 0)
        c = lax.broadcasted_iota(jnp.int32, (tr, tr), 1)
        tri_ref[...] = jnp.where(r < c, 1.0, 0.0).astype(BF16)

    scores = jax.nn.sigmoid(lg_ref[...])
    biased = scores + bias_ref[...]
    rows = [biased[e:e + 1, :] for e in range(N_EXPERTS)]

    def top2_sum(a, b, c, d):
        m1, n1 = jnp.maximum(a, b), jnp.minimum(a, b)
        m2, n2 = jnp.maximum(c, d), jnp.minimum(c, d)
        return jnp.maximum(m1, m2) + jnp.maximum(jnp.minimum(m1, m2), jnp.maximum(n1, n2))

    gscore = [top2_sum(*rows[g * EXPERTS_PER_GROUP:(g + 1) * EXPERTS_PER_GROUP]) for g in range(N_GROUPS)]
    best = gscore[0]
    gsel = jnp.zeros_like(best, dtype=jnp.int32)
    for g in range(1, N_GROUPS):
        upd = gscore[g] > best
        gsel = jnp.where(upd, g, gsel)
        best = jnp.where(upd, gscore[g], best)

    def pick(table, j):
        out = table[j]
        for g in range(1, N_GROUPS):
            out = jnp.where(gsel == g, table[g * EXPERTS_PER_GROUP + j], out)
        return out

    in_b = [pick(rows, j) for j in range(EXPERTS_PER_GROUP)]
    v1, i1 = in_b[0], jnp.zeros_like(gsel)
    for j in range(1, EXPERTS_PER_GROUP):
        upd = in_b[j] > v1
        v1 = jnp.where(upd, in_b[j], v1)
        i1 = jnp.where(upd, j, i1)
    v2 = jnp.full_like(v1, -jnp.inf)
    i2 = jnp.zeros_like(gsel)
    for j in range(EXPERTS_PER_GROUP):
        upd = (i1 != j) & (in_b[j] > v2)
        v2 = jnp.where(upd, in_b[j], v2)
        i2 = jnp.where(upd, j, i2)
    first_is_lo = i1 < i2
    lo = jnp.where(first_is_lo, i1, i2)
    hi = jnp.where(first_is_lo, i2, i1)
    pair = jnp.where(lo == 0, hi - 1, jnp.where(lo == 1, hi + 1, PAIRS_PER_GROUP - 1))
    cls = gsel * PAIRS_PER_GROUP + pair

    cid = lax.broadcasted_iota(jnp.int32, (CLASS_ROWS, tr), 0)
    onehot = jnp.where(cid == cls, 1.0, 0.0)
    before = jnp.dot(onehot.astype(BF16), tri_ref[...], preferred_element_type=F32) + carry_ref[...]
    rank = jnp.sum(onehot * before, axis=0, keepdims=True).astype(jnp.int32)
    oi_ref[...] = cls * RANK_SPAN + rank
    new_carry = carry_ref[...] + jnp.sum(onehot, axis=1, keepdims=True)
    carry_ref[...] = new_carry
    cnt_ref[...] = new_carry.astype(jnp.int32)


def _route(logits_t, router_bias):
    n = logits_t.shape[1]
    tr = _tile(512, n)
    return pl.pallas_call(
        _route_kernel,
        grid=(n // tr,),
        in_specs=[
            pl.BlockSpec((N_EXPERTS, tr), lambda i: (0, i)),
            pl.BlockSpec((N_EXPERTS, 1), lambda i: (0, 0)),
        ],
        out_specs=[
            pl.BlockSpec((1, tr), lambda i: (0, i)),
            pl.BlockSpec((CLASS_ROWS, 1), lambda i: (0, 0)),
        ],
        out_shape=[
            jax.ShapeDtypeStruct((1, n), jnp.int32),
            jax.ShapeDtypeStruct((CLASS_ROWS, 1), jnp.int32),
        ],
        scratch_shapes=[pltpu.VMEM((CLASS_ROWS, 1), F32), pltpu.VMEM((tr, tr), BF16)],
        compiler_params=_params(("arbitrary",)),
        name="route",
    )(logits_t, router_bias.reshape(N_EXPERTS, 1).astype(F32))


SORT_UNROLL = 8
MOE_ROWS = 256
MOE_ROWS_LOG2 = MOE_ROWS.bit_length() - 1
assert 1 << MOE_ROWS_LOG2 == MOE_ROWS


def _plan_kernel(code_ref, cnt_ref, tok_ref, ea_ref, eb_ref, off_ref, nv_ref, nu_ref, start_ref):
    n_blocks = ea_ref.shape[0]
    run = jnp.int32(0)
    blk = jnp.int32(0)
    for c in range(N_CLASSES):
        cnt = cnt_ref[c]
        start_ref[c] = run - c * RANK_SPAN
        e_lo = (c // PAIRS_PER_GROUP) * EXPERTS_PER_GROUP + PAIR_LO[c % PAIRS_PER_GROUP]
        e_hi = (c // PAIRS_PER_GROUP) * EXPERTS_PER_GROUP + PAIR_HI[c % PAIRS_PER_GROUP]
        n_blk = (cnt + (MOE_ROWS - 1)) >> MOE_ROWS_LOG2

        def fill(b, carry, run=run, blk=blk, cnt=cnt, e_lo=e_lo, e_hi=e_hi):
            ea_ref[blk + b] = e_lo
            eb_ref[blk + b] = e_hi
            off_ref[blk + b] = run + b * MOE_ROWS
            nv_ref[blk + b] = jnp.minimum(cnt - b * MOE_ROWS, MOE_ROWS)
            return carry

        lax.fori_loop(0, n_blk, fill, 0)
        run = run + cnt
        blk = blk + n_blk
    nu_ref[0] = blk

    def fill_unused(b, carry):
        ea_ref[b] = ea_ref[blk - 1]
        eb_ref[b] = eb_ref[blk - 1]
        off_ref[b] = 0
        nv_ref[b] = 0
        return carry

    lax.fori_loop(blk, n_blocks, fill_unused, 0)

    def place(i, carry):
        toks = [i * SORT_UNROLL + u for u in range(SORT_UNROLL)]
        codes = [code_ref[t] for t in toks]
        slots = [start_ref[code >> RANK_BITS] + code for code in codes]
        for t, p in zip(toks, slots):
            tok_ref[p] = t
        return carry

    lax.fori_loop(0, code_ref.shape[0] // SORT_UNROLL, place, 0)


def _plan(codes, counts, n_blocks):
    n = codes.shape[0]
    assert n % SORT_UNROLL == 0 and n <= RANK_SPAN
    smem = pl.BlockSpec(memory_space=pltpu.SMEM)
    i32 = lambda size: jax.ShapeDtypeStruct((size,), jnp.int32)
    return pl.pallas_call(
        _plan_kernel,
        in_specs=[smem, smem],
        out_specs=[smem] * 6,
        out_shape=[i32(n), i32(n_blocks), i32(n_blocks), i32(n_blocks), i32(n_blocks), i32(1)],
        scratch_shapes=[pltpu.SMEM((N_CLASSES,), jnp.int32)],
        name="moe_plan",
    )(codes, counts)


ROW_UNROLL = 8


def _for_rows(n, fn):
    groups = n // ROW_UNROLL

    def group(g, c):
        for u in range(ROW_UNROLL):
            fn(g * ROW_UNROLL + u)
        return c

    def single(r, c):
        fn(r)
        return c

    lax.fori_loop(0, groups, group, 0)
    lax.fori_loop(groups * ROW_UNROLL, n, single, 0)


def _row_copy(src_ref, src_row, dst_ref, dst_row, sem):
    return pltpu.make_async_copy(
        src_ref.at[pl.ds(pl.multiple_of(src_row * ROW_TILES, ROW_TILES), ROW_TILES)],
        dst_ref.at[pl.ds(pl.multiple_of(dst_row * ROW_TILES, ROW_TILES), ROW_TILES)],
        sem)


def _expert_kernel(ea_ref, eb_ref, off_ref, nv_ref, nu_ref, tok_ref,
                   h_hbm, wgu_a_ref, wd_a_ref, wgu_b_ref, wd_b_ref, wr_a_ref, wr_b_ref, y_hbm,
                   xbuf, ybuf, gsem, ssem):
    del ea_ref, eb_ref
    tb = xbuf.shape[1] // ROW_TILES
    i = pl.program_id(0)
    slot = i % 2
    n_used = nu_ref[0]

    def start_gather(block, s):
        base = off_ref[block]
        _for_rows(nv_ref[block],
                  lambda r: _row_copy(h_hbm, tok_ref[base + r], xbuf.at[s], r, gsem.at[s]).start())

    def start_scatter(block, s):
        base = off_ref[block]
        _for_rows(nv_ref[block],
                  lambda r: _row_copy(ybuf.at[s], r, y_hbm, tok_ref[base + r], ssem.at[s]).start())

    def wait_rows(block, hbm, buf, sem):
        rows = nv_ref[block] * ROW_TILES

        @pl.when(rows > 0)
        def _():
            pltpu.make_async_copy(hbm.at[pl.ds(0, rows)], buf.at[pl.ds(0, rows)], sem).wait()

    @pl.when(i == 0)
    def _():
        xbuf[...] = jnp.zeros_like(xbuf)
        start_gather(0, 0)

    @pl.when(i < n_used)
    def _():
        @pl.when(i + 1 < n_used)
        def _():
            start_gather(i + 1, 1 - slot)

        wait_rows(i, h_hbm, xbuf.at[slot], gsem.at[slot])

        @pl.when(i >= 2)
        def _():
            wait_rows(i - 2, y_hbm, ybuf.at[slot], ssem.at[slot])

        words = [xbuf[slot, pl.ds(j, tb, stride=ROW_TILES), :] for j in range(PACKED_TILES)]
        halves = [_unpack_bf16_pairs(w) for w in words]
        x = jnp.concatenate([h[0] for h in halves] + [h[1] for h in halves], axis=1)

        def mlp(wgu_ref, wd_ref):
            gu = jnp.dot(x, wgu_ref[0], preferred_element_type=F32)
            gate = gu[:, :D_EXPERT]
            act = gate * jax.nn.sigmoid(gate) * gu[:, D_EXPERT:]
            return jnp.dot(act.astype(BF16), wd_ref[0], preferred_element_type=F32)

        score_a = jax.nn.sigmoid(jnp.dot(x, wr_a_ref[0], preferred_element_type=F32))
        score_b = jax.nn.sigmoid(jnp.dot(x, wr_b_ref[0], preferred_element_type=F32))
        inv_tot = 1.0 / (score_a + score_b)
        w_a = score_a * inv_tot
        w_b = score_b * inv_tot
        ya = mlp(wgu_a_ref, wd_a_ref)
        yb = mlp(wgu_b_ref, wd_b_ref)
        for j in range(ROW_TILES):
            sl = slice(j * LANES, (j + 1) * LANES)
            ybuf[slot, pl.ds(j, tb, stride=ROW_TILES), :] = w_a * ya[:, sl] + w_b * yb[:, sl]
        start_scatter(i, slot)

        @pl.when(i == n_used - 1)
        def _():
            wait_rows(i, y_hbm, ybuf.at[slot], ssem.at[slot])

            @pl.when(i >= 1)
            def _():
                wait_rows(i - 1, y_hbm, ybuf.at[1 - slot], ssem.at[1 - slot])


def _experts(tables, sorted_tok, h_rows, layer, w_gu, w_down, wr_bcast, tb):
    block_ea, block_eb, src_off, n_valid, n_used = tables
    n_blocks = block_ea.shape[0]
    d, two_f = w_gu.shape[2:]
    ea_map = lambda i, ea, eb, off, nv, nu, tok: (ea[i], 0, 0)
    eb_map = lambda i, ea, eb, off, nv, nu, tok: (eb[i], 0, 0)
    lea_map = lambda i, ea, eb, off, nv, nu, tok: (layer, ea[i], 0, 0)
    leb_map = lambda i, ea, eb, off, nv, nu, tok: (layer, eb[i], 0, 0)
    grid_spec = pltpu.PrefetchScalarGridSpec(
        num_scalar_prefetch=6,
        grid=(n_blocks,),
        in_specs=[
            pl.BlockSpec(memory_space=pl.ANY),
            pl.BlockSpec((None, 1, d, two_f), lea_map),
            pl.BlockSpec((None, 1, two_f // 2, d), lea_map),
            pl.BlockSpec((None, 1, d, two_f), leb_map),
            pl.BlockSpec((None, 1, two_f // 2, d), leb_map),
            pl.BlockSpec((1, d, LANES), ea_map),
            pl.BlockSpec((1, d, LANES), eb_map),
        ],
        out_specs=pl.BlockSpec(memory_space=pl.ANY),
        scratch_shapes=[
            pltpu.VMEM((2, tb * ROW_TILES, LANES), jnp.uint32),
            pltpu.VMEM((2, tb * ROW_TILES, LANES), F32),
            pltpu.SemaphoreType.DMA((2,)),
            pltpu.SemaphoreType.DMA((2,)),
        ],
    )
    return pl.pallas_call(
        _expert_kernel,
        grid_spec=grid_spec,
        out_shape=jax.ShapeDtypeStruct(h_rows.shape, F32),
        compiler_params=_params(("arbitrary",)),
        name="moe_experts",
    )(block_ea, block_eb, src_off, n_valid, n_used, sorted_tok, h_rows, w_gu, w_down, w_gu, w_down, wr_bcast, wr_bcast)


def _residual_kernel(y_ref, x_ref, g2_ref, o_ref):
    o_ref[...] = _gated_residual(x_ref, y_ref, g2_ref)


def _residual(y_rows, x_mid, g2, seq):
    n, d = x_mid.shape
    tc = _tile(512, seq)
    per_b = seq // tc
    return pl.pallas_call(
        _residual_kernel,
        grid=(n // tc,),
        in_specs=[
            pl.BlockSpec((tc * ROW_TILES, LANES), lambda i: (i, 0)),
            pl.BlockSpec((tc, d), lambda i: (i, 0)),
            pl.BlockSpec((1, 1, d), lambda i: (i // per_b, 0, 0)),
        ],
        out_specs=pl.BlockSpec((tc, d), lambda i: (i, 0)),
        out_shape=jax.ShapeDtypeStruct((n, d), F32),
        compiler_params=_params(("parallel",)),
        name="moe_residual",
    )(y_rows, x_mid, g2)


def _moe(h_rows, logits_t, router_bias, layer, w_gu, w_down, wr_bcast):
    n = logits_t.shape[1]
    n_blocks = (n + N_CLASSES * (MOE_ROWS - 1) + MOE_ROWS - 1) // MOE_ROWS
    codes, counts = _route(logits_t, router_bias)
    sorted_tok, *tables = _plan(codes.reshape(n), counts.reshape(CLASS_ROWS), n_blocks)
    return _experts(tables, sorted_tok, h_rows, layer, w_gu, w_down, wr_bcast, MOE_ROWS)


def kernel(x, c, positions, norm1, norm2, w_ada, b_ada, a_w_in, a_b_if, a_h_norm, a_w_out, b_w_in, b_q_norm, b_k_norm, b_lam_q1, b_lam_k1, b_lam_q2, b_lam_k2, b_o_norm, b_w_out, w_router, router_bias, moe_w_gu, moe_w_down):
    bsz, seq, d = x.shape
    depth = w_ada.shape[0]
    n = bsz * seq
    xf = x.reshape(n, d)
    mod = _ada_mod(c, w_ada, b_ada)
    w_router_t = w_router.T
    wr_bcast = jnp.broadcast_to(w_router_t[:, :, None], (N_EXPERTS, d, LANES)).astype(BF16)
    pos_row = positions.reshape(n)
    w_gu = moe_w_gu.astype(BF16)
    w_down = moe_w_down.astype(BF16)

    residual = None
    for l in range(depth):
        sh1, sc1, g1, sh2, sc2, g2 = [mod[l, :, i * d:(i + 1) * d].reshape(bsz, 1, d) for i in range(6)]
        j = l // 2
        if l % 2 == 0:
            w_in = a_w_in[j]
            w_main = w_in[:, :A_MAIN_COLS].astype(BF16)
            w_gate = jnp.pad(w_in[:, A_MAIN_COLS:], ((0, 0), (0, LANES - 2 * A_HEADS))).astype(BF16)
        else:
            w_in = b_w_in[j]
            w_qk = _interleave_maps(w_in[:, :2 * D_MODEL])
            w_main, w_gate = jnp.concatenate([w_qk, w_in[:, 2 * D_MODEL:]], axis=1).astype(BF16), None
        outs = _norm_matmul(xf, residual, norm1[l].reshape(1, d), 1.0 + sc1, sh1, w_main, w_gate, seq)
        if residual is not None:
            xf, *outs = outs
        if l % 2 == 0:
            proj, gates = outs
            bias_row = jnp.pad(a_b_if[j], (0, LANES - 2 * A_HEADS)).reshape(1, LANES)
            mixed = _mlstm(proj, gates, bias_row, a_h_norm[j], bsz, seq)
            w_out = a_w_out[j].astype(BF16)
        else:
            (proj,) = outs
            qk = _qk_prep(proj, pos_row, b_q_norm[j], b_k_norm[j], seq)
            lam_tab = jnp.zeros((SUBLANES, LANES), F32)
            for r, v in enumerate((b_lam_q1[j], b_lam_k1[j], b_lam_q2[j], b_lam_k2[j])):
                lam_tab = lam_tab.at[r, :B_HEAD_DIM].set(v)
            lam_init = 0.8 - 0.6 * math.exp(-0.3 * l)
            mixed = _diff_attn(qk, proj, lam_tab, b_o_norm[j], bsz, seq, lam_init)
            w_out = b_w_out[j].astype(BF16)
        xf, h_rows, logits_t = _outproj_norm_router(
            mixed, w_out, xf, g1, norm2[l].reshape(1, d), 1.0 + sc2, sh2, w_router_t, seq)
        residual = (_moe(h_rows, logits_t, router_bias, l, w_gu, w_down, wr_bcast), g2)
    return _residual(residual[0], xf, residual[1], seq).reshape(bsz, seq, d)
```

```python
import functools
import math

import jax
import jax.numpy as jnp
from jax import lax
from jax.experimental import pallas as pl
from jax.experimental.pallas import tpu as pltpu

D_MODEL = 1024
A_HEADS = 4
A_QK_DIM = 128
A_V_DIM = 256
A_CHUNK = 128
A_HQ = A_HEADS * A_QK_DIM
A_HV = A_HEADS * A_V_DIM
A_MAIN_COLS = 2 * A_HQ + 2 * A_HV

B_HEADS = 8
B_HEAD_DIM = 64
B_V_DIM = 128
B_ROT_DIM = 16
ROPE_THETA = 500000.0

N_EXPERTS = 16
N_GROUPS = 4
EXPERTS_PER_GROUP = 4
TOP_K = 2
D_EXPERT = 512
EPS = 1e-6

LANES = 128
SUBLANES = 8
ROW_TILES = D_MODEL // LANES
assert ROW_TILES == SUBLANES
VMEM_LIMIT = 48 * 1024 * 1024

F32 = jnp.float32
BF16 = jnp.bfloat16
HIGHEST = lax.Precision.HIGHEST


def _params(sem):
    return pltpu.CompilerParams(dimension_semantics=sem, vmem_limit_bytes=VMEM_LIMIT)


def _tile(pref, n):
    t = min(pref, n)
    assert n % t == 0, (pref, n)
    return t


def _ada_kernel(c_ref, w_ref, b_ref, o_ref):
    c = c_ref[...]
    c_act = c * jax.nn.sigmoid(c)
    o_ref[0] = jnp.dot(c_act, w_ref[0], precision=HIGHEST, preferred_element_type=F32) + b_ref[0]


def _ada_mod(c, w_ada, b_ada):
    depth, d, six_d = w_ada.shape
    bsz = c.shape[0]
    tn = _tile(1536, six_d)
    return pl.pallas_call(
        _ada_kernel,
        grid=(depth, six_d // tn),
        in_specs=[
            pl.BlockSpec((bsz, d), lambda l, j: (0, 0)),
            pl.BlockSpec((1, d, tn), lambda l, j: (l, 0, j)),
            pl.BlockSpec((1, 1, tn), lambda l, j: (l, 0, j)),
        ],
        out_specs=pl.BlockSpec((1, bsz, tn), lambda l, j: (l, 0, j)),
        out_shape=jax.ShapeDtypeStruct((depth, bsz, six_d), F32),
        compiler_params=_params(("parallel", "parallel")),
        name="ada_mod",
    )(c, w_ada, b_ada.reshape(depth, 1, six_d))


def _modulated_rms(x, g, sc1p, sh):
    y = x * lax.rsqrt(jnp.mean(x * x, axis=-1, keepdims=True) + EPS)
    return (y * g) * sc1p + sh


def _gated_residual(x_ref, y_ref, g_ref):
    rows = x_ref.shape[0]
    g = g_ref[0]
    return jnp.concatenate(
        [x_ref[:, j * LANES:(j + 1) * LANES]
         + g[:, j * LANES:(j + 1) * LANES] * y_ref[pl.ds(j, rows, stride=ROW_TILES), :] for j in range(ROW_TILES)],
        axis=1)


def _norm_mm_kernel(*refs, col_chunk, has_gates, has_residual):
    refs = list(refs)
    x_ref = refs.pop(0)
    if has_residual:
        y_ref, g2_ref = refs.pop(0), refs.pop(0)
    g_ref, sc_ref, sh_ref, w_ref = refs[:4]
    refs = refs[4:]
    wg_ref = refs.pop(0) if has_gates else None
    xo_ref = refs.pop(0) if has_residual else None
    o_ref = refs.pop(0)
    og_ref = refs.pop(0) if has_gates else None

    if has_residual:
        x = _gated_residual(x_ref, y_ref, g2_ref)
        xo_ref[...] = x
    else:
        x = x_ref[...]
    hb = _modulated_rms(x, g_ref[...], sc_ref[0], sh_ref[0]).astype(BF16)
    for c0 in range(0, o_ref.shape[1], col_chunk):
        o_ref[:, c0:c0 + col_chunk] = jnp.dot(
            hb, w_ref[:, c0:c0 + col_chunk], preferred_element_type=F32).astype(o_ref.dtype)
    if has_gates:
        og_ref[...] = jnp.dot(hb, wg_ref[...], preferred_element_type=F32)


def _norm_matmul(x, residual, gain, sc1p, sh, w, wg, seq):
    n, d = x.shape
    cols = w.shape[1]
    tm = _tile(512, seq)
    per_b = seq // tm
    has_gates = wg is not None
    has_residual = residual is not None
    bmap = lambda i: (i // per_b, 0, 0)
    in_specs = [pl.BlockSpec((tm, d), lambda i: (i, 0))]
    args = [x]
    if has_residual:
        in_specs += [pl.BlockSpec((tm * ROW_TILES, LANES), lambda i: (i, 0)), pl.BlockSpec((1, 1, d), bmap)]
        args += list(residual)
    in_specs += [
        pl.BlockSpec((1, d), lambda i: (0, 0)),
        pl.BlockSpec((1, 1, d), bmap),
        pl.BlockSpec((1, 1, d), bmap),
        pl.BlockSpec((d, cols), lambda i: (0, 0)),
    ]
    args += [gain, sc1p, sh, w]
    out_specs, out_shape = [], []
    if has_residual:
        out_specs.append(pl.BlockSpec((tm, d), lambda i: (i, 0)))
        out_shape.append(jax.ShapeDtypeStruct((n, d), F32))
    out_specs.append(pl.BlockSpec((tm, cols), lambda i: (i, 0)))
    out_shape.append(jax.ShapeDtypeStruct((n, cols), BF16))
    if has_gates:
        in_specs.append(pl.BlockSpec((d, LANES), lambda i: (0, 0)))
        out_specs.append(pl.BlockSpec((tm, LANES), lambda i: (i, 0)))
        out_shape.append(jax.ShapeDtypeStruct((n, LANES), F32))
        args.append(wg)
    return pl.pallas_call(
        functools.partial(_norm_mm_kernel, col_chunk=512, has_gates=has_gates, has_residual=has_residual),
        grid=(n // tm,),
        in_specs=in_specs,
        out_specs=out_specs,
        out_shape=out_shape,
        compiler_params=_params(("parallel",)),
        name="norm_inproj",
    )(*args)


def _log_sigmoid(x):
    return jnp.minimum(x, 0.0) - jnp.log1p(jnp.exp(-jnp.abs(x)))


def _mlstm_chunk(r0, p_ref, g_ref, bias_ref, hn_ref, o_ref, ct_ref, n_ref, m_ref):
    L = A_CHUNK
    rows = slice(r0, r0 + L)
    gates = g_ref[rows, :] + bias_ref[...]
    src = lax.broadcasted_iota(jnp.int32, (L, L), 0)
    tgt = lax.broadcasted_iota(jnp.int32, (L, L), 1)
    causal = src <= tgt
    bcum = jnp.dot(jnp.where(src >= tgt, 1.0, 0.0), _log_sigmoid(gates), precision=HIGHEST,
                   preferred_element_type=F32)
    bcum_t = bcum.T
    nt = (((1,), (1,)), ((), ()))

    pre = []
    for h in range(A_HEADS):
        q = p_ref[rows, h * A_QK_DIM:(h + 1) * A_QK_DIM]
        ksf = p_ref[rows, A_HQ + h * A_QK_DIM:A_HQ + (h + 1) * A_QK_DIM].astype(F32) * (A_QK_DIM ** -0.5)
        v = p_ref[rows, 2 * A_HQ + h * A_V_DIM:2 * A_HQ + (h + 1) * A_V_DIM]
        ct_old = ct_ref[h]
        n_old = n_ref[h]
        qk = lax.dot_general(ksf.astype(BF16), q, nt, preferred_element_type=F32)
        qc = lax.dot_general(ct_old.astype(BF16), q, nt, preferred_element_type=F32)
        qn = lax.dot_general(n_old.astype(BF16), q, nt, preferred_element_type=F32)[0:1]
        vt = v.astype(F32).T.astype(BF16)
        pre.append((ksf, vt, ct_old, n_old, qk, qc, qn))

    for h in range(A_HEADS):
        ksf, vt, ct_old, n_old, qk, qc, qn = pre[h]
        og = p_ref[rows, 2 * A_HQ + A_HV + h * A_V_DIM:2 * A_HQ + A_HV + (h + 1) * A_V_DIM].astype(F32)
        b_row = bcum_t[A_HEADS + h:A_HEADS + h + 1, :]
        b_last = b_row[:, L - 1:L]
        c_col = gates[:, h:h + 1] - bcum[:, A_HEADS + h:A_HEADS + h + 1]
        m11 = m_ref[h][:, 0:1]

        log_d = jnp.where(causal, c_col + b_row, -jnp.inf)
        log_inter = b_row + m11
        m_t = jnp.maximum(jnp.max(log_d, axis=0, keepdims=True), log_inter)
        dmat = jnp.exp(log_d - m_t)
        inter = jnp.exp(log_inter - m_t)
        s = qk * dmat
        num = jnp.dot(vt, s.astype(BF16), preferred_element_type=F32) + inter * qc
        den = jnp.sum(s, axis=0, keepdims=True) + inter * qn
        hh = num / jnp.maximum(jnp.abs(den), jnp.exp(-m_t))

        lw_col = b_last + c_col
        m_new = jnp.maximum(b_last + m11, jnp.max(lw_col, axis=0, keepdims=True))
        kw = ksf * jnp.exp(lw_col - m_new)
        decay = jnp.exp(b_last + m11 - m_new)
        ct_ref[h] = decay * ct_old + jnp.dot(vt, kw.astype(BF16), preferred_element_type=F32)
        n_ref[h] = decay * n_old + jnp.broadcast_to(jnp.sum(kw, axis=0, keepdims=True), n_old.shape)
        m_ref[h] = jnp.broadcast_to(m_new, (1, LANES))

        hn = hh * lax.rsqrt(jnp.mean(hh * hh, axis=0, keepdims=True) + EPS)
        hn = (hn * hn_ref[h * A_V_DIM:(h + 1) * A_V_DIM, :]).T
        o_ref[rows, h * A_V_DIM:(h + 1) * A_V_DIM] = (hn * jax.nn.sigmoid(og)).astype(o_ref.dtype)


MLSTM_CHUNKS_PER_STEP = 2


def _mlstm_kernel(p_ref, g_ref, bias_ref, hn_ref, o_ref, ct_ref, n_ref, m_ref):
    @pl.when(pl.program_id(1) == 0)
    def _():
        ct_ref[...] = jnp.zeros_like(ct_ref)
        n_ref[...] = jnp.zeros_like(n_ref)
        m_ref[...] = jnp.zeros_like(m_ref)

    for r0 in range(0, p_ref.shape[0], A_CHUNK):
        _mlstm_chunk(r0, p_ref, g_ref, bias_ref, hn_ref, o_ref, ct_ref, n_ref, m_ref)


def _mlstm(proj, gates, bias_row, h_norm, bsz, seq):
    n = proj.shape[0]
    rows = _tile(MLSTM_CHUNKS_PER_STEP * A_CHUNK, seq)
    nc = seq // rows
    h_norm_cols = jnp.broadcast_to(h_norm[:, None], (A_HV, A_CHUNK))
    return pl.pallas_call(
        _mlstm_kernel,
        grid=(bsz, nc),
        in_specs=[
            pl.BlockSpec((rows, A_MAIN_COLS), lambda b, c: (b * nc + c, 0)),
            pl.BlockSpec((rows, LANES), lambda b, c: (b * nc + c, 0)),
            pl.BlockSpec((1, LANES), lambda b, c: (0, 0)),
            pl.BlockSpec((A_HV, A_CHUNK), lambda b, c: (0, 0)),
        ],
        out_specs=pl.BlockSpec((rows, A_HV), lambda b, c: (b * nc + c, 0)),
        out_shape=jax.ShapeDtypeStruct((n, A_HV), BF16),
        scratch_shapes=[
            pltpu.VMEM((A_HEADS, A_V_DIM, A_QK_DIM), F32),
            pltpu.VMEM((A_HEADS, SUBLANES, A_QK_DIM), F32),
            pltpu.VMEM((A_HEADS, 1, LANES), F32),
        ],
        compiler_params=_params(("parallel", "arbitrary")),
        name="mlstm",
    )(proj, gates, bias_row, h_norm_cols)


def _qk_prep_kernel(p_ref, pos_ref, seg_ref, segt_ref, gain_ref, freq_ref, o_ref):
    tm = p_ref.shape[0]
    x = p_ref[...].astype(F32)
    ss = jnp.dot((x * x).astype(BF16), seg_ref[...], preferred_element_type=F32)
    r = lax.rsqrt(ss * (1.0 / B_HEAD_DIM) + EPS)
    r_hi = r.astype(BF16)
    r_lo = (r - r_hi.astype(F32)).astype(BF16)
    r_full = jnp.dot(jnp.concatenate([r_hi, r_lo], axis=1), segt_ref[...], preferred_element_type=F32)

    assert B_ROT_DIM // 2 == SUBLANES
    ang = freq_ref[:, 0:1] * pos_ref[0].astype(F32)
    cos8 = jnp.cos(ang)
    sin8 = jnp.sin(ang)
    one8 = jnp.ones_like(cos8)
    zero8 = jnp.zeros_like(cos8)
    groups_per_quarter = MAP_QUARTER // SUBLANES

    def chunk_pattern(lower, upper, rest):
        quarter = lambda lead: [lead] + [rest] * (groups_per_quarter - 1)
        return jnp.concatenate(quarter(lower) * 2 + quarter(upper) * 2, axis=0).T

    c_mul = chunk_pattern(cos8, cos8, one8)
    s_mul = chunk_pattern(-sin8, sin8, zero8)
    for j in range(2 * D_MODEL // LANES):
        sl = slice(j * LANES, (j + 1) * LANES)
        y = x[:, sl] * r_full[:, sl] * gain_ref[:, sl]
        out = y * c_mul + pltpu.roll(y, LANES // 2, axis=1) * s_mul
        if j < D_MODEL // LANES:
            out = out * (B_HEAD_DIM ** -0.5 * math.log2(math.e))
        o_ref[:, sl] = out.astype(o_ref.dtype)


MAP_QUARTER = B_HEAD_DIM // 2


def _interleave_maps(t):
    half = B_ROT_DIM // 2
    rest = MAP_QUARTER - half
    lead = t.shape[:-1]
    t = t.reshape(*lead, 2 * B_HEADS, 2, B_HEAD_DIM)
    lower = jnp.concatenate([t[..., 0:half], t[..., B_ROT_DIM:B_ROT_DIM + rest]], axis=-1)
    upper = jnp.concatenate([t[..., half:B_ROT_DIM], t[..., B_ROT_DIM + rest:]], axis=-1)
    out = jnp.stack([lower, upper], axis=-3)
    return out.reshape(*lead, 2 * D_MODEL)


def _qk_prep(proj, positions_row, q_norm, k_norm, seq):
    n = proj.shape[0]
    two_d = 2 * D_MODEL
    tm = _tile(256, seq)
    seg_of_col = _interleave_maps(jnp.arange(two_d) // B_HEAD_DIM)
    seg = (seg_of_col[:, None] == jnp.arange(LANES)[None, :]).astype(BF16)
    gain = _interleave_maps(
        jnp.concatenate([jnp.tile(q_norm, D_MODEL // B_HEAD_DIM), jnp.tile(k_norm, D_MODEL // B_HEAD_DIM)]))
    inv_freq = ROPE_THETA ** (-jnp.arange(0, B_ROT_DIM, 2, dtype=F32) / B_ROT_DIM)
    freq_tab = jnp.broadcast_to(inv_freq[:, None], (B_ROT_DIM // 2, LANES))
    return pl.pallas_call(
        _qk_prep_kernel,
        grid=(n // tm,),
        in_specs=[
            pl.BlockSpec((tm, two_d), lambda i: (i, 0)),
            pl.BlockSpec((1, 1, tm), lambda i: (i, 0, 0)),
            pl.BlockSpec((two_d, LANES), lambda i: (0, 0)),
            pl.BlockSpec((2 * LANES, two_d), lambda i: (0, 0)),
            pl.BlockSpec((1, two_d), lambda i: (0, 0)),
            pl.BlockSpec((SUBLANES, LANES), lambda i: (0, 0)),
        ],
        out_specs=pl.BlockSpec((tm, two_d), lambda i: (i, 0)),
        out_shape=jax.ShapeDtypeStruct((n, two_d), BF16),
        compiler_params=_params(("parallel",)),
        name="qk_prep",
    )(proj, positions_row.reshape(n // tm, 1, tm), seg, jnp.concatenate([seg.T, seg.T], axis=0),
      gain.reshape(1, two_d), freq_tab)


ATTN_TILE = 512
ATTN_KEY_TILE = 256
ATTN_HEADS_PER_STEP = 2


def _diff_attn_kernel(q_ref, k_ref, v_ref, lam_ref, on_ref, o_ref, vt_ref, qm_ref, m_ref, l_ref, acc_ref,
                      *, tile, tk, lam_init):
    seq = q_ref.shape[0]
    heads = q_ref.shape[1] // LANES
    n_tiles = seq // tile
    k_per_q = tile // tk
    lamv = lam_ref[...]
    lam = (jnp.exp(jnp.sum(lamv[0:1] * lamv[1:2], axis=1, keepdims=True))
           - jnp.exp(jnp.sum(lamv[2:3] * lamv[3:4], axis=1, keepdims=True)) + lam_init)
    head_lanes = lambda h: slice(h * LANES, (h + 1) * LANES)
    tchunk = min(256, seq)
    for h in range(heads):
        for c in range(seq // tchunk):
            rows = slice(c * tchunk, (c + 1) * tchunk)
            vt_ref[h, :, rows] = v_ref[rows, head_lanes(h)].astype(F32).T.astype(BF16)
    first_map = (lax.broadcasted_iota(jnp.int32, (tile, LANES), 1) // MAP_QUARTER) % 2 == 0
    key_id = lax.broadcasted_iota(jnp.int32, (tk, tile), 0)
    query_id = lax.broadcasted_iota(jnp.int32, (tk, tile), 1)
    nt = (((1,), (1,)), ((), ()))

    def qk_score(h, qi, ki, c):
        return lax.dot_general(k_ref[ki * tk:(ki + 1) * tk, head_lanes(h)], qm_ref[(h * n_tiles + qi) * 2 + c], nt,
                               preferred_element_type=F32)

    def softmax_pv(h, c, s, ki, mask):
        j = 2 * h + c
        vt = vt_ref[h, :, ki * tk:(ki + 1) * tk]
        if mask is not None:
            s = jnp.where(mask, s, -jnp.inf)
        m_old = m_ref[j]
        m_new = jnp.maximum(m_old, jnp.max(s, axis=0, keepdims=True))
        alpha = jnp.exp2(m_old - m_new)
        p = jnp.exp2(s - m_new)
        l_ref[j] = alpha * l_ref[j] + jnp.sum(p, axis=0, keepdims=True)
        acc_ref[j] = alpha * acc_ref[j] + jnp.dot(vt, p.astype(BF16), preferred_element_type=F32)
        m_ref[j] = m_new

    def reset(h):
        for j in (2 * h, 2 * h + 1):
            m_ref[j] = jnp.full(m_ref.shape[1:], -jnp.inf, F32)
            l_ref[j] = jnp.zeros(l_ref.shape[1:], F32)
            acc_ref[j] = jnp.zeros(acc_ref.shape[1:], F32)

    for h in range(heads):
        for qi in range(n_tiles):
            q = q_ref[qi * tile:(qi + 1) * tile, head_lanes(h)]
            qm_ref[(h * n_tiles + qi) * 2] = jnp.where(first_map, q, jnp.zeros_like(q))
            qm_ref[(h * n_tiles + qi) * 2 + 1] = jnp.where(first_map, jnp.zeros_like(q), q)
        reset(h)
    steps = [(h, qi, ki) for qi in range(n_tiles) for ki in range((qi + 1) * k_per_q) for h in range(heads)]
    scores_next = [qk_score(*steps[0], c) for c in range(2)]
    for n, (h, qi, ki) in enumerate(steps):
        scores = scores_next
        scores_next = [None, None]
        on_diagonal = ki >= qi * k_per_q
        mask = (query_id + qi * tile >= key_id + ki * tk) if on_diagonal else None
        for c in range(2):
            if n + 1 < len(steps):
                scores_next[c] = qk_score(*steps[n + 1], c)
            softmax_pv(h, c, scores[c], ki, mask)
        if ki == (qi + 1) * k_per_q - 1:
            o = acc_ref[2 * h] / l_ref[2 * h] - lam * (acc_ref[2 * h + 1] / l_ref[2 * h + 1])
            o = o * lax.rsqrt(jnp.mean(o * o, axis=0, keepdims=True) + EPS) * on_ref[...] * (1.0 - lam_init)
            o_ref[qi * tile:(qi + 1) * tile, head_lanes(h)] = o.T.astype(o_ref.dtype)
            if qi + 1 < n_tiles:
                reset(h)


def _diff_attn(qk, proj, lam_tab, o_norm, bsz, seq, lam_init):
    n = qk.shape[0]
    hp = ATTN_HEADS_PER_STEP
    groups = B_HEADS // hp
    tile = _tile(ATTN_TILE, seq)
    o_norm_cols = jnp.broadcast_to(o_norm[:, None], (B_V_DIM, tile))
    return pl.pallas_call(
        functools.partial(_diff_attn_kernel, tile=tile, tk=_tile(ATTN_KEY_TILE, tile), lam_init=lam_init),
        grid=(bsz, groups),
        in_specs=[
            pl.BlockSpec((seq, hp * LANES), lambda b, g: (b, g)),
            pl.BlockSpec((seq, hp * LANES), lambda b, g: (b, groups + g)),
            pl.BlockSpec((seq, hp * B_V_DIM), lambda b, g: (b, 2 * groups + g)),
            pl.BlockSpec((SUBLANES, LANES), lambda b, g: (0, 0)),
            pl.BlockSpec((B_V_DIM, tile), lambda b, g: (0, 0)),
        ],
        out_specs=pl.BlockSpec((seq, hp * B_V_DIM), lambda b, g: (b, g)),
        out_shape=jax.ShapeDtypeStruct((n, D_MODEL), BF16),
        scratch_shapes=[
            pltpu.VMEM((hp, B_V_DIM, seq), BF16),
            pltpu.VMEM((hp * 2 * (seq // tile), tile, LANES), BF16),
            pltpu.VMEM((2 * hp, 1, tile), F32),
            pltpu.VMEM((2 * hp, 1, tile), F32),
            pltpu.VMEM((2 * hp, B_V_DIM, tile), F32),
        ],
        compiler_params=_params(("parallel", "parallel")),
        name="diff_attn",
    )(qk, qk, proj, lam_tab, o_norm_cols)


HALF_D = D_MODEL // 2
HI16 = 0xFFFF0000
PACKED_TILES = HALF_D // LANES


def _pack_bf16_pairs(v):
    bits = lax.bitcast_convert_type(v.astype(BF16).astype(F32), jnp.uint32)
    return (bits[:, HALF_D:] & jnp.uint32(HI16)) | (bits[:, :HALF_D] >> 16)


def _unpack_bf16_pairs(words):
    first = lax.bitcast_convert_type(words << 16, F32)
    second = lax.bitcast_convert_type(words & jnp.uint32(HI16), F32)
    return first.astype(BF16), second.astype(BF16)


def _outproj_kernel(a_ref, w_ref, x_ref, g1_ref, gain_ref, sc_ref, sh_ref, wr_ref, xo_ref, hp_ref, lg_ref):
    y = jnp.dot(a_ref[...], w_ref[...], preferred_element_type=F32)
    xm = x_ref[...] + g1_ref[0] * y
    xo_ref[...] = xm
    h2 = _modulated_rms(xm, gain_ref[...], sc_ref[0], sh_ref[0])
    tm = xm.shape[0]
    packed = _pack_bf16_pairs(h2)
    for j in range(ROW_TILES):
        hp_ref[pl.ds(j, tm, stride=ROW_TILES), :] = (
            packed[:, j * LANES:(j + 1) * LANES] if j < PACKED_TILES else jnp.zeros((tm, LANES), jnp.uint32))
    nt = (((1,), (1,)), ((), ()))
    h_hi = h2.astype(BF16)
    h_lo = (h2 - h_hi.astype(F32)).astype(BF16)
    wr = wr_ref[...]
    wr_hi = wr.astype(BF16)
    wr_lo = (wr - wr_hi.astype(F32)).astype(BF16)
    lg_ref[...] = (lax.dot_general(wr_hi, h_hi, nt, preferred_element_type=F32)
                   + lax.dot_general(wr_hi, h_lo, nt, preferred_element_type=F32)
                   + lax.dot_general(wr_lo, h_hi, nt, preferred_element_type=F32))


def _outproj_norm_router(a, w, x, g1, gain, sc1p, sh, w_router_t, seq):
    n, d = x.shape
    tm = _tile(512, seq)
    per_b = seq // tm
    bmap = lambda i: (i // per_b, 0, 0)
    return pl.pallas_call(
        _outproj_kernel,
        grid=(n // tm,),
        in_specs=[
            pl.BlockSpec((tm, a.shape[1]), lambda i: (i, 0)),
            pl.BlockSpec(w.shape, lambda i: (0, 0)),
            pl.BlockSpec((tm, d), lambda i: (i, 0)),
            pl.BlockSpec((1, 1, d), bmap),
            pl.BlockSpec((1, d), lambda i: (0, 0)),
            pl.BlockSpec((1, 1, d), bmap),
            pl.BlockSpec((1, 1, d), bmap),
            pl.BlockSpec((N_EXPERTS, d), lambda i: (0, 0)),
        ],
        out_specs=[
            pl.BlockSpec((tm, d), lambda i: (i, 0)),
            pl.BlockSpec((tm * ROW_TILES, LANES), lambda i: (i, 0)),
            pl.BlockSpec((N_EXPERTS, tm), lambda i: (0, i)),
        ],
        out_shape=[
            jax.ShapeDtypeStruct((n, d), F32),
            jax.ShapeDtypeStruct((n * ROW_TILES, LANES), jnp.uint32),
            jax.ShapeDtypeStruct((N_EXPERTS, n), F32),
        ],
        compiler_params=_params(("parallel",)),
        name="outproj_norm_router",
    )(a, w, x, g1, gain, sc1p, sh, w_router_t)


PAIR_LO = (0, 0, 0, 1, 1, 2)
PAIR_HI = (1, 2, 3, 2, 3, 3)
PAIRS_PER_GROUP = len(PAIR_LO)
N_CLASSES = N_GROUPS * PAIRS_PER_GROUP
CLASS_ROWS = 32
RANK_BITS = 20
RANK_SPAN = 1 << RANK_BITS


def _route_kernel(lg_ref, bias_ref, oi_ref, cnt_ref, carry_ref, tri_ref):
    tr = lg_ref.shape[1]
    step = pl.program_id(0)

    @pl.when(step == 0)
    def _():
        carry_ref[...] = jnp.zeros_like(carry_ref)
        r = lax.broadcasted_iota(jnp.int32, (tr, tr), 0)
        c = lax.broadcasted_iota(jnp.int32, (tr, tr), 1)
        tri_ref[...] = jnp.where(r < c, 1.0, 0.0).astype(BF16)

    scores = jax.nn.sigmoid(lg_ref[...])
    biased = scores + bias_ref[...]
    rows = [biased[e:e + 1, :] for e in range(N_EXPERTS)]

    def top2_sum(a, b, c, d):
        m1, n1 = jnp.maximum(a, b), jnp.minimum(a, b)
        m2, n2 = jnp.maximum(c, d), jnp.minimum(c, d)
        return jnp.maximum(m1, m2) + jnp.maximum(jnp.minimum(m1, m2), jnp.maximum(n1, n2))

    gscore = [top2_sum(*rows[g * EXPERTS_PER_GROUP:(g + 1) * EXPERTS_PER_GROUP]) for g in range(N_GROUPS)]
    best = gscore[0]
    gsel = jnp.zeros_like(best, dtype=jnp.int32)
    for g in range(1, N_GROUPS):
        upd = gscore[g] > best
        gsel = jnp.where(upd, g, gsel)
        best = jnp.where(upd, gscore[g], best)

    def pick(table, j):
        out = table[j]
        for g in range(1, N_GROUPS):
            out = jnp.where(gsel == g, table[g * EXPERTS_PER_GROUP + j], out)
        return out

    in_b = [pick(rows, j) for j in range(EXPERTS_PER_GROUP)]
    v1, i1 = in_b[0], jnp.zeros_like(gsel)
    for j in range(1, EXPERTS_PER_GROUP):
        upd = in_b[j] > v1
        v1 = jnp.where(upd, in_b[j], v1)
        i1 = jnp.where(upd, j, i1)
    v2 = jnp.full_like(v1, -jnp.inf)
    i2 = jnp.zeros_like(gsel)
    for j in range(EXPERTS_PER_GROUP):
        upd = (i1 != j) & (in_b[j] > v2)
        v2 = jnp.where(upd, in_b[j], v2)
        i2 = jnp.where(upd, j, i2)
    first_is_lo = i1 < i2
    lo = jnp.where(first_is_lo, i1, i2)
    hi = jnp.where(first_is_lo, i2, i1)
    pair = jnp.where(lo == 0, hi - 1, jnp.where(lo == 1, hi + 1, PAIRS_PER_GROUP - 1))
    cls = gsel * PAIRS_PER_GROUP + pair

    cid = lax.broadcasted_iota(jnp.int32, (CLASS_ROWS, tr), 0)
    onehot = jnp.where(cid == cls, 1.0, 0.0)
    before = jnp.dot(onehot.astype(BF16), tri_ref[...], preferred_element_type=F32) + carry_ref[...]
    rank = jnp.sum(onehot * before, axis=0, keepdims=True).astype(jnp.int32)
    oi_ref[...] = cls * RANK_SPAN + rank
    new_carry = carry_ref[...] + jnp.sum(onehot, axis=1, keepdims=True)
    carry_ref[...] = new_carry
    cnt_ref[...] = new_carry.astype(jnp.int32)


def _route(logits_t, router_bias):
    n = logits_t.shape[1]
    tr = _tile(512, n)
    return pl.pallas_call(
        _route_kernel,
        grid=(n // tr,),
        in_specs=[
            pl.BlockSpec((N_EXPERTS, tr), lambda i: (0, i)),
            pl.BlockSpec((N_EXPERTS, 1), lambda i: (0, 0)),
        ],
        out_specs=[
            pl.BlockSpec((1, tr), lambda i: (0, i)),
            pl.BlockSpec((CLASS_ROWS, 1), lambda i: (0, 0)),
        ],
        out_shape=[
            jax.ShapeDtypeStruct((1, n), jnp.int32),
            jax.ShapeDtypeStruct((CLASS_ROWS, 1), jnp.int32),
        ],
        scratch_shapes=[pltpu.VMEM((CLASS_ROWS, 1), F32), pltpu.VMEM((tr, tr), BF16)],
        compiler_params=_params(("arbitrary",)),
        name="route",
    )(logits_t, router_bias.reshape(N_EXPERTS, 1).astype(F32))


SORT_UNROLL = 8
MOE_ROWS = 256
MOE_ROWS_LOG2 = MOE_ROWS.bit_length() - 1
assert 1 << MOE_ROWS_LOG2 == MOE_ROWS


def _plan_kernel(code_ref, cnt_ref, tok_ref, ea_ref, eb_ref, off_ref, nv_ref, nu_ref, start_ref):
    n_blocks = ea_ref.shape[0]
    run = jnp.int32(0)
    blk = jnp.int32(0)
    for c in range(N_CLASSES):
        cnt = cnt_ref[c]
        start_ref[c] = run - c * RANK_SPAN
        e_lo = (c // PAIRS_PER_GROUP) * EXPERTS_PER_GROUP + PAIR_LO[c % PAIRS_PER_GROUP]
        e_hi = (c // PAIRS_PER_GROUP) * EXPERTS_PER_GROUP + PAIR_HI[c % PAIRS_PER_GROUP]
        n_blk = (cnt + (MOE_ROWS - 1)) >> MOE_ROWS_LOG2

        def fill(b, carry, run=run, blk=blk, cnt=cnt, e_lo=e_lo, e_hi=e_hi):
            ea_ref[blk + b] = e_lo
            eb_ref[blk + b] = e_hi
            off_ref[blk + b] = run + b * MOE_ROWS
            nv_ref[blk + b] = jnp.minimum(cnt - b * MOE_ROWS, MOE_ROWS)
            return carry

        lax.fori_loop(0, n_blk, fill, 0)
        run = run + cnt
        blk = blk + n_blk
    nu_ref[0] = blk

    def fill_unused(b, carry):
        ea_ref[b] = ea_ref[blk - 1]
        eb_ref[b] = eb_ref[blk - 1]
        off_ref[b] = 0
        nv_ref[b] = 0
        return carry

    lax.fori_loop(blk, n_blocks, fill_unused, 0)

    def place(i, carry):
        toks = [i * SORT_UNROLL + u for u in range(SORT_UNROLL)]
        codes = [code_ref[t] for t in toks]
        slots = [start_ref[code >> RANK_BITS] + code for code in codes]
        for t, p in zip(toks, slots):
            tok_ref[p] = t
        return carry

    lax.fori_loop(0, code_ref.shape[0] // SORT_UNROLL, place, 0)


def _plan(codes, counts, n_blocks):
    n = codes.shape[0]
    assert n % SORT_UNROLL == 0 and n <= RANK_SPAN
    smem = pl.BlockSpec(memory_space=pltpu.SMEM)
    i32 = lambda size: jax.ShapeDtypeStruct((size,), jnp.int32)
    return pl.pallas_call(
        _plan_kernel,
        in_specs=[smem, smem],
        out_specs=[smem] * 6,
        out_shape=[i32(n), i32(n_blocks), i32(n_blocks), i32(n_blocks), i32(n_blocks), i32(1)],
        scratch_shapes=[pltpu.SMEM((N_CLASSES,), jnp.int32)],
        name="moe_plan",
    )(codes, counts)


ROW_UNROLL = 8


def _for_rows(n, fn):
    groups = n // ROW_UNROLL

    def group(g, c):
        for u in range(ROW_UNROLL):
            fn(g * ROW_UNROLL + u)
        return c

    def single(r, c):
        fn(r)
        return c

    lax.fori_loop(0, groups, group, 0)
    lax.fori_loop(groups * ROW_UNROLL, n, single, 0)


def _row_copy(src_ref, src_row, dst_ref, dst_row, sem):
    return pltpu.make_async_copy(
        src_ref.at[pl.ds(pl.multiple_of(src_row * ROW_TILES, ROW_TILES), ROW_TILES)],
        dst_ref.at[pl.ds(pl.multiple_of(dst_row * ROW_TILES, ROW_TILES), ROW_TILES)],
        sem)


def _expert_kernel(ea_ref, eb_ref, off_ref, nv_ref, nu_ref, tok_ref,
                   h_hbm, wgu_a_ref, wd_a_ref, wgu_b_ref, wd_b_ref, wr_a_ref, wr_b_ref, y_hbm,
                   xbuf, ybuf, gsem, ssem):
    del ea_ref, eb_ref
    tb = xbuf.shape[1] // ROW_TILES
    i = pl.program_id(0)
    slot = i % 2
    n_used = nu_ref[0]

    def start_gather(block, s):
        base = off_ref[block]
        _for_rows(nv_ref[block],
                  lambda r: _row_copy(h_hbm, tok_ref[base + r], xbuf.at[s], r, gsem.at[s]).start())

    def start_scatter(block, s):
        base = off_ref[block]
        _for_rows(nv_ref[block],
                  lambda r: _row_copy(ybuf.at[s], r, y_hbm, tok_ref[base + r], ssem.at[s]).start())

    def wait_rows(block, hbm, buf, sem):
        rows = nv_ref[block] * ROW_TILES

        @pl.when(rows > 0)
        def _():
            pltpu.make_async_copy(hbm.at[pl.ds(0, rows)], buf.at[pl.ds(0, rows)], sem).wait()

    @pl.when(i == 0)
    def _():
        xbuf[...] = jnp.zeros_like(xbuf)
        start_gather(0, 0)

    @pl.when(i < n_used)
    def _():
        @pl.when(i + 1 < n_used)
        def _():
            start_gather(i + 1, 1 - slot)

        wait_rows(i, h_hbm, xbuf.at[slot], gsem.at[slot])

        @pl.when(i >= 2)
        def _():
            wait_rows(i - 2, y_hbm, ybuf.at[slot], ssem.at[slot])

        words = [xbuf[slot, pl.ds(j, tb, stride=ROW_TILES), :] for j in range(PACKED_TILES)]
        halves = [_unpack_bf16_pairs(w) for w in words]
        x = jnp.concatenate([h[0] for h in halves] + [h[1] for h in halves], axis=1)

        def mlp(wgu_ref, wd_ref):
            gu = jnp.dot(x, wgu_ref[0], preferred_element_type=F32)
            gate = gu[:, :D_EXPERT]
            act = gate * jax.nn.sigmoid(gate) * gu[:, D_EXPERT:]
            return jnp.dot(act.astype(BF16), wd_ref[0], preferred_element_type=F32)

        score_a = jax.nn.sigmoid(jnp.dot(x, wr_a_ref[0], preferred_element_type=F32))
        score_b = jax.nn.sigmoid(jnp.dot(x, wr_b_ref[0], preferred_element_type=F32))
        inv_tot = 1.0 / (score_a + score_b)
        w_a = score_a * inv_tot
        w_b = score_b * inv_tot
        ya = mlp(wgu_a_ref, wd_a_ref)
        yb = mlp(wgu_b_ref, wd_b_ref)
        for j in range(ROW_TILES):
            sl = slice(j * LANES, (j + 1) * LANES)
            ybuf[slot, pl.ds(j, tb, stride=ROW_TILES), :] = w_a * ya[:, sl] + w_b * yb[:, sl]
        start_scatter(i, slot)

        @pl.when(i == n_used - 1)
        def _():
            wait_rows(i, y_hbm, ybuf.at[slot], ssem.at[slot])

            @pl.when(i >= 1)
            def _():
                wait_rows(i - 1, y_hbm, ybuf.at[1 - slot], ssem.at[1 - slot])


def _experts(tables, sorted_tok, h_rows, layer, w_gu, w_down, wr_bcast, tb):
    block_ea, block_eb, src_off, n_valid, n_used = tables
    n_blocks = block_ea.shape[0]
    d, two_f = w_gu.shape[2:]
    ea_map = lambda i, ea, eb, off, nv, nu, tok: (ea[i], 0, 0)
    eb_map = lambda i, ea, eb, off, nv, nu, tok: (eb[i], 0, 0)
    lea_map = lambda i, ea, eb, off, nv, nu, tok: (layer, ea[i], 0, 0)
    leb_map = lambda i, ea, eb, off, nv, nu, tok: (layer, eb[i], 0, 0)
    grid_spec = pltpu.PrefetchScalarGridSpec(
        num_scalar_prefetch=6,
        grid=(n_blocks,),
        in_specs=[
            pl.BlockSpec(memory_space=pl.ANY),
            pl.BlockSpec((None, 1, d, two_f), lea_map),
            pl.BlockSpec((None, 1, two_f // 2, d), lea_map),
            pl.BlockSpec((None, 1, d, two_f), leb_map),
            pl.BlockSpec((None, 1, two_f // 2, d), leb_map),
            pl.BlockSpec((1, d, LANES), ea_map),
            pl.BlockSpec((1, d, LANES), eb_map),
        ],
        out_specs=pl.BlockSpec(memory_space=pl.ANY),
        scratch_shapes=[
            pltpu.VMEM((2, tb * ROW_TILES, LANES), jnp.uint32),
            pltpu.VMEM((2, tb * ROW_TILES, LANES), F32),
            pltpu.SemaphoreType.DMA((2,)),
            pltpu.SemaphoreType.DMA((2,)),
        ],
    )
    return pl.pallas_call(
        _expert_kernel,
        grid_spec=grid_spec,
        out_shape=jax.ShapeDtypeStruct(h_rows.shape, F32),
        compiler_params=_params(("arbitrary",)),
        name="moe_experts",
    )(block_ea, block_eb, src_off, n_valid, n_used, sorted_tok, h_rows, w_gu, w_down, w_gu, w_down, wr_bcast, wr_bcast)


def _residual_kernel(y_ref, x_ref, g2_ref, o_ref):
    o_ref[...] = _gated_residual(x_ref, y_ref, g2_ref)


def _residual(y_rows, x_mid, g2, seq):
    n, d = x_mid.shape
    tc = _tile(512, seq)
    per_b = seq // tc
    return pl.pallas_call(
        _residual_kernel,
        grid=(n // tc,),
        in_specs=[
            pl.BlockSpec((tc * ROW_TILES, LANES), lambda i: (i, 0)),
            pl.BlockSpec((tc, d), lambda i: (i, 0)),
            pl.BlockSpec((1, 1, d), lambda i: (i // per_b, 0, 0)),
        ],
        out_specs=pl.BlockSpec((tc, d), lambda i: (i, 0)),
        out_shape=jax.ShapeDtypeStruct((n, d), F32),
        compiler_params=_params(("parallel",)),
        name="moe_residual",
    )(y_rows, x_mid, g2)


def _moe(h_rows, logits_t, router_bias, layer, w_gu, w_down, wr_bcast):
    n = logits_t.shape[1]
    n_blocks = (n + N_CLASSES * (MOE_ROWS - 1) + MOE_ROWS - 1) // MOE_ROWS
    codes, counts = _route(logits_t, router_bias)
    sorted_tok, *tables = _plan(codes.reshape(n), counts.reshape(CLASS_ROWS), n_blocks)
    return _experts(tables, sorted_tok, h_rows, layer, w_gu, w_down, wr_bcast, MOE_ROWS)


def kernel(x, c, positions, norm1, norm2, w_ada, b_ada, a_w_in, a_b_if, a_h_norm, a_w_out, b_w_in, b_q_norm, b_k_norm, b_lam_q1, b_lam_k1, b_lam_q2, b_lam_k2, b_o_norm, b_w_out, w_router, router_bias, moe_w_gu, moe_w_down):
    bsz, seq, d = x.shape
    depth = w_ada.shape[0]
    n = bsz * seq
    xf = x.reshape(n, d)
    mod = _ada_mod(c, w_ada, b_ada)
    w_router_t = w_router.T
    wr_bcast = jnp.broadcast_to(w_router_t[:, :, None], (N_EXPERTS, d, LANES)).astype(BF16)
    pos_row = positions.reshape(n)
    w_gu = moe_w_gu.astype(BF16)
    w_down = moe_w_down.astype(BF16)

    residual = None
    for l in range(depth):
        sh1, sc1, g1, sh2, sc2, g2 = [mod[l, :, i * d:(i + 1) * d].reshape(bsz, 1, d) for i in range(6)]
        j = l // 2
        if l % 2 == 0:
            w_in = a_w_in[j]
            w_main = w_in[:, :A_MAIN_COLS].astype(BF16)
            w_gate = jnp.pad(w_in[:, A_MAIN_COLS:], ((0, 0), (0, LANES - 2 * A_HEADS))).astype(BF16)
        else:
            w_in = b_w_in[j]
            w_qk = _interleave_maps(w_in[:, :2 * D_MODEL])
            w_main, w_gate = jnp.concatenate([w_qk, w_in[:, 2 * D_MODEL:]], axis=1).astype(BF16), None
        outs = _norm_matmul(xf, residual, norm1[l].reshape(1, d), 1.0 + sc1, sh1, w_main, w_gate, seq)
        if residual is not None:
            xf, *outs = outs
        if l % 2 == 0:
            proj, gates = outs
            bias_row = jnp.pad(a_b_if[j], (0, LANES - 2 * A_HEADS)).reshape(1, LANES)
            mixed = _mlstm(proj, gates, bias_row, a_h_norm[j], bsz, seq)
            w_out = a_w_out[j].astype(BF16)
        else:
            (proj,) = outs
            qk = _qk_prep(proj, pos_row, b_q_norm[j], b_k_norm[j], seq)
            lam_tab = jnp.zeros((SUBLANES, LANES), F32)
            for r, v in enumerate((b_lam_q1[j], b_lam_k1[j], b_lam_q2[j], b_lam_k2[j])):
                lam_tab = lam_tab.at[r, :B_HEAD_DIM].set(v)
            lam_init = 0.8 - 0.6 * math.exp(-0.3 * l)
            mixed = _diff_attn(qk, proj, lam_tab, b_o_norm[j], bsz, seq, lam_init)
            w_out = b_w_out[j].astype(BF16)
        xf, h_rows, logits_t = _outproj_norm_router(
            mixed, w_out, xf, g1, norm2[l].reshape(1, d), 1.0 + sc2, sh2, w_router_t, seq)
        residual = (_moe(h_rows, logits_t, router_bias, l, w_gu, w_down, wr_bcast), g2)
    return _residual(residual[0], xf, residual[1], seq).reshape(bsz, seq, d)
```

```python
import functools
import math

import jax
import jax.numpy as jnp
from jax import lax
from jax.experimental import pallas as pl
from jax.experimental.pallas import tpu as pltpu

D_MODEL = 1024
A_HEADS = 4
A_QK_DIM = 128
A_V_DIM = 256
A_CHUNK = 128
A_HQ = A_HEADS * A_QK_DIM
A_HV = A_HEADS * A_V_DIM
A_MAIN_COLS = 2 * A_HQ + 2 * A_HV

B_HEADS = 8
B_HEAD_DIM = 64
B_V_DIM = 128
B_ROT_DIM = 16
ROPE_THETA = 500000.0

N_EXPERTS = 16
N_GROUPS = 4
EXPERTS_PER_GROUP = 4
TOP_K = 2
D_EXPERT = 512
EPS = 1e-6

LANES = 128
SUBLANES = 8
ROW_TILES = D_MODEL // LANES
assert ROW_TILES == SUBLANES
VMEM_LIMIT = 48 * 1024 * 1024

F32 = jnp.float32
BF16 = jnp.bfloat16
HIGHEST = lax.Precision.HIGHEST


def _params(sem):
    return pltpu.CompilerParams(dimension_semantics=sem, vmem_limit_bytes=VMEM_LIMIT)


def _tile(pref, n):
    t = min(pref, n)
    assert n % t == 0, (pref, n)
    return t


def _ada_kernel(c_ref, w_ref, b_ref, o_ref):
    c = c_ref[...]
    c_act = c * jax.nn.sigmoid(c)
    o_ref[0] = jnp.dot(c_act, w_ref[0], precision=HIGHEST, preferred_element_type=F32) + b_ref[0]


def _ada_mod(c, w_ada, b_ada):
    depth, d, six_d = w_ada.shape
    bsz = c.shape[0]
    tn = _tile(1536, six_d)
    return pl.pallas_call(
        _ada_kernel,
        grid=(depth, six_d // tn),
        in_specs=[
            pl.BlockSpec((bsz, d), lambda l, j: (0, 0)),
            pl.BlockSpec((1, d, tn), lambda l, j: (l, 0, j)),
            pl.BlockSpec((1, 1, tn), lambda l, j: (l, 0, j)),
        ],
        out_specs=pl.BlockSpec((1, bsz, tn), lambda l, j: (l, 0, j)),
        out_shape=jax.ShapeDtypeStruct((depth, bsz, six_d), F32),
        compiler_params=_params(("parallel", "parallel")),
        name="ada_mod",
    )(c, w_ada, b_ada.reshape(depth, 1, six_d))


def _modulated_rms(x, g, sc1p, sh):
    y = x * lax.rsqrt(jnp.mean(x * x, axis=-1, keepdims=True) + EPS)
    return (y * g) * sc1p + sh


def _gated_residual(x_ref, y_ref, g_ref):
    rows = x_ref.shape[0]
    g = g_ref[0]
    return jnp.concatenate(
        [x_ref[:, j * LANES:(j + 1) * LANES]
         + g[:, j * LANES:(j + 1) * LANES] * y_ref[pl.ds(j, rows, stride=ROW_TILES), :] for j in range(ROW_TILES)],
        axis=1)


def _norm_mm_kernel(*refs, col_chunk, has_gates, has_residual):
    refs = list(refs)
    x_ref = refs.pop(0)
    if has_residual:
        y_ref, g2_ref = refs.pop(0), refs.pop(0)
    g_ref, sc_ref, sh_ref, w_ref = refs[:4]
    refs = refs[4:]
    wg_ref = refs.pop(0) if has_gates else None
    xo_ref = refs.pop(0) if has_residual else None
    o_ref = refs.pop(0)
    og_ref = refs.pop(0) if has_gates else None

    if has_residual:
        x = _gated_residual(x_ref, y_ref, g2_ref)
        xo_ref[...] = x
    else:
        x = x_ref[...]
    hb = _modulated_rms(x, g_ref[...], sc_ref[0], sh_ref[0]).astype(BF16)
    for c0 in range(0, o_ref.shape[1], col_chunk):
        o_ref[:, c0:c0 + col_chunk] = jnp.dot(
            hb, w_ref[:, c0:c0 + col_chunk], preferred_element_type=F32).astype(o_ref.dtype)
    if has_gates:
        og_ref[...] = jnp.dot(hb, wg_ref[...], preferred_element_type=F32)


def _norm_matmul(x, residual, gain, sc1p, sh, w, wg, seq):
    n, d = x.shape
    cols = w.shape[1]
    tm = _tile(512, seq)
    per_b = seq // tm
    has_gates = wg is not None
    has_residual = residual is not None
    bmap = lambda i: (i // per_b, 0, 0)
    in_specs = [pl.BlockSpec((tm, d), lambda i: (i, 0))]
    args = [x]
    if has_residual:
        in_specs += [pl.BlockSpec((tm * ROW_TILES, LANES), lambda i: (i, 0)), pl.BlockSpec((1, 1, d), bmap)]
        args += list(residual)
    in_specs += [
        pl.BlockSpec((1, d), lambda i: (0, 0)),
        pl.BlockSpec((1, 1, d), bmap),
        pl.BlockSpec((1, 1, d), bmap),
        pl.BlockSpec((d, cols), lambda i: (0, 0)),
    ]
    args += [gain, sc1p, sh, w]
    out_specs, out_shape = [], []
    if has_residual:
        out_specs.append(pl.BlockSpec((tm, d), lambda i: (i, 0)))
        out_shape.append(jax.ShapeDtypeStruct((n, d), F32))
    out_specs.append(pl.BlockSpec((tm, cols), lambda i: (i, 0)))
    out_shape.append(jax.ShapeDtypeStruct((n, cols), BF16))
    if has_gates:
        in_specs.append(pl.BlockSpec((d, LANES), lambda i: (0, 0)))
        out_specs.append(pl.BlockSpec((tm, LANES), lambda i: (i, 0)))
        out_shape.append(jax.ShapeDtypeStruct((n, LANES), F32))
        args.append(wg)
    return pl.pallas_call(
        functools.partial(_norm_mm_kernel, col_chunk=512, has_gates=has_gates, has_residual=has_residual),
        grid=(n // tm,),
        in_specs=in_specs,
        out_specs=out_specs,
        out_shape=out_shape,
        compiler_params=_params(("parallel",)),
        name="norm_inproj",
    )(*args)


def _log_sigmoid(x):
    return jnp.minimum(x, 0.0) - jnp.log1p(jnp.exp(-jnp.abs(x)))


def _mlstm_chunk(r0, p_ref, g_ref, bias_ref, hn_ref, o_ref, ct_ref, n_ref, m_ref):
    L = A_CHUNK
    rows = slice(r0, r0 + L)
    gates = g_ref[rows, :] + bias_ref[...]
    src = lax.broadcasted_iota(jnp.int32, (L, L), 0)
    tgt = lax.broadcasted_iota(jnp.int32, (L, L), 1)
    causal = src <= tgt
    bcum = jnp.dot(jnp.where(src >= tgt, 1.0, 0.0), _log_sigmoid(gates), precision=HIGHEST,
                   preferred_element_type=F32)
    bcum_t = bcum.T
    nt = (((1,), (1,)), ((), ()))

    pre = []
    for h in range(A_HEADS):
        q = p_ref[rows, h * A_QK_DIM:(h + 1) * A_QK_DIM]
        ksf = p_ref[rows, A_HQ + h * A_QK_DIM:A_HQ + (h + 1) * A_QK_DIM].astype(F32) * (A_QK_DIM ** -0.5)
        v = p_ref[rows, 2 * A_HQ + h * A_V_DIM:2 * A_HQ + (h + 1) * A_V_DIM]
        ct_old = ct_ref[h]
        n_old = n_ref[h]
        qk = lax.dot_general(ksf.astype(BF16), q, nt, preferred_element_type=F32)
        qc = lax.dot_general(ct_old.astype(BF16), q, nt, preferred_element_type=F32)
        qn = lax.dot_general(n_old.astype(BF16), q, nt, preferred_element_type=F32)[0:1]
        vt = v.astype(F32).T.astype(BF16)
        pre.append((ksf, vt, ct_old, n_old, qk, qc, qn))

    for h in range(A_HEADS):
        ksf, vt, ct_old, n_old, qk, qc, qn = pre[h]
        og = p_ref[rows, 2 * A_HQ + A_HV + h * A_V_DIM:2 * A_HQ + A_HV + (h + 1) * A_V_DIM].astype(F32)
        b_row = bcum_t[A_HEADS + h:A_HEADS + h + 1, :]
        b_last = b_row[:, L - 1:L]
        c_col = gates[:, h:h + 1] - bcum[:, A_HEADS + h:A_HEADS + h + 1]
        m11 = m_ref[h][:, 0:1]

        log_d = jnp.where(causal, c_col + b_row, -jnp.inf)
        log_inter = b_row + m11
        m_t = jnp.maximum(jnp.max(log_d, axis=0, keepdims=True), log_inter)
        dmat = jnp.exp(log_d - m_t)
        inter = jnp.exp(log_inter - m_t)
        s = qk * dmat
        num = jnp.dot(vt, s.astype(BF16), preferred_element_type=F32) + inter * qc
        den = jnp.sum(s, axis=0, keepdims=True) + inter * qn
        hh = num / jnp.maximum(jnp.abs(den), jnp.exp(-m_t))

        lw_col = b_last + c_col
        m_new = jnp.maximum(b_last + m11, jnp.max(lw_col, axis=0, keepdims=True))
        kw = ksf * jnp.exp(lw_col - m_new)
        decay = jnp.exp(b_last + m11 - m_new)
        ct_ref[h] = decay * ct_old + jnp.dot(vt, kw.astype(BF16), preferred_element_type=F32)
        n_ref[h] = decay * n_old + jnp.broadcast_to(jnp.sum(kw, axis=0, keepdims=True), n_old.shape)
        m_ref[h] = jnp.broadcast_to(m_new, (1, LANES))

        hn = hh * lax.rsqrt(jnp.mean(hh * hh, axis=0, keepdims=True) + EPS)
        hn = (hn * hn_ref[h * A_V_DIM:(h + 1) * A_V_DIM, :]).T
        o_ref[rows, h * A_V_DIM:(h + 1) * A_V_DIM] = (hn * jax.nn.sigmoid(og)).astype(o_ref.dtype)


MLSTM_CHUNKS_PER_STEP = 2


def _mlstm_kernel(p_ref, g_ref, bias_ref, hn_ref, o_ref, ct_ref, n_ref, m_ref):
    @pl.when(pl.program_id(1) == 0)
    def _():
        ct_ref[...] = jnp.zeros_like(ct_ref)
        n_ref[...] = jnp.zeros_like(n_ref)
        m_ref[...] = jnp.zeros_like(m_ref)

    for r0 in range(0, p_ref.shape[0], A_CHUNK):
        _mlstm_chunk(r0, p_ref, g_ref, bias_ref, hn_ref, o_ref, ct_ref, n_ref, m_ref)


def _mlstm(proj, gates, bias_row, h_norm, bsz, seq):
    n = proj.shape[0]
    rows = _tile(MLSTM_CHUNKS_PER_STEP * A_CHUNK, seq)
    nc = seq // rows
    h_norm_cols = jnp.broadcast_to(h_norm[:, None], (A_HV, A_CHUNK))
    return pl.pallas_call(
        _mlstm_kernel,
        grid=(bsz, nc),
        in_specs=[
            pl.BlockSpec((rows, A_MAIN_COLS), lambda b, c: (b * nc + c, 0)),
            pl.BlockSpec((rows, LANES), lambda b, c: (b * nc + c, 0)),
            pl.BlockSpec((1, LANES), lambda b, c: (0, 0)),
            pl.BlockSpec((A_HV, A_CHUNK), lambda b, c: (0, 0)),
        ],
        out_specs=pl.BlockSpec((rows, A_HV), lambda b, c: (b * nc + c, 0)),
        out_shape=jax.ShapeDtypeStruct((n, A_HV), BF16),
        scratch_shapes=[
            pltpu.VMEM((A_HEADS, A_V_DIM, A_QK_DIM), F32),
            pltpu.VMEM((A_HEADS, SUBLANES, A_QK_DIM), F32),
            pltpu.VMEM((A_HEADS, 1, LANES), F32),
        ],
        compiler_params=_params(("parallel", "arbitrary")),
        name="mlstm",
    )(proj, gates, bias_row, h_norm_cols)


def _qk_prep_kernel(p_ref, pos_ref, seg_ref, segt_ref, gain_ref, freq_ref, o_ref):
    tm = p_ref.shape[0]
    x = p_ref[...].astype(F32)
    ss = jnp.dot((x * x).astype(BF16), seg_ref[...], preferred_element_type=F32)
    r = lax.rsqrt(ss * (1.0 / B_HEAD_DIM) + EPS)
    r_hi = r.astype(BF16)
    r_lo = (r - r_hi.astype(F32)).astype(BF16)
    r_full = jnp.dot(jnp.concatenate([r_hi, r_lo], axis=1), segt_ref[...], preferred_element_type=F32)

    assert B_ROT_DIM // 2 == SUBLANES
    ang = freq_ref[:, 0:1] * pos_ref[0].astype(F32)
    cos8 = jnp.cos(ang)
    sin8 = jnp.sin(ang)
    one8 = jnp.ones_like(cos8)
    zero8 = jnp.zeros_like(cos8)
    groups_per_quarter = MAP_QUARTER // SUBLANES

    def chunk_pattern(lower, upper, rest):
        quarter = lambda lead: [lead] + [rest] * (groups_per_quarter - 1)
        return jnp.concatenate(quarter(lower) * 2 + quarter(upper) * 2, axis=0).T

    c_mul = chunk_pattern(cos8, cos8, one8)
    s_mul = chunk_pattern(-sin8, sin8, zero8)
    for j in range(2 * D_MODEL // LANES):
        sl = slice(j * LANES, (j + 1) * LANES)
        y = x[:, sl] * r_full[:, sl] * gain_ref[:, sl]
        out = y * c_mul + pltpu.roll(y, LANES // 2, axis=1) * s_mul
        if j < D_MODEL // LANES:
            out = out * (B_HEAD_DIM ** -0.5 * math.log2(math.e))
        o_ref[:, sl] = out.astype(o_ref.dtype)


MAP_QUARTER = B_HEAD_DIM // 2


def _interleave_maps(t):
    half = B_ROT_DIM // 2
    rest = MAP_QUARTER - half
    lead = t.shape[:-1]
    t = t.reshape(*lead, 2 * B_HEADS, 2, B_HEAD_DIM)
    lower = jnp.concatenate([t[..., 0:half], t[..., B_ROT_DIM:B_ROT_DIM + rest]], axis=-1)
    upper = jnp.concatenate([t[..., half:B_ROT_DIM], t[..., B_ROT_DIM + rest:]], axis=-1)
    out = jnp.stack([lower, upper], axis=-3)
    return out.reshape(*lead, 2 * D_MODEL)


def _qk_prep(proj, positions_row, q_norm, k_norm, seq):
    n = proj.shape[0]
    two_d = 2 * D_MODEL
    tm = _tile(512, seq)
    seg_of_col = _interleave_maps(jnp.arange(two_d) // B_HEAD_DIM)
    seg = (seg_of_col[:, None] == jnp.arange(LANES)[None, :]).astype(BF16)
    gain = _interleave_maps(
        jnp.concatenate([jnp.tile(q_norm, D_MODEL // B_HEAD_DIM), jnp.tile(k_norm, D_MODEL // B_HEAD_DIM)]))
    inv_freq = ROPE_THETA ** (-jnp.arange(0, B_ROT_DIM, 2, dtype=F32) / B_ROT_DIM)
    freq_tab = jnp.broadcast_to(inv_freq[:, None], (B_ROT_DIM // 2, LANES))
    return pl.pallas_call(
        _qk_prep_kernel,
        grid=(n // tm,),
        in_specs=[
            pl.BlockSpec((tm, two_d), lambda i: (i, 0)),
            pl.BlockSpec((1, 1, tm), lambda i: (i, 0, 0)),
            pl.BlockSpec((two_d, LANES), lambda i: (0, 0)),
            pl.BlockSpec((2 * LANES, two_d), lambda i: (0, 0)),
            pl.BlockSpec((1, two_d), lambda i: (0, 0)),
            pl.BlockSpec((SUBLANES, LANES), lambda i: (0, 0)),
        ],
        out_specs=pl.BlockSpec((tm, two_d), lambda i: (i, 0)),
        out_shape=jax.ShapeDtypeStruct((n, two_d), BF16),
        compiler_params=_params(("parallel",)),
        name="qk_prep",
    )(proj, positions_row.reshape(n // tm, 1, tm), seg, jnp.concatenate([seg.T, seg.T], axis=0),
      gain.reshape(1, two_d), freq_tab)


ATTN_TILE = 512
ATTN_KEY_TILE = 256
ATTN_HEADS_PER_STEP = 2


def _diff_attn_kernel(q_ref, k_ref, v_ref, lam_ref, on_ref, o_ref, vt_ref, qm_ref, m_ref, l_ref, acc_ref,
                      *, tile, tk, lam_init):
    seq = q_ref.shape[0]
    heads = q_ref.shape[1] // LANES
    n_tiles = seq // tile
    k_per_q = tile // tk
    lamv = lam_ref[...]
    lam = (jnp.exp(jnp.sum(lamv[0:1] * lamv[1:2], axis=1, keepdims=True))
           - jnp.exp(jnp.sum(lamv[2:3] * lamv[3:4], axis=1, keepdims=True)) + lam_init)
    head_lanes = lambda h: slice(h * LANES, (h + 1) * LANES)
    tchunk = min(256, seq)
    for h in range(heads):
        for c in range(seq // tchunk):
            rows = slice(c * tchunk, (c + 1) * tchunk)
            vt_ref[h, :, rows] = v_ref[rows, head_lanes(h)].astype(F32).T.astype(BF16)
    first_map = (lax.broadcasted_iota(jnp.int32, (tile, LANES), 1) // MAP_QUARTER) % 2 == 0
    key_id = lax.broadcasted_iota(jnp.int32, (tk, tile), 0)
    query_id = lax.broadcasted_iota(jnp.int32, (tk, tile), 1)
    nt = (((1,), (1,)), ((), ()))

    def qk_score(h, qi, ki, c):
        return lax.dot_general(k_ref[ki * tk:(ki + 1) * tk, head_lanes(h)], qm_ref[(h * n_tiles + qi) * 2 + c], nt,
                               preferred_element_type=F32)

    def softmax_pv(h, c, s, ki, mask):
        j = 2 * h + c
        vt = vt_ref[h, :, ki * tk:(ki + 1) * tk]
        if mask is not None:
            s = jnp.where(mask, s, -jnp.inf)
        m_old = m_ref[j]
        m_new = jnp.maximum(m_old, jnp.max(s, axis=0, keepdims=True))
        alpha = jnp.exp2(m_old - m_new)
        p = jnp.exp2(s - m_new)
        l_ref[j] = alpha * l_ref[j] + jnp.sum(p, axis=0, keepdims=True)
        acc_ref[j] = alpha * acc_ref[j] + jnp.dot(vt, p.astype(BF16), preferred_element_type=F32)
        m_ref[j] = m_new

    def reset(h):
        for j in (2 * h, 2 * h + 1):
            m_ref[j] = jnp.full(m_ref.shape[1:], -jnp.inf, F32)
            l_ref[j] = jnp.zeros(l_ref.shape[1:], F32)
            acc_ref[j] = jnp.zeros(acc_ref.shape[1:], F32)

    for h in range(heads):
        for qi in range(n_tiles):
            q = q_ref[qi * tile:(qi + 1) * tile, head_lanes(h)]
            qm_ref[(h * n_tiles + qi) * 2] = jnp.where(first_map, q, jnp.zeros_like(q))
            qm_ref[(h * n_tiles + qi) * 2 + 1] = jnp.where(first_map, jnp.zeros_like(q), q)
        reset(h)
    steps = [(h, qi, ki) for qi in range(n_tiles) for ki in range((qi + 1) * k_per_q) for h in range(heads)]
    scores_next = [qk_score(*steps[0], c) for c in range(2)]
    for n, (h, qi, ki) in enumerate(steps):
        scores = scores_next
        scores_next = [None, None]
        on_diagonal = ki >= qi * k_per_q
        mask = (query_id + qi * tile >= key_id + ki * tk) if on_diagonal else None
        for c in range(2):
            if n + 1 < len(steps):
                scores_next[c] = qk_score(*steps[n + 1], c)
            softmax_pv(h, c, scores[c], ki, mask)
        if ki == (qi + 1) * k_per_q - 1:
            o = acc_ref[2 * h] / l_ref[2 * h] - lam * (acc_ref[2 * h + 1] / l_ref[2 * h + 1])
            o = o * lax.rsqrt(jnp.mean(o * o, axis=0, keepdims=True) + EPS) * on_ref[...] * (1.0 - lam_init)
            o_ref[qi * tile:(qi + 1) * tile, head_lanes(h)] = o.T.astype(o_ref.dtype)
            if qi + 1 < n_tiles:
                reset(h)


def _diff_attn(qk, proj, lam_tab, o_norm, bsz, seq, lam_init):
    n = qk.shape[0]
    hp = ATTN_HEADS_PER_STEP
    groups = B_HEADS // hp
    tile = _tile(ATTN_TILE, seq)
    o_norm_cols = jnp.broadcast_to(o_norm[:, None], (B_V_DIM, tile))
    return pl.pallas_call(
        functools.partial(_diff_attn_kernel, tile=tile, tk=_tile(ATTN_KEY_TILE, tile), lam_init=lam_init),
        grid=(bsz, groups),
        in_specs=[
            pl.BlockSpec((seq, hp * LANES), lambda b, g: (b, g)),
            pl.BlockSpec((seq, hp * LANES), lambda b, g: (b, groups + g)),
            pl.BlockSpec((seq, hp * B_V_DIM), lambda b, g: (b, 2 * groups + g)),
            pl.BlockSpec((SUBLANES, LANES), lambda b, g: (0, 0)),
            pl.BlockSpec((B_V_DIM, tile), lambda b, g: (0, 0)),
        ],
        out_specs=pl.BlockSpec((seq, hp * B_V_DIM), lambda b, g: (b, g)),
        out_shape=jax.ShapeDtypeStruct((n, D_MODEL), BF16),
        scratch_shapes=[
            pltpu.VMEM((hp, B_V_DIM, seq), BF16),
            pltpu.VMEM((hp * 2 * (seq // tile), tile, LANES), BF16),
            pltpu.VMEM((2 * hp, 1, tile), F32),
            pltpu.VMEM((2 * hp, 1, tile), F32),
            pltpu.VMEM((2 * hp, B_V_DIM, tile), F32),
        ],
        compiler_params=_params(("parallel", "parallel")),
        name="diff_attn",
    )(qk, qk, proj, lam_tab, o_norm_cols)


HALF_D = D_MODEL // 2
HI16 = 0xFFFF0000
PACKED_TILES = HALF_D // LANES


def _pack_bf16_pairs(v):
    bits = lax.bitcast_convert_type(v.astype(BF16).astype(F32), jnp.uint32)
    return (bits[:, HALF_D:] & jnp.uint32(HI16)) | (bits[:, :HALF_D] >> 16)


def _unpack_bf16_pairs(words):
    first = lax.bitcast_convert_type(words << 16, F32)
    second = lax.bitcast_convert_type(words & jnp.uint32(HI16), F32)
    return first.astype(BF16), second.astype(BF16)


def _outproj_kernel(a_ref, w_ref, x_ref, g1_ref, gain_ref, sc_ref, sh_ref, wr_ref, xo_ref, hp_ref, lg_ref):
    y = jnp.dot(a_ref[...], w_ref[...], preferred_element_type=F32)
    xm = x_ref[...] + g1_ref[0] * y
    xo_ref[...] = xm
    h2 = _modulated_rms(xm, gain_ref[...], sc_ref[0], sh_ref[0])
    tm = xm.shape[0]
    packed = _pack_bf16_pairs(h2)
    for j in range(ROW_TILES):
        hp_ref[pl.ds(j, tm, stride=ROW_TILES), :] = (
            packed[:, j * LANES:(j + 1) * LANES] if j < PACKED_TILES else jnp.zeros((tm, LANES), jnp.uint32))
    nt = (((1,), (1,)), ((), ()))
    h_hi = h2.astype(BF16)
    h_lo = (h2 - h_hi.astype(F32)).astype(BF16)
    wr = wr_ref[...]
    wr_hi = wr.astype(BF16)
    wr_lo = (wr - wr_hi.astype(F32)).astype(BF16)
    lg_ref[...] = (lax.dot_general(wr_hi, h_hi, nt, preferred_element_type=F32)
                   + lax.dot_general(wr_hi, h_lo, nt, preferred_element_type=F32)
                   + lax.dot_general(wr_lo, h_hi, nt, preferred_element_type=F32))


def _outproj_norm_router(a, w, x, g1, gain, sc1p, sh, w_router_t, seq):
    n, d = x.shape
    tm = _tile(512, seq)
    per_b = seq // tm
    bmap = lambda i: (i // per_b, 0, 0)
    return pl.pallas_call(
        _outproj_kernel,
        grid=(n // tm,),
        in_specs=[
            pl.BlockSpec((tm, a.shape[1]), lambda i: (i, 0)),
            pl.BlockSpec(w.shape, lambda i: (0, 0)),
            pl.BlockSpec((tm, d), lambda i: (i, 0)),
            pl.BlockSpec((1, 1, d), bmap),
            pl.BlockSpec((1, d), lambda i: (0, 0)),
            pl.BlockSpec((1, 1, d), bmap),
            pl.BlockSpec((1, 1, d), bmap),
            pl.BlockSpec((N_EXPERTS, d), lambda i: (0, 0)),
        ],
        out_specs=[
            pl.BlockSpec((tm, d), lambda i: (i, 0)),
            pl.BlockSpec((tm * ROW_TILES, LANES), lambda i: (i, 0)),
            pl.BlockSpec((N_EXPERTS, tm), lambda i: (0, i)),
        ],
        out_shape=[
            jax.ShapeDtypeStruct((n, d), F32),
            jax.ShapeDtypeStruct((n * ROW_TILES, LANES), jnp.uint32),
            jax.ShapeDtypeStruct((N_EXPERTS, n), F32),
        ],
        compiler_params=_params(("parallel",)),
        name="outproj_norm_router",
    )(a, w, x, g1, gain, sc1p, sh, w_router_t)


PAIR_LO = (0, 0, 0, 1, 1, 2)
PAIR_HI = (1, 2, 3, 2, 3, 3)
PAIRS_PER_GROUP = len(PAIR_LO)
N_CLASSES = N_GROUPS * PAIRS_PER_GROUP
CLASS_ROWS = 32
RANK_BITS = 20
RANK_SPAN = 1 << RANK_BITS


def _route_kernel(lg_ref, bias_ref, oi_ref, cnt_ref, carry_ref, tri_ref):
    tr = lg_ref.shape[1]
    step = pl.program_id(0)

    @pl.when(step == 0)
    def _():
        carry_ref[...] = jnp.zeros_like(carry_ref)
        r = lax.broadcasted_iota(jnp.int32, (tr, tr), 0)
        c = lax.broadcasted_iota(jnp.int32, (tr, tr), 1)
        tri_ref[...] = jnp.where(r < c, 1.0, 0.0).astype(BF16)

    scores = jax.nn.sigmoid(lg_ref[...])
    biased = scores + bias_ref[...]
    rows = [biased[e:e + 1, :] for e in range(N_EXPERTS)]

    def top2_sum(a, b, c, d):
        m1, n1 = jnp.maximum(a, b), jnp.minimum(a, b)
        m2, n2 = jnp.maximum(c, d), jnp.minimum(c, d)
        return jnp.maximum(m1, m2) + jnp.maximum(jnp.minimum(m1, m2), jnp.maximum(n1, n2))

    gscore = [top2_sum(*rows[g * EXPERTS_PER_GROUP:(g + 1) * EXPERTS_PER_GROUP]) for g in range(N_GROUPS)]
    best = gscore[0]
    gsel = jnp.zeros_like(best, dtype=jnp.int32)
    for g in range(1, N_GROUPS):
        upd = gscore[g] > best
        gsel = jnp.where(upd, g, gsel)
        best = jnp.where(upd, gscore[g], best)

    def pick(table, j):
        out = table[j]
        for g in range(1, N_GROUPS):
            out = jnp.where(gsel == g, table[g * EXPERTS_PER_GROUP + j], out)
        return out

    in_b = [pick(rows, j) for j in range(EXPERTS_PER_GROUP)]
    v1, i1 = in_b[0], jnp.zeros_like(gsel)
    for j in range(1, EXPERTS_PER_GROUP):
        upd = in_b[j] > v1
        v1 = jnp.where(upd, in_b[j], v1)
        i1 = jnp.where(upd, j, i1)
    v2 = jnp.full_like(v1, -jnp.inf)
    i2 = jnp.zeros_like(gsel)
    for j in range(EXPERTS_PER_GROUP):
        upd = (i1 != j) & (in_b[j] > v2)
        v2 = jnp.where(upd, in_b[j], v2)
        i2 = jnp.where(upd, j, i2)
    first_is_lo = i1 < i2
    lo = jnp.where(first_is_lo, i1, i2)
    hi = jnp.where(first_is_lo, i2, i1)
    pair = jnp.where(lo == 0, hi - 1, jnp.where(lo == 1, hi + 1, PAIRS_PER_GROUP - 1))
    cls = gsel * PAIRS_PER_GROUP + pair

    cid = lax.broadcasted_iota(jnp.int32, (CLASS_ROWS, tr), 0)
    onehot = jnp.where(cid == cls, 1.0, 0.0)
    before = jnp.dot(onehot.astype(BF16), tri_ref[...], preferred_element_type=F32) + carry_ref[...]
    rank = jnp.sum(onehot * before, axis=0, keepdims=True).astype(jnp.int32)
    oi_ref[...] = cls * RANK_SPAN + rank
    new_carry = carry_ref[...] + jnp.sum(onehot, axis=1, keepdims=True)
    carry_ref[...] = new_carry
    cnt_ref[...] = new_carry.astype(jnp.int32)


def _route(logits_t, router_bias):
    n = logits_t.shape[1]
    tr = _tile(512, n)
    return pl.pallas_call(
        _route_kernel,
        grid=(n // tr,),
        in_specs=[
            pl.BlockSpec((N_EXPERTS, tr), lambda i: (0, i)),
            pl.BlockSpec((N_EXPERTS, 1), lambda i: (0, 0)),
        ],
        out_specs=[
            pl.BlockSpec((1, tr), lambda i: (0, i)),
            pl.BlockSpec((CLASS_ROWS, 1), lambda i: (0, 0)),
        ],
        out_shape=[
            jax.ShapeDtypeStruct((1, n), jnp.int32),
            jax.ShapeDtypeStruct((CLASS_ROWS, 1), jnp.int32),
        ],
        scratch_shapes=[pltpu.VMEM((CLASS_ROWS, 1), F32), pltpu.VMEM((tr, tr), BF16)],
        compiler_params=_params(("arbitrary",)),
        name="route",
    )(logits_t, router_bias.reshape(N_EXPERTS, 1).astype(F32))


SORT_UNROLL = 8
MOE_ROWS = 256
MOE_ROWS_LOG2 = MOE_ROWS.bit_length() - 1
assert 1 << MOE_ROWS_LOG2 == MOE_ROWS


def _plan_kernel(code_ref, cnt_ref, tok_ref, ea_ref, eb_ref, off_ref, nv_ref, nu_ref, start_ref):
    n_blocks = ea_ref.shape[0]
    run = jnp.int32(0)
    blk = jnp.int32(0)
    for c in range(N_CLASSES):
        cnt = cnt_ref[c]
        start_ref[c] = run - c * RANK_SPAN
        e_lo = (c // PAIRS_PER_GROUP) * EXPERTS_PER_GROUP + PAIR_LO[c % PAIRS_PER_GROUP]
        e_hi = (c // PAIRS_PER_GROUP) * EXPERTS_PER_GROUP + PAIR_HI[c % PAIRS_PER_GROUP]
        n_blk = (cnt + (MOE_ROWS - 1)) >> MOE_ROWS_LOG2

        def fill(b, carry, run=run, blk=blk, cnt=cnt, e_lo=e_lo, e_hi=e_hi):
            ea_ref[blk + b] = e_lo
            eb_ref[blk + b] = e_hi
            off_ref[blk + b] = run + b * MOE_ROWS
            nv_ref[blk + b] = jnp.minimum(cnt - b * MOE_ROWS, MOE_ROWS)
            return carry

        lax.fori_loop(0, n_blk, fill, 0)
        run = run + cnt
        blk = blk + n_blk
    nu_ref[0] = blk

    def fill_unused(b, carry):
        ea_ref[b] = ea_ref[blk - 1]
        eb_ref[b] = eb_ref[blk - 1]
        off_ref[b] = 0
        nv_ref[b] = 0
        return carry

    lax.fori_loop(blk, n_blocks, fill_unused, 0)

    def place(i, carry):
        toks = [i * SORT_UNROLL + u for u in range(SORT_UNROLL)]
        codes = [code_ref[t] for t in toks]
        slots = [start_ref[code >> RANK_BITS] + code for code in codes]
        for t, p in zip(toks, slots):
            tok_ref[p] = t
        return carry

    lax.fori_loop(0, code_ref.shape[0] // SORT_UNROLL, place, 0)


def _plan(codes, counts, n_blocks):
    n = codes.shape[0]
    assert n % SORT_UNROLL == 0 and n <= RANK_SPAN
    smem = pl.BlockSpec(memory_space=pltpu.SMEM)
    i32 = lambda size: jax.ShapeDtypeStruct((size,), jnp.int32)
    return pl.pallas_call(
        _plan_kernel,
        in_specs=[smem, smem],
        out_specs=[smem] * 6,
        out_shape=[i32(n), i32(n_blocks), i32(n_blocks), i32(n_blocks), i32(n_blocks), i32(1)],
        scratch_shapes=[pltpu.SMEM((N_CLASSES,), jnp.int32)],
        name="moe_plan",
    )(codes, counts)


ROW_UNROLL = 8


def _for_rows(n, fn):
    groups = n // ROW_UNROLL

    def group(g, c):
        for u in range(ROW_UNROLL):
            fn(g * ROW_UNROLL + u)
        return c

    def single(r, c):
        fn(r)
        return c

    lax.fori_loop(0, groups, group, 0)
    lax.fori_loop(groups * ROW_UNROLL, n, single, 0)


def _row_copy(src_ref, src_row, dst_ref, dst_row, sem, sublanes=ROW_TILES):
    return pltpu.make_async_copy(
        src_ref.at[pl.ds(pl.multiple_of(src_row * ROW_TILES, ROW_TILES), sublanes)],
        dst_ref.at[pl.ds(pl.multiple_of(dst_row * ROW_TILES, ROW_TILES), sublanes)],
        sem)


def _expert_kernel(ea_ref, eb_ref, off_ref, nv_ref, nu_ref, tok_ref,
                   h_hbm, wgu_a_ref, wd_a_ref, wgu_b_ref, wd_b_ref, wr_a_ref, wr_b_ref, y_hbm,
                   xbuf, ybuf, gsem, ssem):
    del ea_ref, eb_ref
    tb = xbuf.shape[1] // ROW_TILES
    i = pl.program_id(0)
    slot = i % 2
    n_used = nu_ref[0]

    def start_gather(block, s):
        base = off_ref[block]
        _for_rows(nv_ref[block],
                  lambda r: _row_copy(h_hbm, tok_ref[base + r], xbuf.at[s], r, gsem.at[s], PACKED_TILES).start())

    def start_scatter(block, s):
        base = off_ref[block]
        _for_rows(nv_ref[block],
                  lambda r: _row_copy(ybuf.at[s], r, y_hbm, tok_ref[base + r], ssem.at[s]).start())

    def wait_rows(block, hbm, buf, sem, sublanes=ROW_TILES):
        rows = nv_ref[block] * sublanes

        @pl.when(rows > 0)
        def _():
            pltpu.make_async_copy(hbm.at[pl.ds(0, rows)], buf.at[pl.ds(0, rows)], sem).wait()

    @pl.when(i == 0)
    def _():
        xbuf[...] = jnp.zeros_like(xbuf)
        start_gather(0, 0)

    @pl.when(i < n_used)
    def _():
        @pl.when(i + 1 < n_used)
        def _():
            start_gather(i + 1, 1 - slot)

        wait_rows(i, h_hbm, xbuf.at[slot], gsem.at[slot], PACKED_TILES)

        @pl.when(i >= 2)
        def _():
            wait_rows(i - 2, y_hbm, ybuf.at[slot], ssem.at[slot])

        words = [xbuf[slot, pl.ds(j, tb, stride=ROW_TILES), :] for j in range(PACKED_TILES)]
        halves = [_unpack_bf16_pairs(w) for w in words]
        x = jnp.concatenate([h[0] for h in halves] + [h[1] for h in halves], axis=1)

        def mlp(wgu_ref, wd_ref):
            gu = jnp.dot(x, wgu_ref[0], preferred_element_type=F32)
            gate = gu[:, :D_EXPERT]
            act = gate * jax.nn.sigmoid(gate) * gu[:, D_EXPERT:]
            return jnp.dot(act.astype(BF16), wd_ref[0], preferred_element_type=F32)

        score_a = jax.nn.sigmoid(jnp.dot(x, wr_a_ref[0], preferred_element_type=F32))
        score_b = jax.nn.sigmoid(jnp.dot(x, wr_b_ref[0], preferred_element_type=F32))
        inv_tot = 1.0 / (score_a + score_b)
        w_a = score_a * inv_tot
        w_b = score_b * inv_tot
        ya = mlp(wgu_a_ref, wd_a_ref)
        yb = mlp(wgu_b_ref, wd_b_ref)
        for j in range(ROW_TILES):
            sl = slice(j * LANES, (j + 1) * LANES)
            ybuf[slot, pl.ds(j, tb, stride=ROW_TILES), :] = w_a * ya[:, sl] + w_b * yb[:, sl]
        start_scatter(i, slot)

        @pl.when(i == n_used - 1)
        def _():
            wait_rows(i, y_hbm, ybuf.at[slot], ssem.at[slot])

            @pl.when(i >= 1)
            def _():
                wait_rows(i - 1, y_hbm, ybuf.at[1 - slot], ssem.at[1 - slot])


def _experts(tables, sorted_tok, h_rows, layer, w_gu, w_down, wr_bcast, tb):
    block_ea, block_eb, src_off, n_valid, n_used = tables
    n_blocks = block_ea.shape[0]
    d, two_f = w_gu.shape[2:]
    ea_map = lambda i, ea, eb, off, nv, nu, tok: (ea[i], 0, 0)
    eb_map = lambda i, ea, eb, off, nv, nu, tok: (eb[i], 0, 0)
    lea_map = lambda i, ea, eb, off, nv, nu, tok: (layer, ea[i], 0, 0)
    leb_map = lambda i, ea, eb, off, nv, nu, tok: (layer, eb[i], 0, 0)
    grid_spec = pltpu.PrefetchScalarGridSpec(
        num_scalar_prefetch=6,
        grid=(n_blocks,),
        in_specs=[
            pl.BlockSpec(memory_space=pl.ANY),
            pl.BlockSpec((None, 1, d, two_f), lea_map),
            pl.BlockSpec((None, 1, two_f // 2, d), lea_map),
            pl.BlockSpec((None, 1, d, two_f), leb_map),
            pl.BlockSpec((None, 1, two_f // 2, d), leb_map),
            pl.BlockSpec((1, d, LANES), ea_map),
            pl.BlockSpec((1, d, LANES), eb_map),
        ],
        out_specs=pl.BlockSpec(memory_space=pl.ANY),
        scratch_shapes=[
            pltpu.VMEM((2, tb * ROW_TILES, LANES), jnp.uint32),
            pltpu.VMEM((2, tb * ROW_TILES, LANES), F32),
            pltpu.SemaphoreType.DMA((2,)),
            pltpu.SemaphoreType.DMA((2,)),
        ],
    )
    return pl.pallas_call(
        _expert_kernel,
        grid_spec=grid_spec,
        out_shape=jax.ShapeDtypeStruct(h_rows.shape, F32),
        compiler_params=_params(("arbitrary",)),
        name="moe_experts",
    )(block_ea, block_eb, src_off, n_valid, n_used, sorted_tok, h_rows, w_gu, w_down, w_gu, w_down, wr_bcast, wr_bcast)


def _residual_kernel(y_ref, x_ref, g2_ref, o_ref):
    o_ref[...] = _gated_residual(x_ref, y_ref, g2_ref)


def _residual(y_rows, x_mid, g2, seq):
    n, d = x_mid.shape
    tc = _tile(512, seq)
    per_b = seq // tc
    return pl.pallas_call(
        _residual_kernel,
        grid=(n // tc,),
        in_specs=[
            pl.BlockSpec((tc * ROW_TILES, LANES), lambda i: (i, 0)),
            pl.BlockSpec((tc, d), lambda i: (i, 0)),
            pl.BlockSpec((1, 1, d), lambda i: (i // per_b, 0, 0)),
        ],
        out_specs=pl.BlockSpec((tc, d), lambda i: (i, 0)),
        out_shape=jax.ShapeDtypeStruct((n, d), F32),
        compiler_params=_params(("parallel",)),
        name="moe_residual",
    )(y_rows, x_mid, g2)


def _moe(h_rows, logits_t, router_bias, layer, w_gu, w_down, wr_bcast):
    n = logits_t.shape[1]
    n_blocks = (n + N_CLASSES * (MOE_ROWS - 1) + MOE_ROWS - 1) // MOE_ROWS
    codes, counts = _route(logits_t, router_bias)
    sorted_tok, *tables = _plan(codes.reshape(n), counts.reshape(CLASS_ROWS), n_blocks)
    return _experts(tables, sorted_tok, h_rows, layer, w_gu, w_down, wr_bcast, MOE_ROWS)


def kernel(x, c, positions, norm1, norm2, w_ada, b_ada, a_w_in, a_b_if, a_h_norm, a_w_out, b_w_in, b_q_norm, b_k_norm, b_lam_q1, b_lam_k1, b_lam_q2, b_lam_k2, b_o_norm, b_w_out, w_router, router_bias, moe_w_gu, moe_w_down):
    bsz, seq, d = x.shape
    depth = w_ada.shape[0]
    n = bsz * seq
    xf = x.reshape(n, d)
    mod = _ada_mod(c, w_ada, b_ada)
    w_router_t = w_router.T
    wr_bcast = jnp.broadcast_to(w_router_t[:, :, None], (N_EXPERTS, d, LANES)).astype(BF16)
    pos_row = positions.reshape(n)
    w_gu = moe_w_gu.astype(BF16)
    w_down = moe_w_down.astype(BF16)

    residual = None
    for l in range(depth):
        sh1, sc1, g1, sh2, sc2, g2 = [mod[l, :, i * d:(i + 1) * d].reshape(bsz, 1, d) for i in range(6)]
        j = l // 2
        if l % 2 == 0:
            w_in = a_w_in[j]
            w_main = w_in[:, :A_MAIN_COLS].astype(BF16)
            w_gate = jnp.pad(w_in[:, A_MAIN_COLS:], ((0, 0), (0, LANES - 2 * A_HEADS))).astype(BF16)
        else:
            w_in = b_w_in[j]
            w_qk = _interleave_maps(w_in[:, :2 * D_MODEL])
            w_main, w_gate = jnp.concatenate([w_qk, w_in[:, 2 * D_MODEL:]], axis=1).astype(BF16), None
        outs = _norm_matmul(xf, residual, norm1[l].reshape(1, d), 1.0 + sc1, sh1, w_main, w_gate, seq)
        if residual is not None:
            xf, *outs = outs
        if l % 2 == 0:
            proj, gates = outs
            bias_row = jnp.pad(a_b_if[j], (0, LANES - 2 * A_HEADS)).reshape(1, LANES)
            mixed = _mlstm(proj, gates, bias_row, a_h_norm[j], bsz, seq)
            w_out = a_w_out[j].astype(BF16)
        else:
            (proj,) = outs
            qk = _qk_prep(proj, pos_row, b_q_norm[j], b_k_norm[j], seq)
            lam_tab = jnp.zeros((SUBLANES, LANES), F32)
            for r, v in enumerate((b_lam_q1[j], b_lam_k1[j], b_lam_q2[j], b_lam_k2[j])):
                lam_tab = lam_tab.at[r, :B_HEAD_DIM].set(v)
            lam_init = 0.8 - 0.6 * math.exp(-0.3 * l)
            mixed = _diff_attn(qk, proj, lam_tab, b_o_norm[j], bsz, seq, lam_init)
            w_out = b_w_out[j].astype(BF16)
        xf, h_rows, logits_t = _outproj_norm_router(
            mixed, w_out, xf, g1, norm2[l].reshape(1, d), 1.0 + sc2, sh2, w_router_t, seq)
        residual = (_moe(h_rows, logits_t, router_bias, l, w_gu, w_down, wr_bcast), g2)
    return _residual(residual[0], xf, residual[1], seq).reshape(bsz, seq, d)
```

```python
import functools
import math

import jax
import jax.numpy as jnp
from jax import lax
from jax.experimental import pallas as pl
from jax.experimental.pallas import tpu as pltpu

D_MODEL = 1024
A_HEADS = 4
A_QK_DIM = 128
A_V_DIM = 256
A_CHUNK = 128
A_HQ = A_HEADS * A_QK_DIM
A_HV = A_HEADS * A_V_DIM
A_MAIN_COLS = 2 * A_HQ + 2 * A_HV

B_HEADS = 8
B_HEAD_DIM = 64
B_V_DIM = 128
B_ROT_DIM = 16
ROPE_THETA = 500000.0

N_EXPERTS = 16
N_GROUPS = 4
EXPERTS_PER_GROUP = 4
TOP_K = 2
D_EXPERT = 512
EPS = 1e-6

LANES = 128
SUBLANES = 8
ROW_TILES = D_MODEL // LANES
assert ROW_TILES == SUBLANES
VMEM_LIMIT = 48 * 1024 * 1024

F32 = jnp.float32
BF16 = jnp.bfloat16
HIGHEST = lax.Precision.HIGHEST


def _params(sem):
    return pltpu.CompilerParams(dimension_semantics=sem, vmem_limit_bytes=VMEM_LIMIT)


def _tile(pref, n):
    t = min(pref, n)
    assert n % t == 0, (pref, n)
    return t


def _ada_kernel(c_ref, w_ref, b_ref, o_ref):
    c = c_ref[...]
    c_act = c * jax.nn.sigmoid(c)
    o_ref[0] = jnp.dot(c_act, w_ref[0], precision=HIGHEST, preferred_element_type=F32) + b_ref[0]


def _ada_mod(c, w_ada, b_ada):
    depth, d, six_d = w_ada.shape
    bsz = c.shape[0]
    tn = _tile(1536, six_d)
    return pl.pallas_call(
        _ada_kernel,
        grid=(depth, six_d // tn),
        in_specs=[
            pl.BlockSpec((bsz, d), lambda l, j: (0, 0)),
            pl.BlockSpec((1, d, tn), lambda l, j: (l, 0, j)),
            pl.BlockSpec((1, 1, tn), lambda l, j: (l, 0, j)),
        ],
        out_specs=pl.BlockSpec((1, bsz, tn), lambda l, j: (l, 0, j)),
        out_shape=jax.ShapeDtypeStruct((depth, bsz, six_d), F32),
        compiler_params=_params(("parallel", "parallel")),
        name="ada_mod",
    )(c, w_ada, b_ada.reshape(depth, 1, six_d))


def _modulated_rms(x, g, sc1p, sh):
    y = x * lax.rsqrt(jnp.mean(x * x, axis=-1, keepdims=True) + EPS)
    return (y * g) * sc1p + sh


def _gated_residual(x_ref, y_ref, g_ref):
    rows = x_ref.shape[0]
    g = g_ref[0]
    return jnp.concatenate(
        [x_ref[:, j * LANES:(j + 1) * LANES]
         + g[:, j * LANES:(j + 1) * LANES] * y_ref[pl.ds(j, rows, stride=ROW_TILES), :] for j in range(ROW_TILES)],
        axis=1)


def _norm_mm_kernel(*refs, col_chunk, has_gates, has_residual):
    refs = list(refs)
    x_ref = refs.pop(0)
    if has_residual:
        y_ref, g2_ref = refs.pop(0), refs.pop(0)
    g_ref, sc_ref, sh_ref, w_ref = refs[:4]
    refs = refs[4:]
    wg_ref = refs.pop(0) if has_gates else None
    xo_ref = refs.pop(0) if has_residual else None
    o_ref = refs.pop(0)
    og_ref = refs.pop(0) if has_gates else None

    if has_residual:
        x = _gated_residual(x_ref, y_ref, g2_ref)
        xo_ref[...] = x
    else:
        x = x_ref[...]
    hb = _modulated_rms(x, g_ref[...], sc_ref[0], sh_ref[0]).astype(BF16)
    for c0 in range(0, o_ref.shape[1], col_chunk):
        o_ref[:, c0:c0 + col_chunk] = jnp.dot(
            hb, w_ref[:, c0:c0 + col_chunk], preferred_element_type=F32).astype(o_ref.dtype)
    if has_gates:
        og_ref[...] = jnp.dot(hb, wg_ref[...], preferred_element_type=F32)


def _norm_matmul(x, residual, gain, sc1p, sh, w, wg, seq):
    n, d = x.shape
    cols = w.shape[1]
    tm = _tile(512, seq)
    per_b = seq // tm
    has_gates = wg is not None
    has_residual = residual is not None
    bmap = lambda i: (i // per_b, 0, 0)
    in_specs = [pl.BlockSpec((tm, d), lambda i: (i, 0))]
    args = [x]
    if has_residual:
        in_specs += [pl.BlockSpec((tm * ROW_TILES, LANES), lambda i: (i, 0)), pl.BlockSpec((1, 1, d), bmap)]
        args += list(residual)
    in_specs += [
        pl.BlockSpec((1, d), lambda i: (0, 0)),
        pl.BlockSpec((1, 1, d), bmap),
        pl.BlockSpec((1, 1, d), bmap),
        pl.BlockSpec((d, cols), lambda i: (0, 0)),
    ]
    args += [gain, sc1p, sh, w]
    out_specs, out_shape = [], []
    if has_residual:
        out_specs.append(pl.BlockSpec((tm, d), lambda i: (i, 0)))
        out_shape.append(jax.ShapeDtypeStruct((n, d), F32))
    out_specs.append(pl.BlockSpec((tm, cols), lambda i: (i, 0)))
    out_shape.append(jax.ShapeDtypeStruct((n, cols), BF16))
    if has_gates:
        in_specs.append(pl.BlockSpec((d, LANES), lambda i: (0, 0)))
        out_specs.append(pl.BlockSpec((tm, LANES), lambda i: (i, 0)))
        out_shape.append(jax.ShapeDtypeStruct((n, LANES), F32))
        args.append(wg)
    return pl.pallas_call(
        functools.partial(_norm_mm_kernel, col_chunk=512, has_gates=has_gates, has_residual=has_residual),
        grid=(n // tm,),
        in_specs=in_specs,
        out_specs=out_specs,
        out_shape=out_shape,
        compiler_params=_params(("parallel",)),
        name="norm_inproj",
    )(*args)


def _log_sigmoid(x):
    return jnp.minimum(x, 0.0) - jnp.log1p(jnp.exp(-jnp.abs(x)))


def _mlstm_chunk(r0, p_ref, g_ref, bias_ref, hn_ref, o_ref, ct_ref, n_ref, m_ref):
    L = A_CHUNK
    rows = slice(r0, r0 + L)
    gates = g_ref[rows, :] + bias_ref[...]
    src = lax.broadcasted_iota(jnp.int32, (L, L), 0)
    tgt = lax.broadcasted_iota(jnp.int32, (L, L), 1)
    causal = src <= tgt
    bcum = jnp.dot(jnp.where(src >= tgt, 1.0, 0.0), _log_sigmoid(gates), precision=HIGHEST,
                   preferred_element_type=F32)
    bcum_t = bcum.T
    nt = (((1,), (1,)), ((), ()))

    pre = []
    for h in range(A_HEADS):
        q = p_ref[rows, h * A_QK_DIM:(h + 1) * A_QK_DIM]
        ksf = p_ref[rows, A_HQ + h * A_QK_DIM:A_HQ + (h + 1) * A_QK_DIM].astype(F32) * (A_QK_DIM ** -0.5)
        v = p_ref[rows, 2 * A_HQ + h * A_V_DIM:2 * A_HQ + (h + 1) * A_V_DIM]
        ct_old = ct_ref[h]
        n_old = n_ref[h]
        qk = lax.dot_general(ksf.astype(BF16), q, nt, preferred_element_type=F32)
        qc = lax.dot_general(ct_old.astype(BF16), q, nt, preferred_element_type=F32)
        qn = lax.dot_general(n_old.astype(BF16), q, nt, preferred_element_type=F32)[0:1]
        vt = v.astype(F32).T.astype(BF16)
        pre.append((ksf, vt, ct_old, n_old, qk, qc, qn))

    for h in range(A_HEADS):
        ksf, vt, ct_old, n_old, qk, qc, qn = pre[h]
        og = p_ref[rows, 2 * A_HQ + A_HV + h * A_V_DIM:2 * A_HQ + A_HV + (h + 1) * A_V_DIM].astype(F32)
        b_row = bcum_t[A_HEADS + h:A_HEADS + h + 1, :]
        b_last = b_row[:, L - 1:L]
        c_col = gates[:, h:h + 1] - bcum[:, A_HEADS + h:A_HEADS + h + 1]
        m11 = m_ref[h][:, 0:1]

        log_d = jnp.where(causal, c_col + b_row, -jnp.inf)
        log_inter = b_row + m11
        m_t = jnp.maximum(jnp.max(log_d, axis=0, keepdims=True), log_inter)
        dmat = jnp.exp(log_d - m_t)
        inter = jnp.exp(log_inter - m_t)
        s = qk * dmat
        num = jnp.dot(vt, s.astype(BF16), preferred_element_type=F32) + inter * qc
        den = jnp.sum(s, axis=0, keepdims=True) + inter * qn
        hh = num / jnp.maximum(jnp.abs(den), jnp.exp(-m_t))

        lw_col = b_last + c_col
        m_new = jnp.maximum(b_last + m11, jnp.max(lw_col, axis=0, keepdims=True))
        kw = ksf * jnp.exp(lw_col - m_new)
        decay = jnp.exp(b_last + m11 - m_new)
        ct_ref[h] = decay * ct_old + jnp.dot(vt, kw.astype(BF16), preferred_element_type=F32)
        n_ref[h] = decay * n_old + jnp.broadcast_to(jnp.sum(kw, axis=0, keepdims=True), n_old.shape)
        m_ref[h] = jnp.broadcast_to(m_new, (1, LANES))

        hn = hh * lax.rsqrt(jnp.mean(hh * hh, axis=0, keepdims=True) + EPS)
        hn = (hn * hn_ref[h * A_V_DIM:(h + 1) * A_V_DIM, :]).T
        o_ref[rows, h * A_V_DIM:(h + 1) * A_V_DIM] = (hn * jax.nn.sigmoid(og)).astype(o_ref.dtype)


MLSTM_CHUNKS_PER_STEP = 2


def _mlstm_kernel(p_ref, g_ref, bias_ref, hn_ref, o_ref, ct_ref, n_ref, m_ref):
    @pl.when(pl.program_id(1) == 0)
    def _():
        ct_ref[...] = jnp.zeros_like(ct_ref)
        n_ref[...] = jnp.zeros_like(n_ref)
        m_ref[...] = jnp.zeros_like(m_ref)

    for r0 in range(0, p_ref.shape[0], A_CHUNK):
        _mlstm_chunk(r0, p_ref, g_ref, bias_ref, hn_ref, o_ref, ct_ref, n_ref, m_ref)


def _mlstm(proj, gates, bias_row, h_norm, bsz, seq):
    n = proj.shape[0]
    rows = _tile(MLSTM_CHUNKS_PER_STEP * A_CHUNK, seq)
    nc = seq // rows
    h_norm_cols = jnp.broadcast_to(h_norm[:, None], (A_HV, A_CHUNK))
    return pl.pallas_call(
        _mlstm_kernel,
        grid=(bsz, nc),
        in_specs=[
            pl.BlockSpec((rows, A_MAIN_COLS), lambda b, c: (b * nc + c, 0)),
            pl.BlockSpec((rows, LANES), lambda b, c: (b * nc + c, 0)),
            pl.BlockSpec((1, LANES), lambda b, c: (0, 0)),
            pl.BlockSpec((A_HV, A_CHUNK), lambda b, c: (0, 0)),
        ],
        out_specs=pl.BlockSpec((rows, A_HV), lambda b, c: (b * nc + c, 0)),
        out_shape=jax.ShapeDtypeStruct((n, A_HV), BF16),
        scratch_shapes=[
            pltpu.VMEM((A_HEADS, A_V_DIM, A_QK_DIM), F32),
            pltpu.VMEM((A_HEADS, SUBLANES, A_QK_DIM), F32),
            pltpu.VMEM((A_HEADS, 1, LANES), F32),
        ],
        compiler_params=_params(("parallel", "arbitrary")),
        name="mlstm",
    )(proj, gates, bias_row, h_norm_cols)


def _qk_prep_kernel(p_ref, pos_ref, seg_ref, segt_ref, gain_ref, freq_ref, o_ref):
    tm = p_ref.shape[0]
    x = p_ref[...].astype(F32)
    ss = jnp.dot((x * x).astype(BF16), seg_ref[...], preferred_element_type=F32)
    r = lax.rsqrt(ss * (1.0 / B_HEAD_DIM) + EPS)
    r_hi = r.astype(BF16)
    r_lo = (r - r_hi.astype(F32)).astype(BF16)
    r_full = jnp.dot(jnp.concatenate([r_hi, r_lo], axis=1), segt_ref[...], preferred_element_type=F32)

    assert B_ROT_DIM // 2 == SUBLANES
    ang = freq_ref[:, 0:1] * pos_ref[0].astype(F32)
    cos8 = jnp.cos(ang)
    sin8 = jnp.sin(ang)
    one8 = jnp.ones_like(cos8)
    zero8 = jnp.zeros_like(cos8)
    groups_per_quarter = MAP_QUARTER // SUBLANES

    def chunk_pattern(lower, upper, rest):
        quarter = lambda lead: [lead] + [rest] * (groups_per_quarter - 1)
        return jnp.concatenate(quarter(lower) * 2 + quarter(upper) * 2, axis=0).T

    c_mul = chunk_pattern(cos8, cos8, one8)
    s_mul = chunk_pattern(-sin8, sin8, zero8)
    for j in range(2 * D_MODEL // LANES):
        sl = slice(j * LANES, (j + 1) * LANES)
        y = x[:, sl] * r_full[:, sl] * gain_ref[:, sl]
        out = y * c_mul + pltpu.roll(y, LANES // 2, axis=1) * s_mul
        if j < D_MODEL // LANES:
            out = out * (B_HEAD_DIM ** -0.5 * math.log2(math.e))
        o_ref[:, sl] = out.astype(o_ref.dtype)


MAP_QUARTER = B_HEAD_DIM // 2


def _interleave_maps(t):
    half = B_ROT_DIM // 2
    rest = MAP_QUARTER - half
    lead = t.shape[:-1]
    t = t.reshape(*lead, 2 * B_HEADS, 2, B_HEAD_DIM)
    lower = jnp.concatenate([t[..., 0:half], t[..., B_ROT_DIM:B_ROT_DIM + rest]], axis=-1)
    upper = jnp.concatenate([t[..., half:B_ROT_DIM], t[..., B_ROT_DIM + rest:]], axis=-1)
    out = jnp.stack([lower, upper], axis=-3)
    return out.reshape(*lead, 2 * D_MODEL)


def _qk_prep(proj, positions_row, q_norm, k_norm, seq):
    n = proj.shape[0]
    two_d = 2 * D_MODEL
    tm = _tile(512, seq)
    seg_of_col = _interleave_maps(jnp.arange(two_d) // B_HEAD_DIM)
    seg = (seg_of_col[:, None] == jnp.arange(LANES)[None, :]).astype(BF16)
    gain = _interleave_maps(
        jnp.concatenate([jnp.tile(q_norm, D_MODEL // B_HEAD_DIM), jnp.tile(k_norm, D_MODEL // B_HEAD_DIM)]))
    inv_freq = ROPE_THETA ** (-jnp.arange(0, B_ROT_DIM, 2, dtype=F32) / B_ROT_DIM)
    freq_tab = jnp.broadcast_to(inv_freq[:, None], (B_ROT_DIM // 2, LANES))
    return pl.pallas_call(
        _qk_prep_kernel,
        grid=(n // tm,),
        in_specs=[
            pl.BlockSpec((tm, two_d), lambda i: (i, 0)),
            pl.BlockSpec((1, 1, tm), lambda i: (i, 0, 0)),
            pl.BlockSpec((two_d, LANES), lambda i: (0, 0)),
            pl.BlockSpec((2 * LANES, two_d), lambda i: (0, 0)),
            pl.BlockSpec((1, two_d), lambda i: (0, 0)),
            pl.BlockSpec((SUBLANES, LANES), lambda i: (0, 0)),
        ],
        out_specs=pl.BlockSpec((tm, two_d), lambda i: (i, 0)),
        out_shape=jax.ShapeDtypeStruct((n, two_d), BF16),
        compiler_params=_params(("parallel",)),
        name="qk_prep",
    )(proj, positions_row.reshape(n // tm, 1, tm), seg, jnp.concatenate([seg.T, seg.T], axis=0),
      gain.reshape(1, two_d), freq_tab)


ATTN_TILE = 512
ATTN_KEY_TILE = 256
ATTN_HEADS_PER_STEP = 4


def _diff_attn_kernel(q_ref, k_ref, v_ref, lam_ref, on_ref, o_ref, vt_ref, qm_ref, m_ref, l_ref, acc_ref,
                      *, tile, tk, lam_init):
    seq = q_ref.shape[0]
    heads = q_ref.shape[1] // LANES
    n_tiles = seq // tile
    k_per_q = tile // tk
    lamv = lam_ref[...]
    lam = (jnp.exp(jnp.sum(lamv[0:1] * lamv[1:2], axis=1, keepdims=True))
           - jnp.exp(jnp.sum(lamv[2:3] * lamv[3:4], axis=1, keepdims=True)) + lam_init)
    head_lanes = lambda h: slice(h * LANES, (h + 1) * LANES)
    tchunk = min(256, seq)
    for h in range(heads):
        for c in range(seq // tchunk):
            rows = slice(c * tchunk, (c + 1) * tchunk)
            vt_ref[h, :, rows] = v_ref[rows, head_lanes(h)].astype(F32).T.astype(BF16)
    first_map = (lax.broadcasted_iota(jnp.int32, (tile, LANES), 1) // MAP_QUARTER) % 2 == 0
    key_id = lax.broadcasted_iota(jnp.int32, (tk, tile), 0)
    query_id = lax.broadcasted_iota(jnp.int32, (tk, tile), 1)
    nt = (((1,), (1,)), ((), ()))

    def qk_score(h, qi, ki, c):
        return lax.dot_general(k_ref[ki * tk:(ki + 1) * tk, head_lanes(h)], qm_ref[(h * n_tiles + qi) * 2 + c], nt,
                               preferred_element_type=F32)

    def softmax_pv(h, c, s, ki, mask):
        j = 2 * h + c
        vt = vt_ref[h, :, ki * tk:(ki + 1) * tk]
        if mask is not None:
            s = jnp.where(mask, s, -jnp.inf)
        m_old = m_ref[j]
        m_new = jnp.maximum(m_old, jnp.max(s, axis=0, keepdims=True))
        alpha = jnp.exp2(m_old - m_new)
        p = jnp.exp2(s - m_new)
        l_ref[j] = alpha * l_ref[j] + jnp.sum(p, axis=0, keepdims=True)
        acc_ref[j] = alpha * acc_ref[j] + jnp.dot(vt, p.astype(BF16), preferred_element_type=F32)
        m_ref[j] = m_new

    def reset(h):
        for j in (2 * h, 2 * h + 1):
            m_ref[j] = jnp.full(m_ref.shape[1:], -jnp.inf, F32)
            l_ref[j] = jnp.zeros(l_ref.shape[1:], F32)
            acc_ref[j] = jnp.zeros(acc_ref.shape[1:], F32)

    for h in range(heads):
        for qi in range(n_tiles):
            q = q_ref[qi * tile:(qi + 1) * tile, head_lanes(h)]
            qm_ref[(h * n_tiles + qi) * 2] = jnp.where(first_map, q, jnp.zeros_like(q))
            qm_ref[(h * n_tiles + qi) * 2 + 1] = jnp.where(first_map, jnp.zeros_like(q), q)
        reset(h)
    steps = [(h, qi, ki) for qi in range(n_tiles) for ki in range((qi + 1) * k_per_q) for h in range(heads)]
    scores_next = [qk_score(*steps[0], c) for c in range(2)]
    for n, (h, qi, ki) in enumerate(steps):
        scores = scores_next
        scores_next = [None, None]
        on_diagonal = ki >= qi * k_per_q
        mask = (query_id + qi * tile >= key_id + ki * tk) if on_diagonal else None
        for c in range(2):
            if n + 1 < len(steps):
                scores_next[c] = qk_score(*steps[n + 1], c)
            softmax_pv(h, c, scores[c], ki, mask)
        if ki == (qi + 1) * k_per_q - 1:
            o = acc_ref[2 * h] / l_ref[2 * h] - lam * (acc_ref[2 * h + 1] / l_ref[2 * h + 1])
            o = o * lax.rsqrt(jnp.mean(o * o, axis=0, keepdims=True) + EPS) * on_ref[...] * (1.0 - lam_init)
            o_ref[qi * tile:(qi + 1) * tile, head_lanes(h)] = o.T.astype(o_ref.dtype)
            if qi + 1 < n_tiles:
                reset(h)


def _diff_attn(qk, proj, lam_tab, o_norm, bsz, seq, lam_init):
    n = qk.shape[0]
    hp = ATTN_HEADS_PER_STEP
    groups = B_HEADS // hp
    tile = _tile(ATTN_TILE, seq)
    o_norm_cols = jnp.broadcast_to(o_norm[:, None], (B_V_DIM, tile))
    return pl.pallas_call(
        functools.partial(_diff_attn_kernel, tile=tile, tk=_tile(ATTN_KEY_TILE, tile), lam_init=lam_init),
        grid=(bsz, groups),
        in_specs=[
            pl.BlockSpec((seq, hp * LANES), lambda b, g: (b, g)),
            pl.BlockSpec((seq, hp * LANES), lambda b, g: (b, groups + g)),
            pl.BlockSpec((seq, hp * B_V_DIM), lambda b, g: (b, 2 * groups + g)),
            pl.BlockSpec((SUBLANES, LANES), lambda b, g: (0, 0)),
            pl.BlockSpec((B_V_DIM, tile), lambda b, g: (0, 0)),
        ],
        out_specs=pl.BlockSpec((seq, hp * B_V_DIM), lambda b, g: (b, g)),
        out_shape=jax.ShapeDtypeStruct((n, D_MODEL), BF16),
        scratch_shapes=[
            pltpu.VMEM((hp, B_V_DIM, seq), BF16),
            pltpu.VMEM((hp * 2 * (seq // tile), tile, LANES), BF16),
            pltpu.VMEM((2 * hp, 1, tile), F32),
            pltpu.VMEM((2 * hp, 1, tile), F32),
            pltpu.VMEM((2 * hp, B_V_DIM, tile), F32),
        ],
        compiler_params=_params(("parallel", "parallel")),
        name="diff_attn",
    )(qk, qk, proj, lam_tab, o_norm_cols)


HALF_D = D_MODEL // 2
HI16 = 0xFFFF0000
PACKED_TILES = HALF_D // LANES


def _pack_bf16_pairs(v):
    bits = lax.bitcast_convert_type(v.astype(BF16).astype(F32), jnp.uint32)
    return (bits[:, HALF_D:] & jnp.uint32(HI16)) | (bits[:, :HALF_D] >> 16)


def _unpack_bf16_pairs(words):
    first = lax.bitcast_convert_type(words << 16, F32)
    second = lax.bitcast_convert_type(words & jnp.uint32(HI16), F32)
    return first.astype(BF16), second.astype(BF16)


def _outproj_kernel(a_ref, w_ref, x_ref, g1_ref, gain_ref, sc_ref, sh_ref, wr_ref, xo_ref, hp_ref, lg_ref):
    y = jnp.dot(a_ref[...], w_ref[...], preferred_element_type=F32)
    xm = x_ref[...] + g1_ref[0] * y
    xo_ref[...] = xm
    h2 = _modulated_rms(xm, gain_ref[...], sc_ref[0], sh_ref[0])
    tm = xm.shape[0]
    packed = _pack_bf16_pairs(h2)
    for j in range(ROW_TILES):
        hp_ref[pl.ds(j, tm, stride=ROW_TILES), :] = (
            packed[:, j * LANES:(j + 1) * LANES] if j < PACKED_TILES else jnp.zeros((tm, LANES), jnp.uint32))
    nt = (((1,), (1,)), ((), ()))
    h_hi = h2.astype(BF16)
    h_lo = (h2 - h_hi.astype(F32)).astype(BF16)
    wr = wr_ref[...]
    wr_hi = wr.astype(BF16)
    wr_lo = (wr - wr_hi.astype(F32)).astype(BF16)
    lg_ref[...] = (lax.dot_general(wr_hi, h_hi, nt, preferred_element_type=F32)
                   + lax.dot_general(wr_hi, h_lo, nt, preferred_element_type=F32)
                   + lax.dot_general(wr_lo, h_hi, nt, preferred_element_type=F32))


def _outproj_norm_router(a, w, x, g1, gain, sc1p, sh, w_router_t, seq):
    n, d = x.shape
    tm = _tile(512, seq)
    per_b = seq // tm
    bmap = lambda i: (i // per_b, 0, 0)
    return pl.pallas_call(
        _outproj_kernel,
        grid=(n // tm,),
        in_specs=[
            pl.BlockSpec((tm, a.shape[1]), lambda i: (i, 0)),
            pl.BlockSpec(w.shape, lambda i: (0, 0)),
            pl.BlockSpec((tm, d), lambda i: (i, 0)),
            pl.BlockSpec((1, 1, d), bmap),
            pl.BlockSpec((1, d), lambda i: (0, 0)),
            pl.BlockSpec((1, 1, d), bmap),
            pl.BlockSpec((1, 1, d), bmap),
            pl.BlockSpec((N_EXPERTS, d), lambda i: (0, 0)),
        ],
        out_specs=[
            pl.BlockSpec((tm, d), lambda i: (i, 0)),
            pl.BlockSpec((tm * ROW_TILES, LANES), lambda i: (i, 0)),
            pl.BlockSpec((N_EXPERTS, tm), lambda i: (0, i)),
        ],
        out_shape=[
            jax.ShapeDtypeStruct((n, d), F32),
            jax.ShapeDtypeStruct((n * ROW_TILES, LANES), jnp.uint32),
            jax.ShapeDtypeStruct((N_EXPERTS, n), F32),
        ],
        compiler_params=_params(("parallel",)),
        name="outproj_norm_router",
    )(a, w, x, g1, gain, sc1p, sh, w_router_t)


PAIR_LO = (0, 0, 0, 1, 1, 2)
PAIR_HI = (1, 2, 3, 2, 3, 3)
PAIRS_PER_GROUP = len(PAIR_LO)
N_CLASSES = N_GROUPS * PAIRS_PER_GROUP
CLASS_ROWS = 32
RANK_BITS = 20
RANK_SPAN = 1 << RANK_BITS


def _route_kernel(lg_ref, bias_ref, oi_ref, cnt_ref, carry_ref, tri_ref):
    tr = lg_ref.shape[1]
    step = pl.program_id(0)

    @pl.when(step == 0)
    def _():
        carry_ref[...] = jnp.zeros_like(carry_ref)
        r = lax.broadcasted_iota(jnp.int32, (tr, tr), 0)
        c = lax.broadcasted_iota(jnp.int32, (tr, tr), 1)
        tri_ref[...] = jnp.where(r < c, 1.0, 0.0).astype(BF16)

    scores = jax.nn.sigmoid(lg_ref[...])
    biased = scores + bias_ref[...]
    rows = [biased[e:e + 1, :] for e in range(N_EXPERTS)]

    def top2_sum(a, b, c, d):
        m1, n1 = jnp.maximum(a, b), jnp.minimum(a, b)
        m2, n2 = jnp.maximum(c, d), jnp.minimum(c, d)
        return jnp.maximum(m1, m2) + jnp.maximum(jnp.minimum(m1, m2), jnp.maximum(n1, n2))

    gscore = [top2_sum(*rows[g * EXPERTS_PER_GROUP:(g + 1) * EXPERTS_PER_GROUP]) for g in range(N_GROUPS)]
    best = gscore[0]
    gsel = jnp.zeros_like(best, dtype=jnp.int32)
    for g in range(1, N_GROUPS):
        upd = gscore[g] > best
        gsel = jnp.where(upd, g, gsel)
        best = jnp.where(upd, gscore[g], best)

    def pick(table, j):
        out = table[j]
        for g in range(1, N_GROUPS):
            out = jnp.where(gsel == g, table[g * EXPERTS_PER_GROUP + j], out)
        return out

    in_b = [pick(rows, j) for j in range(EXPERTS_PER_GROUP)]
    v1, i1 = in_b[0], jnp.zeros_like(gsel)
    for j in range(1, EXPERTS_PER_GROUP):
        upd = in_b[j] > v1
        v1 = jnp.where(upd, in_b[j], v1)
        i1 = jnp.where(upd, j, i1)
    v2 = jnp.full_like(v1, -jnp.inf)
    i2 = jnp.zeros_like(gsel)
    for j in range(EXPERTS_PER_GROUP):
        upd = (i1 != j) & (in_b[j] > v2)
        v2 = jnp.where(upd, in_b[j], v2)
        i2 = jnp.where(upd, j, i2)
    first_is_lo = i1 < i2
    lo = jnp.where(first_is_lo, i1, i2)
    hi = jnp.where(first_is_lo, i2, i1)
    pair = jnp.where(lo == 0, hi - 1, jnp.where(lo == 1, hi + 1, PAIRS_PER_GROUP - 1))
    cls = gsel * PAIRS_PER_GROUP + pair

    cid = lax.broadcasted_iota(jnp.int32, (CLASS_ROWS, tr), 0)
    onehot = jnp.where(cid == cls, 1.0, 0.0)
    before = jnp.dot(onehot.astype(BF16), tri_ref[...], preferred_element_type=F32) + carry_ref[...]
    rank = jnp.sum(onehot * before, axis=0, keepdims=True).astype(jnp.int32)
    oi_ref[...] = cls * RANK_SPAN + rank
    new_carry = carry_ref[...] + jnp.sum(onehot, axis=1, keepdims=True)
    carry_ref[...] = new_carry
    cnt_ref[...] = new_carry.astype(jnp.int32)


def _route(logits_t, router_bias):
    n = logits_t.shape[1]
    tr = _tile(512, n)
    return pl.pallas_call(
        _route_kernel,
        grid=(n // tr,),
        in_specs=[
            pl.BlockSpec((N_EXPERTS, tr), lambda i: (0, i)),
            pl.BlockSpec((N_EXPERTS, 1), lambda i: (0, 0)),
        ],
        out_specs=[
            pl.BlockSpec((1, tr), lambda i: (0, i)),
            pl.BlockSpec((CLASS_ROWS, 1), lambda i: (0, 0)),
        ],
        out_shape=[
            jax.ShapeDtypeStruct((1, n), jnp.int32),
            jax.ShapeDtypeStruct((CLASS_ROWS, 1), jnp.int32),
        ],
        scratch_shapes=[pltpu.VMEM((CLASS_ROWS, 1), F32), pltpu.VMEM((tr, tr), BF16)],
        compiler_params=_params(("arbitrary",)),
        name="route",
    )(logits_t, router_bias.reshape(N_EXPERTS, 1).astype(F32))


SORT_UNROLL = 8
MOE_ROWS = 256
MOE_ROWS_LOG2 = MOE_ROWS.bit_length() - 1
assert 1 << MOE_ROWS_LOG2 == MOE_ROWS


def _plan_kernel(code_ref, cnt_ref, tok_ref, ea_ref, eb_ref, off_ref, nv_ref, nu_ref, start_ref):
    n_blocks = ea_ref.shape[0]
    run = jnp.int32(0)
    blk = jnp.int32(0)
    for c in range(N_CLASSES):
        cnt = cnt_ref[c]
        start_ref[c] = run - c * RANK_SPAN
        e_lo = (c // PAIRS_PER_GROUP) * EXPERTS_PER_GROUP + PAIR_LO[c % PAIRS_PER_GROUP]
        e_hi = (c // PAIRS_PER_GROUP) * EXPERTS_PER_GROUP + PAIR_HI[c % PAIRS_PER_GROUP]
        n_blk = (cnt + (MOE_ROWS - 1)) >> MOE_ROWS_LOG2

        def fill(b, carry, run=run, blk=blk, cnt=cnt, e_lo=e_lo, e_hi=e_hi):
            ea_ref[blk + b] = e_lo
            eb_ref[blk + b] = e_hi
            off_ref[blk + b] = run + b * MOE_ROWS
            nv_ref[blk + b] = jnp.minimum(cnt - b * MOE_ROWS, MOE_ROWS)
            return carry

        lax.fori_loop(0, n_blk, fill, 0)
        run = run + cnt
        blk = blk + n_blk
    nu_ref[0] = blk

    def fill_unused(b, carry):
        ea_ref[b] = ea_ref[blk - 1]
        eb_ref[b] = eb_ref[blk - 1]
        off_ref[b] = 0
        nv_ref[b] = 0
        return carry

    lax.fori_loop(blk, n_blocks, fill_unused, 0)

    def place(i, carry):
        toks = [i * SORT_UNROLL + u for u in range(SORT_UNROLL)]
        codes = [code_ref[t] for t in toks]
        slots = [start_ref[code >> RANK_BITS] + code for code in codes]
        for t, p in zip(toks, slots):
            tok_ref[p] = t
        return carry

    lax.fori_loop(0, code_ref.shape[0] // SORT_UNROLL, place, 0)


def _plan(codes, counts, n_blocks):
    n = codes.shape[0]
    assert n % SORT_UNROLL == 0 and n <= RANK_SPAN
    smem = pl.BlockSpec(memory_space=pltpu.SMEM)
    i32 = lambda size: jax.ShapeDtypeStruct((size,), jnp.int32)
    return pl.pallas_call(
        _plan_kernel,
        in_specs=[smem, smem],
        out_specs=[smem] * 6,
        out_shape=[i32(n), i32(n_blocks), i32(n_blocks), i32(n_blocks), i32(n_blocks), i32(1)],
        scratch_shapes=[pltpu.SMEM((N_CLASSES,), jnp.int32)],
        name="moe_plan",
    )(codes, counts)


ROW_UNROLL = 8


def _for_rows(n, fn):
    groups = n // ROW_UNROLL

    def group(g, c):
        for u in range(ROW_UNROLL):
            fn(g * ROW_UNROLL + u)
        return c

    def single(r, c):
        fn(r)
        return c

    lax.fori_loop(0, groups, group, 0)
    lax.fori_loop(groups * ROW_UNROLL, n, single, 0)


def _row_copy(src_ref, src_row, dst_ref, dst_row, sem, sublanes=ROW_TILES):
    return pltpu.make_async_copy(
        src_ref.at[pl.ds(pl.multiple_of(src_row * ROW_TILES, ROW_TILES), sublanes)],
        dst_ref.at[pl.ds(pl.multiple_of(dst_row * ROW_TILES, ROW_TILES), sublanes)],
        sem)


def _expert_kernel(ea_ref, eb_ref, off_ref, nv_ref, nu_ref, tok_ref,
                   h_hbm, wgu_a_ref, wd_a_ref, wgu_b_ref, wd_b_ref, wr_a_ref, wr_b_ref, y_hbm,
                   xbuf, ybuf, gsem, ssem):
    del ea_ref, eb_ref
    tb = xbuf.shape[1] // ROW_TILES
    i = pl.program_id(0)
    slot = i % 2
    n_used = nu_ref[0]

    def start_gather(block, s):
        base = off_ref[block]
        _for_rows(nv_ref[block],
                  lambda r: _row_copy(h_hbm, tok_ref[base + r], xbuf.at[s], r, gsem.at[s], PACKED_TILES).start())

    def start_scatter(block, s):
        base = off_ref[block]
        _for_rows(nv_ref[block],
                  lambda r: _row_copy(ybuf.at[s], r, y_hbm, tok_ref[base + r], ssem.at[s]).start())

    def wait_rows(block, hbm, buf, sem, sublanes=ROW_TILES):
        rows = nv_ref[block] * sublanes

        @pl.when(rows > 0)
        def _():
            pltpu.make_async_copy(hbm.at[pl.ds(0, rows)], buf.at[pl.ds(0, rows)], sem).wait()

    @pl.when(i == 0)
    def _():
        xbuf[...] = jnp.zeros_like(xbuf)
        start_gather(0, 0)

    @pl.when(i < n_used)
    def _():
        @pl.when(i + 1 < n_used)
        def _():
            start_gather(i + 1, 1 - slot)

        wait_rows(i, h_hbm, xbuf.at[slot], gsem.at[slot], PACKED_TILES)

        @pl.when(i >= 2)
        def _():
            wait_rows(i - 2, y_hbm, ybuf.at[slot], ssem.at[slot])

        words = [xbuf[slot, pl.ds(j, tb, stride=ROW_TILES), :] for j in range(PACKED_TILES)]
        halves = [_unpack_bf16_pairs(w) for w in words]
        x = jnp.concatenate([h[0] for h in halves] + [h[1] for h in halves], axis=1)

        def mlp(wgu_ref, wd_ref):
            gu = jnp.dot(x, wgu_ref[0], preferred_element_type=F32)
            gate = gu[:, :D_EXPERT]
            act = gate * jax.nn.sigmoid(gate) * gu[:, D_EXPERT:]
            return jnp.dot(act.astype(BF16), wd_ref[0], preferred_element_type=F32)

        score_a = jax.nn.sigmoid(jnp.dot(x, wr_a_ref[0], preferred_element_type=F32))
        score_b = jax.nn.sigmoid(jnp.dot(x, wr_b_ref[0], preferred_element_type=F32))
        inv_tot = 1.0 / (score_a + score_b)
        w_a = score_a * inv_tot
        w_b = score_b * inv_tot
        ya = mlp(wgu_a_ref, wd_a_ref)
        yb = mlp(wgu_b_ref, wd_b_ref)
        for j in range(ROW_TILES):
            sl = slice(j * LANES, (j + 1) * LANES)
            ybuf[slot, pl.ds(j, tb, stride=ROW_TILES), :] = w_a * ya[:, sl] + w_b * yb[:, sl]
        start_scatter(i, slot)

        @pl.when(i == n_used - 1)
        def _():
            wait_rows(i, y_hbm, ybuf.at[slot], ssem.at[slot])

            @pl.when(i >= 1)
            def _():
                wait_rows(i - 1, y_hbm, ybuf.at[1 - slot], ssem.at[1 - slot])


def _experts(tables, sorted_tok, h_rows, layer, w_gu, w_down, wr_bcast, tb):
    block_ea, block_eb, src_off, n_valid, n_used = tables
    n_blocks = block_ea.shape[0]
    d, two_f = w_gu.shape[2:]
    ea_map = lambda i, ea, eb, off, nv, nu, tok: (ea[i], 0, 0)
    eb_map = lambda i, ea, eb, off, nv, nu, tok: (eb[i], 0, 0)
    lea_map = lambda i, ea, eb, off, nv, nu, tok: (layer, ea[i], 0, 0)
    leb_map = lambda i, ea, eb, off, nv, nu, tok: (layer, eb[i], 0, 0)
    grid_spec = pltpu.PrefetchScalarGridSpec(
        num_scalar_prefetch=6,
        grid=(n_blocks,),
        in_specs=[
            pl.BlockSpec(memory_space=pl.ANY),
            pl.BlockSpec((None, 1, d, two_f), lea_map),
            pl.BlockSpec((None, 1, two_f // 2, d), lea_map),
            pl.BlockSpec((None, 1, d, two_f), leb_map),
            pl.BlockSpec((None, 1, two_f // 2, d), leb_map),
            pl.BlockSpec((1, d, LANES), ea_map),
            pl.BlockSpec((1, d, LANES), eb_map),
        ],
        out_specs=pl.BlockSpec(memory_space=pl.ANY),
        scratch_shapes=[
            pltpu.VMEM((2, tb * ROW_TILES, LANES), jnp.uint32),
            pltpu.VMEM((2, tb * ROW_TILES, LANES), F32),
            pltpu.SemaphoreType.DMA((2,)),
            pltpu.SemaphoreType.DMA((2,)),
        ],
    )
    return pl.pallas_call(
        _expert_kernel,
        grid_spec=grid_spec,
        out_shape=jax.ShapeDtypeStruct(h_rows.shape, F32),
        compiler_params=_params(("arbitrary",)),
        name="moe_experts",
    )(block_ea, block_eb, src_off, n_valid, n_used, sorted_tok, h_rows, w_gu, w_down, w_gu, w_down, wr_bcast, wr_bcast)


def _residual_kernel(y_ref, x_ref, g2_ref, o_ref):
    o_ref[...] = _gated_residual(x_ref, y_ref, g2_ref)


def _residual(y_rows, x_mid, g2, seq):
    n, d = x_mid.shape
    tc = _tile(512, seq)
    per_b = seq // tc
    return pl.pallas_call(
        _residual_kernel,
        grid=(n // tc,),
        in_specs=[
            pl.BlockSpec((tc * ROW_TILES, LANES), lambda i: (i, 0)),
            pl.BlockSpec((tc, d), lambda i: (i, 0)),
            pl.BlockSpec((1, 1, d), lambda i: (i // per_b, 0, 0)),
        ],
        out_specs=pl.BlockSpec((tc, d), lambda i: (i, 0)),
        out_shape=jax.ShapeDtypeStruct((n, d), F32),
        compiler_params=_params(("parallel",)),
        name="moe_residual",
    )(y_rows, x_mid, g2)


def _moe(h_rows, logits_t, router_bias, layer, w_gu, w_down, wr_bcast):
    n = logits_t.shape[1]
    n_blocks = (n + N_CLASSES * (MOE_ROWS - 1) + MOE_ROWS - 1) // MOE_ROWS
    codes, counts = _route(logits_t, router_bias)
    sorted_tok, *tables = _plan(codes.reshape(n), counts.reshape(CLASS_ROWS), n_blocks)
    return _experts(tables, sorted_tok, h_rows, layer, w_gu, w_down, wr_bcast, MOE_ROWS)


def kernel(x, c, positions, norm1, norm2, w_ada, b_ada, a_w_in, a_b_if, a_h_norm, a_w_out, b_w_in, b_q_norm, b_k_norm, b_lam_q1, b_lam_k1, b_lam_q2, b_lam_k2, b_o_norm, b_w_out, w_router, router_bias, moe_w_gu, moe_w_down):
    bsz, seq, d = x.shape
    depth = w_ada.shape[0]
    n = bsz * seq
    xf = x.reshape(n, d)
    mod = _ada_mod(c, w_ada, b_ada)
    w_router_t = w_router.T
    wr_bcast = jnp.broadcast_to(w_router_t[:, :, None], (N_EXPERTS, d, LANES)).astype(BF16)
    pos_row = positions.reshape(n)
    w_gu = moe_w_gu.astype(BF16)
    w_down = moe_w_down.astype(BF16)

    residual = None
    for l in range(depth):
        sh1, sc1, g1, sh2, sc2, g2 = [mod[l, :, i * d:(i + 1) * d].reshape(bsz, 1, d) for i in range(6)]
        j = l // 2
        if l % 2 == 0:
            w_in = a_w_in[j]
            w_main = w_in[:, :A_MAIN_COLS].astype(BF16)
            w_gate = jnp.pad(w_in[:, A_MAIN_COLS:], ((0, 0), (0, LANES - 2 * A_HEADS))).astype(BF16)
        else:
            w_in = b_w_in[j]
            w_qk = _interleave_maps(w_in[:, :2 * D_MODEL])
            w_main, w_gate = jnp.concatenate([w_qk, w_in[:, 2 * D_MODEL:]], axis=1).astype(BF16), None
        outs = _norm_matmul(xf, residual, norm1[l].reshape(1, d), 1.0 + sc1, sh1, w_main, w_gate, seq)
        if residual is not None:
            xf, *outs = outs
        if l % 2 == 0:
            proj, gates = outs
            bias_row = jnp.pad(a_b_if[j], (0, LANES - 2 * A_HEADS)).reshape(1, LANES)
            mixed = _mlstm(proj, gates, bias_row, a_h_norm[j], bsz, seq)
            w_out = a_w_out[j].astype(BF16)
        else:
            (proj,) = outs
            qk = _qk_prep(proj, pos_row, b_q_norm[j], b_k_norm[j], seq)
            lam_tab = jnp.zeros((SUBLANES, LANES), F32)
            for r, v in enumerate((b_lam_q1[j], b_lam_k1[j], b_lam_q2[j], b_lam_k2[j])):
                lam_tab = lam_tab.at[r, :B_HEAD_DIM].set(v)
            lam_init = 0.8 - 0.6 * math.exp(-0.3 * l)
            mixed = _diff_attn(qk, proj, lam_tab, b_o_norm[j], bsz, seq, lam_init)
            w_out = b_w_out[j].astype(BF16)
        xf, h_rows, logits_t = _outproj_norm_router(
            mixed, w_out, xf, g1, norm2[l].reshape(1, d), 1.0 + sc2, sh2, w_router_t, seq)
        residual = (_moe(h_rows, logits_t, router_bias, l, w_gu, w_down, wr_bcast), g2)
    return _residual(residual[0], xf, residual[1], seq).reshape(bsz, seq, d)
```

```python
import functools
import math

import jax
import jax.numpy as jnp
from jax import lax
from jax.experimental import pallas as pl
from jax.experimental.pallas import tpu as pltpu

D_MODEL = 1024
A_HEADS = 4
A_QK_DIM = 128
A_V_DIM = 256
A_CHUNK = 128
A_HQ = A_HEADS * A_QK_DIM
A_HV = A_HEADS * A_V_DIM
A_MAIN_COLS = 2 * A_HQ + 2 * A_HV

B_HEADS = 8
B_HEAD_DIM = 64
B_V_DIM = 128
B_ROT_DIM = 16
ROPE_THETA = 500000.0

N_EXPERTS = 16
N_GROUPS = 4
EXPERTS_PER_GROUP = 4
TOP_K = 2
D_EXPERT = 512
EPS = 1e-6

LANES = 128
SUBLANES = 8
ROW_TILES = D_MODEL // LANES
assert ROW_TILES == SUBLANES
VMEM_LIMIT = 48 * 1024 * 1024

F32 = jnp.float32
BF16 = jnp.bfloat16
HIGHEST = lax.Precision.HIGHEST


def _params(sem):
    return pltpu.CompilerParams(dimension_semantics=sem, vmem_limit_bytes=VMEM_LIMIT)


def _tile(pref, n):
    t = min(pref, n)
    assert n % t == 0, (pref, n)
    return t


def _ada_kernel(c_ref, w_ref, b_ref, o_ref):
    c = c_ref[...]
    c_act = c * jax.nn.sigmoid(c)
    o_ref[0] = jnp.dot(c_act, w_ref[0], precision=HIGHEST, preferred_element_type=F32) + b_ref[0]


def _ada_mod(c, w_ada, b_ada):
    depth, d, six_d = w_ada.shape
    bsz = c.shape[0]
    tn = _tile(1536, six_d)
    return pl.pallas_call(
        _ada_kernel,
        grid=(depth, six_d // tn),
        in_specs=[
            pl.BlockSpec((bsz, d), lambda l, j: (0, 0)),
            pl.BlockSpec((1, d, tn), lambda l, j: (l, 0, j)),
            pl.BlockSpec((1, 1, tn), lambda l, j: (l, 0, j)),
        ],
        out_specs=pl.BlockSpec((1, bsz, tn), lambda l, j: (l, 0, j)),
        out_shape=jax.ShapeDtypeStruct((depth, bsz, six_d), F32),
        compiler_params=_params(("parallel", "parallel")),
        name="ada_mod",
    )(c, w_ada, b_ada.reshape(depth, 1, six_d))


def _modulated_rms(x, g, sc1p, sh):
    y = x * lax.rsqrt(jnp.mean(x * x, axis=-1, keepdims=True) + EPS)
    return (y * g) * sc1p + sh


def _gated_residual(x_ref, y_ref, g_ref):
    rows = x_ref.shape[0]
    g = g_ref[0]
    return jnp.concatenate(
        [x_ref[:, j * LANES:(j + 1) * LANES]
         + g[:, j * LANES:(j + 1) * LANES] * y_ref[pl.ds(j, rows, stride=ROW_TILES), :] for j in range(ROW_TILES)],
        axis=1)


def _norm_mm_kernel(*refs, col_chunk, has_gates, has_residual):
    refs = list(refs)
    x_ref = refs.pop(0)
    if has_residual:
        y_ref, g2_ref = refs.pop(0), refs.pop(0)
    g_ref, sc_ref, sh_ref, w_ref = refs[:4]
    refs = refs[4:]
    wg_ref = refs.pop(0) if has_gates else None
    xo_ref = refs.pop(0) if has_residual else None
    o_ref = refs.pop(0)
    og_ref = refs.pop(0) if has_gates else None

    if has_residual:
        x = _gated_residual(x_ref, y_ref, g2_ref)
        xo_ref[...] = x
    else:
        x = x_ref[...]
    hb = _modulated_rms(x, g_ref[...], sc_ref[0], sh_ref[0]).astype(BF16)
    for c0 in range(0, o_ref.shape[1], col_chunk):
        o_ref[:, c0:c0 + col_chunk] = jnp.dot(
            hb, w_ref[:, c0:c0 + col_chunk], preferred_element_type=F32).astype(o_ref.dtype)
    if has_gates:
        og_ref[...] = jnp.dot(hb, wg_ref[...], preferred_element_type=F32)


def _norm_matmul(x, residual, gain, sc1p, sh, w, wg, seq):
    n, d = x.shape
    cols = w.shape[1]
    tm = _tile(512, seq)
    per_b = seq // tm
    has_gates = wg is not None
    has_residual = residual is not None
    bmap = lambda i: (i // per_b, 0, 0)
    in_specs = [pl.BlockSpec((tm, d), lambda i: (i, 0))]
    args = [x]
    if has_residual:
        in_specs += [pl.BlockSpec((tm * ROW_TILES, LANES), lambda i: (i, 0)), pl.BlockSpec((1, 1, d), bmap)]
        args += list(residual)
    in_specs += [
        pl.BlockSpec((1, d), lambda i: (0, 0)),
        pl.BlockSpec((1, 1, d), bmap),
        pl.BlockSpec((1, 1, d), bmap),
        pl.BlockSpec((d, cols), lambda i: (0, 0)),
    ]
    args += [gain, sc1p, sh, w]
    out_specs, out_shape = [], []
    if has_residual:
        out_specs.append(pl.BlockSpec((tm, d), lambda i: (i, 0)))
        out_shape.append(jax.ShapeDtypeStruct((n, d), F32))
    out_specs.append(pl.BlockSpec((tm, cols), lambda i: (i, 0)))
    out_shape.append(jax.ShapeDtypeStruct((n, cols), BF16))
    if has_gates:
        in_specs.append(pl.BlockSpec((d, LANES), lambda i: (0, 0)))
        out_specs.append(pl.BlockSpec((tm, LANES), lambda i: (i, 0)))
        out_shape.append(jax.ShapeDtypeStruct((n, LANES), F32))
        args.append(wg)
    return pl.pallas_call(
        functools.partial(_norm_mm_kernel, col_chunk=512, has_gates=has_gates, has_residual=has_residual),
        grid=(n // tm,),
        in_specs=in_specs,
        out_specs=out_specs,
        out_shape=out_shape,
        compiler_params=_params(("parallel",)),
        name="norm_inproj",
    )(*args)


def _log_sigmoid(x):
    return jnp.minimum(x, 0.0) - jnp.log1p(jnp.exp(-jnp.abs(x)))


def _mlstm_chunk(r0, p_ref, g_ref, bias_ref, hn_ref, o_ref, ct_ref, n_ref, m_ref):
    L = A_CHUNK
    rows = slice(r0, r0 + L)
    gates = g_ref[rows, :] + bias_ref[...]
    src = lax.broadcasted_iota(jnp.int32, (L, L), 0)
    tgt = lax.broadcasted_iota(jnp.int32, (L, L), 1)
    causal = src <= tgt
    bcum = jnp.dot(jnp.where(src >= tgt, 1.0, 0.0), _log_sigmoid(gates), precision=HIGHEST,
                   preferred_element_type=F32)
    bcum_t = bcum.T
    nt = (((1,), (1,)), ((), ()))

    pre = []
    for h in range(A_HEADS):
        q = p_ref[rows, h * A_QK_DIM:(h + 1) * A_QK_DIM]
        ksf = p_ref[rows, A_HQ + h * A_QK_DIM:A_HQ + (h + 1) * A_QK_DIM].astype(F32) * (A_QK_DIM ** -0.5)
        v = p_ref[rows, 2 * A_HQ + h * A_V_DIM:2 * A_HQ + (h + 1) * A_V_DIM]
        ct_old = ct_ref[h]
        n_old = n_ref[h]
        qk = lax.dot_general(ksf.astype(BF16), q, nt, preferred_element_type=F32)
        qc = lax.dot_general(ct_old.astype(BF16), q, nt, preferred_element_type=F32)
        qn = lax.dot_general(n_old.astype(BF16), q, nt, preferred_element_type=F32)[0:1]
        vt = v.astype(F32).T.astype(BF16)
        pre.append((ksf, vt, ct_old, n_old, qk, qc, qn))

    for h in range(A_HEADS):
        ksf, vt, ct_old, n_old, qk, qc, qn = pre[h]
        og = p_ref[rows, 2 * A_HQ + A_HV + h * A_V_DIM:2 * A_HQ + A_HV + (h + 1) * A_V_DIM].astype(F32)
        b_row = bcum_t[A_HEADS + h:A_HEADS + h + 1, :]
        b_last = b_row[:, L - 1:L]
        c_col = gates[:, h:h + 1] - bcum[:, A_HEADS + h:A_HEADS + h + 1]
        m11 = m_ref[h][:, 0:1]

        log_d = jnp.where(causal, c_col + b_row, -jnp.inf)
        log_inter = b_row + m11
        m_t = jnp.maximum(jnp.max(log_d, axis=0, keepdims=True), log_inter)
        dmat = jnp.exp(log_d - m_t)
        inter = jnp.exp(log_inter - m_t)
        s = qk * dmat
        num = jnp.dot(vt, s.astype(BF16), preferred_element_type=F32) + inter * qc
        den = jnp.sum(s, axis=0, keepdims=True) + inter * qn
        hh = num / jnp.maximum(jnp.abs(den), jnp.exp(-m_t))

        lw_col = b_last + c_col
        m_new = jnp.maximum(b_last + m11, jnp.max(lw_col, axis=0, keepdims=True))
        kw = ksf * jnp.exp(lw_col - m_new)
        decay = jnp.exp(b_last + m11 - m_new)
        ct_ref[h] = decay * ct_old + jnp.dot(vt, kw.astype(BF16), preferred_element_type=F32)
        n_ref[h] = decay * n_old + jnp.broadcast_to(jnp.sum(kw, axis=0, keepdims=True), n_old.shape)
        m_ref[h] = jnp.broadcast_to(m_new, (1, LANES))

        hn = hh * lax.rsqrt(jnp.mean(hh * hh, axis=0, keepdims=True) + EPS)
        hn = (hn * hn_ref[h * A_V_DIM:(h + 1) * A_V_DIM, :]).T
        o_ref[rows, h * A_V_DIM:(h + 1) * A_V_DIM] = (hn * jax.nn.sigmoid(og)).astype(o_ref.dtype)


MLSTM_CHUNKS_PER_STEP = 2


def _mlstm_kernel(p_ref, g_ref, bias_ref, hn_ref, o_ref, ct_ref, n_ref, m_ref):
    @pl.when(pl.program_id(1) == 0)
    def _():
        ct_ref[...] = jnp.zeros_like(ct_ref)
        n_ref[...] = jnp.zeros_like(n_ref)
        m_ref[...] = jnp.zeros_like(m_ref)

    for r0 in range(0, p_ref.shape[0], A_CHUNK):
        _mlstm_chunk(r0, p_ref, g_ref, bias_ref, hn_ref, o_ref, ct_ref, n_ref, m_ref)


def _mlstm(proj, gates, bias_row, h_norm, bsz, seq):
    n = proj.shape[0]
    rows = _tile(MLSTM_CHUNKS_PER_STEP * A_CHUNK, seq)
    nc = seq // rows
    h_norm_cols = jnp.broadcast_to(h_norm[:, None], (A_HV, A_CHUNK))
    return pl.pallas_call(
        _mlstm_kernel,
        grid=(bsz, nc),
        in_specs=[
            pl.BlockSpec((rows, A_MAIN_COLS), lambda b, c: (b * nc + c, 0)),
            pl.BlockSpec((rows, LANES), lambda b, c: (b * nc + c, 0)),
            pl.BlockSpec((1, LANES), lambda b, c: (0, 0)),
            pl.BlockSpec((A_HV, A_CHUNK), lambda b, c: (0, 0)),
        ],
        out_specs=pl.BlockSpec((rows, A_HV), lambda b, c: (b * nc + c, 0)),
        out_shape=jax.ShapeDtypeStruct((n, A_HV), BF16),
        scratch_shapes=[
            pltpu.VMEM((A_HEADS, A_V_DIM, A_QK_DIM), F32),
            pltpu.VMEM((A_HEADS, SUBLANES, A_QK_DIM), F32),
            pltpu.VMEM((A_HEADS, 1, LANES), F32),
        ],
        compiler_params=_params(("parallel", "arbitrary")),
        name="mlstm",
    )(proj, gates, bias_row, h_norm_cols)


def _qk_prep_kernel(p_ref, pos_ref, seg_ref, segt_ref, gain_ref, freq_ref, o_ref):
    tm = p_ref.shape[0]
    x = p_ref[...].astype(F32)
    ss = jnp.dot((x * x).astype(BF16), seg_ref[...], preferred_element_type=F32)
    r = lax.rsqrt(ss * (1.0 / B_HEAD_DIM) + EPS)
    r_hi = r.astype(BF16)
    r_lo = (r - r_hi.astype(F32)).astype(BF16)
    r_full = jnp.dot(jnp.concatenate([r_hi, r_lo], axis=1), segt_ref[...], preferred_element_type=F32)

    assert B_ROT_DIM // 2 == SUBLANES
    ang = freq_ref[:, 0:1] * pos_ref[0].astype(F32)
    cos8 = jnp.cos(ang)
    sin8 = jnp.sin(ang)
    one8 = jnp.ones_like(cos8)
    zero8 = jnp.zeros_like(cos8)
    groups_per_quarter = MAP_QUARTER // SUBLANES

    def chunk_pattern(lower, upper, rest):
        quarter = lambda lead: [lead] + [rest] * (groups_per_quarter - 1)
        return jnp.concatenate(quarter(lower) * 2 + quarter(upper) * 2, axis=0).T

    c_mul = chunk_pattern(cos8, cos8, one8)
    s_mul = chunk_pattern(-sin8, sin8, zero8)
    for j in range(2 * D_MODEL // LANES):
        sl = slice(j * LANES, (j + 1) * LANES)
        y = x[:, sl] * r_full[:, sl] * gain_ref[:, sl]
        out = y * c_mul + pltpu.roll(y, LANES // 2, axis=1) * s_mul
        if j < D_MODEL // LANES:
            out = out * (B_HEAD_DIM ** -0.5 * math.log2(math.e))
        o_ref[:, sl] = out.astype(o_ref.dtype)


MAP_QUARTER = B_HEAD_DIM // 2


def _interleave_maps(t):
    half = B_ROT_DIM // 2
    rest = MAP_QUARTER - half
    lead = t.shape[:-1]
    t = t.reshape(*lead, 2 * B_HEADS, 2, B_HEAD_DIM)
    lower = jnp.concatenate([t[..., 0:half], t[..., B_ROT_DIM:B_ROT_DIM + rest]], axis=-1)
    upper = jnp.concatenate([t[..., half:B_ROT_DIM], t[..., B_ROT_DIM + rest:]], axis=-1)
    out = jnp.stack([lower, upper], axis=-3)
    return out.reshape(*lead, 2 * D_MODEL)


def _qk_prep(proj, positions_row, q_norm, k_norm, seq):
    n = proj.shape[0]
    two_d = 2 * D_MODEL
    tm = _tile(512, seq)
    seg_of_col = _interleave_maps(jnp.arange(two_d) // B_HEAD_DIM)
    seg = (seg_of_col[:, None] == jnp.arange(LANES)[None, :]).astype(BF16)
    gain = _interleave_maps(
        jnp.concatenate([jnp.tile(q_norm, D_MODEL // B_HEAD_DIM), jnp.tile(k_norm, D_MODEL // B_HEAD_DIM)]))
    inv_freq = ROPE_THETA ** (-jnp.arange(0, B_ROT_DIM, 2, dtype=F32) / B_ROT_DIM)
    freq_tab = jnp.broadcast_to(inv_freq[:, None], (B_ROT_DIM // 2, LANES))
    return pl.pallas_call(
        _qk_prep_kernel,
        grid=(n // tm,),
        in_specs=[
            pl.BlockSpec((tm, two_d), lambda i: (i, 0)),
            pl.BlockSpec((1, 1, tm), lambda i: (i, 0, 0)),
            pl.BlockSpec((two_d, LANES), lambda i: (0, 0)),
            pl.BlockSpec((2 * LANES, two_d), lambda i: (0, 0)),
            pl.BlockSpec((1, two_d), lambda i: (0, 0)),
            pl.BlockSpec((SUBLANES, LANES), lambda i: (0, 0)),
        ],
        out_specs=pl.BlockSpec((tm, two_d), lambda i: (i, 0)),
        out_shape=jax.ShapeDtypeStruct((n, two_d), BF16),
        compiler_params=_params(("parallel",)),
        name="qk_prep",
    )(proj, positions_row.reshape(n // tm, 1, tm), seg, jnp.concatenate([seg.T, seg.T], axis=0),
      gain.reshape(1, two_d), freq_tab)


ATTN_TILE = 512
ATTN_KEY_TILE = 256
ATTN_HEADS_PER_STEP = 4


def _diff_attn_kernel(q_ref, k_ref, v_ref, lam_ref, on_ref, o_ref, vt_ref, qm_ref, m_ref, l_ref, acc_ref,
                      *, tile, tk, lam_init):
    seq = q_ref.shape[0]
    heads = q_ref.shape[1] // LANES
    n_tiles = seq // tile
    k_per_q = tile // tk
    lamv = lam_ref[...]
    lam = (jnp.exp(jnp.sum(lamv[0:1] * lamv[1:2], axis=1, keepdims=True))
           - jnp.exp(jnp.sum(lamv[2:3] * lamv[3:4], axis=1, keepdims=True)) + lam_init)
    head_lanes = lambda h: slice(h * LANES, (h + 1) * LANES)
    tchunk = min(256, seq)
    for h in range(heads):
        for c in range(seq // tchunk):
            rows = slice(c * tchunk, (c + 1) * tchunk)
            vt_ref[h, :, rows] = v_ref[rows, head_lanes(h)].astype(F32).T.astype(BF16)
    first_map = (lax.broadcasted_iota(jnp.int32, (tile, LANES), 1) // MAP_QUARTER) % 2 == 0
    key_id = lax.broadcasted_iota(jnp.int32, (tk, tile), 0)
    query_id = lax.broadcasted_iota(jnp.int32, (tk, tile), 1)
    nt = (((1,), (1,)), ((), ()))

    def qk_score(h, qi, ki, c):
        return lax.dot_general(k_ref[ki * tk:(ki + 1) * tk, head_lanes(h)], qm_ref[(h * n_tiles + qi) * 2 + c], nt,
                               preferred_element_type=F32)

    def softmax_pv(h, c, s, ki, mask):
        j = 2 * h + c
        vt = vt_ref[h, :, ki * tk:(ki + 1) * tk]
        if mask is not None:
            s = jnp.where(mask, s, -jnp.inf)
        m_old = m_ref[j]
        m_new = jnp.maximum(m_old, jnp.max(s, axis=0, keepdims=True))
        alpha = jnp.exp2(m_old - m_new)
        p = jnp.exp2(s - m_new)
        l_ref[j] = alpha * l_ref[j] + jnp.sum(p, axis=0, keepdims=True)
        acc_ref[j] = alpha * acc_ref[j] + jnp.dot(vt, p.astype(BF16), preferred_element_type=F32)
        m_ref[j] = m_new

    def reset(h):
        for j in (2 * h, 2 * h + 1):
            m_ref[j] = jnp.full(m_ref.shape[1:], -jnp.inf, F32)
            l_ref[j] = jnp.zeros(l_ref.shape[1:], F32)
            acc_ref[j] = jnp.zeros(acc_ref.shape[1:], F32)

    for h in range(heads):
        for qi in range(n_tiles):
            q = q_ref[qi * tile:(qi + 1) * tile, head_lanes(h)]
            qm_ref[(h * n_tiles + qi) * 2] = jnp.where(first_map, q, jnp.zeros_like(q))
            qm_ref[(h * n_tiles + qi) * 2 + 1] = jnp.where(first_map, jnp.zeros_like(q), q)
        reset(h)
    steps = [(h, qi, ki) for qi in range(n_tiles) for ki in range((qi + 1) * k_per_q) for h in range(heads)]
    scores_next = [qk_score(*steps[0], c) for c in range(2)]
    for n, (h, qi, ki) in enumerate(steps):
        scores = scores_next
        scores_next = [None, None]
        on_diagonal = ki >= qi * k_per_q
        mask = (query_id + qi * tile >= key_id + ki * tk) if on_diagonal else None
        for c in range(2):
            if n + 1 < len(steps):
                scores_next[c] = qk_score(*steps[n + 1], c)
            softmax_pv(h, c, scores[c], ki, mask)
        if ki == (qi + 1) * k_per_q - 1:
            o = acc_ref[2 * h] / l_ref[2 * h] - lam * (acc_ref[2 * h + 1] / l_ref[2 * h + 1])
            o = o * lax.rsqrt(jnp.mean(o * o, axis=0, keepdims=True) + EPS) * on_ref[...] * (1.0 - lam_init)
            o_ref[qi * tile:(qi + 1) * tile, head_lanes(h)] = o.T.astype(o_ref.dtype)
            if qi + 1 < n_tiles:
                reset(h)


def _diff_attn(qk, proj, lam_tab, o_norm, bsz, seq, lam_init):
    n = qk.shape[0]
    hp = ATTN_HEADS_PER_STEP
    groups = B_HEADS // hp
    tile = _tile(ATTN_TILE, seq)
    o_norm_cols = jnp.broadcast_to(o_norm[:, None], (B_V_DIM, tile))
    return pl.pallas_call(
        functools.partial(_diff_attn_kernel, tile=tile, tk=_tile(ATTN_KEY_TILE, tile), lam_init=lam_init),
        grid=(bsz, groups),
        in_specs=[
            pl.BlockSpec((seq, hp * LANES), lambda b, g: (b, g)),
            pl.BlockSpec((seq, hp * LANES), lambda b, g: (b, groups + g)),
            pl.BlockSpec((seq, hp * B_V_DIM), lambda b, g: (b, 2 * groups + g)),
            pl.BlockSpec((SUBLANES, LANES), lambda b, g: (0, 0)),
            pl.BlockSpec((B_V_DIM, tile), lambda b, g: (0, 0)),
        ],
        out_specs=pl.BlockSpec((seq, hp * B_V_DIM), lambda b, g: (b, g)),
        out_shape=jax.ShapeDtypeStruct((n, D_MODEL), BF16),
        scratch_shapes=[
            pltpu.VMEM((hp, B_V_DIM, seq), BF16),
            pltpu.VMEM((hp * 2 * (seq // tile), tile, LANES), BF16),
            pltpu.VMEM((2 * hp, 1, tile), F32),
            pltpu.VMEM((2 * hp, 1, tile), F32),
            pltpu.VMEM((2 * hp, B_V_DIM, tile), F32),
        ],
        compiler_params=_params(("parallel", "parallel")),
        name="diff_attn",
    )(qk, qk, proj, lam_tab, o_norm_cols)


HALF_D = D_MODEL // 2
HI16 = 0xFFFF0000
PACKED_TILES = HALF_D // LANES


def _pack_bf16_pairs(v):
    bits = lax.bitcast_convert_type(v.astype(BF16).astype(F32), jnp.uint32)
    return (bits[:, HALF_D:] & jnp.uint32(HI16)) | (bits[:, :HALF_D] >> 16)


def _unpack_bf16_pairs(words):
    first = lax.bitcast_convert_type(words << 16, F32)
    second = lax.bitcast_convert_type(words & jnp.uint32(HI16), F32)
    return first.astype(BF16), second.astype(BF16)


OUTPROJ_ROW_CHUNK = 128


def _outproj_kernel(a_ref, w_ref, x_ref, g1_ref, gain_ref, sc_ref, sh_ref, wr_ref, xo_ref, hp_ref, lg_ref):
    tm = x_ref.shape[0]
    rc = min(OUTPROJ_ROW_CHUNK, tm)
    nt = (((1,), (1,)), ((), ()))
    wr = wr_ref[...]
    wr_hi = wr.astype(BF16)
    wr_lo = (wr - wr_hi.astype(F32)).astype(BF16)
    ys = [jnp.dot(a_ref[r0:r0 + rc, :], w_ref[...], preferred_element_type=F32) for r0 in range(0, tm, rc)]
    for y, r0 in zip(ys, range(0, tm, rc)):
        xm = x_ref[r0:r0 + rc, :] + g1_ref[0] * y
        xo_ref[r0:r0 + rc, :] = xm
        h2 = _modulated_rms(xm, gain_ref[...], sc_ref[0], sh_ref[0])
        packed = _pack_bf16_pairs(h2)
        for j in range(ROW_TILES):
            hp_ref[pl.ds(r0 * ROW_TILES + j, rc, stride=ROW_TILES), :] = (
                packed[:, j * LANES:(j + 1) * LANES] if j < PACKED_TILES else jnp.zeros((rc, LANES), jnp.uint32))
        h_hi = h2.astype(BF16)
        h_lo = (h2 - h_hi.astype(F32)).astype(BF16)
        lg_ref[:, r0:r0 + rc] = (lax.dot_general(wr_hi, h_hi, nt, preferred_element_type=F32)
                                 + lax.dot_general(wr_hi, h_lo, nt, preferred_element_type=F32)
                                 + lax.dot_general(wr_lo, h_hi, nt, preferred_element_type=F32))


def _outproj_norm_router(a, w, x, g1, gain, sc1p, sh, w_router_t, seq):
    n, d = x.shape
    tm = _tile(512, seq)
    per_b = seq // tm
    bmap = lambda i: (i // per_b, 0, 0)
    return pl.pallas_call(
        _outproj_kernel,
        grid=(n // tm,),
        in_specs=[
            pl.BlockSpec((tm, a.shape[1]), lambda i: (i, 0)),
            pl.BlockSpec(w.shape, lambda i: (0, 0)),
            pl.BlockSpec((tm, d), lambda i: (i, 0)),
            pl.BlockSpec((1, 1, d), bmap),
            pl.BlockSpec((1, d), lambda i: (0, 0)),
            pl.BlockSpec((1, 1, d), bmap),
            pl.BlockSpec((1, 1, d), bmap),
            pl.BlockSpec((N_EXPERTS, d), lambda i: (0, 0)),
        ],
        out_specs=[
            pl.BlockSpec((tm, d), lambda i: (i, 0)),
            pl.BlockSpec((tm * ROW_TILES, LANES), lambda i: (i, 0)),
            pl.BlockSpec((N_EXPERTS, tm), lambda i: (0, i)),
        ],
        out_shape=[
            jax.ShapeDtypeStruct((n, d), F32),
            jax.ShapeDtypeStruct((n * ROW_TILES, LANES), jnp.uint32),
            jax.ShapeDtypeStruct((N_EXPERTS, n), F32),
        ],
        compiler_params=_params(("parallel",)),
        name="outproj_norm_router",
    )(a, w, x, g1, gain, sc1p, sh, w_router_t)


PAIR_LO = (0, 0, 0, 1, 1, 2)
PAIR_HI = (1, 2, 3, 2, 3, 3)
PAIRS_PER_GROUP = len(PAIR_LO)
N_CLASSES = N_GROUPS * PAIRS_PER_GROUP
CLASS_ROWS = 32
RANK_BITS = 20
RANK_SPAN = 1 << RANK_BITS


def _route_kernel(lg_ref, bias_ref, oi_ref, cnt_ref, carry_ref, tri_ref):
    tr = lg_ref.shape[1]
    step = pl.program_id(0)

    @pl.when(step == 0)
    def _():
        carry_ref[...] = jnp.zeros_like(carry_ref)
        r = lax.broadcasted_iota(jnp.int32, (tr, tr), 0)
        c = lax.broadcasted_iota(jnp.int32, (tr, tr), 1)
        tri_ref[...] = jnp.where(r < c, 1.0, 0.0).astype(BF16)

    scores = jax.nn.sigmoid(lg_ref[...])
    biased = scores + bias_ref[...]
    rows = [biased[e:e + 1, :] for e in range(N_EXPERTS)]

    def top2_sum(a, b, c, d):
        m1, n1 = jnp.maximum(a, b), jnp.minimum(a, b)
        m2, n2 = jnp.maximum(c, d), jnp.minimum(c, d)
        return jnp.maximum(m1, m2) + jnp.maximum(jnp.minimum(m1, m2), jnp.maximum(n1, n2))

    gscore = [top2_sum(*rows[g * EXPERTS_PER_GROUP:(g + 1) * EXPERTS_PER_GROUP]) for g in range(N_GROUPS)]
    best = gscore[0]
    gsel = jnp.zeros_like(best, dtype=jnp.int32)
    for g in range(1, N_GROUPS):
        upd = gscore[g] > best
        gsel = jnp.where(upd, g, gsel)
        best = jnp.where(upd, gscore[g], best)

    def pick(table, j):
        out = table[j]
        for g in range(1, N_GROUPS):
            out = jnp.where(gsel == g, table[g * EXPERTS_PER_GROUP + j], out)
        return out

    in_b = [pick(rows, j) for j in range(EXPERTS_PER_GROUP)]
    v1, i1 = in_b[0], jnp.zeros_like(gsel)
    for j in range(1, EXPERTS_PER_GROUP):
        upd = in_b[j] > v1
        v1 = jnp.where(upd, in_b[j], v1)
        i1 = jnp.where(upd, j, i1)
    v2 = jnp.full_like(v1, -jnp.inf)
    i2 = jnp.zeros_like(gsel)
    for j in range(EXPERTS_PER_GROUP):
        upd = (i1 != j) & (in_b[j] > v2)
        v2 = jnp.where(upd, in_b[j], v2)
        i2 = jnp.where(upd, j, i2)
    first_is_lo = i1 < i2
    lo = jnp.where(first_is_lo, i1, i2)
    hi = jnp.where(first_is_lo, i2, i1)
    pair = jnp.where(lo == 0, hi - 1, jnp.where(lo == 1, hi + 1, PAIRS_PER_GROUP - 1))
    cls = gsel * PAIRS_PER_GROUP + pair

    cid = lax.broadcasted_iota(jnp.int32, (CLASS_ROWS, tr), 0)
    onehot = jnp.where(cid == cls, 1.0, 0.0)
    before = jnp.dot(onehot.astype(BF16), tri_ref[...], preferred_element_type=F32) + carry_ref[...]
    rank = jnp.sum(onehot * before, axis=0, keepdims=True).astype(jnp.int32)
    oi_ref[...] = cls * RANK_SPAN + rank
    new_carry = carry_ref[...] + jnp.sum(onehot, axis=1, keepdims=True)
    carry_ref[...] = new_carry
    cnt_ref[...] = new_carry.astype(jnp.int32)


def _route(logits_t, router_bias):
    n = logits_t.shape[1]
    tr = _tile(512, n)
    return pl.pallas_call(
        _route_kernel,
        grid=(n // tr,),
        in_specs=[
            pl.BlockSpec((N_EXPERTS, tr), lambda i: (0, i)),
            pl.BlockSpec((N_EXPERTS, 1), lambda i: (0, 0)),
        ],
        out_specs=[
            pl.BlockSpec((1, tr), lambda i: (0, i)),
            pl.BlockSpec((CLASS_ROWS, 1), lambda i: (0, 0)),
        ],
        out_shape=[
            jax.ShapeDtypeStruct((1, n), jnp.int32),
            jax.ShapeDtypeStruct((CLASS_ROWS, 1), jnp.int32),
        ],
        scratch_shapes=[pltpu.VMEM((CLASS_ROWS, 1), F32), pltpu.VMEM((tr, tr), BF16)],
        compiler_params=_params(("arbitrary",)),
        name="route",
    )(logits_t, router_bias.reshape(N_EXPERTS, 1).astype(F32))


SORT_UNROLL = 8
MOE_ROWS = 256
MOE_ROWS_LOG2 = MOE_ROWS.bit_length() - 1
assert 1 << MOE_ROWS_LOG2 == MOE_ROWS


def _plan_kernel(code_ref, cnt_ref, tok_ref, ea_ref, eb_ref, off_ref, nv_ref, nu_ref, start_ref):
    n_blocks = ea_ref.shape[0]
    run = jnp.int32(0)
    blk = jnp.int32(0)
    for c in range(N_CLASSES):
        cnt = cnt_ref[c]
        start_ref[c] = run - c * RANK_SPAN
        e_lo = (c // PAIRS_PER_GROUP) * EXPERTS_PER_GROUP + PAIR_LO[c % PAIRS_PER_GROUP]
        e_hi = (c // PAIRS_PER_GROUP) * EXPERTS_PER_GROUP + PAIR_HI[c % PAIRS_PER_GROUP]
        n_blk = (cnt + (MOE_ROWS - 1)) >> MOE_ROWS_LOG2

        def fill(b, carry, run=run, blk=blk, cnt=cnt, e_lo=e_lo, e_hi=e_hi):
            ea_ref[blk + b] = e_lo
            eb_ref[blk + b] = e_hi
            off_ref[blk + b] = run + b * MOE_ROWS
            nv_ref[blk + b] = jnp.minimum(cnt - b * MOE_ROWS, MOE_ROWS)
            return carry

        lax.fori_loop(0, n_blk, fill, 0)
        run = run + cnt
        blk = blk + n_blk
    nu_ref[0] = blk

    def fill_unused(b, carry):
        ea_ref[b] = ea_ref[blk - 1]
        eb_ref[b] = eb_ref[blk - 1]
        off_ref[b] = 0
        nv_ref[b] = 0
        return carry

    lax.fori_loop(blk, n_blocks, fill_unused, 0)

    def place(i, carry):
        toks = [i * SORT_UNROLL + u for u in range(SORT_UNROLL)]
        codes = [code_ref[t] for t in toks]
        slots = [start_ref[code >> RANK_BITS] + code for code in codes]
        for t, p in zip(toks, slots):
            tok_ref[p] = t
        return carry

    lax.fori_loop(0, code_ref.shape[0] // SORT_UNROLL, place, 0)


def _plan(codes, counts, n_blocks):
    n = codes.shape[0]
    assert n % SORT_UNROLL == 0 and n <= RANK_SPAN
    smem = pl.BlockSpec(memory_space=pltpu.SMEM)
    i32 = lambda size: jax.ShapeDtypeStruct((size,), jnp.int32)
    return pl.pallas_call(
        _plan_kernel,
        in_specs=[smem, smem],
        out_specs=[smem] * 6,
        out_shape=[i32(n), i32(n_blocks), i32(n_blocks), i32(n_blocks), i32(n_blocks), i32(1)],
        scratch_shapes=[pltpu.SMEM((N_CLASSES,), jnp.int32)],
        name="moe_plan",
    )(codes, counts)


ROW_UNROLL = 8


def _for_rows(n, fn):
    groups = n // ROW_UNROLL

    def group(g, c):
        for u in range(ROW_UNROLL):
            fn(g * ROW_UNROLL + u)
        return c

    def single(r, c):
        fn(r)
        return c

    lax.fori_loop(0, groups, group, 0)
    lax.fori_loop(groups * ROW_UNROLL, n, single, 0)


def _row_copy(src_ref, src_row, dst_ref, dst_row, sem, sublanes=ROW_TILES):
    return pltpu.make_async_copy(
        src_ref.at[pl.ds(pl.multiple_of(src_row * ROW_TILES, ROW_TILES), sublanes)],
        dst_ref.at[pl.ds(pl.multiple_of(dst_row * ROW_TILES, ROW_TILES), sublanes)],
        sem)


def _expert_kernel(ea_ref, eb_ref, off_ref, nv_ref, nu_ref, tok_ref,
                   h_hbm, wgu_a_ref, wd_a_ref, wgu_b_ref, wd_b_ref, wr_a_ref, wr_b_ref, y_hbm,
                   xbuf, ybuf, gsem, ssem):
    del ea_ref, eb_ref
    tb = xbuf.shape[1] // ROW_TILES
    i = pl.program_id(0)
    slot = i % 2
    n_used = nu_ref[0]

    def start_gather(block, s):
        base = off_ref[block]
        _for_rows(nv_ref[block],
                  lambda r: _row_copy(h_hbm, tok_ref[base + r], xbuf.at[s], r, gsem.at[s], PACKED_TILES).start())

    def start_scatter(block, s):
        base = off_ref[block]
        _for_rows(nv_ref[block],
                  lambda r: _row_copy(ybuf.at[s], r, y_hbm, tok_ref[base + r], ssem.at[s]).start())

    def wait_rows(block, hbm, buf, sem, sublanes=ROW_TILES):
        rows = nv_ref[block] * sublanes

        @pl.when(rows > 0)
        def _():
            pltpu.make_async_copy(hbm.at[pl.ds(0, rows)], buf.at[pl.ds(0, rows)], sem).wait()

    @pl.when(i == 0)
    def _():
        xbuf[...] = jnp.zeros_like(xbuf)
        start_gather(0, 0)

    @pl.when(i < n_used)
    def _():
        @pl.when(i + 1 < n_used)
        def _():
            start_gather(i + 1, 1 - slot)

        wait_rows(i, h_hbm, xbuf.at[slot], gsem.at[slot], PACKED_TILES)

        @pl.when(i >= 2)
        def _():
            wait_rows(i - 2, y_hbm, ybuf.at[slot], ssem.at[slot])

        words = [xbuf[slot, pl.ds(j, tb, stride=ROW_TILES), :] for j in range(PACKED_TILES)]
        halves = [_unpack_bf16_pairs(w) for w in words]
        x = jnp.concatenate([h[0] for h in halves] + [h[1] for h in halves], axis=1)

        def mlp(wgu_ref, wd_ref):
            gu = jnp.dot(x, wgu_ref[0], preferred_element_type=F32)
            gate = gu[:, :D_EXPERT]
            act = gate * jax.nn.sigmoid(gate) * gu[:, D_EXPERT:]
            return jnp.dot(act.astype(BF16), wd_ref[0], preferred_element_type=F32)

        score_a = jax.nn.sigmoid(jnp.dot(x, wr_a_ref[0], preferred_element_type=F32))
        score_b = jax.nn.sigmoid(jnp.dot(x, wr_b_ref[0], preferred_element_type=F32))
        inv_tot = 1.0 / (score_a + score_b)
        w_a = score_a * inv_tot
        w_b = score_b * inv_tot
        ya = mlp(wgu_a_ref, wd_a_ref)
        yb = mlp(wgu_b_ref, wd_b_ref)
        for j in range(ROW_TILES):
            sl = slice(j * LANES, (j + 1) * LANES)
            ybuf[slot, pl.ds(j, tb, stride=ROW_TILES), :] = w_a * ya[:, sl] + w_b * yb[:, sl]
        start_scatter(i, slot)

        @pl.when(i == n_used - 1)
        def _():
            wait_rows(i, y_hbm, ybuf.at[slot], ssem.at[slot])

            @pl.when(i >= 1)
            def _():
                wait_rows(i - 1, y_hbm, ybuf.at[1 - slot], ssem.at[1 - slot])


def _experts(tables, sorted_tok, h_rows, layer, w_gu, w_down, wr_bcast, tb):
    block_ea, block_eb, src_off, n_valid, n_used = tables
    n_blocks = block_ea.shape[0]
    d, two_f = w_gu.shape[2:]
    ea_map = lambda i, ea, eb, off, nv, nu, tok: (ea[i], 0, 0)
    eb_map = lambda i, ea, eb, off, nv, nu, tok: (eb[i], 0, 0)
    lea_map = lambda i, ea, eb, off, nv, nu, tok: (layer, ea[i], 0, 0)
    leb_map = lambda i, ea, eb, off, nv, nu, tok: (layer, eb[i], 0, 0)
    grid_spec = pltpu.PrefetchScalarGridSpec(
        num_scalar_prefetch=6,
        grid=(n_blocks,),
        in_specs=[
            pl.BlockSpec(memory_space=pl.ANY),
            pl.BlockSpec((None, 1, d, two_f), lea_map),
            pl.BlockSpec((None, 1, two_f // 2, d), lea_map),
            pl.BlockSpec((None, 1, d, two_f), leb_map),
            pl.BlockSpec((None, 1, two_f // 2, d), leb_map),
            pl.BlockSpec((1, d, LANES), ea_map),
            pl.BlockSpec((1, d, LANES), eb_map),
        ],
        out_specs=pl.BlockSpec(memory_space=pl.ANY),
        scratch_shapes=[
            pltpu.VMEM((2, tb * ROW_TILES, LANES), jnp.uint32),
            pltpu.VMEM((2, tb * ROW_TILES, LANES), F32),
            pltpu.SemaphoreType.DMA((2,)),
            pltpu.SemaphoreType.DMA((2,)),
        ],
    )
    return pl.pallas_call(
        _expert_kernel,
        grid_spec=grid_spec,
        out_shape=jax.ShapeDtypeStruct(h_rows.shape, F32),
        compiler_params=_params(("arbitrary",)),
        name="moe_experts",
    )(block_ea, block_eb, src_off, n_valid, n_used, sorted_tok, h_rows, w_gu, w_down, w_gu, w_down, wr_bcast, wr_bcast)


def _residual_kernel(y_ref, x_ref, g2_ref, o_ref):
    o_ref[...] = _gated_residual(x_ref, y_ref, g2_ref)


def _residual(y_rows, x_mid, g2, seq):
    n, d = x_mid.shape
    tc = _tile(512, seq)
    per_b = seq // tc
    return pl.pallas_call(
        _residual_kernel,
        grid=(n // tc,),
        in_specs=[
            pl.BlockSpec((tc * ROW_TILES, LANES), lambda i: (i, 0)),
            pl.BlockSpec((tc, d), lambda i: (i, 0)),
            pl.BlockSpec((1, 1, d), lambda i: (i // per_b, 0, 0)),
        ],
        out_specs=pl.BlockSpec((tc, d), lambda i: (i, 0)),
        out_shape=jax.ShapeDtypeStruct((n, d), F32),
        compiler_params=_params(("parallel",)),
        name="moe_residual",
    )(y_rows, x_mid, g2)


def _moe(h_rows, logits_t, router_bias, layer, w_gu, w_down, wr_bcast):
    n = logits_t.shape[1]
    n_blocks = (n + N_CLASSES * (MOE_ROWS - 1) + MOE_ROWS - 1) // MOE_ROWS
    codes, counts = _route(logits_t, router_bias)
    sorted_tok, *tables = _plan(codes.reshape(n), counts.reshape(CLASS_ROWS), n_blocks)
    return _experts(tables, sorted_tok, h_rows, layer, w_gu, w_down, wr_bcast, MOE_ROWS)


def kernel(x, c, positions, norm1, norm2, w_ada, b_ada, a_w_in, a_b_if, a_h_norm, a_w_out, b_w_in, b_q_norm, b_k_norm, b_lam_q1, b_lam_k1, b_lam_q2, b_lam_k2, b_o_norm, b_w_out, w_router, router_bias, moe_w_gu, moe_w_down):
    bsz, seq, d = x.shape
    depth = w_ada.shape[0]
    n = bsz * seq
    xf = x.reshape(n, d)
    mod = _ada_mod(c, w_ada, b_ada)
    w_router_t = w_router.T
    wr_bcast = jnp.broadcast_to(w_router_t[:, :, None], (N_EXPERTS, d, LANES)).astype(BF16)
    pos_row = positions.reshape(n)
    w_gu = moe_w_gu.astype(BF16)
    w_down = moe_w_down.astype(BF16)

    residual = None
    for l in range(depth):
        sh1, sc1, g1, sh2, sc2, g2 = [mod[l, :, i * d:(i + 1) * d].reshape(bsz, 1, d) for i in range(6)]
        j = l // 2
        if l % 2 == 0:
            w_in = a_w_in[j]
            w_main = w_in[:, :A_MAIN_COLS].astype(BF16)
            w_gate = jnp.pad(w_in[:, A_MAIN_COLS:], ((0, 0), (0, LANES - 2 * A_HEADS))).astype(BF16)
        else:
            w_in = b_w_in[j]
            w_qk = _interleave_maps(w_in[:, :2 * D_MODEL])
            w_main, w_gate = jnp.concatenate([w_qk, w_in[:, 2 * D_MODEL:]], axis=1).astype(BF16), None
        outs = _norm_matmul(xf, residual, norm1[l].reshape(1, d), 1.0 + sc1, sh1, w_main, w_gate, seq)
        if residual is not None:
            xf, *outs = outs
        if l % 2 == 0:
            proj, gates = outs
            bias_row = jnp.pad(a_b_if[j], (0, LANES - 2 * A_HEADS)).reshape(1, LANES)
            mixed = _mlstm(proj, gates, bias_row, a_h_norm[j], bsz, seq)
            w_out = a_w_out[j].astype(BF16)
        else:
            (proj,) = outs
            qk = _qk_prep(proj, pos_row, b_q_norm[j], b_k_norm[j], seq)
            lam_tab = jnp.zeros((SUBLANES, LANES), F32)
            for r, v in enumerate((b_lam_q1[j], b_lam_k1[j], b_lam_q2[j], b_lam_k2[j])):
                lam_tab = lam_tab.at[r, :B_HEAD_DIM].set(v)
            lam_init = 0.8 - 0.6 * math.exp(-0.3 * l)
            mixed = _diff_attn(qk, proj, lam_tab, b_o_norm[j], bsz, seq, lam_init)
            w_out = b_w_out[j].astype(BF16)
        xf, h_rows, logits_t = _outproj_norm_router(
            mixed, w_out, xf, g1, norm2[l].reshape(1, d), 1.0 + sc2, sh2, w_router_t, seq)
        residual = (_moe(h_rows, logits_t, router_bias, l, w_gu, w_down, wr_bcast), g2)
    return _residual(residual[0], xf, residual[1], seq).reshape(bsz, seq, d)
```

```python
import functools
import math

import jax
import jax.numpy as jnp
from jax import lax
from jax.experimental import pallas as pl
from jax.experimental.pallas import tpu as pltpu

D_MODEL = 1024
A_HEADS = 4
A_QK_DIM = 128
A_V_DIM = 256
A_CHUNK = 128
A_HQ = A_HEADS * A_QK_DIM
A_HV = A_HEADS * A_V_DIM
A_MAIN_COLS = 2 * A_HQ + 2 * A_HV

B_HEADS = 8
B_HEAD_DIM = 64
B_V_DIM = 128
B_ROT_DIM = 16
ROPE_THETA = 500000.0

N_EXPERTS = 16
N_GROUPS = 4
EXPERTS_PER_GROUP = 4
TOP_K = 2
D_EXPERT = 512
EPS = 1e-6

LANES = 128
SUBLANES = 8
ROW_TILES = D_MODEL // LANES
assert ROW_TILES == SUBLANES
VMEM_LIMIT = 48 * 1024 * 1024

F32 = jnp.float32
BF16 = jnp.bfloat16
HIGHEST = lax.Precision.HIGHEST


def _params(sem):
    return pltpu.CompilerParams(dimension_semantics=sem, vmem_limit_bytes=VMEM_LIMIT)


def _tile(pref, n):
    t = min(pref, n)
    assert n % t == 0, (pref, n)
    return t


def _ada_kernel(c_ref, w_ref, b_ref, o_ref):
    c = c_ref[...]
    c_act = c * jax.nn.sigmoid(c)
    o_ref[0] = jnp.dot(c_act, w_ref[0], precision=HIGHEST, preferred_element_type=F32) + b_ref[0]


def _ada_mod(c, w_ada, b_ada):
    depth, d, six_d = w_ada.shape
    bsz = c.shape[0]
    tn = _tile(1536, six_d)
    return pl.pallas_call(
        _ada_kernel,
        grid=(depth, six_d // tn),
        in_specs=[
            pl.BlockSpec((bsz, d), lambda l, j: (0, 0)),
            pl.BlockSpec((1, d, tn), lambda l, j: (l, 0, j)),
            pl.BlockSpec((1, 1, tn), lambda l, j: (l, 0, j)),
        ],
        out_specs=pl.BlockSpec((1, bsz, tn), lambda l, j: (l, 0, j)),
        out_shape=jax.ShapeDtypeStruct((depth, bsz, six_d), F32),
        compiler_params=_params(("parallel", "parallel")),
        name="ada_mod",
    )(c, w_ada, b_ada.reshape(depth, 1, six_d))


def _modulated_rms(x, g, sc1p, sh):
    y = x * lax.rsqrt(jnp.mean(x * x, axis=-1, keepdims=True) + EPS)
    return (y * g) * sc1p + sh


def _gated_residual(x_ref, y_ref, g_ref):
    rows = x_ref.shape[0]
    g = g_ref[0]
    return jnp.concatenate(
        [x_ref[:, j * LANES:(j + 1) * LANES]
         + g[:, j * LANES:(j + 1) * LANES] * y_ref[pl.ds(j, rows, stride=ROW_TILES), :] for j in range(ROW_TILES)],
        axis=1)


def _norm_mm_kernel(*refs, col_chunk, has_gates, has_residual):
    refs = list(refs)
    x_ref = refs.pop(0)
    if has_residual:
        y_ref, g2_ref = refs.pop(0), refs.pop(0)
    g_ref, sc_ref, sh_ref, w_ref = refs[:4]
    refs = refs[4:]
    wg_ref = refs.pop(0) if has_gates else None
    xo_ref = refs.pop(0) if has_residual else None
    o_ref = refs.pop(0)
    og_ref = refs.pop(0) if has_gates else None

    if has_residual:
        x = _gated_residual(x_ref, y_ref, g2_ref)
        xo_ref[...] = x
    else:
        x = x_ref[...]
    hb = _modulated_rms(x, g_ref[...], sc_ref[0], sh_ref[0]).astype(BF16)
    for c0 in range(0, o_ref.shape[1], col_chunk):
        o_ref[:, c0:c0 + col_chunk] = jnp.dot(
            hb, w_ref[:, c0:c0 + col_chunk], preferred_element_type=F32).astype(o_ref.dtype)
    if has_gates:
        og_ref[...] = jnp.dot(hb, wg_ref[...], preferred_element_type=F32)


def _norm_matmul(x, residual, gain, sc1p, sh, w, wg, seq):
    n, d = x.shape
    cols = w.shape[1]
    tm = _tile(512, seq)
    per_b = seq // tm
    has_gates = wg is not None
    has_residual = residual is not None
    bmap = lambda i: (i // per_b, 0, 0)
    in_specs = [pl.BlockSpec((tm, d), lambda i: (i, 0))]
    args = [x]
    if has_residual:
        in_specs += [pl.BlockSpec((tm * ROW_TILES, LANES), lambda i: (i, 0)), pl.BlockSpec((1, 1, d), bmap)]
        args += list(residual)
    in_specs += [
        pl.BlockSpec((1, d), lambda i: (0, 0)),
        pl.BlockSpec((1, 1, d), bmap),
        pl.BlockSpec((1, 1, d), bmap),
        pl.BlockSpec((d, cols), lambda i: (0, 0)),
    ]
    args += [gain, sc1p, sh, w]
    out_specs, out_shape = [], []
    if has_residual:
        out_specs.append(pl.BlockSpec((tm, d), lambda i: (i, 0)))
        out_shape.append(jax.ShapeDtypeStruct((n, d), F32))
    out_specs.append(pl.BlockSpec((tm, cols), lambda i: (i, 0)))
    out_shape.append(jax.ShapeDtypeStruct((n, cols), BF16))
    if has_gates:
        in_specs.append(pl.BlockSpec((d, LANES), lambda i: (0, 0)))
        out_specs.append(pl.BlockSpec((tm, LANES), lambda i: (i, 0)))
        out_shape.append(jax.ShapeDtypeStruct((n, LANES), F32))
        args.append(wg)
    return pl.pallas_call(
        functools.partial(_norm_mm_kernel, col_chunk=512, has_gates=has_gates, has_residual=has_residual),
        grid=(n // tm,),
        in_specs=in_specs,
        out_specs=out_specs,
        out_shape=out_shape,
        compiler_params=_params(("parallel",)),
        name="norm_inproj",
    )(*args)


def _log_sigmoid(x):
    return jnp.minimum(x, 0.0) - jnp.log1p(jnp.exp(-jnp.abs(x)))


def _mlstm_chunk(r0, p_ref, g_ref, bias_ref, hn_ref, o_ref, ct_ref, n_ref, m_ref):
    L = A_CHUNK
    rows = slice(r0, r0 + L)
    gates = g_ref[rows, :] + bias_ref[...]
    src = lax.broadcasted_iota(jnp.int32, (L, L), 0)
    tgt = lax.broadcasted_iota(jnp.int32, (L, L), 1)
    causal = src <= tgt
    bcum = jnp.dot(jnp.where(src >= tgt, 1.0, 0.0), _log_sigmoid(gates), precision=HIGHEST,
                   preferred_element_type=F32)
    bcum_t = bcum.T
    nt = (((1,), (1,)), ((), ()))

    pre = []
    for h in range(A_HEADS):
        q = p_ref[rows, h * A_QK_DIM:(h + 1) * A_QK_DIM]
        ksf = p_ref[rows, A_HQ + h * A_QK_DIM:A_HQ + (h + 1) * A_QK_DIM].astype(F32) * (A_QK_DIM ** -0.5)
        v = p_ref[rows, 2 * A_HQ + h * A_V_DIM:2 * A_HQ + (h + 1) * A_V_DIM]
        ct_old = ct_ref[h]
        n_old = n_ref[h]
        qk = lax.dot_general(ksf.astype(BF16), q, nt, preferred_element_type=F32)
        qc = lax.dot_general(ct_old.astype(BF16), q, nt, preferred_element_type=F32)
        qn = lax.dot_general(n_old.astype(BF16), q, nt, preferred_element_type=F32)[0:1]
        vt = v.astype(F32).T.astype(BF16)
        pre.append((ksf, vt, ct_old, n_old, qk, qc, qn))

    for h in range(A_HEADS):
        ksf, vt, ct_old, n_old, qk, qc, qn = pre[h]
        og = p_ref[rows, 2 * A_HQ + A_HV + h * A_V_DIM:2 * A_HQ + A_HV + (h + 1) * A_V_DIM].astype(F32)
        b_row = bcum_t[A_HEADS + h:A_HEADS + h + 1, :]
        b_last = b_row[:, L - 1:L]
        c_col = gates[:, h:h + 1] - bcum[:, A_HEADS + h:A_HEADS + h + 1]
        m11 = m_ref[h][:, 0:1]

        log_d = jnp.where(causal, c_col + b_row, -jnp.inf)
        log_inter = b_row + m11
        m_t = jnp.maximum(jnp.max(log_d, axis=0, keepdims=True), log_inter)
        dmat = jnp.exp(log_d - m_t)
        inter = jnp.exp(log_inter - m_t)
        s = qk * dmat
        num = jnp.dot(vt, s.astype(BF16), preferred_element_type=F32) + inter * qc
        den = jnp.sum(s, axis=0, keepdims=True) + inter * qn
        hh = num / jnp.maximum(jnp.abs(den), jnp.exp(-m_t))

        lw_col = b_last + c_col
        m_new = jnp.maximum(b_last + m11, jnp.max(lw_col, axis=0, keepdims=True))
        kw = ksf * jnp.exp(lw_col - m_new)
        decay = jnp.exp(b_last + m11 - m_new)
        ct_ref[h] = decay * ct_old + jnp.dot(vt, kw.astype(BF16), preferred_element_type=F32)
        n_ref[h] = decay * n_old + jnp.broadcast_to(jnp.sum(kw, axis=0, keepdims=True), n_old.shape)
        m_ref[h] = jnp.broadcast_to(m_new, (1, LANES))

        hn = hh * lax.rsqrt(jnp.mean(hh * hh, axis=0, keepdims=True) + EPS)
        hn = (hn * hn_ref[h * A_V_DIM:(h + 1) * A_V_DIM, :]).T
        o_ref[rows, h * A_V_DIM:(h + 1) * A_V_DIM] = (hn * jax.nn.sigmoid(og)).astype(o_ref.dtype)


MLSTM_CHUNKS_PER_STEP = 2


def _mlstm_kernel(p_ref, g_ref, bias_ref, hn_ref, o_ref, ct_ref, n_ref, m_ref):
    @pl.when(pl.program_id(1) == 0)
    def _():
        ct_ref[...] = jnp.zeros_like(ct_ref)
        n_ref[...] = jnp.zeros_like(n_ref)
        m_ref[...] = jnp.zeros_like(m_ref)

    for r0 in range(0, p_ref.shape[0], A_CHUNK):
        _mlstm_chunk(r0, p_ref, g_ref, bias_ref, hn_ref, o_ref, ct_ref, n_ref, m_ref)


def _mlstm(proj, gates, bias_row, h_norm, bsz, seq):
    n = proj.shape[0]
    rows = _tile(MLSTM_CHUNKS_PER_STEP * A_CHUNK, seq)
    nc = seq // rows
    h_norm_cols = jnp.broadcast_to(h_norm[:, None], (A_HV, A_CHUNK))
    return pl.pallas_call(
        _mlstm_kernel,
        grid=(bsz, nc),
        in_specs=[
            pl.BlockSpec((rows, A_MAIN_COLS), lambda b, c: (b * nc + c, 0)),
            pl.BlockSpec((rows, LANES), lambda b, c: (b * nc + c, 0)),
            pl.BlockSpec((1, LANES), lambda b, c: (0, 0)),
            pl.BlockSpec((A_HV, A_CHUNK), lambda b, c: (0, 0)),
        ],
        out_specs=pl.BlockSpec((rows, A_HV), lambda b, c: (b * nc + c, 0)),
        out_shape=jax.ShapeDtypeStruct((n, A_HV), BF16),
        scratch_shapes=[
            pltpu.VMEM((A_HEADS, A_V_DIM, A_QK_DIM), F32),
            pltpu.VMEM((A_HEADS, SUBLANES, A_QK_DIM), F32),
            pltpu.VMEM((A_HEADS, 1, LANES), F32),
        ],
        compiler_params=_params(("parallel", "arbitrary")),
        name="mlstm",
    )(proj, gates, bias_row, h_norm_cols)


QK_PREP_ROW_CHUNK = 256


def _qk_prep_kernel(p_ref, pos_ref, seg_ref, segt_ref, gain_ref, freq_ref, o_ref):
    tm = p_ref.shape[0]
    rc = min(QK_PREP_ROW_CHUNK, tm)
    chunks = [slice(r0, r0 + rc) for r0 in range(0, tm, rc)]
    xs = [p_ref[rows, :].astype(F32) for rows in chunks]
    sss = [jnp.dot((x * x).astype(BF16), seg_ref[...], preferred_element_type=F32) for x in xs]
    r_fulls = []
    for ss in sss:
        r = lax.rsqrt(ss * (1.0 / B_HEAD_DIM) + EPS)
        r_hi = r.astype(BF16)
        r_lo = (r - r_hi.astype(F32)).astype(BF16)
        r_fulls.append(jnp.dot(jnp.concatenate([r_hi, r_lo], axis=1), segt_ref[...], preferred_element_type=F32))

    assert B_ROT_DIM // 2 == SUBLANES
    ang = freq_ref[:, 0:1] * pos_ref[0].astype(F32)
    cos8 = jnp.cos(ang)
    sin8 = jnp.sin(ang)
    one8 = jnp.ones_like(cos8)
    zero8 = jnp.zeros_like(cos8)
    groups_per_quarter = MAP_QUARTER // SUBLANES

    def chunk_pattern(lower, upper, rest):
        quarter = lambda lead: [lead] + [rest] * (groups_per_quarter - 1)
        return jnp.concatenate(quarter(lower) * 2 + quarter(upper) * 2, axis=0).T

    c_all = chunk_pattern(cos8, cos8, one8)
    s_all = chunk_pattern(-sin8, sin8, zero8)
    for rows, x, r_full in zip(chunks, xs, r_fulls):
        c_mul, s_mul = c_all[rows, :], s_all[rows, :]
        for j in range(2 * D_MODEL // LANES):
            sl = slice(j * LANES, (j + 1) * LANES)
            y = x[:, sl] * r_full[:, sl] * gain_ref[:, sl]
            out = y * c_mul + pltpu.roll(y, LANES // 2, axis=1) * s_mul
            if j < D_MODEL // LANES:
                out = out * (B_HEAD_DIM ** -0.5 * math.log2(math.e))
            o_ref[rows, sl] = out.astype(o_ref.dtype)


MAP_QUARTER = B_HEAD_DIM // 2


def _interleave_maps(t):
    half = B_ROT_DIM // 2
    rest = MAP_QUARTER - half
    lead = t.shape[:-1]
    t = t.reshape(*lead, 2 * B_HEADS, 2, B_HEAD_DIM)
    lower = jnp.concatenate([t[..., 0:half], t[..., B_ROT_DIM:B_ROT_DIM + rest]], axis=-1)
    upper = jnp.concatenate([t[..., half:B_ROT_DIM], t[..., B_ROT_DIM + rest:]], axis=-1)
    out = jnp.stack([lower, upper], axis=-3)
    return out.reshape(*lead, 2 * D_MODEL)


def _qk_prep(proj, positions_row, q_norm, k_norm, seq):
    n = proj.shape[0]
    two_d = 2 * D_MODEL
    tm = _tile(512, seq)
    seg_of_col = _interleave_maps(jnp.arange(two_d) // B_HEAD_DIM)
    seg = (seg_of_col[:, None] == jnp.arange(LANES)[None, :]).astype(BF16)
    gain = _interleave_maps(
        jnp.concatenate([jnp.tile(q_norm, D_MODEL // B_HEAD_DIM), jnp.tile(k_norm, D_MODEL // B_HEAD_DIM)]))
    inv_freq = ROPE_THETA ** (-jnp.arange(0, B_ROT_DIM, 2, dtype=F32) / B_ROT_DIM)
    freq_tab = jnp.broadcast_to(inv_freq[:, None], (B_ROT_DIM // 2, LANES))
    return pl.pallas_call(
        _qk_prep_kernel,
        grid=(n // tm,),
        in_specs=[
            pl.BlockSpec((tm, two_d), lambda i: (i, 0)),
            pl.BlockSpec((1, 1, tm), lambda i: (i, 0, 0)),
            pl.BlockSpec((two_d, LANES), lambda i: (0, 0)),
            pl.BlockSpec((2 * LANES, two_d), lambda i: (0, 0)),
            pl.BlockSpec((1, two_d), lambda i: (0, 0)),
            pl.BlockSpec((SUBLANES, LANES), lambda i: (0, 0)),
        ],
        out_specs=pl.BlockSpec((tm, two_d), lambda i: (i, 0)),
        out_shape=jax.ShapeDtypeStruct((n, two_d), BF16),
        compiler_params=_params(("parallel",)),
        name="qk_prep",
    )(proj, positions_row.reshape(n // tm, 1, tm), seg, jnp.concatenate([seg.T, seg.T], axis=0),
      gain.reshape(1, two_d), freq_tab)


ATTN_TILE = 512
ATTN_KEY_TILE = 256
ATTN_HEADS_PER_STEP = 4


def _diff_attn_kernel(q_ref, k_ref, v_ref, lam_ref, on_ref, o_ref, vt_ref, qm_ref, m_ref, l_ref, acc_ref,
                      *, tile, tk, lam_init):
    seq = q_ref.shape[0]
    heads = q_ref.shape[1] // LANES
    n_tiles = seq // tile
    k_per_q = tile // tk
    lamv = lam_ref[...]
    lam = (jnp.exp(jnp.sum(lamv[0:1] * lamv[1:2], axis=1, keepdims=True))
           - jnp.exp(jnp.sum(lamv[2:3] * lamv[3:4], axis=1, keepdims=True)) + lam_init)
    head_lanes = lambda h: slice(h * LANES, (h + 1) * LANES)
    tchunk = min(256, seq)
    for h in range(heads):
        for c in range(seq // tchunk):
            rows = slice(c * tchunk, (c + 1) * tchunk)
            vt_ref[h, :, rows] = v_ref[rows, head_lanes(h)].astype(F32).T.astype(BF16)
    first_map = (lax.broadcasted_iota(jnp.int32, (tile, LANES), 1) // MAP_QUARTER) % 2 == 0
    key_id = lax.broadcasted_iota(jnp.int32, (tk, tile), 0)
    query_id = lax.broadcasted_iota(jnp.int32, (tk, tile), 1)
    nt = (((1,), (1,)), ((), ()))

    def qk_score(h, qi, ki, c):
        return lax.dot_general(k_ref[ki * tk:(ki + 1) * tk, head_lanes(h)], qm_ref[(h * n_tiles + qi) * 2 + c], nt,
                               preferred_element_type=F32)

    def softmax_pv(h, c, s, ki, mask):
        j = 2 * h + c
        vt = vt_ref[h, :, ki * tk:(ki + 1) * tk]
        if mask is not None:
            s = jnp.where(mask, s, -jnp.inf)
        m_old = m_ref[j]
        m_new = jnp.maximum(m_old, jnp.max(s, axis=0, keepdims=True))
        alpha = jnp.exp2(m_old - m_new)
        p = jnp.exp2(s - m_new)
        l_ref[j] = alpha * l_ref[j] + jnp.sum(p, axis=0, keepdims=True)
        acc_ref[j] = alpha * acc_ref[j] + jnp.dot(vt, p.astype(BF16), preferred_element_type=F32)
        m_ref[j] = m_new

    def reset(h):
        for j in (2 * h, 2 * h + 1):
            m_ref[j] = jnp.full(m_ref.shape[1:], -jnp.inf, F32)
            l_ref[j] = jnp.zeros(l_ref.shape[1:], F32)
            acc_ref[j] = jnp.zeros(acc_ref.shape[1:], F32)

    for h in range(heads):
        for qi in range(n_tiles):
            q = q_ref[qi * tile:(qi + 1) * tile, head_lanes(h)]
            qm_ref[(h * n_tiles + qi) * 2] = jnp.where(first_map, q, jnp.zeros_like(q))
            qm_ref[(h * n_tiles + qi) * 2 + 1] = jnp.where(first_map, jnp.zeros_like(q), q)
        reset(h)
    steps = [(h, qi, ki) for qi in range(n_tiles) for ki in range((qi + 1) * k_per_q) for h in range(heads)]
    scores_next = [qk_score(*steps[0], c) for c in range(2)]
    for n, (h, qi, ki) in enumerate(steps):
        scores = scores_next
        scores_next = [None, None]
        on_diagonal = ki >= qi * k_per_q
        mask = (query_id + qi * tile >= key_id + ki * tk) if on_diagonal else None
        for c in range(2):
            if n + 1 < len(steps):
                scores_next[c] = qk_score(*steps[n + 1], c)
            softmax_pv(h, c, scores[c], ki, mask)
        if ki == (qi + 1) * k_per_q - 1:
            o = acc_ref[2 * h] / l_ref[2 * h] - lam * (acc_ref[2 * h + 1] / l_ref[2 * h + 1])
            o = o * lax.rsqrt(jnp.mean(o * o, axis=0, keepdims=True) + EPS) * on_ref[...] * (1.0 - lam_init)
            o_ref[qi * tile:(qi + 1) * tile, head_lanes(h)] = o.T.astype(o_ref.dtype)
            if qi + 1 < n_tiles:
                reset(h)


def _diff_attn(qk, proj, lam_tab, o_norm, bsz, seq, lam_init):
    n = qk.shape[0]
    hp = ATTN_HEADS_PER_STEP
    groups = B_HEADS // hp
    tile = _tile(ATTN_TILE, seq)
    o_norm_cols = jnp.broadcast_to(o_norm[:, None], (B_V_DIM, tile))
    return pl.pallas_call(
        functools.partial(_diff_attn_kernel, tile=tile, tk=_tile(ATTN_KEY_TILE, tile), lam_init=lam_init),
        grid=(bsz, groups),
        in_specs=[
            pl.BlockSpec((seq, hp * LANES), lambda b, g: (b, g)),
            pl.BlockSpec((seq, hp * LANES), lambda b, g: (b, groups + g)),
            pl.BlockSpec((seq, hp * B_V_DIM), lambda b, g: (b, 2 * groups + g)),
            pl.BlockSpec((SUBLANES, LANES), lambda b, g: (0, 0)),
            pl.BlockSpec((B_V_DIM, tile), lambda b, g: (0, 0)),
        ],
        out_specs=pl.BlockSpec((seq, hp * B_V_DIM), lambda b, g: (b, g)),
        out_shape=jax.ShapeDtypeStruct((n, D_MODEL), BF16),
        scratch_shapes=[
            pltpu.VMEM((hp, B_V_DIM, seq), BF16),
            pltpu.VMEM((hp * 2 * (seq // tile), tile, LANES), BF16),
            pltpu.VMEM((2 * hp, 1, tile), F32),
            pltpu.VMEM((2 * hp, 1, tile), F32),
            pltpu.VMEM((2 * hp, B_V_DIM, tile), F32),
        ],
        compiler_params=_params(("parallel", "parallel")),
        name="diff_attn",
    )(qk, qk, proj, lam_tab, o_norm_cols)


HALF_D = D_MODEL // 2
HI16 = 0xFFFF0000
PACKED_TILES = HALF_D // LANES


def _pack_bf16_pairs(v):
    bits = lax.bitcast_convert_type(v.astype(BF16).astype(F32), jnp.uint32)
    return (bits[:, HALF_D:] & jnp.uint32(HI16)) | (bits[:, :HALF_D] >> 16)


def _unpack_bf16_pairs(words):
    first = lax.bitcast_convert_type(words << 16, F32)
    second = lax.bitcast_convert_type(words & jnp.uint32(HI16), F32)
    return first.astype(BF16), second.astype(BF16)


OUTPROJ_ROW_CHUNK = 128


def _outproj_kernel(a_ref, w_ref, x_ref, g1_ref, gain_ref, sc_ref, sh_ref, wr_ref, xo_ref, hp_ref, lg_ref):
    tm = x_ref.shape[0]
    rc = min(OUTPROJ_ROW_CHUNK, tm)
    nt = (((1,), (1,)), ((), ()))
    wr = wr_ref[...]
    wr_hi = wr.astype(BF16)
    wr_lo = (wr - wr_hi.astype(F32)).astype(BF16)
    ys = [jnp.dot(a_ref[r0:r0 + rc, :], w_ref[...], preferred_element_type=F32) for r0 in range(0, tm, rc)]
    for y, r0 in zip(ys, range(0, tm, rc)):
        xm = x_ref[r0:r0 + rc, :] + g1_ref[0] * y
        xo_ref[r0:r0 + rc, :] = xm
        h2 = _modulated_rms(xm, gain_ref[...], sc_ref[0], sh_ref[0])
        packed = _pack_bf16_pairs(h2)
        for j in range(ROW_TILES):
            hp_ref[pl.ds(r0 * ROW_TILES + j, rc, stride=ROW_TILES), :] = (
                packed[:, j * LANES:(j + 1) * LANES] if j < PACKED_TILES else jnp.zeros((rc, LANES), jnp.uint32))
        h_hi = h2.astype(BF16)
        h_lo = (h2 - h_hi.astype(F32)).astype(BF16)
        lg_ref[:, r0:r0 + rc] = (lax.dot_general(wr_hi, h_hi, nt, preferred_element_type=F32)
                                 + lax.dot_general(wr_hi, h_lo, nt, preferred_element_type=F32)
                                 + lax.dot_general(wr_lo, h_hi, nt, preferred_element_type=F32))


def _outproj_norm_router(a, w, x, g1, gain, sc1p, sh, w_router_t, seq):
    n, d = x.shape
    tm = _tile(512, seq)
    per_b = seq // tm
    bmap = lambda i: (i // per_b, 0, 0)
    return pl.pallas_call(
        _outproj_kernel,
        grid=(n // tm,),
        in_specs=[
            pl.BlockSpec((tm, a.shape[1]), lambda i: (i, 0)),
            pl.BlockSpec(w.shape, lambda i: (0, 0)),
            pl.BlockSpec((tm, d), lambda i: (i, 0)),
            pl.BlockSpec((1, 1, d), bmap),
            pl.BlockSpec((1, d), lambda i: (0, 0)),
            pl.BlockSpec((1, 1, d), bmap),
            pl.BlockSpec((1, 1, d), bmap),
            pl.BlockSpec((N_EXPERTS, d), lambda i: (0, 0)),
        ],
        out_specs=[
            pl.BlockSpec((tm, d), lambda i: (i, 0)),
            pl.BlockSpec((tm * ROW_TILES, LANES), lambda i: (i, 0)),
            pl.BlockSpec((N_EXPERTS, tm), lambda i: (0, i)),
        ],
        out_shape=[
            jax.ShapeDtypeStruct((n, d), F32),
            jax.ShapeDtypeStruct((n * ROW_TILES, LANES), jnp.uint32),
            jax.ShapeDtypeStruct((N_EXPERTS, n), F32),
        ],
        compiler_params=_params(("parallel",)),
        name="outproj_norm_router",
    )(a, w, x, g1, gain, sc1p, sh, w_router_t)


PAIR_LO = (0, 0, 0, 1, 1, 2)
PAIR_HI = (1, 2, 3, 2, 3, 3)
PAIRS_PER_GROUP = len(PAIR_LO)
N_CLASSES = N_GROUPS * PAIRS_PER_GROUP
CLASS_ROWS = 32
RANK_BITS = 20
RANK_SPAN = 1 << RANK_BITS


def _route_kernel(lg_ref, bias_ref, oi_ref, cnt_ref, carry_ref, tri_ref):
    tr = lg_ref.shape[1]
    step = pl.program_id(0)

    @pl.when(step == 0)
    def _():
        carry_ref[...] = jnp.zeros_like(carry_ref)
        r = lax.broadcasted_iota(jnp.int32, (tr, tr), 0)
        c = lax.broadcasted_iota(jnp.int32, (tr, tr), 1)
        tri_ref[...] = jnp.where(r < c, 1.0, 0.0).astype(BF16)

    scores = jax.nn.sigmoid(lg_ref[...])
    biased = scores + bias_ref[...]
    rows = [biased[e:e + 1, :] for e in range(N_EXPERTS)]

    def top2_sum(a, b, c, d):
        m1, n1 = jnp.maximum(a, b), jnp.minimum(a, b)
        m2, n2 = jnp.maximum(c, d), jnp.minimum(c, d)
        return jnp.maximum(m1, m2) + jnp.maximum(jnp.minimum(m1, m2), jnp.maximum(n1, n2))

    gscore = [top2_sum(*rows[g * EXPERTS_PER_GROUP:(g + 1) * EXPERTS_PER_GROUP]) for g in range(N_GROUPS)]
    best = gscore[0]
    gsel = jnp.zeros_like(best, dtype=jnp.int32)
    for g in range(1, N_GROUPS):
        upd = gscore[g] > best
        gsel = jnp.where(upd, g, gsel)
        best = jnp.where(upd, gscore[g], best)

    def pick(table, j):
        out = table[j]
        for g in range(1, N_GROUPS):
            out = jnp.where(gsel == g, table[g * EXPERTS_PER_GROUP + j], out)
        return out

    in_b = [pick(rows, j) for j in range(EXPERTS_PER_GROUP)]
    v1, i1 = in_b[0], jnp.zeros_like(gsel)
    for j in range(1, EXPERTS_PER_GROUP):
        upd = in_b[j] > v1
        v1 = jnp.where(upd, in_b[j], v1)
        i1 = jnp.where(upd, j, i1)
    v2 = jnp.full_like(v1, -jnp.inf)
    i2 = jnp.zeros_like(gsel)
    for j in range(EXPERTS_PER_GROUP):
        upd = (i1 != j) & (in_b[j] > v2)
        v2 = jnp.where(upd, in_b[j], v2)
        i2 = jnp.where(upd, j, i2)
    first_is_lo = i1 < i2
    lo = jnp.where(first_is_lo, i1, i2)
    hi = jnp.where(first_is_lo, i2, i1)
    pair = jnp.where(lo == 0, hi - 1, jnp.where(lo == 1, hi + 1, PAIRS_PER_GROUP - 1))
    cls = gsel * PAIRS_PER_GROUP + pair

    cid = lax.broadcasted_iota(jnp.int32, (CLASS_ROWS, tr), 0)
    onehot = jnp.where(cid == cls, 1.0, 0.0)
    before = jnp.dot(onehot.astype(BF16), tri_ref[...], preferred_element_type=F32) + carry_ref[...]
    rank = jnp.sum(onehot * before, axis=0, keepdims=True).astype(jnp.int32)
    oi_ref[...] = cls * RANK_SPAN + rank
    new_carry = carry_ref[...] + jnp.sum(onehot, axis=1, keepdims=True)
    carry_ref[...] = new_carry
    cnt_ref[...] = new_carry.astype(jnp.int32)


def _route(logits_t, router_bias):
    n = logits_t.shape[1]
    tr = _tile(512, n)
    return pl.pallas_call(
        _route_kernel,
        grid=(n // tr,),
        in_specs=[
            pl.BlockSpec((N_EXPERTS, tr), lambda i: (0, i)),
            pl.BlockSpec((N_EXPERTS, 1), lambda i: (0, 0)),
        ],
        out_specs=[
            pl.BlockSpec((1, tr), lambda i: (0, i)),
            pl.BlockSpec((CLASS_ROWS, 1), lambda i: (0, 0)),
        ],
        out_shape=[
            jax.ShapeDtypeStruct((1, n), jnp.int32),
            jax.ShapeDtypeStruct((CLASS_ROWS, 1), jnp.int32),
        ],
        scratch_shapes=[pltpu.VMEM((CLASS_ROWS, 1), F32), pltpu.VMEM((tr, tr), BF16)],
        compiler_params=_params(("arbitrary",)),
        name="route",
    )(logits_t, router_bias.reshape(N_EXPERTS, 1).astype(F32))


SORT_UNROLL = 8
MOE_ROWS = 256
MOE_ROWS_LOG2 = MOE_ROWS.bit_length() - 1
assert 1 << MOE_ROWS_LOG2 == MOE_ROWS


def _plan_kernel(code_ref, cnt_ref, tok_ref, ea_ref, eb_ref, off_ref, nv_ref, nu_ref, start_ref):
    n_blocks = ea_ref.shape[0]
    run = jnp.int32(0)
    blk = jnp.int32(0)
    for c in range(N_CLASSES):
        cnt = cnt_ref[c]
        start_ref[c] = run - c * RANK_SPAN
        e_lo = (c // PAIRS_PER_GROUP) * EXPERTS_PER_GROUP + PAIR_LO[c % PAIRS_PER_GROUP]
        e_hi = (c // PAIRS_PER_GROUP) * EXPERTS_PER_GROUP + PAIR_HI[c % PAIRS_PER_GROUP]
        n_blk = (cnt + (MOE_ROWS - 1)) >> MOE_ROWS_LOG2

        def fill(b, carry, run=run, blk=blk, cnt=cnt, e_lo=e_lo, e_hi=e_hi):
            ea_ref[blk + b] = e_lo
            eb_ref[blk + b] = e_hi
            off_ref[blk + b] = run + b * MOE_ROWS
            nv_ref[blk + b] = jnp.minimum(cnt - b * MOE_ROWS, MOE_ROWS)
            return carry

        lax.fori_loop(0, n_blk, fill, 0)
        run = run + cnt
        blk = blk + n_blk
    nu_ref[0] = blk

    def fill_unused(b, carry):
        ea_ref[b] = ea_ref[blk - 1]
        eb_ref[b] = eb_ref[blk - 1]
        off_ref[b] = 0
        nv_ref[b] = 0
        return carry

    lax.fori_loop(blk, n_blocks, fill_unused, 0)

    def place(i, carry):
        toks = [i * SORT_UNROLL + u for u in range(SORT_UNROLL)]
        codes = [code_ref[t] for t in toks]
        slots = [start_ref[code >> RANK_BITS] + code for code in codes]
        for t, p in zip(toks, slots):
            tok_ref[p] = t
        return carry

    lax.fori_loop(0, code_ref.shape[0] // SORT_UNROLL, place, 0)


def _plan(codes, counts, n_blocks):
    n = codes.shape[0]
    assert n % SORT_UNROLL == 0 and n <= RANK_SPAN
    smem = pl.BlockSpec(memory_space=pltpu.SMEM)
    i32 = lambda size: jax.ShapeDtypeStruct((size,), jnp.int32)
    return pl.pallas_call(
        _plan_kernel,
        in_specs=[smem, smem],
        out_specs=[smem] * 6,
        out_shape=[i32(n), i32(n_blocks), i32(n_blocks), i32(n_blocks), i32(n_blocks), i32(1)],
        scratch_shapes=[pltpu.SMEM((N_CLASSES,), jnp.int32)],
        name="moe_plan",
    )(codes, counts)


ROW_UNROLL = 8


def _for_rows(n, fn):
    groups = n // ROW_UNROLL

    def group(g, c):
        for u in range(ROW_UNROLL):
            fn(g * ROW_UNROLL + u)
        return c

    def single(r, c):
        fn(r)
        return c

    lax.fori_loop(0, groups, group, 0)
    lax.fori_loop(groups * ROW_UNROLL, n, single, 0)


def _row_copy(src_ref, src_row, dst_ref, dst_row, sem, sublanes=ROW_TILES):
    return pltpu.make_async_copy(
        src_ref.at[pl.ds(pl.multiple_of(src_row * ROW_TILES, ROW_TILES), sublanes)],
        dst_ref.at[pl.ds(pl.multiple_of(dst_row * ROW_TILES, ROW_TILES), sublanes)],
        sem)


def _expert_kernel(ea_ref, eb_ref, off_ref, nv_ref, nu_ref, tok_ref,
                   h_hbm, wgu_a_ref, wd_a_ref, wgu_b_ref, wd_b_ref, wr_a_ref, wr_b_ref, y_hbm,
                   xbuf, ybuf, gsem, ssem):
    del ea_ref, eb_ref
    tb = xbuf.shape[1] // ROW_TILES
    i = pl.program_id(0)
    slot = i % 2
    n_used = nu_ref[0]

    def start_gather(block, s):
        base = off_ref[block]
        _for_rows(nv_ref[block],
                  lambda r: _row_copy(h_hbm, tok_ref[base + r], xbuf.at[s], r, gsem.at[s], PACKED_TILES).start())

    def start_scatter(block, s):
        base = off_ref[block]
        _for_rows(nv_ref[block],
                  lambda r: _row_copy(ybuf.at[s], r, y_hbm, tok_ref[base + r], ssem.at[s]).start())

    def wait_rows(block, hbm, buf, sem, sublanes=ROW_TILES):
        rows = nv_ref[block] * sublanes

        @pl.when(rows > 0)
        def _():
            pltpu.make_async_copy(hbm.at[pl.ds(0, rows)], buf.at[pl.ds(0, rows)], sem).wait()

    @pl.when(i == 0)
    def _():
        xbuf[...] = jnp.zeros_like(xbuf)
        start_gather(0, 0)

    @pl.when(i < n_used)
    def _():
        @pl.when(i + 1 < n_used)
        def _():
            start_gather(i + 1, 1 - slot)

        wait_rows(i, h_hbm, xbuf.at[slot], gsem.at[slot], PACKED_TILES)

        @pl.when(i >= 2)
        def _():
            wait_rows(i - 2, y_hbm, ybuf.at[slot], ssem.at[slot])

        words = [xbuf[slot, pl.ds(j, tb, stride=ROW_TILES), :] for j in range(PACKED_TILES)]
        halves = [_unpack_bf16_pairs(w) for w in words]
        x = jnp.concatenate([h[0] for h in halves] + [h[1] for h in halves], axis=1)

        def mlp(wgu_ref, wd_ref):
            gu = jnp.dot(x, wgu_ref[0], preferred_element_type=F32)
            gate = gu[:, :D_EXPERT]
            act = gate * jax.nn.sigmoid(gate) * gu[:, D_EXPERT:]
            return jnp.dot(act.astype(BF16), wd_ref[0], preferred_element_type=F32)

        score_a = jax.nn.sigmoid(jnp.dot(x, wr_a_ref[0], preferred_element_type=F32))
        score_b = jax.nn.sigmoid(jnp.dot(x, wr_b_ref[0], preferred_element_type=F32))
        inv_tot = 1.0 / (score_a + score_b)
        w_a = score_a * inv_tot
        w_b = score_b * inv_tot
        ya = mlp(wgu_a_ref, wd_a_ref)
        yb = mlp(wgu_b_ref, wd_b_ref)
        for j in range(ROW_TILES):
            sl = slice(j * LANES, (j + 1) * LANES)
            ybuf[slot, pl.ds(j, tb, stride=ROW_TILES), :] = w_a * ya[:, sl] + w_b * yb[:, sl]
        start_scatter(i, slot)

        @pl.when(i == n_used - 1)
        def _():
            wait_rows(i, y_hbm, ybuf.at[slot], ssem.at[slot])

            @pl.when(i >= 1)
            def _():
                wait_rows(i - 1, y_hbm, ybuf.at[1 - slot], ssem.at[1 - slot])


def _experts(tables, sorted_tok, h_rows, layer, w_gu, w_down, wr_bcast, tb):
    block_ea, block_eb, src_off, n_valid, n_used = tables
    n_blocks = block_ea.shape[0]
    d, two_f = w_gu.shape[2:]
    ea_map = lambda i, ea, eb, off, nv, nu, tok: (ea[i], 0, 0)
    eb_map = lambda i, ea, eb, off, nv, nu, tok: (eb[i], 0, 0)
    lea_map = lambda i, ea, eb, off, nv, nu, tok: (layer, ea[i], 0, 0)
    leb_map = lambda i, ea, eb, off, nv, nu, tok: (layer, eb[i], 0, 0)
    grid_spec = pltpu.PrefetchScalarGridSpec(
        num_scalar_prefetch=6,
        grid=(n_blocks,),
        in_specs=[
            pl.BlockSpec(memory_space=pl.ANY),
            pl.BlockSpec((None, 1, d, two_f), lea_map),
            pl.BlockSpec((None, 1, two_f // 2, d), lea_map),
            pl.BlockSpec((None, 1, d, two_f), leb_map),
            pl.BlockSpec((None, 1, two_f // 2, d), leb_map),
            pl.BlockSpec((1, d, LANES), ea_map),
            pl.BlockSpec((1, d, LANES), eb_map),
        ],
        out_specs=pl.BlockSpec(memory_space=pl.ANY),
        scratch_shapes=[
            pltpu.VMEM((2, tb * ROW_TILES, LANES), jnp.uint32),
            pltpu.VMEM((2, tb * ROW_TILES, LANES), F32),
            pltpu.SemaphoreType.DMA((2,)),
            pltpu.SemaphoreType.DMA((2,)),
        ],
    )
    return pl.pallas_call(
        _expert_kernel,
        grid_spec=grid_spec,
        out_shape=jax.ShapeDtypeStruct(h_rows.shape, F32),
        compiler_params=_params(("arbitrary",)),
        name="moe_experts",
    )(block_ea, block_eb, src_off, n_valid, n_used, sorted_tok, h_rows, w_gu, w_down, w_gu, w_down, wr_bcast, wr_bcast)


def _residual_kernel(y_ref, x_ref, g2_ref, o_ref):
    o_ref[...] = _gated_residual(x_ref, y_ref, g2_ref)


def _residual(y_rows, x_mid, g2, seq):
    n, d = x_mid.shape
    tc = _tile(512, seq)
    per_b = seq // tc
    return pl.pallas_call(
        _residual_kernel,
        grid=(n // tc,),
        in_specs=[
            pl.BlockSpec((tc * ROW_TILES, LANES), lambda i: (i, 0)),
            pl.BlockSpec((tc, d), lambda i: (i, 0)),
            pl.BlockSpec((1, 1, d), lambda i: (i // per_b, 0, 0)),
        ],
        out_specs=pl.BlockSpec((tc, d), lambda i: (i, 0)),
        out_shape=jax.ShapeDtypeStruct((n, d), F32),
        compiler_params=_params(("parallel",)),
        name="moe_residual",
    )(y_rows, x_mid, g2)


def _moe(h_rows, logits_t, router_bias, layer, w_gu, w_down, wr_bcast):
    n = logits_t.shape[1]
    n_blocks = (n + N_CLASSES * (MOE_ROWS - 1) + MOE_ROWS - 1) // MOE_ROWS
    codes, counts = _route(logits_t, router_bias)
    sorted_tok, *tables = _plan(codes.reshape(n), counts.reshape(CLASS_ROWS), n_blocks)
    return _experts(tables, sorted_tok, h_rows, layer, w_gu, w_down, wr_bcast, MOE_ROWS)


def kernel(x, c, positions, norm1, norm2, w_ada, b_ada, a_w_in, a_b_if, a_h_norm, a_w_out, b_w_in, b_q_norm, b_k_norm, b_lam_q1, b_lam_k1, b_lam_q2, b_lam_k2, b_o_norm, b_w_out, w_router, router_bias, moe_w_gu, moe_w_down):
    bsz, seq, d = x.shape
    depth = w_ada.shape[0]
    n = bsz * seq
    xf = x.reshape(n, d)
    mod = _ada_mod(c, w_ada, b_ada)
    w_router_t = w_router.T
    wr_bcast = jnp.broadcast_to(w_router_t[:, :, None], (N_EXPERTS, d, LANES)).astype(BF16)
    pos_row = positions.reshape(n)
    w_gu = moe_w_gu.astype(BF16)
    w_down = moe_w_down.astype(BF16)

    residual = None
    for l in range(depth):
        sh1, sc1, g1, sh2, sc2, g2 = [mod[l, :, i * d:(i + 1) * d].reshape(bsz, 1, d) for i in range(6)]
        j = l // 2
        if l % 2 == 0:
            w_in = a_w_in[j]
            w_main = w_in[:, :A_MAIN_COLS].astype(BF16)
            w_gate = jnp.pad(w_in[:, A_MAIN_COLS:], ((0, 0), (0, LANES - 2 * A_HEADS))).astype(BF16)
        else:
            w_in = b_w_in[j]
            w_qk = _interleave_maps(w_in[:, :2 * D_MODEL])
            w_main, w_gate = jnp.concatenate([w_qk, w_in[:, 2 * D_MODEL:]], axis=1).astype(BF16), None
        outs = _norm_matmul(xf, residual, norm1[l].reshape(1, d), 1.0 + sc1, sh1, w_main, w_gate, seq)
        if residual is not None:
            xf, *outs = outs
        if l % 2 == 0:
            proj, gates = outs
            bias_row = jnp.pad(a_b_if[j], (0, LANES - 2 * A_HEADS)).reshape(1, LANES)
            mixed = _mlstm(proj, gates, bias_row, a_h_norm[j], bsz, seq)
            w_out = a_w_out[j].astype(BF16)
        else:
            (proj,) = outs
            qk = _qk_prep(proj, pos_row, b_q_norm[j], b_k_norm[j], seq)
            lam_tab = jnp.zeros((SUBLANES, LANES), F32)
            for r, v in enumerate((b_lam_q1[j], b_lam_k1[j], b_lam_q2[j], b_lam_k2[j])):
                lam_tab = lam_tab.at[r, :B_HEAD_DIM].set(v)
            lam_init = 0.8 - 0.6 * math.exp(-0.3 * l)
            mixed = _diff_attn(qk, proj, lam_tab, b_o_norm[j], bsz, seq, lam_init)
            w_out = b_w_out[j].astype(BF16)
        xf, h_rows, logits_t = _outproj_norm_router(
            mixed, w_out, xf, g1, norm2[l].reshape(1, d), 1.0 + sc2, sh2, w_router_t, seq)
        residual = (_moe(h_rows, logits_t, router_bias, l, w_gu, w_down, wr_bcast), g2)
    return _residual(residual[0], xf, residual[1], seq).reshape(bsz, seq, d)
```

```python
import functools
import math

import jax
import jax.numpy as jnp
from jax import lax
from jax.experimental import pallas as pl
from jax.experimental.pallas import tpu as pltpu

D_MODEL = 1024
A_HEADS = 4
A_QK_DIM = 128
A_V_DIM = 256
A_CHUNK = 128
A_HQ = A_HEADS * A_QK_DIM
A_HV = A_HEADS * A_V_DIM
A_MAIN_COLS = 2 * A_HQ + 2 * A_HV

B_HEADS = 8
B_HEAD_DIM = 64
B_V_DIM = 128
B_ROT_DIM = 16
ROPE_THETA = 500000.0

N_EXPERTS = 16
N_GROUPS = 4
EXPERTS_PER_GROUP = 4
TOP_K = 2
D_EXPERT = 512
EPS = 1e-6

LANES = 128
SUBLANES = 8
ROW_TILES = D_MODEL // LANES
assert ROW_TILES == SUBLANES
VMEM_LIMIT = 48 * 1024 * 1024

F32 = jnp.float32
BF16 = jnp.bfloat16
HIGHEST = lax.Precision.HIGHEST


def _params(sem):
    return pltpu.CompilerParams(dimension_semantics=sem, vmem_limit_bytes=VMEM_LIMIT)


def _tile(pref, n):
    t = min(pref, n)
    assert n % t == 0, (pref, n)
    return t


def _ada_kernel(c_ref, w_ref, b_ref, o_ref):
    c = c_ref[...]
    c_act = c * jax.nn.sigmoid(c)
    o_ref[0] = jnp.dot(c_act, w_ref[0], precision=HIGHEST, preferred_element_type=F32) + b_ref[0]


def _ada_mod(c, w_ada, b_ada):
    depth, d, six_d = w_ada.shape
    bsz = c.shape[0]
    tn = _tile(1536, six_d)
    return pl.pallas_call(
        _ada_kernel,
        grid=(depth, six_d // tn),
        in_specs=[
            pl.BlockSpec((bsz, d), lambda l, j: (0, 0)),
            pl.BlockSpec((1, d, tn), lambda l, j: (l, 0, j)),
            pl.BlockSpec((1, 1, tn), lambda l, j: (l, 0, j)),
        ],
        out_specs=pl.BlockSpec((1, bsz, tn), lambda l, j: (l, 0, j)),
        out_shape=jax.ShapeDtypeStruct((depth, bsz, six_d), F32),
        compiler_params=_params(("parallel", "parallel")),
        name="ada_mod",
    )(c, w_ada, b_ada.reshape(depth, 1, six_d))


def _modulated_rms(x, g, sc1p, sh):
    y = x * lax.rsqrt(jnp.mean(x * x, axis=-1, keepdims=True) + EPS)
    return (y * g) * sc1p + sh


def _gated_residual(x_ref, y_ref, g_ref):
    rows = x_ref.shape[0]
    g = g_ref[0]
    return jnp.concatenate(
        [x_ref[:, j * LANES:(j + 1) * LANES]
         + g[:, j * LANES:(j + 1) * LANES] * y_ref[pl.ds(j, rows, stride=ROW_TILES), :] for j in range(ROW_TILES)],
        axis=1)


def _norm_mm_kernel(*refs, col_chunk, has_gates, has_residual):
    refs = list(refs)
    x_ref = refs.pop(0)
    if has_residual:
        y_ref, g2_ref = refs.pop(0), refs.pop(0)
    g_ref, sc_ref, sh_ref, w_ref = refs[:4]
    refs = refs[4:]
    wg_ref = refs.pop(0) if has_gates else None
    xo_ref = refs.pop(0) if has_residual else None
    o_ref = refs.pop(0)
    og_ref = refs.pop(0) if has_gates else None

    if has_residual:
        x = _gated_residual(x_ref, y_ref, g2_ref)
        xo_ref[...] = x
    else:
        x = x_ref[...]
    hb = _modulated_rms(x, g_ref[...], sc_ref[0], sh_ref[0]).astype(BF16)
    for c0 in range(0, o_ref.shape[1], col_chunk):
        o_ref[:, c0:c0 + col_chunk] = jnp.dot(
            hb, w_ref[:, c0:c0 + col_chunk], preferred_element_type=F32).astype(o_ref.dtype)
    if has_gates:
        og_ref[...] = jnp.dot(hb, wg_ref[...], preferred_element_type=F32)


def _norm_matmul(x, residual, gain, sc1p, sh, w, wg, seq):
    n, d = x.shape
    cols = w.shape[1]
    tm = _tile(512, seq)
    per_b = seq // tm
    has_gates = wg is not None
    has_residual = residual is not None
    bmap = lambda i: (i // per_b, 0, 0)
    in_specs = [pl.BlockSpec((tm, d), lambda i: (i, 0))]
    args = [x]
    if has_residual:
        in_specs += [pl.BlockSpec((tm * ROW_TILES, LANES), lambda i: (i, 0)), pl.BlockSpec((1, 1, d), bmap)]
        args += list(residual)
    in_specs += [
        pl.BlockSpec((1, d), lambda i: (0, 0)),
        pl.BlockSpec((1, 1, d), bmap),
        pl.BlockSpec((1, 1, d), bmap),
        pl.BlockSpec((d, cols), lambda i: (0, 0)),
    ]
    args += [gain, sc1p, sh, w]
    out_specs, out_shape = [], []
    if has_residual:
        out_specs.append(pl.BlockSpec((tm, d), lambda i: (i, 0)))
        out_shape.append(jax.ShapeDtypeStruct((n, d), F32))
    out_specs.append(pl.BlockSpec((tm, cols), lambda i: (i, 0)))
    out_shape.append(jax.ShapeDtypeStruct((n, cols), BF16))
    if has_gates:
        in_specs.append(pl.BlockSpec((d, LANES), lambda i: (0, 0)))
        out_specs.append(pl.BlockSpec((tm, LANES), lambda i: (i, 0)))
        out_shape.append(jax.ShapeDtypeStruct((n, LANES), F32))
        args.append(wg)
    return pl.pallas_call(
        functools.partial(_norm_mm_kernel, col_chunk=512, has_gates=has_gates, has_residual=has_residual),
        grid=(n // tm,),
        in_specs=in_specs,
        out_specs=out_specs,
        out_shape=out_shape,
        compiler_params=_params(("parallel",)),
        name="norm_inproj",
    )(*args)


def _log_sigmoid(x):
    return jnp.minimum(x, 0.0) - jnp.log1p(jnp.exp(-jnp.abs(x)))


def _mlstm_chunk(r0, p_ref, g_ref, bias_ref, hn_ref, o_ref, ct_ref, n_ref, m_ref):
    L = A_CHUNK
    rows = slice(r0, r0 + L)
    gates = g_ref[rows, :] + bias_ref[...]
    src = lax.broadcasted_iota(jnp.int32, (L, L), 0)
    tgt = lax.broadcasted_iota(jnp.int32, (L, L), 1)
    causal = src <= tgt
    bcum = jnp.dot(jnp.where(src >= tgt, 1.0, 0.0), _log_sigmoid(gates), precision=HIGHEST,
                   preferred_element_type=F32)
    bcum_t = bcum.T
    nt = (((1,), (1,)), ((), ()))

    pre = []
    for h in range(A_HEADS):
        q = p_ref[rows, h * A_QK_DIM:(h + 1) * A_QK_DIM]
        ksf = p_ref[rows, A_HQ + h * A_QK_DIM:A_HQ + (h + 1) * A_QK_DIM].astype(F32) * (A_QK_DIM ** -0.5)
        v = p_ref[rows, 2 * A_HQ + h * A_V_DIM:2 * A_HQ + (h + 1) * A_V_DIM]
        ct_old = ct_ref[h]
        n_old = n_ref[h]
        qk = lax.dot_general(ksf.astype(BF16), q, nt, preferred_element_type=F32)
        qc = lax.dot_general(ct_old.astype(BF16), q, nt, preferred_element_type=F32)
        qn = lax.dot_general(n_old.astype(BF16), q, nt, preferred_element_type=F32)[0:1]
        vt = v.astype(F32).T.astype(BF16)
        pre.append((ksf, vt, ct_old, n_old, qk, qc, qn))

    for h in range(A_HEADS):
        ksf, vt, ct_old, n_old, qk, qc, qn = pre[h]
        og = p_ref[rows, 2 * A_HQ + A_HV + h * A_V_DIM:2 * A_HQ + A_HV + (h + 1) * A_V_DIM].astype(F32)
        b_row = bcum_t[A_HEADS + h:A_HEADS + h + 1, :]
        b_last = b_row[:, L - 1:L]
        c_col = gates[:, h:h + 1] - bcum[:, A_HEADS + h:A_HEADS + h + 1]
        m11 = m_ref[h][:, 0:1]

        log_d = jnp.where(causal, c_col + b_row, -jnp.inf)
        log_inter = b_row + m11
        m_t = jnp.maximum(jnp.max(log_d, axis=0, keepdims=True), log_inter)
        dmat = jnp.exp(log_d - m_t)
        inter = jnp.exp(log_inter - m_t)
        s = qk * dmat
        num = jnp.dot(vt, s.astype(BF16), preferred_element_type=F32) + inter * qc
        den = jnp.sum(s, axis=0, keepdims=True) + inter * qn
        hh = num / jnp.maximum(jnp.abs(den), jnp.exp(-m_t))

        lw_col = b_last + c_col
        m_new = jnp.maximum(b_last + m11, jnp.max(lw_col, axis=0, keepdims=True))
        kw = ksf * jnp.exp(lw_col - m_new)
        decay = jnp.exp(b_last + m11 - m_new)
        ct_ref[h] = decay * ct_old + jnp.dot(vt, kw.astype(BF16), preferred_element_type=F32)
        n_ref[h] = decay * n_old + jnp.broadcast_to(jnp.sum(kw, axis=0, keepdims=True), n_old.shape)
        m_ref[h] = jnp.broadcast_to(m_new, (1, LANES))

        hn = hh * lax.rsqrt(jnp.mean(hh * hh, axis=0, keepdims=True) + EPS)
        hn = (hn * hn_ref[h * A_V_DIM:(h + 1) * A_V_DIM, :]).T
        o_ref[rows, h * A_V_DIM:(h + 1) * A_V_DIM] = (hn * jax.nn.sigmoid(og)).astype(o_ref.dtype)


MLSTM_CHUNKS_PER_STEP = 2


def _mlstm_kernel(p_ref, g_ref, bias_ref, hn_ref, o_ref, ct_ref, n_ref, m_ref):
    @pl.when(pl.program_id(1) == 0)
    def _():
        ct_ref[...] = jnp.zeros_like(ct_ref)
        n_ref[...] = jnp.zeros_like(n_ref)
        m_ref[...] = jnp.zeros_like(m_ref)

    for r0 in range(0, p_ref.shape[0], A_CHUNK):
        _mlstm_chunk(r0, p_ref, g_ref, bias_ref, hn_ref, o_ref, ct_ref, n_ref, m_ref)


def _mlstm(proj, gates, bias_row, h_norm, bsz, seq):
    n = proj.shape[0]
    rows = _tile(MLSTM_CHUNKS_PER_STEP * A_CHUNK, seq)
    nc = seq // rows
    h_norm_cols = jnp.broadcast_to(h_norm[:, None], (A_HV, A_CHUNK))
    return pl.pallas_call(
        _mlstm_kernel,
        grid=(bsz, nc),
        in_specs=[
            pl.BlockSpec((rows, A_MAIN_COLS), lambda b, c: (b * nc + c, 0)),
            pl.BlockSpec((rows, LANES), lambda b, c: (b * nc + c, 0)),
            pl.BlockSpec((1, LANES), lambda b, c: (0, 0)),
            pl.BlockSpec((A_HV, A_CHUNK), lambda b, c: (0, 0)),
        ],
        out_specs=pl.BlockSpec((rows, A_HV), lambda b, c: (b * nc + c, 0)),
        out_shape=jax.ShapeDtypeStruct((n, A_HV), BF16),
        scratch_shapes=[
            pltpu.VMEM((A_HEADS, A_V_DIM, A_QK_DIM), F32),
            pltpu.VMEM((A_HEADS, SUBLANES, A_QK_DIM), F32),
            pltpu.VMEM((A_HEADS, 1, LANES), F32),
        ],
        compiler_params=_params(("parallel", "arbitrary")),
        name="mlstm",
    )(proj, gates, bias_row, h_norm_cols)


QK_PREP_ROW_CHUNK = 128


def _qk_prep_kernel(p_ref, pos_ref, seg_ref, segt_ref, gain_ref, freq_ref, o_ref):
    tm = p_ref.shape[0]
    rc = min(QK_PREP_ROW_CHUNK, tm)
    chunks = [slice(r0, r0 + rc) for r0 in range(0, tm, rc)]
    xs = [p_ref[rows, :].astype(F32) for rows in chunks]
    sss = [jnp.dot((x * x).astype(BF16), seg_ref[...], preferred_element_type=F32) for x in xs]
    r_fulls = []
    for ss in sss:
        r = lax.rsqrt(ss * (1.0 / B_HEAD_DIM) + EPS)
        r_hi = r.astype(BF16)
        r_lo = (r - r_hi.astype(F32)).astype(BF16)
        r_fulls.append(jnp.dot(jnp.concatenate([r_hi, r_lo], axis=1), segt_ref[...], preferred_element_type=F32))

    assert B_ROT_DIM // 2 == SUBLANES
    ang = freq_ref[:, 0:1] * pos_ref[0].astype(F32)
    cos8 = jnp.cos(ang)
    sin8 = jnp.sin(ang)
    one8 = jnp.ones_like(cos8)
    zero8 = jnp.zeros_like(cos8)
    groups_per_quarter = MAP_QUARTER // SUBLANES

    def chunk_pattern(lower, upper, rest):
        quarter = lambda lead: [lead] + [rest] * (groups_per_quarter - 1)
        return jnp.concatenate(quarter(lower) * 2 + quarter(upper) * 2, axis=0).T

    c_all = chunk_pattern(cos8, cos8, one8)
    s_all = chunk_pattern(-sin8, sin8, zero8)
    for rows, x, r_full in zip(chunks, xs, r_fulls):
        c_mul, s_mul = c_all[rows, :], s_all[rows, :]
        for j in range(2 * D_MODEL // LANES):
            sl = slice(j * LANES, (j + 1) * LANES)
            y = x[:, sl] * r_full[:, sl] * gain_ref[:, sl]
            out = y * c_mul + pltpu.roll(y, LANES // 2, axis=1) * s_mul
            if j < D_MODEL // LANES:
                out = out * (B_HEAD_DIM ** -0.5 * math.log2(math.e))
            o_ref[rows, sl] = out.astype(o_ref.dtype)


MAP_QUARTER = B_HEAD_DIM // 2


def _interleave_maps(t):
    half = B_ROT_DIM // 2
    rest = MAP_QUARTER - half
    lead = t.shape[:-1]
    t = t.reshape(*lead, 2 * B_HEADS, 2, B_HEAD_DIM)
    lower = jnp.concatenate([t[..., 0:half], t[..., B_ROT_DIM:B_ROT_DIM + rest]], axis=-1)
    upper = jnp.concatenate([t[..., half:B_ROT_DIM], t[..., B_ROT_DIM + rest:]], axis=-1)
    out = jnp.stack([lower, upper], axis=-3)
    return out.reshape(*lead, 2 * D_MODEL)


def _qk_prep(proj, positions_row, q_norm, k_norm, seq):
    n = proj.shape[0]
    two_d = 2 * D_MODEL
    tm = _tile(512, seq)
    seg_of_col = _interleave_maps(jnp.arange(two_d) // B_HEAD_DIM)
    seg = (seg_of_col[:, None] == jnp.arange(LANES)[None, :]).astype(BF16)
    gain = _interleave_maps(
        jnp.concatenate([jnp.tile(q_norm, D_MODEL // B_HEAD_DIM), jnp.tile(k_norm, D_MODEL // B_HEAD_DIM)]))
    inv_freq = ROPE_THETA ** (-jnp.arange(0, B_ROT_DIM, 2, dtype=F32) / B_ROT_DIM)
    freq_tab = jnp.broadcast_to(inv_freq[:, None], (B_ROT_DIM // 2, LANES))
    return pl.pallas_call(
        _qk_prep_kernel,
        grid=(n // tm,),
        in_specs=[
            pl.BlockSpec((tm, two_d), lambda i: (i, 0)),
            pl.BlockSpec((1, 1, tm), lambda i: (i, 0, 0)),
            pl.BlockSpec((two_d, LANES), lambda i: (0, 0)),
            pl.BlockSpec((2 * LANES, two_d), lambda i: (0, 0)),
            pl.BlockSpec((1, two_d), lambda i: (0, 0)),
            pl.BlockSpec((SUBLANES, LANES), lambda i: (0, 0)),
        ],
        out_specs=pl.BlockSpec((tm, two_d), lambda i: (i, 0)),
        out_shape=jax.ShapeDtypeStruct((n, two_d), BF16),
        compiler_params=_params(("parallel",)),
        name="qk_prep",
    )(proj, positions_row.reshape(n // tm, 1, tm), seg, jnp.concatenate([seg.T, seg.T], axis=0),
      gain.reshape(1, two_d), freq_tab)


ATTN_TILE = 512
ATTN_KEY_TILE = 256
ATTN_HEADS_PER_STEP = 4


def _diff_attn_kernel(q_ref, k_ref, v_ref, lam_ref, on_ref, o_ref, vt_ref, qm_ref, m_ref, l_ref, acc_ref,
                      *, tile, tk, lam_init):
    seq = q_ref.shape[0]
    heads = q_ref.shape[1] // LANES
    n_tiles = seq // tile
    k_per_q = tile // tk
    lamv = lam_ref[...]
    lam = (jnp.exp(jnp.sum(lamv[0:1] * lamv[1:2], axis=1, keepdims=True))
           - jnp.exp(jnp.sum(lamv[2:3] * lamv[3:4], axis=1, keepdims=True)) + lam_init)
    head_lanes = lambda h: slice(h * LANES, (h + 1) * LANES)
    tchunk = min(256, seq)
    for h in range(heads):
        for c in range(seq // tchunk):
            rows = slice(c * tchunk, (c + 1) * tchunk)
            vt_ref[h, :, rows] = v_ref[rows, head_lanes(h)].astype(F32).T.astype(BF16)
    first_map = (lax.broadcasted_iota(jnp.int32, (tile, LANES), 1) // MAP_QUARTER) % 2 == 0
    key_id = lax.broadcasted_iota(jnp.int32, (tk, tile), 0)
    query_id = lax.broadcasted_iota(jnp.int32, (tk, tile), 1)
    nt = (((1,), (1,)), ((), ()))

    def qk_score(h, qi, ki, c):
        return lax.dot_general(k_ref[ki * tk:(ki + 1) * tk, head_lanes(h)], qm_ref[(h * n_tiles + qi) * 2 + c], nt,
                               preferred_element_type=F32)

    def softmax_pv(h, c, s, ki, mask):
        j = 2 * h + c
        vt = vt_ref[h, :, ki * tk:(ki + 1) * tk]
        if mask is not None:
            s = jnp.where(mask, s, -jnp.inf)
        m_old = m_ref[j]
        m_new = jnp.maximum(m_old, jnp.max(s, axis=0, keepdims=True))
        alpha = jnp.exp2(m_old - m_new)
        p = jnp.exp2(s - m_new)
        l_ref[j] = alpha * l_ref[j] + jnp.sum(p, axis=0, keepdims=True)
        acc_ref[j] = alpha * acc_ref[j] + jnp.dot(vt, p.astype(BF16), preferred_element_type=F32)
        m_ref[j] = m_new

    def reset(h):
        for j in (2 * h, 2 * h + 1):
            m_ref[j] = jnp.full(m_ref.shape[1:], -jnp.inf, F32)
            l_ref[j] = jnp.zeros(l_ref.shape[1:], F32)
            acc_ref[j] = jnp.zeros(acc_ref.shape[1:], F32)

    for h in range(heads):
        for qi in range(n_tiles):
            q = q_ref[qi * tile:(qi + 1) * tile, head_lanes(h)]
            qm_ref[(h * n_tiles + qi) * 2] = jnp.where(first_map, q, jnp.zeros_like(q))
            qm_ref[(h * n_tiles + qi) * 2 + 1] = jnp.where(first_map, jnp.zeros_like(q), q)
        reset(h)
    steps = [(h, qi, ki) for qi in range(n_tiles) for ki in range((qi + 1) * k_per_q) for h in range(heads)]
    scores_next = [qk_score(*steps[0], c) for c in range(2)]
    for n, (h, qi, ki) in enumerate(steps):
        scores = scores_next
        scores_next = [None, None]
        on_diagonal = ki >= qi * k_per_q
        mask = (query_id + qi * tile >= key_id + ki * tk) if on_diagonal else None
        for c in range(2):
            if n + 1 < len(steps):
                scores_next[c] = qk_score(*steps[n + 1], c)
            softmax_pv(h, c, scores[c], ki, mask)
        if ki == (qi + 1) * k_per_q - 1:
            o = acc_ref[2 * h] / l_ref[2 * h] - lam * (acc_ref[2 * h + 1] / l_ref[2 * h + 1])
            o = o * lax.rsqrt(jnp.mean(o * o, axis=0, keepdims=True) + EPS) * on_ref[...] * (1.0 - lam_init)
            o_ref[qi * tile:(qi + 1) * tile, head_lanes(h)] = o.T.astype(o_ref.dtype)
            if qi + 1 < n_tiles:
                reset(h)


def _diff_attn(qk, proj, lam_tab, o_norm, bsz, seq, lam_init):
    n = qk.shape[0]
    hp = ATTN_HEADS_PER_STEP
    groups = B_HEADS // hp
    tile = _tile(ATTN_TILE, seq)
    o_norm_cols = jnp.broadcast_to(o_norm[:, None], (B_V_DIM, tile))
    return pl.pallas_call(
        functools.partial(_diff_attn_kernel, tile=tile, tk=_tile(ATTN_KEY_TILE, tile), lam_init=lam_init),
        grid=(bsz, groups),
        in_specs=[
            pl.BlockSpec((seq, hp * LANES), lambda b, g: (b, g)),
            pl.BlockSpec((seq, hp * LANES), lambda b, g: (b, groups + g)),
            pl.BlockSpec((seq, hp * B_V_DIM), lambda b, g: (b, 2 * groups + g)),
            pl.BlockSpec((SUBLANES, LANES), lambda b, g: (0, 0)),
            pl.BlockSpec((B_V_DIM, tile), lambda b, g: (0, 0)),
        ],
        out_specs=pl.BlockSpec((seq, hp * B_V_DIM), lambda b, g: (b, g)),
        out_shape=jax.ShapeDtypeStruct((n, D_MODEL), BF16),
        scratch_shapes=[
            pltpu.VMEM((hp, B_V_DIM, seq), BF16),
            pltpu.VMEM((hp * 2 * (seq // tile), tile, LANES), BF16),
            pltpu.VMEM((2 * hp, 1, tile), F32),
            pltpu.VMEM((2 * hp, 1, tile), F32),
            pltpu.VMEM((2 * hp, B_V_DIM, tile), F32),
        ],
        compiler_params=_params(("parallel", "parallel")),
        name="diff_attn",
    )(qk, qk, proj, lam_tab, o_norm_cols)


HALF_D = D_MODEL // 2
HI16 = 0xFFFF0000
PACKED_TILES = HALF_D // LANES


def _pack_bf16_pairs(v):
    bits = lax.bitcast_convert_type(v.astype(BF16).astype(F32), jnp.uint32)
    return (bits[:, HALF_D:] & jnp.uint32(HI16)) | (bits[:, :HALF_D] >> 16)


def _unpack_bf16_pairs(words):
    first = lax.bitcast_convert_type(words << 16, F32)
    second = lax.bitcast_convert_type(words & jnp.uint32(HI16), F32)
    return first.astype(BF16), second.astype(BF16)


OUTPROJ_ROW_CHUNK = 128


def _outproj_kernel(a_ref, w_ref, x_ref, g1_ref, gain_ref, sc_ref, sh_ref, wr_ref, xo_ref, hp_ref, lg_ref):
    tm = x_ref.shape[0]
    rc = min(OUTPROJ_ROW_CHUNK, tm)
    nt = (((1,), (1,)), ((), ()))
    wr = wr_ref[...]
    wr_hi = wr.astype(BF16)
    wr_lo = (wr - wr_hi.astype(F32)).astype(BF16)
    ys = [jnp.dot(a_ref[r0:r0 + rc, :], w_ref[...], preferred_element_type=F32) for r0 in range(0, tm, rc)]
    for y, r0 in zip(ys, range(0, tm, rc)):
        xm = x_ref[r0:r0 + rc, :] + g1_ref[0] * y
        xo_ref[r0:r0 + rc, :] = xm
        h2 = _modulated_rms(xm, gain_ref[...], sc_ref[0], sh_ref[0])
        packed = _pack_bf16_pairs(h2)
        for j in range(ROW_TILES):
            hp_ref[pl.ds(r0 * ROW_TILES + j, rc, stride=ROW_TILES), :] = (
                packed[:, j * LANES:(j + 1) * LANES] if j < PACKED_TILES else jnp.zeros((rc, LANES), jnp.uint32))
        h_hi = h2.astype(BF16)
        h_lo = (h2 - h_hi.astype(F32)).astype(BF16)
        lg_ref[:, r0:r0 + rc] = (lax.dot_general(wr_hi, h_hi, nt, preferred_element_type=F32)
                                 + lax.dot_general(wr_hi, h_lo, nt, preferred_element_type=F32)
                                 + lax.dot_general(wr_lo, h_hi, nt, preferred_element_type=F32))


def _outproj_norm_router(a, w, x, g1, gain, sc1p, sh, w_router_t, seq):
    n, d = x.shape
    tm = _tile(512, seq)
    per_b = seq // tm
    bmap = lambda i: (i // per_b, 0, 0)
    return pl.pallas_call(
        _outproj_kernel,
        grid=(n // tm,),
        in_specs=[
            pl.BlockSpec((tm, a.shape[1]), lambda i: (i, 0)),
            pl.BlockSpec(w.shape, lambda i: (0, 0)),
            pl.BlockSpec((tm, d), lambda i: (i, 0)),
            pl.BlockSpec((1, 1, d), bmap),
            pl.BlockSpec((1, d), lambda i: (0, 0)),
            pl.BlockSpec((1, 1, d), bmap),
            pl.BlockSpec((1, 1, d), bmap),
            pl.BlockSpec((N_EXPERTS, d), lambda i: (0, 0)),
        ],
        out_specs=[
            pl.BlockSpec((tm, d), lambda i: (i, 0)),
            pl.BlockSpec((tm * ROW_TILES, LANES), lambda i: (i, 0)),
            pl.BlockSpec((N_EXPERTS, tm), lambda i: (0, i)),
        ],
        out_shape=[
            jax.ShapeDtypeStruct((n, d), F32),
            jax.ShapeDtypeStruct((n * ROW_TILES, LANES), jnp.uint32),
            jax.ShapeDtypeStruct((N_EXPERTS, n), F32),
        ],
        compiler_params=_params(("parallel",)),
        name="outproj_norm_router",
    )(a, w, x, g1, gain, sc1p, sh, w_router_t)


PAIR_LO = (0, 0, 0, 1, 1, 2)
PAIR_HI = (1, 2, 3, 2, 3, 3)
PAIRS_PER_GROUP = len(PAIR_LO)
N_CLASSES = N_GROUPS * PAIRS_PER_GROUP
CLASS_ROWS = 32
RANK_BITS = 20
RANK_SPAN = 1 << RANK_BITS


def _route_kernel(lg_ref, bias_ref, oi_ref, cnt_ref, carry_ref, tri_ref):
    tr = lg_ref.shape[1]
    step = pl.program_id(0)

    @pl.when(step == 0)
    def _():
        carry_ref[...] = jnp.zeros_like(carry_ref)
        r = lax.broadcasted_iota(jnp.int32, (tr, tr), 0)
        c = lax.broadcasted_iota(jnp.int32, (tr, tr), 1)
        tri_ref[...] = jnp.where(r < c, 1.0, 0.0).astype(BF16)

    scores = jax.nn.sigmoid(lg_ref[...])
    biased = scores + bias_ref[...]
    rows = [biased[e:e + 1, :] for e in range(N_EXPERTS)]

    def top2_sum(a, b, c, d):
        m1, n1 = jnp.maximum(a, b), jnp.minimum(a, b)
        m2, n2 = jnp.maximum(c, d), jnp.minimum(c, d)
        return jnp.maximum(m1, m2) + jnp.maximum(jnp.minimum(m1, m2), jnp.maximum(n1, n2))

    gscore = [top2_sum(*rows[g * EXPERTS_PER_GROUP:(g + 1) * EXPERTS_PER_GROUP]) for g in range(N_GROUPS)]
    best = gscore[0]
    gsel = jnp.zeros_like(best, dtype=jnp.int32)
    for g in range(1, N_GROUPS):
        upd = gscore[g] > best
        gsel = jnp.where(upd, g, gsel)
        best = jnp.where(upd, gscore[g], best)

    def pick(table, j):
        out = table[j]
        for g in range(1, N_GROUPS):
            out = jnp.where(gsel == g, table[g * EXPERTS_PER_GROUP + j], out)
        return out

    in_b = [pick(rows, j) for j in range(EXPERTS_PER_GROUP)]
    v1, i1 = in_b[0], jnp.zeros_like(gsel)
    for j in range(1, EXPERTS_PER_GROUP):
        upd = in_b[j] > v1
        v1 = jnp.where(upd, in_b[j], v1)
        i1 = jnp.where(upd, j, i1)
    v2 = jnp.full_like(v1, -jnp.inf)
    i2 = jnp.zeros_like(gsel)
    for j in range(EXPERTS_PER_GROUP):
        upd = (i1 != j) & (in_b[j] > v2)
        v2 = jnp.where(upd, in_b[j], v2)
        i2 = jnp.where(upd, j, i2)
    first_is_lo = i1 < i2
    lo = jnp.where(first_is_lo, i1, i2)
    hi = jnp.where(first_is_lo, i2, i1)
    pair = jnp.where(lo == 0, hi - 1, jnp.where(lo == 1, hi + 1, PAIRS_PER_GROUP - 1))
    cls = gsel * PAIRS_PER_GROUP + pair

    cid = lax.broadcasted_iota(jnp.int32, (CLASS_ROWS, tr), 0)
    onehot = jnp.where(cid == cls, 1.0, 0.0)
    before = jnp.dot(onehot.astype(BF16), tri_ref[...], preferred_element_type=F32) + carry_ref[...]
    rank = jnp.sum(onehot * before, axis=0, keepdims=True).astype(jnp.int32)
    oi_ref[...] = cls * RANK_SPAN + rank
    new_carry = carry_ref[...] + jnp.sum(onehot, axis=1, keepdims=True)
    carry_ref[...] = new_carry
    cnt_ref[...] = new_carry.astype(jnp.int32)


def _route(logits_t, router_bias):
    n = logits_t.shape[1]
    tr = _tile(512, n)
    return pl.pallas_call(
        _route_kernel,
        grid=(n // tr,),
        in_specs=[
            pl.BlockSpec((N_EXPERTS, tr), lambda i: (0, i)),
            pl.BlockSpec((N_EXPERTS, 1), lambda i: (0, 0)),
        ],
        out_specs=[
            pl.BlockSpec((1, tr), lambda i: (0, i)),
            pl.BlockSpec((CLASS_ROWS, 1), lambda i: (0, 0)),
        ],
        out_shape=[
            jax.ShapeDtypeStruct((1, n), jnp.int32),
            jax.ShapeDtypeStruct((CLASS_ROWS, 1), jnp.int32),
        ],
        scratch_shapes=[pltpu.VMEM((CLASS_ROWS, 1), F32), pltpu.VMEM((tr, tr), BF16)],
        compiler_params=_params(("arbitrary",)),
        name="route",
    )(logits_t, router_bias.reshape(N_EXPERTS, 1).astype(F32))


SORT_UNROLL = 8
MOE_ROWS = 256
MOE_ROWS_LOG2 = MOE_ROWS.bit_length() - 1
assert 1 << MOE_ROWS_LOG2 == MOE_ROWS


def _plan_kernel(code_ref, cnt_ref, tok_ref, ea_ref, eb_ref, off_ref, nv_ref, nu_ref, start_ref):
    n_blocks = ea_ref.shape[0]
    run = jnp.int32(0)
    blk = jnp.int32(0)
    for c in range(N_CLASSES):
        cnt = cnt_ref[c]
        start_ref[c] = run - c * RANK_SPAN
        e_lo = (c // PAIRS_PER_GROUP) * EXPERTS_PER_GROUP + PAIR_LO[c % PAIRS_PER_GROUP]
        e_hi = (c // PAIRS_PER_GROUP) * EXPERTS_PER_GROUP + PAIR_HI[c % PAIRS_PER_GROUP]
        n_blk = (cnt + (MOE_ROWS - 1)) >> MOE_ROWS_LOG2

        def fill(b, carry, run=run, blk=blk, cnt=cnt, e_lo=e_lo, e_hi=e_hi):
            ea_ref[blk + b] = e_lo
            eb_ref[blk + b] = e_hi
            off_ref[blk + b] = run + b * MOE_ROWS
            nv_ref[blk + b] = jnp.minimum(cnt - b * MOE_ROWS, MOE_ROWS)
            return carry

        lax.fori_loop(0, n_blk, fill, 0)
        run = run + cnt
        blk = blk + n_blk
    nu_ref[0] = blk

    def fill_unused(b, carry):
        ea_ref[b] = ea_ref[blk - 1]
        eb_ref[b] = eb_ref[blk - 1]
        off_ref[b] = 0
        nv_ref[b] = 0
        return carry

    lax.fori_loop(blk, n_blocks, fill_unused, 0)

    def place(i, carry):
        toks = [i * SORT_UNROLL + u for u in range(SORT_UNROLL)]
        codes = [code_ref[t] for t in toks]
        slots = [start_ref[code >> RANK_BITS] + code for code in codes]
        for t, p in zip(toks, slots):
            tok_ref[p] = t
        return carry

    lax.fori_loop(0, code_ref.shape[0] // SORT_UNROLL, place, 0)


def _plan(codes, counts, n_blocks):
    n = codes.shape[0]
    assert n % SORT_UNROLL == 0 and n <= RANK_SPAN
    smem = pl.BlockSpec(memory_space=pltpu.SMEM)
    i32 = lambda size: jax.ShapeDtypeStruct((size,), jnp.int32)
    return pl.pallas_call(
        _plan_kernel,
        in_specs=[smem, smem],
        out_specs=[smem] * 6,
        out_shape=[i32(n), i32(n_blocks), i32(n_blocks), i32(n_blocks), i32(n_blocks), i32(1)],
        scratch_shapes=[pltpu.SMEM((N_CLASSES,), jnp.int32)],
        name="moe_plan",
    )(codes, counts)


ROW_UNROLL = 8


def _for_rows(n, fn):
    groups = n // ROW_UNROLL

    def group(g, c):
        for u in range(ROW_UNROLL):
            fn(g * ROW_UNROLL + u)
        return c

    def single(r, c):
        fn(r)
        return c

    lax.fori_loop(0, groups, group, 0)
    lax.fori_loop(groups * ROW_UNROLL, n, single, 0)


def _row_copy(src_ref, src_row, dst_ref, dst_row, sem, sublanes=ROW_TILES):
    return pltpu.make_async_copy(
        src_ref.at[pl.ds(pl.multiple_of(src_row * ROW_TILES, ROW_TILES), sublanes)],
        dst_ref.at[pl.ds(pl.multiple_of(dst_row * ROW_TILES, ROW_TILES), sublanes)],
        sem)


def _expert_kernel(ea_ref, eb_ref, off_ref, nv_ref, nu_ref, tok_ref,
                   h_hbm, wgu_a_ref, wd_a_ref, wgu_b_ref, wd_b_ref, wr_a_ref, wr_b_ref, y_hbm,
                   xbuf, ybuf, gsem, ssem):
    del ea_ref, eb_ref
    tb = xbuf.shape[1] // ROW_TILES
    i = pl.program_id(0)
    slot = i % 2
    n_used = nu_ref[0]

    def start_gather(block, s):
        base = off_ref[block]
        _for_rows(nv_ref[block],
                  lambda r: _row_copy(h_hbm, tok_ref[base + r], xbuf.at[s], r, gsem.at[s], PACKED_TILES).start())

    def start_scatter(block, s):
        base = off_ref[block]
        _for_rows(nv_ref[block],
                  lambda r: _row_copy(ybuf.at[s], r, y_hbm, tok_ref[base + r], ssem.at[s]).start())

    def wait_rows(block, hbm, buf, sem, sublanes=ROW_TILES):
        rows = nv_ref[block] * sublanes

        @pl.when(rows > 0)
        def _():
            pltpu.make_async_copy(hbm.at[pl.ds(0, rows)], buf.at[pl.ds(0, rows)], sem).wait()

    @pl.when(i == 0)
    def _():
        xbuf[...] = jnp.zeros_like(xbuf)
        start_gather(0, 0)

    @pl.when(i < n_used)
    def _():
        @pl.when(i + 1 < n_used)
        def _():
            start_gather(i + 1, 1 - slot)

        wait_rows(i, h_hbm, xbuf.at[slot], gsem.at[slot], PACKED_TILES)

        @pl.when(i >= 2)
        def _():
            wait_rows(i - 2, y_hbm, ybuf.at[slot], ssem.at[slot])

        words = [xbuf[slot, pl.ds(j, tb, stride=ROW_TILES), :] for j in range(PACKED_TILES)]
        halves = [_unpack_bf16_pairs(w) for w in words]
        x = jnp.concatenate([h[0] for h in halves] + [h[1] for h in halves], axis=1)

        def mlp(wgu_ref, wd_ref):
            gu = jnp.dot(x, wgu_ref[0], preferred_element_type=F32)
            gate = gu[:, :D_EXPERT]
            act = gate * jax.nn.sigmoid(gate) * gu[:, D_EXPERT:]
            return jnp.dot(act.astype(BF16), wd_ref[0], preferred_element_type=F32)

        score_a = jax.nn.sigmoid(jnp.dot(x, wr_a_ref[0], preferred_element_type=F32))
        score_b = jax.nn.sigmoid(jnp.dot(x, wr_b_ref[0], preferred_element_type=F32))
        inv_tot = 1.0 / (score_a + score_b)
        w_a = score_a * inv_tot
        w_b = score_b * inv_tot
        ya = mlp(wgu_a_ref, wd_a_ref)
        yb = mlp(wgu_b_ref, wd_b_ref)
        for j in range(ROW_TILES):
            sl = slice(j * LANES, (j + 1) * LANES)
            ybuf[slot, pl.ds(j, tb, stride=ROW_TILES), :] = w_a * ya[:, sl] + w_b * yb[:, sl]
        start_scatter(i, slot)

        @pl.when(i == n_used - 1)
        def _():
            wait_rows(i, y_hbm, ybuf.at[slot], ssem.at[slot])

            @pl.when(i >= 1)
            def _():
                wait_rows(i - 1, y_hbm, ybuf.at[1 - slot], ssem.at[1 - slot])


def _experts(tables, sorted_tok, h_rows, layer, w_gu, w_down, wr_bcast, tb):
    block_ea, block_eb, src_off, n_valid, n_used = tables
    n_blocks = block_ea.shape[0]
    d, two_f = w_gu.shape[2:]
    ea_map = lambda i, ea, eb, off, nv, nu, tok: (ea[i], 0, 0)
    eb_map = lambda i, ea, eb, off, nv, nu, tok: (eb[i], 0, 0)
    lea_map = lambda i, ea, eb, off, nv, nu, tok: (layer, ea[i], 0, 0)
    leb_map = lambda i, ea, eb, off, nv, nu, tok: (layer, eb[i], 0, 0)
    grid_spec = pltpu.PrefetchScalarGridSpec(
        num_scalar_prefetch=6,
        grid=(n_blocks,),
        in_specs=[
            pl.BlockSpec(memory_space=pl.ANY),
            pl.BlockSpec((None, 1, d, two_f), lea_map),
            pl.BlockSpec((None, 1, two_f // 2, d), lea_map),
            pl.BlockSpec((None, 1, d, two_f), leb_map),
            pl.BlockSpec((None, 1, two_f // 2, d), leb_map),
            pl.BlockSpec((1, d, LANES), ea_map),
            pl.BlockSpec((1, d, LANES), eb_map),
        ],
        out_specs=pl.BlockSpec(memory_space=pl.ANY),
        scratch_shapes=[
            pltpu.VMEM((2, tb * ROW_TILES, LANES), jnp.uint32),
            pltpu.VMEM((2, tb * ROW_TILES, LANES), F32),
            pltpu.SemaphoreType.DMA((2,)),
            pltpu.SemaphoreType.DMA((2,)),
        ],
    )
    return pl.pallas_call(
        _expert_kernel,
        grid_spec=grid_spec,
        out_shape=jax.ShapeDtypeStruct(h_rows.shape, F32),
        compiler_params=_params(("arbitrary",)),
        name="moe_experts",
    )(block_ea, block_eb, src_off, n_valid, n_used, sorted_tok, h_rows, w_gu, w_down, w_gu, w_down, wr_bcast, wr_bcast)


def _residual_kernel(y_ref, x_ref, g2_ref, o_ref):
    o_ref[...] = _gated_residual(x_ref, y_ref, g2_ref)


def _residual(y_rows, x_mid, g2, seq):
    n, d = x_mid.shape
    tc = _tile(512, seq)
    per_b = seq // tc
    return pl.pallas_call(
        _residual_kernel,
        grid=(n // tc,),
        in_specs=[
            pl.BlockSpec((tc * ROW_TILES, LANES), lambda i: (i, 0)),
            pl.BlockSpec((tc, d), lambda i: (i, 0)),
            pl.BlockSpec((1, 1, d), lambda i: (i // per_b, 0, 0)),
        ],
        out_specs=pl.BlockSpec((tc, d), lambda i: (i, 0)),
        out_shape=jax.ShapeDtypeStruct((n, d), F32),
        compiler_params=_params(("parallel",)),
        name="moe_residual",
    )(y_rows, x_mid, g2)


def _moe(h_rows, logits_t, router_bias, layer, w_gu, w_down, wr_bcast):
    n = logits_t.shape[1]
    n_blocks = (n + N_CLASSES * (MOE_ROWS - 1) + MOE_ROWS - 1) // MOE_ROWS
    codes, counts = _route(logits_t, router_bias)
    sorted_tok, *tables = _plan(codes.reshape(n), counts.reshape(CLASS_ROWS), n_blocks)
    return _experts(tables, sorted_tok, h_rows, layer, w_gu, w_down, wr_bcast, MOE_ROWS)


def kernel(x, c, positions, norm1, norm2, w_ada, b_ada, a_w_in, a_b_if, a_h_norm, a_w_out, b_w_in, b_q_norm, b_k_norm, b_lam_q1, b_lam_k1, b_lam_q2, b_lam_k2, b_o_norm, b_w_out, w_router, router_bias, moe_w_gu, moe_w_down):
    bsz, seq, d = x.shape
    depth = w_ada.shape[0]
    n = bsz * seq
    xf = x.reshape(n, d)
    mod = _ada_mod(c, w_ada, b_ada)
    w_router_t = w_router.T
    wr_bcast = jnp.broadcast_to(w_router_t[:, :, None], (N_EXPERTS, d, LANES)).astype(BF16)
    pos_row = positions.reshape(n)
    w_gu = moe_w_gu.astype(BF16)
    w_down = moe_w_down.astype(BF16)

    residual = None
    for l in range(depth):
        sh1, sc1, g1, sh2, sc2, g2 = [mod[l, :, i * d:(i + 1) * d].reshape(bsz, 1, d) for i in range(6)]
        j = l // 2
        if l % 2 == 0:
            w_in = a_w_in[j]
            w_main = w_in[:, :A_MAIN_COLS].astype(BF16)
            w_gate = jnp.pad(w_in[:, A_MAIN_COLS:], ((0, 0), (0, LANES - 2 * A_HEADS))).astype(BF16)
        else:
            w_in = b_w_in[j]
            w_qk = _interleave_maps(w_in[:, :2 * D_MODEL])
            w_main, w_gate = jnp.concatenate([w_qk, w_in[:, 2 * D_MODEL:]], axis=1).astype(BF16), None
        outs = _norm_matmul(xf, residual, norm1[l].reshape(1, d), 1.0 + sc1, sh1, w_main, w_gate, seq)
        if residual is not None:
            xf, *outs = outs
        if l % 2 == 0:
            proj, gates = outs
            bias_row = jnp.pad(a_b_if[j], (0, LANES - 2 * A_HEADS)).reshape(1, LANES)
            mixed = _mlstm(proj, gates, bias_row, a_h_norm[j], bsz, seq)
            w_out = a_w_out[j].astype(BF16)
        else:
            (proj,) = outs
            qk = _qk_prep(proj, pos_row, b_q_norm[j], b_k_norm[j], seq)
            lam_tab = jnp.zeros((SUBLANES, LANES), F32)
            for r, v in enumerate((b_lam_q1[j], b_lam_k1[j], b_lam_q2[j], b_lam_k2[j])):
                lam_tab = lam_tab.at[r, :B_HEAD_DIM].set(v)
            lam_init = 0.8 - 0.6 * math.exp(-0.3 * l)
            mixed = _diff_attn(qk, proj, lam_tab, b_o_norm[j], bsz, seq, lam_init)
            w_out = b_w_out[j].astype(BF16)
        xf, h_rows, logits_t = _outproj_norm_router(
            mixed, w_out, xf, g1, norm2[l].reshape(1, d), 1.0 + sc2, sh2, w_router_t, seq)
        residual = (_moe(h_rows, logits_t, router_bias, l, w_gu, w_down, wr_bcast), g2)
    return _residual(residual[0], xf, residual[1], seq).reshape(bsz, seq, d)
```
